```python
import math
import jax
import jax.numpy as jnp
from jax import lax
import numpy as np

D_MODEL = 1024
BATCH = 8
SEQ = 8192
DEPTH = 1

D_SSD = D_MODEL
SSD_HEAD_DIM = 64
N_SSD_HEADS = D_SSD // SSD_HEAD_DIM
N_GROUPS = 2
HEADS_PER_GROUP = N_SSD_HEADS // N_GROUPS
D_STATE = 128
SSD_CONV = 4
CHUNK = 128
D_XBC = D_SSD + 2 * N_GROUPS * D_STATE
D_SC = D_MODEL
SC_GROUP_DIM = 64
N_SC_GROUPS = D_SC // SC_GROUP_DIM
SC_CONV = 3
D_MIX = D_SSD + D_SC
PROJ_SPLITS = (D_SSD, D_SSD + D_XBC, D_SSD + D_XBC + N_SSD_HEADS,
               D_SSD + D_XBC + N_SSD_HEADS + D_SC, D_SSD + D_XBC + N_SSD_HEADS + 2 * D_SC)
PROJ_DIM = D_SSD + D_XBC + N_SSD_HEADS + 3 * D_SC
N_EXPERTS = 32
TOP_K = 4
EXPERT_FF = D_MODEL
SWIGLU_LIMIT = 7.0
SWIGLU_ALPHA = 1.702
EXPERT_BLOCK = 256
EPS = 1e-5

kernel_name = 'hybrid_ssd_shortconv_moe'


def rms_norm(x, w):
    xf = x.astype(jnp.float32)
    y = xf * lax.rsqrt(jnp.mean(xf * xf, axis=-1, keepdims=True) + EPS)
    return (y * w.astype(jnp.float32)).astype(x.dtype)


def gated_group_rms_norm(y, z, w, n_groups):
    g = (y * jax.nn.silu(z)).astype(jnp.float32)
    shp = g.shape
    g = g.reshape(*shp[:-1], n_groups, shp[-1] // n_groups)
    g = g * lax.rsqrt(jnp.mean(g * g, axis=-1, keepdims=True) + EPS)
    return (g.reshape(shp) * w.astype(jnp.float32)).astype(y.dtype)


def causal_depthwise_conv(u, w):
    k, c = w.shape
    return lax.conv_general_dilated(
        u, w[:, None, :], window_strides=(1,), padding=[(k - 1, 0)],
        dimension_numbers=('NWC', 'WIO', 'NWC'), feature_group_count=c)


def ssd_chunked_scan(xdt, adt, bm, cm):
    b, s = xdt.shape[:2]
    nc = s // CHUNK

    def to_chunks(t):
        return jnp.moveaxis(t.reshape(b, nc, CHUNK, *t.shape[2:]), 1, 0)

    causal = jnp.tril(jnp.ones((CHUNK, CHUNK), dtype=bool))[None, :, :, None, None]

    def step(state, inp):
        xc, ac, bc, cc = inp
        acum = jnp.cumsum(ac, axis=1)
        seg = acum[:, :, None] - acum[:, None]
        decay_in = jnp.exp(jnp.where(causal, seg, -jnp.inf))
        cb = jnp.einsum('bign,bjgn->bijg', cc, bc)
        y_intra = jnp.einsum('bijg,bijge,bjgep->bigep', cb, decay_in, xc)
        y_inter = jnp.einsum('bign,bgepn->bigep', cc, state) * jnp.exp(acum)[..., None]
        decay_out = jnp.exp(acum[:, -1:] - acum)
        new_state = (state * jnp.exp(acum[:, -1])[..., None, None]
                     + jnp.einsum('bjgn,bjge,bjgep->bgepn', bc, decay_out, xc))
        return new_state, y_intra + y_inter

    state0 = jnp.zeros((b, N_GROUPS, HEADS_PER_GROUP, SSD_HEAD_DIM, D_STATE), jnp.float32)
    _, y = lax.scan(step, state0, (to_chunks(xdt), to_chunks(adt), to_chunks(bm), to_chunks(cm)))
    return jnp.moveaxis(y, 0, 1).reshape(b, s, N_GROUPS, HEADS_PER_GROUP, SSD_HEAD_DIM)


def mixer_block(xn, w_in, ssd_conv_w, ssd_conv_b, dt_bias, a_log, d_skip, ssd_norm_w,
                sc_conv_w, sc_norm_w, w_out):
    b, s, _ = xn.shape
    proj = xn @ w_in
    z, xbc, dt_raw, g_b, g_c, u = jnp.split(proj, PROJ_SPLITS, axis=-1)
    xbc = jax.nn.silu(causal_depthwise_conv(xbc, ssd_conv_w) + ssd_conv_b)
    xs, bs, cs = jnp.split(xbc, [D_SSD, D_SSD + N_GROUPS * D_STATE], axis=-1)
    dt = jax.nn.softplus(dt_raw.astype(jnp.float32) + dt_bias.astype(jnp.float32))
    a = -jnp.exp(a_log.astype(jnp.float32))
    xs_h = xs.astype(jnp.float32).reshape(b, s, N_GROUPS, HEADS_PER_GROUP, SSD_HEAD_DIM)
    dt_h = dt.reshape(b, s, N_GROUPS, HEADS_PER_GROUP)
    y = ssd_chunked_scan(
        xs_h * dt_h[..., None],
        dt_h * a.reshape(N_GROUPS, HEADS_PER_GROUP),
        bs.astype(jnp.float32).reshape(b, s, N_GROUPS, D_STATE),
        cs.astype(jnp.float32).reshape(b, s, N_GROUPS, D_STATE))
    y = y + d_skip.astype(jnp.float32).reshape(N_GROUPS, HEADS_PER_GROUP)[..., None] * xs_h
    y_ssd = gated_group_rms_norm(y.reshape(b, s, D_SSD).astype(xn.dtype), z, ssd_norm_w, N_GROUPS)
    v = causal_depthwise_conv(g_c * u, sc_conv_w)
    y_sc = rms_norm(g_b * v, sc_norm_w)
    return jnp.concatenate([y_ssd, y_sc], axis=-1) @ w_out


def moe_block(xn, w_router, b_router, w_gate_up, b_gate_up, w_down, b_down):
    b, s, d = xn.shape
    t = b * s
    xf = xn.reshape(t, d)
    logits = (xf @ w_router + b_router).astype(jnp.float32)
    top_logit, top_idx = lax.top_k(logits, TOP_K)
    gates = jax.nn.softmax(top_logit, axis=-1)
    n_assign = t * TOP_K
    flat_e = top_idx.reshape(n_assign).astype(jnp.int32)
    flat_tok = jnp.arange(n_assign, dtype=jnp.int32) // TOP_K
    flat_g = gates.reshape(n_assign)
    order = jnp.argsort(flat_e)
    se = flat_e[order]
    counts = jnp.bincount(flat_e, length=N_EXPERTS).astype(jnp.int32)
    starts = jnp.cumsum(counts) - counts
    pcounts = (counts + EXPERT_BLOCK - 1) // EXPERT_BLOCK * EXPERT_BLOCK
    pends = jnp.cumsum(pcounts)
    pstarts = pends - pcounts
    dest = pstarts[se] + jnp.arange(n_assign, dtype=jnp.int32) - starts[se]
    n_rows = n_assign + N_EXPERTS * EXPERT_BLOCK
    n_blocks = n_rows // EXPERT_BLOCK
    row_tok = jnp.zeros((n_rows,), jnp.int32).at[dest].set(flat_tok[order])
    row_gate = jnp.zeros((n_rows,), jnp.float32).at[dest].set(flat_g[order])
    block_start = jnp.arange(n_blocks, dtype=jnp.int32) * EXPERT_BLOCK
    blk_e = jnp.minimum(jnp.searchsorted(pends, block_start, side='right'), N_EXPERTS - 1)

    def expert_block(args):
        tok, g, e = args
        xb = xf[tok]
        gu = xb @ w_gate_up[e] + b_gate_up[e]
        gate, up = gu[:, :EXPERT_FF], gu[:, EXPERT_FF:]
        gate = jnp.minimum(gate, SWIGLU_LIMIT)
        up = jnp.clip(up, -SWIGLU_LIMIT, SWIGLU_LIMIT)
        hid = (up + 1.0) * (gate * jax.nn.sigmoid(SWIGLU_ALPHA * gate))
        out = hid @ w_down[e] + b_down[e]
        return (out * g[:, None]).astype(xf.dtype)

    outs = lax.map(expert_block, (row_tok.reshape(n_blocks, EXPERT_BLOCK),
                                  row_gate.reshape(n_blocks, EXPERT_BLOCK), blk_e))
    y = jnp.zeros_like(xf).at[row_tok].add(outs.reshape(n_rows, d))
    return y.reshape(b, s, d)


def setup_inputs(seed: int = 0) -> dict:
    key = jax.random.key(seed)
    ks = jax.random.split(key, 20)
    L, D, E, F = DEPTH, D_MODEL, N_EXPERTS, EXPERT_FF
    nrm = jax.random.normal
    x = nrm(ks[0], (BATCH, SEQ, D), jnp.float32)
    norm1_w = 1.0 + 0.02 * nrm(ks[1], (L, D), jnp.float32)
    col_scale = jnp.ones((PROJ_DIM,), jnp.float32).at[PROJ_SPLITS[1]:PROJ_SPLITS[2]].set(0.1)
    w_in = nrm(ks[2], (L, D, PROJ_DIM), jnp.float32) * (D ** -0.5) * col_scale
    ssd_conv_w = nrm(ks[3], (L, SSD_CONV, D_XBC), jnp.float32) * (SSD_CONV ** -0.5)
    ssd_conv_b = 0.02 * nrm(ks[4], (L, D_XBC), jnp.float32)
    dt0 = jnp.exp(jax.random.uniform(ks[5], (L, N_SSD_HEADS), jnp.float32,
                                     math.log(1e-3), math.log(1e-1)))
    dt_bias = dt0 + jnp.log(-jnp.expm1(-dt0))
    a_log = jnp.log(jax.random.uniform(ks[6], (L, N_SSD_HEADS), jnp.float32, 1.0, 16.0))
    d_skip = 1.0 + 0.1 * nrm(ks[7], (L, N_SSD_HEADS), jnp.float32)
    ssd_norm_w = 1.0 + 0.02 * nrm(ks[8], (L, D_SSD), jnp.float32)
    sc_conv_w = nrm(ks[9], (L, SC_CONV, D_SC), jnp.float32) * (SC_CONV ** -0.5)
    sc_norm_w = 1.0 + 0.02 * nrm(ks[10], (L, D_SC), jnp.float32)
    w_out = nrm(ks[11], (L, D_MIX, D), jnp.float32) * (D_MIX ** -0.5)
    norm2_w = 1.0 + 0.02 * nrm(ks[12], (L, D), jnp.float32)
    w_router = nrm(ks[13], (L, D, E), jnp.float32) * (D ** -0.5)
    b_router = 0.01 * nrm(ks[14], (L, E), jnp.float32)
    w_gate_up = nrm(ks[15], (L, E, D, 2 * F), jnp.float32) * (D ** -0.5)
    b_gate_up = 0.01 * nrm(ks[16], (L, E, 2 * F), jnp.float32)
    w_down = nrm(ks[17], (L, E, F, D), jnp.float32) * (F ** -0.5)
    b_down = 0.01 * nrm(ks[18], (L, E, D), jnp.float32)
    final_norm_w = 1.0 + 0.02 * nrm(ks[19], (D,), jnp.float32)
    return {'x': x, 'norm1_w': norm1_w, 'w_in': w_in, 'ssd_conv_w': ssd_conv_w,
            'ssd_conv_b': ssd_conv_b, 'dt_bias': dt_bias, 'a_log': a_log, 'd_skip': d_skip,
            'ssd_norm_w': ssd_norm_w, 'sc_conv_w': sc_conv_w, 'sc_norm_w': sc_norm_w,
            'w_out': w_out, 'norm2_w': norm2_w, 'w_router': w_router, 'b_router': b_router,
            'w_gate_up': w_gate_up, 'b_gate_up': b_gate_up, 'w_down': w_down, 'b_down': b_down,
            'final_norm_w': final_norm_w}


def reference(x, norm1_w, w_in, ssd_conv_w, ssd_conv_b, dt_bias, a_log, d_skip, ssd_norm_w,
              sc_conv_w, sc_norm_w, w_out, norm2_w, w_router, b_router, w_gate_up, b_gate_up,
              w_down, b_down, final_norm_w):
    h = x
    for l in range(DEPTH):
        xn = rms_norm(h, norm1_w[l])
        h = h + mixer_block(xn, w_in[l], ssd_conv_w[l], ssd_conv_b[l], dt_bias[l], a_log[l],
                            d_skip[l], ssd_norm_w[l], sc_conv_w[l], sc_norm_w[l], w_out[l])
        xn = rms_norm(h, norm2_w[l])
        h = h + moe_block(xn, w_router[l], b_router[l], w_gate_up[l], b_gate_up[l],
                          w_down[l], b_down[l])
    return rms_norm(h, final_norm_w)
```

```python
import functools

import jax
import jax.numpy as jnp
from jax import lax
from jax.experimental import pallas as pl
from jax.experimental.pallas import tpu as pltpu

D_MODEL = 1024
D_SSD = 1024
SSD_HEAD_DIM = 64
N_SSD_HEADS = 16
N_GROUPS = 2
HEADS_PER_GROUP = 8
D_STATE = 128
SSD_CONV = 4
CHUNK = 128
D_XBC = D_SSD + 2 * N_GROUPS * D_STATE
D_SC = 1024
SC_CONV = 3
D_MIX = D_SSD + D_SC
N_EXPERTS = 32
TOP_K = 4
EXPERT_FF = 1024
SWIGLU_LIMIT = 7.0
SWIGLU_ALPHA = 1.702
EPS = 1e-5

VMEM_LIMIT_BYTES = 56 * 1024 * 1024
SUBLANES = 8
ROW_BLOCK = 256
IN_TILE = 256
OUT_TILE = 256
MOVE_TILE = 256

F32 = jnp.float32
BF16 = jnp.bfloat16


def _dot(a, b):
    return jnp.dot(a, b, preferred_element_type=F32)


def _split3(v):
    hi = v.astype(BF16)
    r1 = v - hi.astype(F32)
    mid = r1.astype(BF16)
    lo = (r1 - mid.astype(F32)).astype(BF16)
    return hi, mid, lo


def _dot_exact_rhs01(v, m01):
    hi, mid, lo = _split3(v)
    return _dot(hi, m01) + _dot(mid, m01) + _dot(lo, m01)


def _dot_exact_lhs01(m01, v):
    hi, mid, lo = _split3(v)
    return _dot(m01, hi) + _dot(m01, mid) + _dot(m01, lo)


def _silu(v):
    return v * (1.0 / (1.0 + jnp.exp(-v)))


def _softplus(v):
    return jnp.maximum(v, 0.0) + jnp.log1p(jnp.exp(-jnp.abs(v)))


def _shifted(cur, tail, s):
    if s == 0:
        return cur
    rc = pltpu.roll(cur, s, axis=0)
    rt = pltpu.roll(tail, s, axis=0)
    row = lax.broadcasted_iota(jnp.int32, tail.shape, 0)
    first = jnp.where(row < s, rt, rc[0:SUBLANES])
    return jnp.concatenate([first, rc[SUBLANES:]], axis=0)


def _inproj_kernel(x_ref, nw_ref, w_ref, wdt_ref, wdtt_ref,
                   z_ref, xbc_ref, gb_ref, gc_ref, u_ref, dt_ref, dtt_ref):
    x = x_ref[...]
    xn = x * lax.rsqrt(jnp.mean(x * x, axis=-1, keepdims=True) + EPS) * nw_ref[...]
    xb = xn.astype(BF16)
    o = 0
    for ref in (z_ref, xbc_ref, gb_ref, gc_ref, u_ref):
        n = ref.shape[-1]
        ref[...] = _dot(xb, w_ref[:, o:o + n]).astype(ref.dtype)
        o += n
    dt_ref[...] = _dot(xb, wdt_ref[...])
    dtt_ref[...] = lax.dot_general(wdtt_ref[...], xb, (((1,), (1,)), ((), ())),
                                   preferred_element_type=F32)


def _inproj(x2, nw, w_main, w_dt, w_dtt):
    t = x2.shape[0]
    tm = IN_TILE
    row = lambda n: pl.BlockSpec((tm, n), lambda i: (i, 0))
    full = lambda a: pl.BlockSpec(a.shape, lambda i: (0,) * a.ndim)
    return pl.pallas_call(
        _inproj_kernel,
        grid=(t // tm,),
        in_specs=[row(D_MODEL), full(nw), full(w_main), full(w_dt), full(w_dtt)],
        out_specs=[row(D_SSD), row(D_XBC), row(D_SC), row(D_SC), row(D_SC),
                   row(N_SSD_HEADS), pl.BlockSpec((N_SSD_HEADS, tm), lambda i: (0, i))],
        out_shape=[jax.ShapeDtypeStruct((t, D_SSD), BF16),
                   jax.ShapeDtypeStruct((t, D_XBC), BF16),
                   jax.ShapeDtypeStruct((t, D_SC), BF16),
                   jax.ShapeDtypeStruct((t, D_SC), BF16),
                   jax.ShapeDtypeStruct((t, D_SC), BF16),
                   jax.ShapeDtypeStruct((t, N_SSD_HEADS), F32),
                   jax.ShapeDtypeStruct((N_SSD_HEADS, t), F32)],
        compiler_params=pltpu.CompilerParams(
            dimension_semantics=("arbitrary",), vmem_limit_bytes=VMEM_LIMIT_BYTES),
    )(x2, nw, w_main, w_dt, w_dtt)


def _mixer_kernel(z_ref, xbc_ref, gb_ref, gc_ref, u_ref, dt_ref, dtt_ref,
                  cw_ref, cb_ref, dtb_ref, dtbt_ref, alog_ref, alogt_ref, dskip_ref,
                  nw_ref, scw_ref, scnw_ref, out_ref,
                  state_ref, tail_ref, sctail_ref):
    q = CHUNK

    @pl.when(pl.program_id(1) == 0)
    def _():
        state_ref[...] = jnp.zeros_like(state_ref)
        tail_ref[...] = jnp.zeros_like(tail_ref)
        sctail_ref[...] = jnp.zeros_like(sctail_ref)

    raw = xbc_ref[...].astype(F32)
    tail = tail_ref[...]
    conv = cb_ref[...]
    for k in range(SSD_CONV):
        conv = conv + cw_ref[k:k + 1, :] * _shifted(raw, tail, SSD_CONV - 1 - k)
    tail_ref[...] = raw[q - SUBLANES:q]
    act = _silu(conv)
    xs = act[:, :D_SSD]
    n_bc = N_GROUPS * D_STATE
    b_all = act[:, D_SSD:D_SSD + n_bc].astype(BF16)
    c_all = act[:, D_SSD + n_bc:].astype(BF16)

    dt = _softplus(dt_ref[...] + dtb_ref[...])
    dtt = _softplus(dtt_ref[...] + dtbt_ref[...])
    adt = dt * (-jnp.exp(alog_ref[...]))
    adtt = dtt * (-jnp.exp(alogt_ref[...]))
    ri = lax.broadcasted_iota(jnp.int32, (q, q), 0)
    ci = lax.broadcasted_iota(jnp.int32, (q, q), 1)
    causal = ri >= ci
    tri = jnp.where(causal, 1.0, 0.0).astype(BF16)
    trit = jnp.where(ri <= ci, 1.0, 0.0).astype(BF16)
    acum = _dot_exact_lhs01(tri, adt)
    acumt = _dot_exact_rhs01(adtt, trit)
    a_last = acum[q - 1:q, :]

    hh = lax.broadcasted_iota(jnp.int32, (N_SSD_HEADS, D_SSD), 0)
    hl = lax.broadcasted_iota(jnp.int32, (N_SSD_HEADS, D_SSD), 1) // SSD_HEAD_DIM
    expand = jnp.where(hh == hl, 1.0, 0.0).astype(BF16)

    dt_x = _dot_exact_rhs01(dt, expand)
    xdt = xs * dt_x
    xdt_b = xdt.astype(BF16)
    dec_in = _dot_exact_rhs01(jnp.exp(acum), expand)
    dec_out = _dot_exact_rhs01(jnp.exp(a_last - acum), expand)
    dec_all = _dot_exact_rhs01(jnp.exp(a_last), expand)
    xw_b = (xdt * dec_out).astype(BF16)

    lane_lo = lax.broadcasted_iota(jnp.int32, (q, 2 * SSD_HEAD_DIM), 1) < SSD_HEAD_DIM
    gw = HEADS_PER_GROUP * SSD_HEAD_DIM
    y_parts = []
    for g in range(N_GROUPS):
        b_g = b_all[:, g * D_STATE:(g + 1) * D_STATE]
        c_g = c_all[:, g * D_STATE:(g + 1) * D_STATE]
        cb = lax.dot_general(c_g, b_g, (((1,), (1,)), ((), ())), preferred_element_type=F32)
        st = state_ref[:, g * gw:(g + 1) * gw]
        y_inter = _dot(c_g, st.astype(BF16)) * dec_in[:, g * gw:(g + 1) * gw]
        new_st = lax.dot_general(b_g, xw_b[:, g * gw:(g + 1) * gw], (((0,), (0,)), ((), ())),
                                 preferred_element_type=F32)
        state_ref[:, g * gw:(g + 1) * gw] = st * dec_all[:, g * gw:(g + 1) * gw] + new_st
        pair_parts = []
        for pr in range(HEADS_PER_GROUP // 2):
            h0 = g * HEADS_PER_GROUP + 2 * pr
            lo = h0 * SSD_HEAD_DIM
            rhs = xdt_b[:, lo:lo + 2 * SSD_HEAD_DIM]
            ys = []
            for h in (h0, h0 + 1):
                seg = acum[:, h:h + 1] - acumt[h:h + 1, :]
                m = (cb * jnp.exp(jnp.where(causal, seg, -jnp.inf))).astype(BF16)
                ys.append(_dot(m, rhs))
            pair_parts.append(jnp.where(lane_lo, ys[0], ys[1]))
        y_parts.append(jnp.concatenate(pair_parts, axis=-1) + y_inter)
    y = jnp.concatenate(y_parts, axis=-1) + _dot_exact_rhs01(dskip_ref[...], expand) * xs

    gated = y * _silu(z_ref[...].astype(F32))
    outs = []
    for g in range(N_GROUPS):
        gg = gated[:, g * gw:(g + 1) * gw]
        outs.append(gg * lax.rsqrt(jnp.mean(gg * gg, axis=-1, keepdims=True) + EPS))
    y_ssd = jnp.concatenate(outs, axis=-1) * nw_ref[...]
    out_ref[:, :D_SSD] = y_ssd.astype(out_ref.dtype)

    cu = gc_ref[...].astype(F32) * u_ref[...].astype(F32)
    sctail = sctail_ref[...]
    v = jnp.zeros_like(cu)
    for k in range(SC_CONV):
        v = v + scw_ref[k:k + 1, :] * _shifted(cu, sctail, SC_CONV - 1 - k)
    sctail_ref[...] = cu[q - SUBLANES:q]
    gv = gb_ref[...].astype(F32) * v
    y_sc = gv * lax.rsqrt(jnp.mean(gv * gv, axis=-1, keepdims=True) + EPS) * scnw_ref[...]
    out_ref[:, D_SSD:] = y_sc.astype(out_ref.dtype)


def _mixer(z, xbc, gb, gc, u, dt, dtt, batch, seq, cw, cb, dtb, alog, dskip, nw, scw, scnw):
    q = CHUNK
    nc = seq // q
    t = batch * seq
    row = lambda n: pl.BlockSpec((q, n), lambda b, c: (b * nc + c, 0))
    full = lambda a: pl.BlockSpec(a.shape, lambda b, c: (0,) * a.ndim)
    dtbt, alogt = dtb.reshape(N_SSD_HEADS, 1), alog.reshape(N_SSD_HEADS, 1)
    params = (cw, cb, dtb, dtbt, alog, alogt, dskip, nw, scw, scnw)
    return pl.pallas_call(
        _mixer_kernel,
        grid=(batch, nc),
        in_specs=[row(D_SSD), row(D_XBC), row(D_SC), row(D_SC), row(D_SC), row(N_SSD_HEADS),
                  pl.BlockSpec((N_SSD_HEADS, q), lambda b, c: (0, b * nc + c))]
                 + [full(a) for a in params],
        out_specs=row(D_MIX),
        out_shape=jax.ShapeDtypeStruct((t, D_MIX), BF16),
        scratch_shapes=[pltpu.VMEM((D_STATE, D_SSD), F32),
                        pltpu.VMEM((SUBLANES, D_XBC), F32),
                        pltpu.VMEM((SUBLANES, D_SC), F32)],
        compiler_params=pltpu.CompilerParams(
            dimension_semantics=("arbitrary", "arbitrary"), vmem_limit_bytes=VMEM_LIMIT_BYTES),
    )(z, xbc, gb, gc, u, dt, dtt, *params)


def _outproj_kernel(ymix_ref, x_ref, wout_ref, nw_ref, wrh_ref, wrl_ref, br_ref,
                    h1_ref, xn_ref, idx_ref, gate_ref, rank_ref, cnt_ref, run_ref):
    tm = x_ref.shape[0]

    @pl.when(pl.program_id(0) == 0)
    def _():
        run_ref[...] = jnp.zeros_like(run_ref)

    h1 = x_ref[...] + _dot(ymix_ref[...], wout_ref[...])
    h1_ref[...] = h1
    xn = h1 * lax.rsqrt(jnp.mean(h1 * h1, axis=-1, keepdims=True) + EPS) * nw_ref[...]
    xn_ref[...] = xn

    xh = xn.astype(BF16)
    xl = (xn - xh.astype(F32)).astype(BF16)
    logits = (_dot(xh, wrh_ref[...]) + _dot(xh, wrl_ref[...]) + _dot(xl, wrh_ref[...])
              + br_ref[...])

    lane = lax.broadcasted_iota(jnp.int32, (tm, N_EXPERTS), 1).astype(F32)
    work = logits
    vals, sels = [], []
    for _ in range(TOP_K):
        m = jnp.max(work, axis=-1, keepdims=True)
        first = jnp.min(jnp.where(work == m, lane, float(N_EXPERTS)), axis=-1, keepdims=True)
        sel = lane == first
        vals.append(m)
        sels.append(sel)
        work = jnp.where(sel, -jnp.inf, work)
    exps = [jnp.exp(v - vals[0]) for v in vals]
    denom = exps[0] + exps[1] + exps[2] + exps[3]

    onehot = jnp.zeros((tm, N_EXPERTS), F32)
    for sel in sels:
        onehot = onehot + jnp.where(sel, 1.0, 0.0)
    ri = lax.broadcasted_iota(jnp.int32, (tm, tm), 0)
    ci = lax.broadcasted_iota(jnp.int32, (tm, tm), 1)
    strict = jnp.where(ri > ci, 1.0, 0.0).astype(BF16)
    before = run_ref[...] + _dot(strict, onehot.astype(BF16))
    run_ref[...] = run_ref[...] + jnp.sum(onehot, axis=0, keepdims=True)
    cnt_ref[...] = run_ref[...]

    slot = lax.broadcasted_iota(jnp.int32, (tm, TOP_K), 1)
    idx = jnp.zeros((tm, TOP_K), F32)
    gate = jnp.zeros((tm, TOP_K), F32)
    rank = jnp.zeros((tm, TOP_K), F32)
    for k in range(TOP_K):
        e_k = jnp.sum(jnp.where(sels[k], lane, 0.0), axis=-1, keepdims=True)
        r_k = jnp.sum(jnp.where(sels[k], before, 0.0), axis=-1, keepdims=True)
        idx = jnp.where(slot == k, e_k, idx)
        gate = jnp.where(slot == k, exps[k] / denom, gate)
        rank = jnp.where(slot == k, r_k, rank)
    idx_ref[...] = idx.astype(jnp.int32)
    gate_ref[...] = gate
    rank_ref[...] = rank.astype(jnp.int32)


def _outproj(ymix, x2, w_out, nw, wr_hi, wr_lo, br):
    t = x2.shape[0]
    tm = OUT_TILE
    row = lambda n: pl.BlockSpec((tm, n), lambda i: (i, 0))
    full = lambda a: pl.BlockSpec(a.shape, lambda i: (0,) * a.ndim)
    return pl.pallas_call(
        _outproj_kernel,
        grid=(t // tm,),
        in_specs=[row(D_MIX), row(D_MODEL), full(w_out), full(nw), full(wr_hi), full(wr_lo),
                  full(br)],
        out_specs=[row(D_MODEL), row(D_MODEL), row(TOP_K), row(TOP_K), row(TOP_K),
                   pl.BlockSpec((1, N_EXPERTS), lambda i: (0, 0))],
        out_shape=[jax.ShapeDtypeStruct((t, D_MODEL), F32),
                   jax.ShapeDtypeStruct((t, D_MODEL), F32),
                   jax.ShapeDtypeStruct((t, TOP_K), jnp.int32),
                   jax.ShapeDtypeStruct((t, TOP_K), F32),
                   jax.ShapeDtypeStruct((t, TOP_K), jnp.int32),
                   jax.ShapeDtypeStruct((1, N_EXPERTS), F32)],
        scratch_shapes=[pltpu.VMEM((1, N_EXPERTS), F32)],
        compiler_params=pltpu.CompilerParams(
            dimension_semantics=("arbitrary",), vmem_limit_bytes=VMEM_LIMIT_BYTES),
    )(ymix, x2, w_out, nw, wr_hi, wr_lo, br)


def _dest_kernel(pstart_ref, idx_ref, rank_ref, dest_ref):
    idx = idx_ref[...]
    base = jnp.zeros_like(idx)
    for e in range(N_EXPERTS):
        base = jnp.where(idx == e, pstart_ref[e], base)
    dest_ref[...] = base + rank_ref[...]


def _dest(pstarts, idx_flat, rank_flat):
    full = lambda a: pl.BlockSpec(a.shape, lambda i, ps: (0,) * a.ndim)
    return pl.pallas_call(
        _dest_kernel,
        grid_spec=pltpu.PrefetchScalarGridSpec(
            num_scalar_prefetch=1, grid=(1,),
            in_specs=[full(idx_flat), full(rank_flat)], out_specs=full(idx_flat)),
        out_shape=jax.ShapeDtypeStruct(idx_flat.shape, jnp.int32),
    )(pstarts, idx_flat, rank_flat)


def _dispatch_kernel(dest_ref, xn_ref, xs_hbm, zero_ref, sem):
    tm = xn_ref.shape[0]
    n_tok_tiles = pl.num_programs(0) - (N_EXPERTS * ROW_BLOCK) // (tm * TOP_K)
    i = pl.program_id(0)

    @pl.when(i < n_tok_tiles)
    def _():
        def issue(r, carry):
            for k in range(TOP_K):
                d = dest_ref[r * TOP_K + k]
                pltpu.make_async_copy(xn_ref.at[pl.ds(r, 1)], xs_hbm.at[pl.ds(d, 1)], sem).start()
            return carry

        lax.fori_loop(0, tm, issue, 0)

    @pl.when(i >= n_tok_tiles)
    def _():
        zero_ref[...] = jnp.zeros_like(zero_ref)

        def issue(r, carry):
            for k in range(TOP_K):
                d = dest_ref[r * TOP_K + k]
                pltpu.make_async_copy(zero_ref, xs_hbm.at[pl.ds(d, 1)], sem).start()
            return carry

        lax.fori_loop(0, tm, issue, 0)

    for _ in range(TOP_K):
        pltpu.make_async_copy(xn_ref, xs_hbm.at[pl.ds(0, tm)], sem).wait()


def _dispatch(dest_all, xn, n_rows):
    t, d = xn.shape
    tm = MOVE_TILE
    n_tok_tiles = t // tm
    return pl.pallas_call(
        _dispatch_kernel,
        grid=(dest_all.shape[0] // (tm * TOP_K),),
        in_specs=[pl.BlockSpec((tm * TOP_K,), lambda i: (i,), memory_space=pltpu.SMEM),
                  pl.BlockSpec((tm, d), lambda i: (jnp.minimum(i, n_tok_tiles - 1), 0))],
        out_specs=pl.BlockSpec(memory_space=pl.ANY),
        out_shape=jax.ShapeDtypeStruct((n_rows, d), F32),
        scratch_shapes=[pltpu.VMEM((1, d), F32), pltpu.SemaphoreType.DMA(())],
        compiler_params=pltpu.CompilerParams(
            dimension_semantics=("arbitrary",), has_side_effects=True),
    )(dest_all, xn)


def _expert_kernel(blk_e_ref, blk_n_ref, xs_ref, wgu_ref, bgu_ref, wd_ref, bd_ref, out_ref):
    i = pl.program_id(0)
    n_valid = blk_n_ref[i]

    @pl.when(n_valid > 0)
    def _():
        gu = _dot(xs_ref[...].astype(BF16), wgu_ref[0]) + bgu_ref[0]
        gate = jnp.minimum(gu[:, :EXPERT_FF], SWIGLU_LIMIT)
        up = jnp.clip(gu[:, EXPERT_FF:], -SWIGLU_LIMIT, SWIGLU_LIMIT)
        hid = (up + 1.0) * (gate * (1.0 / (1.0 + jnp.exp(-SWIGLU_ALPHA * gate))))
        out_ref[...] = _dot(hid.astype(BF16), wd_ref[0]) + bd_ref[0]

    @pl.when(n_valid == 0)
    def _():
        out_ref[...] = jnp.zeros_like(out_ref)


def _experts(blk_e, blk_n, xs, wgu, bgu, wd, bd):
    n_rows, d = xs.shape
    rb = ROW_BLOCK
    by_expert = lambda a: pl.BlockSpec((1,) + a.shape[1:], lambda i, be, bn: (be[i], 0, 0))
    return pl.pallas_call(
        _expert_kernel,
        grid_spec=pltpu.PrefetchScalarGridSpec(
            num_scalar_prefetch=2, grid=(n_rows // rb,),
            in_specs=[pl.BlockSpec((rb, d), lambda i, be, bn: (i, 0)),
                      by_expert(wgu), by_expert(bgu), by_expert(wd), by_expert(bd)],
            out_specs=pl.BlockSpec((rb, d), lambda i, be, bn: (i, 0))),
        out_shape=jax.ShapeDtypeStruct((n_rows, d), F32),
        compiler_params=pltpu.CompilerParams(
            dimension_semantics=("arbitrary",), vmem_limit_bytes=VMEM_LIMIT_BYTES),
    )(blk_e, blk_n, xs, wgu, bgu, wd, bd)


def _combine_kernel(dest_ref, gate_ref, h1_ref, fw_ref, ys_hbm, out_ref, buf, sem):
    tm = h1_ref.shape[0]

    def issue(r, carry):
        for k in range(TOP_K):
            d = dest_ref[r * TOP_K + k]
            pltpu.make_async_copy(ys_hbm.at[pl.ds(d, 1)], buf.at[k, pl.ds(r, 1)], sem).start()
        return carry

    lax.fori_loop(0, tm, issue, 0)
    for k in range(TOP_K):
        pltpu.make_async_copy(ys_hbm.at[pl.ds(0, tm)], buf.at[k], sem).wait()

    gate = gate_ref[...]
    h2 = h1_ref[...]
    for k in range(TOP_K):
        h2 = h2 + gate[:, k:k + 1] * buf[k]
    out_ref[...] = h2 * lax.rsqrt(jnp.mean(h2 * h2, axis=-1, keepdims=True) + EPS) * fw_ref[...]


def _combine(dest_1d, gates, h1, fw, ys):
    t, d = h1.shape
    tm = MOVE_TILE
    return pl.pallas_call(
        _combine_kernel,
        grid=(t // tm,),
        in_specs=[pl.BlockSpec((tm * TOP_K,), lambda i: (i,), memory_space=pltpu.SMEM),
                  pl.BlockSpec((tm, TOP_K), lambda i: (i, 0)),
                  pl.BlockSpec((tm, d), lambda i: (i, 0)),
                  pl.BlockSpec((1, d), lambda i: (0, 0)),
                  pl.BlockSpec(memory_space=pl.ANY)],
        out_specs=pl.BlockSpec((tm, d), lambda i: (i, 0)),
        out_shape=jax.ShapeDtypeStruct((t, d), F32),
        scratch_shapes=[pltpu.VMEM((TOP_K, tm, d), F32), pltpu.SemaphoreType.DMA(())],
        compiler_params=pltpu.CompilerParams(
            dimension_semantics=("arbitrary",), vmem_limit_bytes=VMEM_LIMIT_BYTES),
    )(dest_1d, gates, h1, fw, ys)


def _block_plan(counts, n_blocks):
    rb = ROW_BLOCK
    pcounts = (counts + rb - 1) // rb * rb
    pends = jnp.cumsum(pcounts)
    pstarts = pends - pcounts
    block_start = jnp.arange(n_blocks, dtype=jnp.int32) * rb
    blk_e = jnp.minimum(jnp.searchsorted(pends, block_start, side='right'),
                        N_EXPERTS - 1).astype(jnp.int32)
    blk_n = jnp.clip(counts[blk_e] - (block_start - pstarts[blk_e]), 0, rb).astype(jnp.int32)
    n_pad = N_EXPERTS * rb
    padc = pcounts - counts
    cpad = jnp.cumsum(padc)
    p = jnp.arange(n_pad, dtype=jnp.int32)
    e = jnp.minimum(jnp.searchsorted(cpad, p, side='right'), N_EXPERTS - 1)
    in_expert = pstarts[e] + counts[e] + (p - (cpad[e] - padc[e]))
    at_end = pends[-1] + (p - cpad[-1])
    pad_rows = jnp.where(p < cpad[-1], in_expert, at_end).astype(jnp.int32)
    return pstarts.astype(jnp.int32), blk_e, blk_n, pad_rows


def kernel(x, norm1_w, w_in, ssd_conv_w, ssd_conv_b, dt_bias, a_log, d_skip, ssd_norm_w,
           sc_conv_w, sc_norm_w, w_out, norm2_w, w_router, b_router, w_gate_up, b_gate_up,
           w_down, b_down, final_norm_w):
    assert norm1_w.shape[0] == 1, "single-layer problem"
    batch, seq, d = x.shape
    t = batch * seq
    x2 = x.reshape(t, d)
    row = lambda a: a.reshape(1, -1)

    s0, s1, s2 = D_SSD, D_SSD + D_XBC, D_SSD + D_XBC + N_SSD_HEADS
    w = w_in[0]
    w_main = jnp.concatenate([w[:, :s1], w[:, s2:]], axis=1).astype(BF16)
    w_dt = w[:, s1:s2].astype(BF16)
    z, xbc, gb, gc, u, dt, dtt = _inproj(x2, row(norm1_w[0]), w_main, w_dt, w_dt.T)

    ymix = _mixer(z, xbc, gb, gc, u, dt, dtt, batch, seq,
                  ssd_conv_w[0], row(ssd_conv_b[0]), row(dt_bias[0]), row(a_log[0]),
                  row(d_skip[0]), row(ssd_norm_w[0]), sc_conv_w[0], row(sc_norm_w[0]))

    wr = w_router[0]
    wr_hi = wr.astype(BF16)
    wr_lo = (wr - wr_hi.astype(F32)).astype(BF16)
    h1, xn2, idx, gates, rank, counts = _outproj(
        ymix, x2, w_out[0].astype(BF16), row(norm2_w[0]), wr_hi, wr_lo, row(b_router[0]))

    n_assign = t * TOP_K
    n_rows = n_assign + N_EXPERTS * ROW_BLOCK
    pstarts, blk_e, blk_n, pad_rows = _block_plan(counts[0].astype(jnp.int32),
                                                  n_rows // ROW_BLOCK)
    dest = _dest(pstarts, idx.reshape(n_assign // 128, 128), rank.reshape(n_assign // 128, 128))
    dest_1d = dest.reshape(n_assign)

    xs = _dispatch(jnp.concatenate([dest_1d, pad_rows]), xn2, n_rows)
    ys = _experts(blk_e, blk_n, xs, w_gate_up[0].astype(BF16), b_gate_up[0][:, None, :],
                  w_down[0].astype(BF16), b_down[0][:, None, :])
    out = _combine(dest_1d, gates, h1, row(final_norm_w), ys)
    return out.reshape(batch, seq, d)
```

```python
import functools

import jax
import jax.numpy as jnp
from jax import lax
from jax.experimental import pallas as pl
from jax.experimental.pallas import tpu as pltpu

D_MODEL = 1024
D_SSD = 1024
SSD_HEAD_DIM = 64
N_SSD_HEADS = 16
N_GROUPS = 2
HEADS_PER_GROUP = 8
D_STATE = 128
SSD_CONV = 4
CHUNK = 128
D_XBC = D_SSD + 2 * N_GROUPS * D_STATE
D_SC = 1024
SC_CONV = 3
D_MIX = D_SSD + D_SC
N_EXPERTS = 32
TOP_K = 4
EXPERT_FF = 1024
SWIGLU_LIMIT = 7.0
SWIGLU_ALPHA = 1.702
EPS = 1e-5

VMEM_LIMIT_BYTES = 56 * 1024 * 1024
SUBLANES = 8
N_DMA_THREADS = 2
ROW_BLOCK = 512
EXPERT_SUB = 256
IN_TILE = 512
OUT_TILE = 256
MOVE_TILE = 256

F32 = jnp.float32
BF16 = jnp.bfloat16


def _dot(a, b):
    return jnp.dot(a, b, preferred_element_type=F32)


def _split3(v):
    hi = v.astype(BF16)
    r1 = v - hi.astype(F32)
    mid = r1.astype(BF16)
    lo = (r1 - mid.astype(F32)).astype(BF16)
    return hi, mid, lo


def _dot_exact_rhs01(v, m01):
    hi, mid, lo = _split3(v)
    return _dot(hi, m01) + _dot(mid, m01) + _dot(lo, m01)


def _dot_exact_lhs01(m01, v):
    hi, mid, lo = _split3(v)
    return _dot(m01, hi) + _dot(m01, mid) + _dot(m01, lo)


def _silu(v):
    return v * (1.0 / (1.0 + jnp.exp(-v)))


def _softplus(v):
    return jnp.maximum(v, 0.0) + jnp.log1p(jnp.exp(-jnp.abs(v)))


def _shifted(cur, tail, s):
    if s == 0:
        return cur
    rc = pltpu.roll(cur, s, axis=0)
    rt = pltpu.roll(tail, s, axis=0)
    row = lax.broadcasted_iota(jnp.int32, tail.shape, 0)
    first = jnp.where(row < s, rt, rc[0:SUBLANES])
    return jnp.concatenate([first, rc[SUBLANES:]], axis=0)


def _inproj_kernel(x_ref, nw_ref, w_ref, wdt_ref, wdtt_ref,
                   z_ref, xbc_ref, gb_ref, gc_ref, u_ref, dt_ref, dtt_ref):
    x = x_ref[...]
    xn = x * lax.rsqrt(jnp.mean(x * x, axis=-1, keepdims=True) + EPS) * nw_ref[...]
    xb = xn.astype(BF16)
    o = 0
    for ref in (z_ref, xbc_ref, gb_ref, gc_ref, u_ref):
        n = ref.shape[-1]
        ref[...] = _dot(xb, w_ref[:, o:o + n]).astype(ref.dtype)
        o += n
    dt_ref[...] = _dot(xb, wdt_ref[...])
    dtt_ref[...] = lax.dot_general(wdtt_ref[...], xb, (((1,), (1,)), ((), ())),
                                   preferred_element_type=F32)


def _inproj(x2, nw, w_main, w_dt, w_dtt):
    t = x2.shape[0]
    tm = IN_TILE
    row = lambda n: pl.BlockSpec((tm, n), lambda i: (i, 0))
    full = lambda a: pl.BlockSpec(a.shape, lambda i: (0,) * a.ndim)
    return pl.pallas_call(
        _inproj_kernel,
        grid=(t // tm,),
        in_specs=[row(D_MODEL), full(nw), full(w_main), full(w_dt), full(w_dtt)],
        out_specs=[row(D_SSD), row(D_XBC), row(D_SC), row(D_SC), row(D_SC),
                   row(N_SSD_HEADS), pl.BlockSpec((N_SSD_HEADS, tm), lambda i: (0, i))],
        out_shape=[jax.ShapeDtypeStruct((t, D_SSD), BF16),
                   jax.ShapeDtypeStruct((t, D_XBC), BF16),
                   jax.ShapeDtypeStruct((t, D_SC), BF16),
                   jax.ShapeDtypeStruct((t, D_SC), BF16),
                   jax.ShapeDtypeStruct((t, D_SC), BF16),
                   jax.ShapeDtypeStruct((t, N_SSD_HEADS), F32),
                   jax.ShapeDtypeStruct((N_SSD_HEADS, t), F32)],
        compiler_params=pltpu.CompilerParams(
            dimension_semantics=("arbitrary",), vmem_limit_bytes=VMEM_LIMIT_BYTES),
    )(x2, nw, w_main, w_dt, w_dtt)


def _mixer_kernel(z_ref, xbc_ref, gb_ref, gc_ref, u_ref, dt_ref, dtt_ref,
                  cw_ref, cb_ref, dtb_ref, dtbt_ref, alog_ref, alogt_ref, dskip_ref,
                  nw_ref, scw_ref, scnw_ref, out_ref,
                  state_ref, tail_ref, sctail_ref):
    q = CHUNK

    @pl.when(pl.program_id(1) == 0)
    def _():
        state_ref[...] = jnp.zeros_like(state_ref)
        tail_ref[...] = jnp.zeros_like(tail_ref)
        sctail_ref[...] = jnp.zeros_like(sctail_ref)

    raw = xbc_ref[...].astype(F32)
    tail = tail_ref[...]
    conv = cb_ref[...]
    for k in range(SSD_CONV):
        conv = conv + cw_ref[k:k + 1, :] * _shifted(raw, tail, SSD_CONV - 1 - k)
    tail_ref[...] = raw[q - SUBLANES:q]
    act = _silu(conv)
    xs = act[:, :D_SSD]
    n_bc = N_GROUPS * D_STATE
    b_all = act[:, D_SSD:D_SSD + n_bc].astype(BF16)
    c_all = act[:, D_SSD + n_bc:].astype(BF16)

    dt = _softplus(dt_ref[...] + dtb_ref[...])
    dtt = _softplus(dtt_ref[...] + dtbt_ref[...])
    adt = dt * (-jnp.exp(alog_ref[...]))
    adtt = dtt * (-jnp.exp(alogt_ref[...]))
    ri = lax.broadcasted_iota(jnp.int32, (q, q), 0)
    ci = lax.broadcasted_iota(jnp.int32, (q, q), 1)
    causal = ri >= ci
    tri = jnp.where(causal, 1.0, 0.0).astype(BF16)
    trit = jnp.where(ri <= ci, 1.0, 0.0).astype(BF16)
    acum = _dot_exact_lhs01(tri, adt)
    acumt = _dot_exact_rhs01(adtt, trit)
    a_last = acum[q - 1:q, :]

    hh = lax.broadcasted_iota(jnp.int32, (N_SSD_HEADS, D_SSD), 0)
    hl = lax.broadcasted_iota(jnp.int32, (N_SSD_HEADS, D_SSD), 1) // SSD_HEAD_DIM
    expand = jnp.where(hh == hl, 1.0, 0.0).astype(BF16)

    dt_x = _dot_exact_rhs01(dt, expand)
    xdt = xs * dt_x
    xdt_b = xdt.astype(BF16)
    dec_in = _dot_exact_rhs01(jnp.exp(acum), expand)
    dec_out = _dot_exact_rhs01(jnp.exp(a_last - acum), expand)
    dec_all = _dot_exact_rhs01(jnp.exp(a_last), expand)
    xw_b = (xdt * dec_out).astype(BF16)

    lane_lo = lax.broadcasted_iota(jnp.int32, (q, 2 * SSD_HEAD_DIM), 1) < SSD_HEAD_DIM
    gw = HEADS_PER_GROUP * SSD_HEAD_DIM
    y_parts = []
    for g in range(N_GROUPS):
        b_g = b_all[:, g * D_STATE:(g + 1) * D_STATE]
        c_g = c_all[:, g * D_STATE:(g + 1) * D_STATE]
        cb = lax.dot_general(c_g, b_g, (((1,), (1,)), ((), ())), preferred_element_type=F32)
        st = state_ref[:, g * gw:(g + 1) * gw]
        y_inter = _dot(c_g, st.astype(BF16)) * dec_in[:, g * gw:(g + 1) * gw]
        new_st = lax.dot_general(b_g, xw_b[:, g * gw:(g + 1) * gw], (((0,), (0,)), ((), ())),
                                 preferred_element_type=F32)
        state_ref[:, g * gw:(g + 1) * gw] = st * dec_all[:, g * gw:(g + 1) * gw] + new_st
        pair_parts = []
        for pr in range(HEADS_PER_GROUP // 2):
            h0 = g * HEADS_PER_GROUP + 2 * pr
            lo = h0 * SSD_HEAD_DIM
            rhs = xdt_b[:, lo:lo + 2 * SSD_HEAD_DIM]
            ys = []
            for h in (h0, h0 + 1):
                seg = acum[:, h:h + 1] - acumt[h:h + 1, :]
                m = (cb * jnp.exp(jnp.where(causal, seg, -jnp.inf))).astype(BF16)
                ys.append(_dot(m, rhs))
            pair_parts.append(jnp.where(lane_lo, ys[0], ys[1]))
        y_parts.append(jnp.concatenate(pair_parts, axis=-1) + y_inter)
    y = jnp.concatenate(y_parts, axis=-1) + _dot_exact_rhs01(dskip_ref[...], expand) * xs

    gated = y * _silu(z_ref[...].astype(F32))
    outs = []
    for g in range(N_GROUPS):
        gg = gated[:, g * gw:(g + 1) * gw]
        outs.append(gg * lax.rsqrt(jnp.mean(gg * gg, axis=-1, keepdims=True) + EPS))
    y_ssd = jnp.concatenate(outs, axis=-1) * nw_ref[...]
    out_ref[:, :D_SSD] = y_ssd.astype(out_ref.dtype)

    cu = gc_ref[...].astype(F32) * u_ref[...].astype(F32)
    sctail = sctail_ref[...]
    v = jnp.zeros_like(cu)
    for k in range(SC_CONV):
        v = v + scw_ref[k:k + 1, :] * _shifted(cu, sctail, SC_CONV - 1 - k)
    sctail_ref[...] = cu[q - SUBLANES:q]
    gv = gb_ref[...].astype(F32) * v
    y_sc = gv * lax.rsqrt(jnp.mean(gv * gv, axis=-1, keepdims=True) + EPS) * scnw_ref[...]
    out_ref[:, D_SSD:] = y_sc.astype(out_ref.dtype)


def _mixer(z, xbc, gb, gc, u, dt, dtt, batch, seq, cw, cb, dtb, alog, dskip, nw, scw, scnw):
    q = CHUNK
    nc = seq // q
    t = batch * seq
    row = lambda n: pl.BlockSpec((q, n), lambda b, c: (b * nc + c, 0))
    full = lambda a: pl.BlockSpec(a.shape, lambda b, c: (0,) * a.ndim)
    dtbt, alogt = dtb.reshape(N_SSD_HEADS, 1), alog.reshape(N_SSD_HEADS, 1)
    params = (cw, cb, dtb, dtbt, alog, alogt, dskip, nw, scw, scnw)
    return pl.pallas_call(
        _mixer_kernel,
        grid=(batch, nc),
        in_specs=[row(D_SSD), row(D_XBC), row(D_SC), row(D_SC), row(D_SC), row(N_SSD_HEADS),
                  pl.BlockSpec((N_SSD_HEADS, q), lambda b, c: (0, b * nc + c))]
                 + [full(a) for a in params],
        out_specs=row(D_MIX),
        out_shape=jax.ShapeDtypeStruct((t, D_MIX), BF16),
        scratch_shapes=[pltpu.VMEM((D_STATE, D_SSD), F32),
                        pltpu.VMEM((SUBLANES, D_XBC), F32),
                        pltpu.VMEM((SUBLANES, D_SC), F32)],
        compiler_params=pltpu.CompilerParams(
            dimension_semantics=("arbitrary", "arbitrary"), vmem_limit_bytes=VMEM_LIMIT_BYTES),
    )(z, xbc, gb, gc, u, dt, dtt, *params)


def _outproj_kernel(ymix_ref, x_ref, wout_ref, nw_ref, wrh_ref, wrl_ref, br_ref,
                    h1_ref, xn_ref, idx_ref, gate_ref, rank_ref, cnt_ref, run_ref):
    tm = x_ref.shape[0]

    @pl.when(pl.program_id(0) == 0)
    def _():
        run_ref[...] = jnp.zeros_like(run_ref)

    h1 = x_ref[...] + _dot(ymix_ref[...], wout_ref[...])
    h1_ref[...] = h1
    xn = h1 * lax.rsqrt(jnp.mean(h1 * h1, axis=-1, keepdims=True) + EPS) * nw_ref[...]
    xn_ref[...] = xn

    xh = xn.astype(BF16)
    xl = (xn - xh.astype(F32)).astype(BF16)
    logits = (_dot(xh, wrh_ref[...]) + _dot(xh, wrl_ref[...]) + _dot(xl, wrh_ref[...])
              + br_ref[...])

    lane = lax.broadcasted_iota(jnp.int32, (tm, N_EXPERTS), 1).astype(F32)
    work = logits
    vals, sels = [], []
    for _ in range(TOP_K):
        m = jnp.max(work, axis=-1, keepdims=True)
        first = jnp.min(jnp.where(work == m, lane, float(N_EXPERTS)), axis=-1, keepdims=True)
        sel = lane == first
        vals.append(m)
        sels.append(sel)
        work = jnp.where(sel, -jnp.inf, work)
    exps = [jnp.exp(v - vals[0]) for v in vals]
    denom = exps[0] + exps[1] + exps[2] + exps[3]

    onehot = jnp.zeros((tm, N_EXPERTS), F32)
    for sel in sels:
        onehot = onehot + jnp.where(sel, 1.0, 0.0)
    ri = lax.broadcasted_iota(jnp.int32, (tm, tm), 0)
    ci = lax.broadcasted_iota(jnp.int32, (tm, tm), 1)
    strict = jnp.where(ri > ci, 1.0, 0.0).astype(BF16)
    before = run_ref[...] + _dot(strict, onehot.astype(BF16))
    run_ref[...] = run_ref[...] + jnp.sum(onehot, axis=0, keepdims=True)
    cnt_ref[...] = run_ref[...]

    slot = lax.broadcasted_iota(jnp.int32, (tm, TOP_K), 1)
    idx = jnp.zeros((tm, TOP_K), F32)
    gate = jnp.zeros((tm, TOP_K), F32)
    rank = jnp.zeros((tm, TOP_K), F32)
    for k in range(TOP_K):
        e_k = jnp.sum(jnp.where(sels[k], lane, 0.0), axis=-1, keepdims=True)
        r_k = jnp.sum(jnp.where(sels[k], before, 0.0), axis=-1, keepdims=True)
        idx = jnp.where(slot == k, e_k, idx)
        gate = jnp.where(slot == k, exps[k] / denom, gate)
        rank = jnp.where(slot == k, r_k, rank)
    idx_ref[...] = idx.astype(jnp.int32)
    gate_ref[...] = gate
    rank_ref[...] = rank.astype(jnp.int32)


def _outproj(ymix, x2, w_out, nw, wr_hi, wr_lo, br):
    t = x2.shape[0]
    tm = OUT_TILE
    row = lambda n: pl.BlockSpec((tm, n), lambda i: (i, 0))
    full = lambda a: pl.BlockSpec(a.shape, lambda i: (0,) * a.ndim)
    return pl.pallas_call(
        _outproj_kernel,
        grid=(t // tm,),
        in_specs=[row(D_MIX), row(D_MODEL), full(w_out), full(nw), full(wr_hi), full(wr_lo),
                  full(br)],
        out_specs=[row(D_MODEL), row(D_MODEL), row(TOP_K), row(TOP_K), row(TOP_K),
                   pl.BlockSpec((1, N_EXPERTS), lambda i: (0, 0))],
        out_shape=[jax.ShapeDtypeStruct((t, D_MODEL), F32),
                   jax.ShapeDtypeStruct((t, D_MODEL), F32),
                   jax.ShapeDtypeStruct((t, TOP_K), jnp.int32),
                   jax.ShapeDtypeStruct((t, TOP_K), F32),
                   jax.ShapeDtypeStruct((t, TOP_K), jnp.int32),
                   jax.ShapeDtypeStruct((1, N_EXPERTS), F32)],
        scratch_shapes=[pltpu.VMEM((1, N_EXPERTS), F32)],
        compiler_params=pltpu.CompilerParams(
            dimension_semantics=("arbitrary",), vmem_limit_bytes=VMEM_LIMIT_BYTES),
    )(ymix, x2, w_out, nw, wr_hi, wr_lo, br)


def _dest_kernel(pstart_ref, idx_ref, rank_ref, dest_ref):
    idx = idx_ref[...]
    base = jnp.zeros_like(idx)
    for e in range(N_EXPERTS):
        base = jnp.where(idx == e, pstart_ref[e], base)
    dest_ref[...] = base + rank_ref[...]


def _dest(pstarts, idx_flat, rank_flat):
    full = lambda a: pl.BlockSpec(a.shape, lambda i, ps: (0,) * a.ndim)
    return pl.pallas_call(
        _dest_kernel,
        grid_spec=pltpu.PrefetchScalarGridSpec(
            num_scalar_prefetch=1, grid=(1,),
            in_specs=[full(idx_flat), full(rank_flat)], out_specs=full(idx_flat)),
        out_shape=jax.ShapeDtypeStruct(idx_flat.shape, jnp.int32),
    )(pstarts, idx_flat, rank_flat)


def _dispatch_kernel(dest_ref, xn_ref, xs_hbm, zero_ref, sem):
    tm = xn_ref.shape[0]
    n_tok_tiles = pl.num_programs(0) - (N_EXPERTS * ROW_BLOCK) // (tm * TOP_K)
    i = pl.program_id(0)

    @pl.when(i < n_tok_tiles)
    def _():
        def issue(r, carry):
            for k in range(TOP_K):
                d = dest_ref[r * TOP_K + k]
                pltpu.make_async_copy(xn_ref.at[pl.ds(r, 1)], xs_hbm.at[pl.ds(d, 1)],
                                      sem).start(priority=k % N_DMA_THREADS)
            return carry

        lax.fori_loop(0, tm, issue, 0)

    @pl.when(i >= n_tok_tiles)
    def _():
        zero_ref[...] = jnp.zeros_like(zero_ref)

        def issue(r, carry):
            for k in range(TOP_K):
                d = dest_ref[r * TOP_K + k]
                pltpu.make_async_copy(zero_ref, xs_hbm.at[pl.ds(d, 1)],
                                      sem).start(priority=k % N_DMA_THREADS)
            return carry

        lax.fori_loop(0, tm, issue, 0)

    for _ in range(TOP_K):
        pltpu.make_async_copy(xn_ref, xs_hbm.at[pl.ds(0, tm)], sem).wait()


def _dispatch(dest_all, xn, n_rows):
    t, d = xn.shape
    tm = MOVE_TILE
    n_tok_tiles = t // tm
    return pl.pallas_call(
        _dispatch_kernel,
        grid=(dest_all.shape[0] // (tm * TOP_K),),
        in_specs=[pl.BlockSpec((tm * TOP_K,), lambda i: (i,), memory_space=pltpu.SMEM),
                  pl.BlockSpec((tm, d), lambda i: (jnp.minimum(i, n_tok_tiles - 1), 0))],
        out_specs=pl.BlockSpec(memory_space=pl.ANY),
        out_shape=jax.ShapeDtypeStruct((n_rows, d), F32),
        scratch_shapes=[pltpu.VMEM((1, d), F32), pltpu.SemaphoreType.DMA(())],
        compiler_params=pltpu.CompilerParams(
            dimension_semantics=("arbitrary",), has_side_effects=True),
    )(dest_all, xn)


def _expert_kernel(blk_e_ref, blk_n_ref, xs_ref, wgu_ref, bgu_ref, wd_ref, bd_ref, out_ref,
                   wgu_b, wd_b):
    i = pl.program_id(0)
    n_valid = blk_n_ref[i]

    @pl.when((i == 0) | (blk_e_ref[i] != blk_e_ref[jnp.maximum(i - 1, 0)]))
    def _():
        wgu_b[...] = wgu_ref[0].astype(BF16)
        wd_b[...] = wd_ref[0].astype(BF16)

    @pl.when(n_valid > 0)
    def _():
        for r in range(0, ROW_BLOCK, EXPERT_SUB):
            gu = _dot(xs_ref[r:r + EXPERT_SUB].astype(BF16), wgu_b[...]) + bgu_ref[0]
            gate = jnp.minimum(gu[:, :EXPERT_FF], SWIGLU_LIMIT)
            up = jnp.clip(gu[:, EXPERT_FF:], -SWIGLU_LIMIT, SWIGLU_LIMIT)
            hid = (up + 1.0) * (gate * (1.0 / (1.0 + jnp.exp(-SWIGLU_ALPHA * gate))))
            out_ref[r:r + EXPERT_SUB] = _dot(hid.astype(BF16), wd_b[...]) + bd_ref[0]

    @pl.when(n_valid == 0)
    def _():
        out_ref[...] = jnp.zeros_like(out_ref)


def _experts(blk_e, blk_n, xs, wgu, bgu, wd, bd):
    n_rows, d = xs.shape
    rb = ROW_BLOCK
    by_expert = lambda a: pl.BlockSpec((1,) + a.shape[1:], lambda i, be, bn: (be[i], 0, 0))
    return pl.pallas_call(
        _expert_kernel,
        grid_spec=pltpu.PrefetchScalarGridSpec(
            num_scalar_prefetch=2, grid=(n_rows // rb,),
            in_specs=[pl.BlockSpec((rb, d), lambda i, be, bn: (i, 0)),
                      by_expert(wgu), by_expert(bgu), by_expert(wd), by_expert(bd)],
            out_specs=pl.BlockSpec((rb, d), lambda i, be, bn: (i, 0)),
            scratch_shapes=[pltpu.VMEM(wgu.shape[1:], BF16), pltpu.VMEM(wd.shape[1:], BF16)]),
        out_shape=jax.ShapeDtypeStruct((n_rows, d), F32),
        compiler_params=pltpu.CompilerParams(
            dimension_semantics=("arbitrary",), vmem_limit_bytes=VMEM_LIMIT_BYTES),
    )(blk_e, blk_n, xs, wgu, bgu, wd, bd)


def _combine_kernel(dest_ref, gate_ref, h1_ref, fw_ref, ys_hbm, out_ref, buf, sem):
    tm = h1_ref.shape[0]

    def issue(r, carry):
        for k in range(TOP_K):
            d = dest_ref[r * TOP_K + k]
            pltpu.make_async_copy(ys_hbm.at[pl.ds(d, 1)], buf.at[k, pl.ds(r, 1)],
                                  sem).start(priority=k % N_DMA_THREADS)
        return carry

    lax.fori_loop(0, tm, issue, 0)
    for k in range(TOP_K):
        pltpu.make_async_copy(ys_hbm.at[pl.ds(0, tm)], buf.at[k], sem).wait()

    gate = gate_ref[...]
    h2 = h1_ref[...]
    for k in range(TOP_K):
        h2 = h2 + gate[:, k:k + 1] * buf[k]
    out_ref[...] = h2 * lax.rsqrt(jnp.mean(h2 * h2, axis=-1, keepdims=True) + EPS) * fw_ref[...]


def _combine(dest_1d, gates, h1, fw, ys):
    t, d = h1.shape
    tm = MOVE_TILE
    return pl.pallas_call(
        _combine_kernel,
        grid=(t // tm,),
        in_specs=[pl.BlockSpec((tm * TOP_K,), lambda i: (i,), memory_space=pltpu.SMEM),
                  pl.BlockSpec((tm, TOP_K), lambda i: (i, 0)),
                  pl.BlockSpec((tm, d), lambda i: (i, 0)),
                  pl.BlockSpec((1, d), lambda i: (0, 0)),
                  pl.BlockSpec(memory_space=pl.ANY)],
        out_specs=pl.BlockSpec((tm, d), lambda i: (i, 0)),
        out_shape=jax.ShapeDtypeStruct((t, d), F32),
        scratch_shapes=[pltpu.VMEM((TOP_K, tm, d), F32), pltpu.SemaphoreType.DMA(())],
        compiler_params=pltpu.CompilerParams(
            dimension_semantics=("arbitrary",), vmem_limit_bytes=VMEM_LIMIT_BYTES),
    )(dest_1d, gates, h1, fw, ys)


def _block_plan(counts, n_blocks):
    rb = ROW_BLOCK
    pcounts = (counts + rb - 1) // rb * rb
    pends = jnp.cumsum(pcounts)
    pstarts = pends - pcounts
    block_start = jnp.arange(n_blocks, dtype=jnp.int32) * rb
    blk_e = jnp.minimum(jnp.sum(block_start[:, None] >= pends[None, :], axis=1),
                        N_EXPERTS - 1).astype(jnp.int32)
    blk_n = jnp.clip(counts[blk_e] - (block_start - pstarts[blk_e]), 0, rb).astype(jnp.int32)
    n_pad = N_EXPERTS * rb
    padc = pcounts - counts
    cpad = jnp.cumsum(padc)
    p = jnp.arange(n_pad, dtype=jnp.int32)
    e = jnp.minimum(jnp.sum(p[:, None] >= cpad[None, :], axis=1), N_EXPERTS - 1)
    in_expert = pstarts[e] + counts[e] + (p - (cpad[e] - padc[e]))
    at_end = pends[-1] + (p - cpad[-1])
    pad_rows = jnp.where(p < cpad[-1], in_expert, at_end).astype(jnp.int32)
    return pstarts.astype(jnp.int32), blk_e, blk_n, pad_rows


def kernel(x, norm1_w, w_in, ssd_conv_w, ssd_conv_b, dt_bias, a_log, d_skip, ssd_norm_w,
           sc_conv_w, sc_norm_w, w_out, norm2_w, w_router, b_router, w_gate_up, b_gate_up,
           w_down, b_down, final_norm_w):
    assert norm1_w.shape[0] == 1, "single-layer problem"
    batch, seq, d = x.shape
    t = batch * seq
    x2 = x.reshape(t, d)
    row = lambda a: a.reshape(1, -1)

    s0, s1, s2 = D_SSD, D_SSD + D_XBC, D_SSD + D_XBC + N_SSD_HEADS
    w = w_in[0]
    w_main = jnp.concatenate([w[:, :s1], w[:, s2:]], axis=1).astype(BF16)
    w_dt = w[:, s1:s2].astype(BF16)
    z, xbc, gb, gc, u, dt, dtt = _inproj(x2, row(norm1_w[0]), w_main, w_dt, w_dt.T)

    ymix = _mixer(z, xbc, gb, gc, u, dt, dtt, batch, seq,
                  ssd_conv_w[0], row(ssd_conv_b[0]), row(dt_bias[0]), row(a_log[0]),
                  row(d_skip[0]), row(ssd_norm_w[0]), sc_conv_w[0], row(sc_norm_w[0]))

    wr = w_router[0]
    wr_hi = wr.astype(BF16)
    wr_lo = (wr - wr_hi.astype(F32)).astype(BF16)
    h1, xn2, idx, gates, rank, counts = _outproj(
        ymix, x2, w_out[0].astype(BF16), row(norm2_w[0]), wr_hi, wr_lo, row(b_router[0]))

    n_assign = t * TOP_K
    n_rows = n_assign + N_EXPERTS * ROW_BLOCK
    pstarts, blk_e, blk_n, pad_rows = _block_plan(counts[0].astype(jnp.int32),
                                                  n_rows // ROW_BLOCK)
    dest = _dest(pstarts, idx.reshape(n_assign // 128, 128), rank.reshape(n_assign // 128, 128))
    dest_1d = dest.reshape(n_assign)

    xs = _dispatch(jnp.concatenate([dest_1d, pad_rows]), xn2, n_rows)
    ys = _experts(blk_e, blk_n, xs, w_gate_up[0], b_gate_up[0][:, None, :],
                  w_down[0], b_down[0][:, None, :])
    out = _combine(dest_1d, gates, h1, row(final_norm_w), ys)
    return out.reshape(batch, seq, d)
```

```python
import functools

import jax
import jax.numpy as jnp
from jax import lax
from jax.experimental import pallas as pl
from jax.experimental.pallas import tpu as pltpu
from jax.experimental.pallas import tpu_sc as plsc

D_MODEL = 1024
D_SSD = 1024
SSD_HEAD_DIM = 64
N_SSD_HEADS = 16
N_GROUPS = 2
HEADS_PER_GROUP = 8
D_STATE = 128
SSD_CONV = 4
CHUNK = 128
D_XBC = D_SSD + 2 * N_GROUPS * D_STATE
D_SC = 1024
SC_CONV = 3
D_MIX = D_SSD + D_SC
N_EXPERTS = 32
TOP_K = 4
EXPERT_FF = 1024
SWIGLU_LIMIT = 7.0
SWIGLU_ALPHA = 1.702
EPS = 1e-5

VMEM_LIMIT_BYTES = 56 * 1024 * 1024
SUBLANES = 8
ROW_BLOCK = 512
EXPERT_SUB = 256
IN_TILE = 512
OUT_TILE = 256
COMBINE_TILE = 512
SC_WINDOW = 128
SC_ROW_WORDS = 128

F32 = jnp.float32
BF16 = jnp.bfloat16


def _dot(a, b):
    return jnp.dot(a, b, preferred_element_type=F32)


def _split3(v):
    hi = v.astype(BF16)
    r1 = v - hi.astype(F32)
    mid = r1.astype(BF16)
    lo = (r1 - mid.astype(F32)).astype(BF16)
    return hi, mid, lo


def _dot_exact_rhs01(v, m01):
    hi, mid, lo = _split3(v)
    return _dot(hi, m01) + _dot(mid, m01) + _dot(lo, m01)


def _dot_exact_lhs01(m01, v):
    hi, mid, lo = _split3(v)
    return _dot(m01, hi) + _dot(m01, mid) + _dot(m01, lo)


def _silu(v):
    return v * (1.0 / (1.0 + jnp.exp(-v)))


def _softplus(v):
    return jnp.maximum(v, 0.0) + jnp.log1p(jnp.exp(-jnp.abs(v)))


def _shifted(cur, tail, s):
    if s == 0:
        return cur
    rc = pltpu.roll(cur, s, axis=0)
    rt = pltpu.roll(tail, s, axis=0)
    row = lax.broadcasted_iota(jnp.int32, tail.shape, 0)
    first = jnp.where(row < s, rt, rc[0:SUBLANES])
    return jnp.concatenate([first, rc[SUBLANES:]], axis=0)


def _inproj_kernel(x_ref, nw_ref, w_ref, wdt_ref, wdtt_ref,
                   z_ref, xbc_ref, gb_ref, gc_ref, u_ref, dt_ref, dtt_ref):
    x = x_ref[...]
    xn = x * lax.rsqrt(jnp.mean(x * x, axis=-1, keepdims=True) + EPS) * nw_ref[...]
    xb = xn.astype(BF16)
    o = 0
    for ref in (z_ref, xbc_ref, gb_ref, gc_ref, u_ref):
        n = ref.shape[-1]
        ref[...] = _dot(xb, w_ref[:, o:o + n]).astype(ref.dtype)
        o += n
    dt_ref[...] = _dot(xb, wdt_ref[...])
    dtt_ref[...] = lax.dot_general(wdtt_ref[...], xb, (((1,), (1,)), ((), ())),
                                   preferred_element_type=F32)


def _inproj(x2, nw, w_main, w_dt, w_dtt):
    t = x2.shape[0]
    tm = IN_TILE
    row = lambda n: pl.BlockSpec((tm, n), lambda i: (i, 0))
    full = lambda a: pl.BlockSpec(a.shape, lambda i: (0,) * a.ndim)
    return pl.pallas_call(
        _inproj_kernel,
        grid=(t // tm,),
        in_specs=[row(D_MODEL), full(nw), full(w_main), full(w_dt), full(w_dtt)],
        out_specs=[row(D_SSD), row(D_XBC), row(D_SC), row(D_SC), row(D_SC),
                   row(N_SSD_HEADS), pl.BlockSpec((N_SSD_HEADS, tm), lambda i: (0, i))],
        out_shape=[jax.ShapeDtypeStruct((t, D_SSD), BF16),
                   jax.ShapeDtypeStruct((t, D_XBC), BF16),
                   jax.ShapeDtypeStruct((t, D_SC), BF16),
                   jax.ShapeDtypeStruct((t, D_SC), BF16),
                   jax.ShapeDtypeStruct((t, D_SC), BF16),
                   jax.ShapeDtypeStruct((t, N_SSD_HEADS), F32),
                   jax.ShapeDtypeStruct((N_SSD_HEADS, t), F32)],
        compiler_params=pltpu.CompilerParams(
            dimension_semantics=("arbitrary",), vmem_limit_bytes=VMEM_LIMIT_BYTES),
    )(x2, nw, w_main, w_dt, w_dtt)


def _mixer_kernel(z_ref, xbc_ref, gb_ref, gc_ref, u_ref, dt_ref, dtt_ref,
                  cw_ref, cb_ref, dtb_ref, dtbt_ref, alog_ref, alogt_ref, dskip_ref,
                  nw_ref, scw_ref, scnw_ref, out_ref,
                  state_ref, tail_ref, sctail_ref):
    q = CHUNK

    @pl.when(pl.program_id(1) == 0)
    def _():
        state_ref[...] = jnp.zeros_like(state_ref)
        tail_ref[...] = jnp.zeros_like(tail_ref)
        sctail_ref[...] = jnp.zeros_like(sctail_ref)

    raw = xbc_ref[...].astype(F32)
    tail = tail_ref[...]
    conv = cb_ref[...]
    for k in range(SSD_CONV):
        conv = conv + cw_ref[k:k + 1, :] * _shifted(raw, tail, SSD_CONV - 1 - k)
    tail_ref[...] = raw[q - SUBLANES:q]
    act = _silu(conv)
    xs = act[:, :D_SSD]
    n_bc = N_GROUPS * D_STATE
    b_all = act[:, D_SSD:D_SSD + n_bc].astype(BF16)
    c_all = act[:, D_SSD + n_bc:].astype(BF16)

    dt = _softplus(dt_ref[...] + dtb_ref[...])
    dtt = _softplus(dtt_ref[...] + dtbt_ref[...])
    adt = dt * (-jnp.exp(alog_ref[...]))
    adtt = dtt * (-jnp.exp(alogt_ref[...]))
    ri = lax.broadcasted_iota(jnp.int32, (q, q), 0)
    ci = lax.broadcasted_iota(jnp.int32, (q, q), 1)
    causal = ri >= ci
    tri = jnp.where(causal, 1.0, 0.0).astype(BF16)
    trit = jnp.where(ri <= ci, 1.0, 0.0).astype(BF16)
    acum = _dot_exact_lhs01(tri, adt)
    acumt = _dot_exact_rhs01(adtt, trit)
    a_last = acum[q - 1:q, :]

    hh = lax.broadcasted_iota(jnp.int32, (N_SSD_HEADS, D_SSD), 0)
    hl = lax.broadcasted_iota(jnp.int32, (N_SSD_HEADS, D_SSD), 1) // SSD_HEAD_DIM
    expand = jnp.where(hh == hl, 1.0, 0.0).astype(BF16)

    dt_x = _dot_exact_rhs01(dt, expand)
    xdt = xs * dt_x
    xdt_b = xdt.astype(BF16)
    dec_in = _dot_exact_rhs01(jnp.exp(acum), expand)
    dec_out = _dot_exact_rhs01(jnp.exp(a_last - acum), expand)
    dec_all = _dot_exact_rhs01(jnp.exp(a_last), expand)
    xw_b = (xdt * dec_out).astype(BF16)

    lane_lo = lax.broadcasted_iota(jnp.int32, (q, 2 * SSD_HEAD_DIM), 1) < SSD_HEAD_DIM
    gw = HEADS_PER_GROUP * SSD_HEAD_DIM
    y_parts = []
    for g in range(N_GROUPS):
        b_g = b_all[:, g * D_STATE:(g + 1) * D_STATE]
        c_g = c_all[:, g * D_STATE:(g + 1) * D_STATE]
        cb = lax.dot_general(c_g, b_g, (((1,), (1,)), ((), ())), preferred_element_type=F32)
        st = state_ref[:, g * gw:(g + 1) * gw]
        y_inter = _dot(c_g, st.astype(BF16)) * dec_in[:, g * gw:(g + 1) * gw]
        new_st = lax.dot_general(b_g, xw_b[:, g * gw:(g + 1) * gw], (((0,), (0,)), ((), ())),
                                 preferred_element_type=F32)
        state_ref[:, g * gw:(g + 1) * gw] = st * dec_all[:, g * gw:(g + 1) * gw] + new_st
        pair_parts = []
        for pr in range(HEADS_PER_GROUP // 2):
            h0 = g * HEADS_PER_GROUP + 2 * pr
            lo = h0 * SSD_HEAD_DIM
            rhs = xdt_b[:, lo:lo + 2 * SSD_HEAD_DIM]
            ys = []
            for h in (h0, h0 + 1):
                seg = acum[:, h:h + 1] - acumt[h:h + 1, :]
                m = (cb * jnp.exp(jnp.where(causal, seg, -jnp.inf))).astype(BF16)
                ys.append(_dot(m, rhs))
            pair_parts.append(jnp.where(lane_lo, ys[0], ys[1]))
        y_parts.append(jnp.concatenate(pair_parts, axis=-1) + y_inter)
    y = jnp.concatenate(y_parts, axis=-1) + _dot_exact_rhs01(dskip_ref[...], expand) * xs

    gated = y * _silu(z_ref[...].astype(F32))
    outs = []
    for g in range(N_GROUPS):
        gg = gated[:, g * gw:(g + 1) * gw]
        outs.append(gg * lax.rsqrt(jnp.mean(gg * gg, axis=-1, keepdims=True) + EPS))
    y_ssd = jnp.concatenate(outs, axis=-1) * nw_ref[...]
    out_ref[:, :D_SSD] = y_ssd.astype(out_ref.dtype)

    cu = gc_ref[...].astype(F32) * u_ref[...].astype(F32)
    sctail = sctail_ref[...]
    v = jnp.zeros_like(cu)
    for k in range(SC_CONV):
        v = v + scw_ref[k:k + 1, :] * _shifted(cu, sctail, SC_CONV - 1 - k)
    sctail_ref[...] = cu[q - SUBLANES:q]
    gv = gb_ref[...].astype(F32) * v
    y_sc = gv * lax.rsqrt(jnp.mean(gv * gv, axis=-1, keepdims=True) + EPS) * scnw_ref[...]
    out_ref[:, D_SSD:] = y_sc.astype(out_ref.dtype)


def _mixer(z, xbc, gb, gc, u, dt, dtt, batch, seq, cw, cb, dtb, alog, dskip, nw, scw, scnw):
    q = CHUNK
    nc = seq // q
    t = batch * seq
    row = lambda n: pl.BlockSpec((q, n), lambda b, c: (b * nc + c, 0))
    full = lambda a: pl.BlockSpec(a.shape, lambda b, c: (0,) * a.ndim)
    dtbt, alogt = dtb.reshape(N_SSD_HEADS, 1), alog.reshape(N_SSD_HEADS, 1)
    params = (cw, cb, dtb, dtbt, alog, alogt, dskip, nw, scw, scnw)
    return pl.pallas_call(
        _mixer_kernel,
        grid=(batch, nc),
        in_specs=[row(D_SSD), row(D_XBC), row(D_SC), row(D_SC), row(D_SC), row(N_SSD_HEADS),
                  pl.BlockSpec((N_SSD_HEADS, q), lambda b, c: (0, b * nc + c))]
                 + [full(a) for a in params],
        out_specs=row(D_MIX),
        out_shape=jax.ShapeDtypeStruct((t, D_MIX), BF16),
        scratch_shapes=[pltpu.VMEM((D_STATE, D_SSD), F32),
                        pltpu.VMEM((SUBLANES, D_XBC), F32),
                        pltpu.VMEM((SUBLANES, D_SC), F32)],
        compiler_params=pltpu.CompilerParams(
            dimension_semantics=("arbitrary", "arbitrary"), vmem_limit_bytes=VMEM_LIMIT_BYTES),
    )(z, xbc, gb, gc, u, dt, dtt, *params)


def _outproj_kernel(ymix_ref, x_ref, wout_ref, nw_ref, wrh_ref, wrl_ref, br_ref,
                    h1_ref, xn_ref, idx_ref, gate_ref, rank_ref, cnt_ref, run_ref):
    tm = x_ref.shape[0]

    @pl.when(pl.program_id(0) == 0)
    def _():
        run_ref[...] = jnp.zeros_like(run_ref)

    h1 = x_ref[...] + _dot(ymix_ref[...], wout_ref[...])
    h1_ref[...] = h1
    xn = h1 * lax.rsqrt(jnp.mean(h1 * h1, axis=-1, keepdims=True) + EPS) * nw_ref[...]
    xn_ref[...] = xn

    xh = xn.astype(BF16)
    xl = (xn - xh.astype(F32)).astype(BF16)
    logits = (_dot(xh, wrh_ref[...]) + _dot(xh, wrl_ref[...]) + _dot(xl, wrh_ref[...])
              + br_ref[...])

    lane = lax.broadcasted_iota(jnp.int32, (tm, N_EXPERTS), 1).astype(F32)
    work = logits
    vals, sels = [], []
    for _ in range(TOP_K):
        m = jnp.max(work, axis=-1, keepdims=True)
        first = jnp.min(jnp.where(work == m, lane, float(N_EXPERTS)), axis=-1, keepdims=True)
        sel = lane == first
        vals.append(m)
        sels.append(sel)
        work = jnp.where(sel, -jnp.inf, work)
    exps = [jnp.exp(v - vals[0]) for v in vals]
    denom = exps[0] + exps[1] + exps[2] + exps[3]

    onehot = jnp.zeros((tm, N_EXPERTS), F32)
    for sel in sels:
        onehot = onehot + jnp.where(sel, 1.0, 0.0)
    ri = lax.broadcasted_iota(jnp.int32, (tm, tm), 0)
    ci = lax.broadcasted_iota(jnp.int32, (tm, tm), 1)
    strict = jnp.where(ri > ci, 1.0, 0.0).astype(BF16)
    before = run_ref[...] + _dot(strict, onehot.astype(BF16))
    run_ref[...] = run_ref[...] + jnp.sum(onehot, axis=0, keepdims=True)
    cnt_ref[...] = run_ref[...]

    slot = lax.broadcasted_iota(jnp.int32, (tm, TOP_K), 1)
    idx = jnp.zeros((tm, TOP_K), F32)
    gate = jnp.zeros((tm, TOP_K), F32)
    rank = jnp.zeros((tm, TOP_K), F32)
    for k in range(TOP_K):
        e_k = jnp.sum(jnp.where(sels[k], lane, 0.0), axis=-1, keepdims=True)
        r_k = jnp.sum(jnp.where(sels[k], before, 0.0), axis=-1, keepdims=True)
        idx = jnp.where(slot == k, e_k, idx)
        gate = jnp.where(slot == k, exps[k] / denom, gate)
        rank = jnp.where(slot == k, r_k, rank)
    idx_ref[...] = idx.astype(jnp.int32)
    gate_ref[...] = gate
    rank_ref[...] = rank.astype(jnp.int32)


def _outproj(ymix, x2, w_out, nw, wr_hi, wr_lo, br):
    t = x2.shape[0]
    tm = OUT_TILE
    row = lambda n: pl.BlockSpec((tm, n), lambda i: (i, 0))
    full = lambda a: pl.BlockSpec(a.shape, lambda i: (0,) * a.ndim)
    return pl.pallas_call(
        _outproj_kernel,
        grid=(t // tm,),
        in_specs=[row(D_MIX), row(D_MODEL), full(w_out), full(nw), full(wr_hi), full(wr_lo),
                  full(br)],
        out_specs=[row(D_MODEL), row(D_MODEL), row(TOP_K), row(TOP_K), row(TOP_K),
                   pl.BlockSpec((1, N_EXPERTS), lambda i: (0, 0))],
        out_shape=[jax.ShapeDtypeStruct((t, D_MODEL), F32),
                   jax.ShapeDtypeStruct((t, D_MODEL), F32),
                   jax.ShapeDtypeStruct((t, TOP_K), jnp.int32),
                   jax.ShapeDtypeStruct((t, TOP_K), F32),
                   jax.ShapeDtypeStruct((t, TOP_K), jnp.int32),
                   jax.ShapeDtypeStruct((1, N_EXPERTS), F32)],
        scratch_shapes=[pltpu.VMEM((1, N_EXPERTS), F32)],
        compiler_params=pltpu.CompilerParams(
            dimension_semantics=("arbitrary",), vmem_limit_bytes=VMEM_LIMIT_BYTES),
    )(ymix, x2, w_out, nw, wr_hi, wr_lo, br)


def _dest_kernel(pstart_ref, idx_ref, rank_ref, dest_ref):
    idx = idx_ref[...]
    base = jnp.zeros_like(idx)
    for e in range(N_EXPERTS):
        base = jnp.where(idx == e, pstart_ref[e], base)
    dest_ref[...] = base + rank_ref[...]


def _dest(pstarts, idx_flat, rank_flat):
    full = lambda a: pl.BlockSpec(a.shape, lambda i, ps: (0,) * a.ndim)
    return pl.pallas_call(
        _dest_kernel,
        grid_spec=pltpu.PrefetchScalarGridSpec(
            num_scalar_prefetch=1, grid=(1,),
            in_specs=[full(idx_flat), full(rank_flat)], out_specs=full(idx_flat)),
        out_shape=jax.ShapeDtypeStruct(idx_flat.shape, jnp.int32),
    )(pstarts, idx_flat, rank_flat)


def _invert_permutation(perm):
    n = perm.shape[0]
    win = SC_WINDOW
    vals = jnp.broadcast_to(jnp.arange(n, dtype=jnp.int32)[:, None], (n, SC_ROW_WORDS))
    mesh = plsc.VectorSubcoreMesh(core_axis_name="core", subcore_axis_name="subcore")

    @functools.partial(pl.kernel, out_type=jax.ShapeDtypeStruct((n, SC_ROW_WORDS), jnp.int32),
                       mesh=mesh, scratch_types=[])
    def scatter_rows(vals_hbm, idx_hbm, out_hbm):
        def body(vals_vmem, idx_vmem):
            pltpu.sync_copy(vals_vmem, out_hbm.at[idx_vmem.at[0]])

        pltpu.emit_pipeline(
            body,
            grid=(n // win,),
            in_specs=[pl.BlockSpec((win, SC_ROW_WORDS), lambda i: (i, 0)),
                      pl.BlockSpec((1, win), lambda i: (0, i))],
            out_specs=[],
            core_axis_name="subcore",
            dimension_semantics=(pltpu.PARALLEL,),
        )(vals_hbm, idx_hbm)

    return scatter_rows(vals, perm.reshape(1, n))[:, 0]


def _expert_kernel(n_blocks, blk_e_ref, tok_ref, dst_ref, x_hbm, wgu_ref, bgu_ref, wd_ref, bd_ref,
                   y_hbm, xbuf, ybuf, wgu_b, wd_b, gsem, ssem):
    s = pl.program_id(0)
    rb = ROW_BLOCK

    def gather_row(slot, r, tok):
        return pltpu.make_async_copy(x_hbm.at[pl.ds(tok, 1)], xbuf.at[slot, pl.ds(r, 1)],
                                     gsem.at[slot])

    def scatter_row(slot, r, dst):
        return pltpu.make_async_copy(ybuf.at[slot, pl.ds(r, 1)], y_hbm.at[pl.ds(dst, 1)],
                                     ssem.at[slot])

    def wait_gather(slot):
        pltpu.make_async_copy(x_hbm.at[pl.ds(0, rb)], xbuf.at[slot], gsem.at[slot]).wait()

    def wait_scatter(slot):
        pltpu.make_async_copy(ybuf.at[slot], y_hbm.at[pl.ds(0, rb)], ssem.at[slot]).wait()

    @pl.when(s < n_blocks)
    def _():
        slot = s % 2

        def issue(r, carry):
            gather_row(slot, r, tok_ref[r]).start(priority=0)
            return carry

        lax.fori_loop(0, rb, issue, 0)

    @pl.when(s >= 1)
    def _():
        j = s - 1
        slot = j % 2

        @pl.when((j == 0) | (blk_e_ref[j] != blk_e_ref[jnp.maximum(j - 1, 0)]))
        def _():
            wgu_b[...] = wgu_ref[0].astype(BF16)
            wd_b[...] = wd_ref[0].astype(BF16)

        wait_gather(slot)

        @pl.when(j >= 2)
        def _():
            wait_scatter(slot)

        for r in range(0, rb, EXPERT_SUB):
            gu = _dot(xbuf[slot, r:r + EXPERT_SUB].astype(BF16), wgu_b[...]) + bgu_ref[0]
            gate = jnp.minimum(gu[:, :EXPERT_FF], SWIGLU_LIMIT)
            up = jnp.clip(gu[:, EXPERT_FF:], -SWIGLU_LIMIT, SWIGLU_LIMIT)
            hid = (up + 1.0) * (gate * (1.0 / (1.0 + jnp.exp(-SWIGLU_ALPHA * gate))))
            ybuf[slot, r:r + EXPERT_SUB] = _dot(hid.astype(BF16), wd_b[...]) + bd_ref[0]

        def issue(r, carry):
            scatter_row(slot, r, dst_ref[r]).start(priority=1)
            return carry

        lax.fori_loop(0, rb, issue, 0)

    @pl.when(s == n_blocks)
    def _():
        wait_scatter((n_blocks - 1) % 2)
        if n_blocks >= 2:
            wait_scatter(n_blocks % 2)


def _experts(blk_e, row_tok, row_dst, xn, wgu, bgu, wd, bd):
    n_rows = row_tok.shape[0]
    d = xn.shape[1]
    rb = ROW_BLOCK
    nb = n_rows // rb
    by_expert = lambda a: pl.BlockSpec((1,) + a.shape[1:],
                                       lambda s, be: (be[jnp.maximum(s - 1, 0)], 0, 0))
    return pl.pallas_call(
        functools.partial(_expert_kernel, nb),
        grid_spec=pltpu.PrefetchScalarGridSpec(
            num_scalar_prefetch=1, grid=(nb + 1,),
            in_specs=[pl.BlockSpec((rb,), lambda s, be: (jnp.minimum(s, nb - 1),),
                                   memory_space=pltpu.SMEM),
                      pl.BlockSpec((rb,), lambda s, be: (jnp.maximum(s - 1, 0),),
                                   memory_space=pltpu.SMEM),
                      pl.BlockSpec(memory_space=pl.ANY),
                      by_expert(wgu), by_expert(bgu), by_expert(wd), by_expert(bd)],
            out_specs=pl.BlockSpec(memory_space=pl.ANY),
            scratch_shapes=[pltpu.VMEM((2, rb, d), F32), pltpu.VMEM((2, rb, d), F32),
                            pltpu.VMEM(wgu.shape[1:], BF16), pltpu.VMEM(wd.shape[1:], BF16),
                            pltpu.SemaphoreType.DMA((2,)), pltpu.SemaphoreType.DMA((2,))]),
        out_shape=jax.ShapeDtypeStruct((n_rows, d), F32),
        compiler_params=pltpu.CompilerParams(
            dimension_semantics=("arbitrary",), vmem_limit_bytes=VMEM_LIMIT_BYTES),
    )(blk_e, row_tok, row_dst, xn, wgu, bgu, wd, bd)


def _combine_kernel(gate_ref, h1_ref, fw_ref, y0_ref, y1_ref, y2_ref, y3_ref, out_ref):
    gate = gate_ref[...]
    h2 = h1_ref[...]
    for k, y_ref in enumerate((y0_ref, y1_ref, y2_ref, y3_ref)):
        h2 = h2 + gate[:, k:k + 1] * y_ref[...]
    out_ref[...] = h2 * lax.rsqrt(jnp.mean(h2 * h2, axis=-1, keepdims=True) + EPS) * fw_ref[...]


def _combine(gates, h1, fw, y):
    t, d = h1.shape
    tm = COMBINE_TILE
    nt = t // tm
    slot = lambda k: pl.BlockSpec((tm, d), lambda i: (k * nt + i, 0))
    return pl.pallas_call(
        _combine_kernel,
        grid=(nt,),
        in_specs=[pl.BlockSpec((tm, TOP_K), lambda i: (i, 0)),
                  pl.BlockSpec((tm, d), lambda i: (i, 0)),
                  pl.BlockSpec((1, d), lambda i: (0, 0))] + [slot(k) for k in range(TOP_K)],
        out_specs=pl.BlockSpec((tm, d), lambda i: (i, 0)),
        out_shape=jax.ShapeDtypeStruct((t, d), F32),
        compiler_params=pltpu.CompilerParams(
            dimension_semantics=("arbitrary",), vmem_limit_bytes=VMEM_LIMIT_BYTES),
    )(gates, h1, fw, y, y, y, y)


def _block_plan(counts, n_blocks):
    rb = ROW_BLOCK
    pcounts = (counts + rb - 1) // rb * rb
    pends = jnp.cumsum(pcounts)
    pstarts = pends - pcounts
    block_start = jnp.arange(n_blocks, dtype=jnp.int32) * rb
    blk_e = jnp.minimum(jnp.sum(block_start[:, None] >= pends[None, :], axis=1),
                        N_EXPERTS - 1).astype(jnp.int32)
    n_pad = N_EXPERTS * rb
    padc = pcounts - counts
    cpad = jnp.cumsum(padc)
    p = jnp.arange(n_pad, dtype=jnp.int32)
    e = jnp.minimum(jnp.sum(p[:, None] >= cpad[None, :], axis=1), N_EXPERTS - 1)
    in_expert = pstarts[e] + counts[e] + (p - (cpad[e] - padc[e]))
    at_end = pends[-1] + (p - cpad[-1])
    pad_rows = jnp.where(p < cpad[-1], in_expert, at_end).astype(jnp.int32)
    return pstarts.astype(jnp.int32), blk_e, pad_rows


def kernel(x, norm1_w, w_in, ssd_conv_w, ssd_conv_b, dt_bias, a_log, d_skip, ssd_norm_w,
           sc_conv_w, sc_norm_w, w_out, norm2_w, w_router, b_router, w_gate_up, b_gate_up,
           w_down, b_down, final_norm_w):
    assert norm1_w.shape[0] == 1, "single-layer problem"
    batch, seq, d = x.shape
    t = batch * seq
    x2 = x.reshape(t, d)
    row = lambda a: a.reshape(1, -1)

    s0, s1, s2 = D_SSD, D_SSD + D_XBC, D_SSD + D_XBC + N_SSD_HEADS
    w = w_in[0]
    w_main = jnp.concatenate([w[:, :s1], w[:, s2:]], axis=1).astype(BF16)
    w_dt = w[:, s1:s2].astype(BF16)
    z, xbc, gb, gc, u, dt, dtt = _inproj(x2, row(norm1_w[0]), w_main, w_dt, w_dt.T)

    ymix = _mixer(z, xbc, gb, gc, u, dt, dtt, batch, seq,
                  ssd_conv_w[0], row(ssd_conv_b[0]), row(dt_bias[0]), row(a_log[0]),
                  row(d_skip[0]), row(ssd_norm_w[0]), sc_conv_w[0], row(sc_norm_w[0]))

    wr = w_router[0]
    wr_hi = wr.astype(BF16)
    wr_lo = (wr - wr_hi.astype(F32)).astype(BF16)
    h1, xn2, idx, gates, rank, counts = _outproj(
        ymix, x2, w_out[0].astype(BF16), row(norm2_w[0]), wr_hi, wr_lo, row(b_router[0]))

    n_assign = t * TOP_K
    n_rows = n_assign + N_EXPERTS * ROW_BLOCK
    pstarts, blk_e, pad_rows = _block_plan(counts[0].astype(jnp.int32), n_rows // ROW_BLOCK)
    dest = _dest(pstarts, idx.reshape(n_assign // 128, 128), rank.reshape(n_assign // 128, 128))

    dest_slot_major = dest.reshape(t, TOP_K).T.reshape(n_assign)
    src = _invert_permutation(jnp.concatenate([dest_slot_major, pad_rows]))
    row_tok = jnp.where(src < n_assign, src % t, 0)
    y = _experts(blk_e, row_tok, src, xn2, w_gate_up[0], b_gate_up[0][:, None, :],
                 w_down[0], b_down[0][:, None, :])
    out = _combine(gates, h1, row(final_norm_w), y)
    return out.reshape(batch, seq, d)
```

```python
import functools

import jax
import jax.numpy as jnp
from jax import lax
from jax.experimental import pallas as pl
from jax.experimental.pallas import tpu as pltpu
from jax.experimental.pallas import tpu_sc as plsc

D_MODEL = 1024
D_SSD = 1024
SSD_HEAD_DIM = 64
N_SSD_HEADS = 16
N_GROUPS = 2
HEADS_PER_GROUP = 8
D_STATE = 128
SSD_CONV = 4
CHUNK = 128
D_XBC = D_SSD + 2 * N_GROUPS * D_STATE
D_SC = 1024
SC_CONV = 3
D_MIX = D_SSD + D_SC
N_EXPERTS = 32
TOP_K = 4
EXPERT_FF = 1024
SWIGLU_LIMIT = 7.0
SWIGLU_ALPHA = 1.702
EPS = 1e-5

VMEM_LIMIT_BYTES = 56 * 1024 * 1024
SUBLANES = 8
ROW_BLOCK = 512
EXPERT_SUB = 256
IN_TILE = 512
OUT_TILE = 256
COMBINE_TILE = 512
LANES = 128
ROW_PARTS = D_MODEL // LANES
SC_WINDOW = 128

F32 = jnp.float32
BF16 = jnp.bfloat16


def _dot(a, b):
    return jnp.dot(a, b, preferred_element_type=F32)


def _split3(v):
    hi = v.astype(BF16)
    r1 = v - hi.astype(F32)
    mid = r1.astype(BF16)
    lo = (r1 - mid.astype(F32)).astype(BF16)
    return hi, mid, lo


def _dot_exact_rhs01(v, m01):
    hi, mid, lo = _split3(v)
    return _dot(hi, m01) + _dot(mid, m01) + _dot(lo, m01)


def _dot_exact_lhs01(m01, v):
    hi, mid, lo = _split3(v)
    return _dot(m01, hi) + _dot(m01, mid) + _dot(m01, lo)


def _silu(v):
    return v * (1.0 / (1.0 + jnp.exp(-v)))


def _softplus(v):
    return jnp.maximum(v, 0.0) + jnp.log1p(jnp.exp(-jnp.abs(v)))


def _load_token_rows(ref):
    rows = ref.shape[0] // ROW_PARTS
    return jnp.concatenate([ref[pl.ds(c, rows, stride=ROW_PARTS), :] for c in range(ROW_PARTS)],
                           axis=-1)


def _store_token_rows(ref, v):
    rows = v.shape[0]
    for c in range(ROW_PARTS):
        ref[pl.ds(c, rows, stride=ROW_PARTS), :] = v[:, c * LANES:(c + 1) * LANES]


def _shifted(cur, tail, s):
    if s == 0:
        return cur
    rc = pltpu.roll(cur, s, axis=0)
    rt = pltpu.roll(tail, s, axis=0)
    row = lax.broadcasted_iota(jnp.int32, tail.shape, 0)
    first = jnp.where(row < s, rt, rc[0:SUBLANES])
    return jnp.concatenate([first, rc[SUBLANES:]], axis=0)


def _inproj_kernel(x_ref, nw_ref, w_ref, wdt_ref, wdtt_ref,
                   z_ref, xbc_ref, gb_ref, gc_ref, u_ref, dt_ref, dtt_ref):
    x = x_ref[...]
    xn = x * lax.rsqrt(jnp.mean(x * x, axis=-1, keepdims=True) + EPS) * nw_ref[...]
    xb = xn.astype(BF16)
    o = 0
    for ref in (z_ref, xbc_ref, gb_ref, gc_ref, u_ref):
        n = ref.shape[-1]
        ref[...] = _dot(xb, w_ref[:, o:o + n]).astype(ref.dtype)
        o += n
    dt_ref[...] = _dot(xb, wdt_ref[...])
    dtt_ref[...] = lax.dot_general(wdtt_ref[...], xb, (((1,), (1,)), ((), ())),
                                   preferred_element_type=F32)


def _inproj(x2, nw, w_main, w_dt, w_dtt):
    t = x2.shape[0]
    tm = IN_TILE
    row = lambda n: pl.BlockSpec((tm, n), lambda i: (i, 0))
    full = lambda a: pl.BlockSpec(a.shape, lambda i: (0,) * a.ndim)
    return pl.pallas_call(
        _inproj_kernel,
        grid=(t // tm,),
        in_specs=[row(D_MODEL), full(nw), full(w_main), full(w_dt), full(w_dtt)],
        out_specs=[row(D_SSD), row(D_XBC), row(D_SC), row(D_SC), row(D_SC),
                   row(N_SSD_HEADS), pl.BlockSpec((N_SSD_HEADS, tm), lambda i: (0, i))],
        out_shape=[jax.ShapeDtypeStruct((t, D_SSD), BF16),
                   jax.ShapeDtypeStruct((t, D_XBC), BF16),
                   jax.ShapeDtypeStruct((t, D_SC), BF16),
                   jax.ShapeDtypeStruct((t, D_SC), BF16),
                   jax.ShapeDtypeStruct((t, D_SC), BF16),
                   jax.ShapeDtypeStruct((t, N_SSD_HEADS), F32),
                   jax.ShapeDtypeStruct((N_SSD_HEADS, t), F32)],
        compiler_params=pltpu.CompilerParams(
            dimension_semantics=("arbitrary",), vmem_limit_bytes=VMEM_LIMIT_BYTES),
    )(x2, nw, w_main, w_dt, w_dtt)


def _mixer_kernel(z_ref, xbc_ref, gb_ref, gc_ref, u_ref, dt_ref, dtt_ref,
                  cw_ref, cb_ref, dtb_ref, dtbt_ref, alog_ref, alogt_ref, dskip_ref,
                  nw_ref, scw_ref, scnw_ref, out_ref,
                  state_ref, tail_ref, sctail_ref):
    q = CHUNK

    @pl.when(pl.program_id(1) == 0)
    def _():
        state_ref[...] = jnp.zeros_like(state_ref)
        tail_ref[...] = jnp.zeros_like(tail_ref)
        sctail_ref[...] = jnp.zeros_like(sctail_ref)

    raw = xbc_ref[...].astype(F32)
    tail = tail_ref[...]
    conv = cb_ref[...]
    for k in range(SSD_CONV):
        conv = conv + cw_ref[k:k + 1, :] * _shifted(raw, tail, SSD_CONV - 1 - k)
    tail_ref[...] = raw[q - SUBLANES:q]
    act = _silu(conv)
    xs = act[:, :D_SSD]
    n_bc = N_GROUPS * D_STATE
    b_all = act[:, D_SSD:D_SSD + n_bc].astype(BF16)
    c_all = act[:, D_SSD + n_bc:].astype(BF16)

    dt = _softplus(dt_ref[...] + dtb_ref[...])
    dtt = _softplus(dtt_ref[...] + dtbt_ref[...])
    adt = dt * (-jnp.exp(alog_ref[...]))
    adtt = dtt * (-jnp.exp(alogt_ref[...]))
    ri = lax.broadcasted_iota(jnp.int32, (q, q), 0)
    ci = lax.broadcasted_iota(jnp.int32, (q, q), 1)
    causal = ri >= ci
    tri = jnp.where(causal, 1.0, 0.0).astype(BF16)
    trit = jnp.where(ri <= ci, 1.0, 0.0).astype(BF16)
    acum = _dot_exact_lhs01(tri, adt)
    acumt = _dot_exact_rhs01(adtt, trit)
    a_last = acum[q - 1:q, :]

    hh = lax.broadcasted_iota(jnp.int32, (N_SSD_HEADS, D_SSD), 0)
    hl = lax.broadcasted_iota(jnp.int32, (N_SSD_HEADS, D_SSD), 1) // SSD_HEAD_DIM
    expand = jnp.where(hh == hl, 1.0, 0.0).astype(BF16)

    dt_x = _dot_exact_rhs01(dt, expand)
    xdt = xs * dt_x
    xdt_b = xdt.astype(BF16)
    dec_in = _dot_exact_rhs01(jnp.exp(acum), expand)
    dec_out = _dot_exact_rhs01(jnp.exp(a_last - acum), expand)
    dec_all = _dot_exact_rhs01(jnp.exp(a_last), expand)
    xw_b = (xdt * dec_out).astype(BF16)

    lane_lo = lax.broadcasted_iota(jnp.int32, (q, 2 * SSD_HEAD_DIM), 1) < SSD_HEAD_DIM
    gw = HEADS_PER_GROUP * SSD_HEAD_DIM
    y_parts = []
    for g in range(N_GROUPS):
        b_g = b_all[:, g * D_STATE:(g + 1) * D_STATE]
        c_g = c_all[:, g * D_STATE:(g + 1) * D_STATE]
        cb = lax.dot_general(c_g, b_g, (((1,), (1,)), ((), ())), preferred_element_type=F32)
        st = state_ref[:, g * gw:(g + 1) * gw]
        y_inter = _dot(c_g, st.astype(BF16)) * dec_in[:, g * gw:(g + 1) * gw]
        new_st = lax.dot_general(b_g, xw_b[:, g * gw:(g + 1) * gw], (((0,), (0,)), ((), ())),
                                 preferred_element_type=F32)
        state_ref[:, g * gw:(g + 1) * gw] = st * dec_all[:, g * gw:(g + 1) * gw] + new_st
        pair_parts = []
        for pr in range(HEADS_PER_GROUP // 2):
            h0 = g * HEADS_PER_GROUP + 2 * pr
            lo = h0 * SSD_HEAD_DIM
            rhs = xdt_b[:, lo:lo + 2 * SSD_HEAD_DIM]
            ys = []
            for h in (h0, h0 + 1):
                seg = acum[:, h:h + 1] - acumt[h:h + 1, :]
                m = (cb * jnp.exp(jnp.where(causal, seg, -jnp.inf))).astype(BF16)
                ys.append(_dot(m, rhs))
            pair_parts.append(jnp.where(lane_lo, ys[0], ys[1]))
        y_parts.append(jnp.concatenate(pair_parts, axis=-1) + y_inter)
    y = jnp.concatenate(y_parts, axis=-1) + _dot_exact_rhs01(dskip_ref[...], expand) * xs

    gated = y * _silu(z_ref[...].astype(F32))
    outs = []
    for g in range(N_GROUPS):
        gg = gated[:, g * gw:(g + 1) * gw]
        outs.append(gg * lax.rsqrt(jnp.mean(gg * gg, axis=-1, keepdims=True) + EPS))
    y_ssd = jnp.concatenate(outs, axis=-1) * nw_ref[...]
    out_ref[:, :D_SSD] = y_ssd.astype(out_ref.dtype)

    cu = gc_ref[...].astype(F32) * u_ref[...].astype(F32)
    sctail = sctail_ref[...]
    v = jnp.zeros_like(cu)
    for k in range(SC_CONV):
        v = v + scw_ref[k:k + 1, :] * _shifted(cu, sctail, SC_CONV - 1 - k)
    sctail_ref[...] = cu[q - SUBLANES:q]
    gv = gb_ref[...].astype(F32) * v
    y_sc = gv * lax.rsqrt(jnp.mean(gv * gv, axis=-1, keepdims=True) + EPS) * scnw_ref[...]
    out_ref[:, D_SSD:] = y_sc.astype(out_ref.dtype)


def _mixer(z, xbc, gb, gc, u, dt, dtt, batch, seq, cw, cb, dtb, alog, dskip, nw, scw, scnw):
    q = CHUNK
    nc = seq // q
    t = batch * seq
    row = lambda n: pl.BlockSpec((q, n), lambda b, c: (b * nc + c, 0))
    full = lambda a: pl.BlockSpec(a.shape, lambda b, c: (0,) * a.ndim)
    dtbt, alogt = dtb.reshape(N_SSD_HEADS, 1), alog.reshape(N_SSD_HEADS, 1)
    params = (cw, cb, dtb, dtbt, alog, alogt, dskip, nw, scw, scnw)
    return pl.pallas_call(
        _mixer_kernel,
        grid=(batch, nc),
        in_specs=[row(D_SSD), row(D_XBC), row(D_SC), row(D_SC), row(D_SC), row(N_SSD_HEADS),
                  pl.BlockSpec((N_SSD_HEADS, q), lambda b, c: (0, b * nc + c))]
                 + [full(a) for a in params],
        out_specs=row(D_MIX),
        out_shape=jax.ShapeDtypeStruct((t, D_MIX), BF16),
        scratch_shapes=[pltpu.VMEM((D_STATE, D_SSD), F32),
                        pltpu.VMEM((SUBLANES, D_XBC), F32),
                        pltpu.VMEM((SUBLANES, D_SC), F32)],
        compiler_params=pltpu.CompilerParams(
            dimension_semantics=("arbitrary", "arbitrary"), vmem_limit_bytes=VMEM_LIMIT_BYTES),
    )(z, xbc, gb, gc, u, dt, dtt, *params)


def _outproj_kernel(ymix_ref, x_ref, wout_ref, nw_ref, wrh_ref, wrl_ref, br_ref,
                    h1_ref, xn_ref, idx_ref, gate_ref, rank_ref, cnt_ref, run_ref):
    tm = x_ref.shape[0]

    @pl.when(pl.program_id(0) == 0)
    def _():
        run_ref[...] = jnp.zeros_like(run_ref)

    h1 = x_ref[...] + _dot(ymix_ref[...], wout_ref[...])
    h1_ref[...] = h1
    xn = h1 * lax.rsqrt(jnp.mean(h1 * h1, axis=-1, keepdims=True) + EPS) * nw_ref[...]
    _store_token_rows(xn_ref, xn)

    xh = xn.astype(BF16)
    xl = (xn - xh.astype(F32)).astype(BF16)
    logits = (_dot(xh, wrh_ref[...]) + _dot(xh, wrl_ref[...]) + _dot(xl, wrh_ref[...])
              + br_ref[...])

    lane = lax.broadcasted_iota(jnp.int32, (tm, N_EXPERTS), 1).astype(F32)
    work = logits
    vals, sels = [], []
    for _ in range(TOP_K):
        m = jnp.max(work, axis=-1, keepdims=True)
        first = jnp.min(jnp.where(work == m, lane, float(N_EXPERTS)), axis=-1, keepdims=True)
        sel = lane == first
        vals.append(m)
        sels.append(sel)
        work = jnp.where(sel, -jnp.inf, work)
    exps = [jnp.exp(v - vals[0]) for v in vals]
    denom = exps[0] + exps[1] + exps[2] + exps[3]

    onehot = jnp.zeros((tm, N_EXPERTS), F32)
    for sel in sels:
        onehot = onehot + jnp.where(sel, 1.0, 0.0)
    ri = lax.broadcasted_iota(jnp.int32, (tm, tm), 0)
    ci = lax.broadcasted_iota(jnp.int32, (tm, tm), 1)
    strict = jnp.where(ri > ci, 1.0, 0.0).astype(BF16)
    before = run_ref[...] + _dot(strict, onehot.astype(BF16))
    run_ref[...] = run_ref[...] + jnp.sum(onehot, axis=0, keepdims=True)
    cnt_ref[...] = run_ref[...]

    slot = lax.broadcasted_iota(jnp.int32, (tm, TOP_K), 1)
    idx = jnp.zeros((tm, TOP_K), F32)
    gate = jnp.zeros((tm, TOP_K), F32)
    rank = jnp.zeros((tm, TOP_K), F32)
    for k in range(TOP_K):
        e_k = jnp.sum(jnp.where(sels[k], lane, 0.0), axis=-1, keepdims=True)
        r_k = jnp.sum(jnp.where(sels[k], before, 0.0), axis=-1, keepdims=True)
        idx = jnp.where(slot == k, e_k, idx)
        gate = jnp.where(slot == k, exps[k] / denom, gate)
        rank = jnp.where(slot == k, r_k, rank)
    idx_ref[...] = idx.astype(jnp.int32)
    gate_ref[...] = gate
    rank_ref[...] = rank.astype(jnp.int32)


def _outproj(ymix, x2, w_out, nw, wr_hi, wr_lo, br):
    t = x2.shape[0]
    tm = OUT_TILE
    row = lambda n: pl.BlockSpec((tm, n), lambda i: (i, 0))
    full = lambda a: pl.BlockSpec(a.shape, lambda i: (0,) * a.ndim)
    return pl.pallas_call(
        _outproj_kernel,
        grid=(t // tm,),
        in_specs=[row(D_MIX), row(D_MODEL), full(w_out), full(nw), full(wr_hi), full(wr_lo),
                  full(br)],
        out_specs=[row(D_MODEL), pl.BlockSpec((tm * ROW_PARTS, LANES), lambda i: (i, 0)),
                   row(TOP_K), row(TOP_K), row(TOP_K),
                   pl.BlockSpec((1, N_EXPERTS), lambda i: (0, 0))],
        out_shape=[jax.ShapeDtypeStruct((t, D_MODEL), F32),
                   jax.ShapeDtypeStruct((t * ROW_PARTS, LANES), F32),
                   jax.ShapeDtypeStruct((t, TOP_K), jnp.int32),
                   jax.ShapeDtypeStruct((t, TOP_K), F32),
                   jax.ShapeDtypeStruct((t, TOP_K), jnp.int32),
                   jax.ShapeDtypeStruct((1, N_EXPERTS), F32)],
        scratch_shapes=[pltpu.VMEM((1, N_EXPERTS), F32)],
        compiler_params=pltpu.CompilerParams(
            dimension_semantics=("arbitrary",), vmem_limit_bytes=VMEM_LIMIT_BYTES),
    )(ymix, x2, w_out, nw, wr_hi, wr_lo, br)


def _dest_kernel(pstart_ref, idx_ref, rank_ref, dest_ref):
    idx = idx_ref[...]
    base = jnp.zeros_like(idx)
    for e in range(N_EXPERTS):
        base = jnp.where(idx == e, pstart_ref[e], base)
    dest_ref[...] = base + rank_ref[...]


def _dest(pstarts, idx_flat, rank_flat):
    full = lambda a: pl.BlockSpec(a.shape, lambda i, ps: (0,) * a.ndim)
    return pl.pallas_call(
        _dest_kernel,
        grid_spec=pltpu.PrefetchScalarGridSpec(
            num_scalar_prefetch=1, grid=(1,),
            in_specs=[full(idx_flat), full(rank_flat)], out_specs=full(idx_flat)),
        out_shape=jax.ShapeDtypeStruct(idx_flat.shape, jnp.int32),
    )(pstarts, idx_flat, rank_flat)


def _sc_mesh():
    return plsc.VectorSubcoreMesh(core_axis_name="core", subcore_axis_name="subcore")


def _sc_scatter_rows(x, idx, n_out):
    n, width = x.shape
    n_planes = idx.shape[0]
    win = SC_WINDOW

    @functools.partial(pl.kernel, out_type=jax.ShapeDtypeStruct((n_out, width), x.dtype),
                       mesh=_sc_mesh(), scratch_types=[])
    def scatter(x_hbm, idx_hbm, out_hbm):
        def body(x_vmem, idx_vmem):
            for k in range(n_planes):
                pltpu.sync_copy(x_vmem, out_hbm.at[idx_vmem.at[k]])

        pltpu.emit_pipeline(
            body,
            grid=(n // win,),
            in_specs=[pl.BlockSpec((win, width), lambda i: (i, 0)),
                      pl.BlockSpec((n_planes, win), lambda i: (0, i))],
            out_specs=[],
            core_axis_name=("core", "subcore"),
            dimension_semantics=(pltpu.PARALLEL,),
        )(x_hbm, idx_hbm)

    return scatter(x, idx)


def _sc_gather_rows(y, idx):
    n = idx.shape[0]
    width = y.shape[1]
    win = SC_WINDOW

    @functools.partial(pl.kernel, out_type=jax.ShapeDtypeStruct((n, width), y.dtype),
                       mesh=_sc_mesh(), scratch_types=[])
    def gather(y_hbm, idx_hbm, out_hbm):
        def body(idx_vmem, out_vmem):
            pltpu.sync_copy(y_hbm.at[idx_vmem.at[0]], out_vmem)

        pltpu.emit_pipeline(
            body,
            grid=(n // win,),
            in_specs=[pl.BlockSpec((1, win), lambda i: (0, i))],
            out_specs=[pl.BlockSpec((win, width), lambda i: (i, 0))],
            core_axis_name=("core", "subcore"),
            dimension_semantics=(pltpu.PARALLEL,),
        )(idx_hbm, out_hbm)

    return gather(y, idx.reshape(1, n))


def _expert_kernel(blk_e_ref, blk_n_ref, xs_ref, wgu_ref, bgu_ref, wd_ref, bd_ref, out_ref,
                   wgu_b, wd_b):
    i = pl.program_id(0)
    n_valid = blk_n_ref[i]

    @pl.when((i == 0) | (blk_e_ref[i] != blk_e_ref[jnp.maximum(i - 1, 0)]))
    def _():
        wgu_b[...] = wgu_ref[0].astype(BF16)
        wd_b[...] = wd_ref[0].astype(BF16)

    sub = EXPERT_SUB * ROW_PARTS
    for r in range(0, ROW_BLOCK, EXPERT_SUB):
        xs = _load_token_rows(xs_ref.at[pl.ds(r * ROW_PARTS, sub)])
        rows = r + lax.broadcasted_iota(jnp.int32, xs.shape, 0)
        xb = jnp.where(rows < n_valid, xs, 0.0).astype(BF16)
        gu = _dot(xb, wgu_b[...]) + bgu_ref[0]
        gate = jnp.minimum(gu[:, :EXPERT_FF], SWIGLU_LIMIT)
        up = jnp.clip(gu[:, EXPERT_FF:], -SWIGLU_LIMIT, SWIGLU_LIMIT)
        hid = (up + 1.0) * (gate * (1.0 / (1.0 + jnp.exp(-SWIGLU_ALPHA * gate))))
        _store_token_rows(out_ref.at[pl.ds(r * ROW_PARTS, sub)],
                          _dot(hid.astype(BF16), wd_b[...]) + bd_ref[0])


def _experts(blk_e, blk_n, xs, wgu, bgu, wd, bd):
    rb = ROW_BLOCK * ROW_PARTS
    by_expert = lambda a: pl.BlockSpec((1,) + a.shape[1:], lambda i, be, bn: (be[i], 0, 0))
    return pl.pallas_call(
        _expert_kernel,
        grid_spec=pltpu.PrefetchScalarGridSpec(
            num_scalar_prefetch=2, grid=(xs.shape[0] // rb,),
            in_specs=[pl.BlockSpec((rb, LANES), lambda i, be, bn: (i, 0)),
                      by_expert(wgu), by_expert(bgu), by_expert(wd), by_expert(bd)],
            out_specs=pl.BlockSpec((rb, LANES), lambda i, be, bn: (i, 0)),
            scratch_shapes=[pltpu.VMEM(wgu.shape[1:], BF16), pltpu.VMEM(wd.shape[1:], BF16)]),
        out_shape=jax.ShapeDtypeStruct(xs.shape, F32),
        compiler_params=pltpu.CompilerParams(
            dimension_semantics=("arbitrary",), vmem_limit_bytes=VMEM_LIMIT_BYTES),
    )(blk_e, blk_n, xs, wgu, bgu, wd, bd)


def _combine_kernel(gate_ref, h1_ref, fw_ref, y0_ref, y1_ref, y2_ref, y3_ref, out_ref):
    gate = gate_ref[...]
    h2 = h1_ref[...]
    for k, y_ref in enumerate((y0_ref, y1_ref, y2_ref, y3_ref)):
        h2 = h2 + gate[:, k:k + 1] * _load_token_rows(y_ref)
    out_ref[...] = h2 * lax.rsqrt(jnp.mean(h2 * h2, axis=-1, keepdims=True) + EPS) * fw_ref[...]


def _combine(gates, h1, fw, y):
    t, d = h1.shape
    tm = COMBINE_TILE
    nt = t // tm
    slot = lambda k: pl.BlockSpec((tm * ROW_PARTS, LANES), lambda i: (k * nt + i, 0))
    return pl.pallas_call(
        _combine_kernel,
        grid=(nt,),
        in_specs=[pl.BlockSpec((tm, TOP_K), lambda i: (i, 0)),
                  pl.BlockSpec((tm, d), lambda i: (i, 0)),
                  pl.BlockSpec((1, d), lambda i: (0, 0))] + [slot(k) for k in range(TOP_K)],
        out_specs=pl.BlockSpec((tm, d), lambda i: (i, 0)),
        out_shape=jax.ShapeDtypeStruct((t, d), F32),
        compiler_params=pltpu.CompilerParams(
            dimension_semantics=("arbitrary",), vmem_limit_bytes=VMEM_LIMIT_BYTES),
    )(gates, h1, fw, y, y, y, y)


def _block_plan(counts, n_blocks):
    rb = ROW_BLOCK
    pcounts = (counts + rb - 1) // rb * rb
    pends = jnp.cumsum(pcounts)
    pstarts = pends - pcounts
    block_start = jnp.arange(n_blocks, dtype=jnp.int32) * rb
    blk_e = jnp.minimum(jnp.sum(block_start[:, None] >= pends[None, :], axis=1),
                        N_EXPERTS - 1).astype(jnp.int32)
    blk_n = jnp.clip(counts[blk_e] - (block_start - pstarts[blk_e]), 0, rb).astype(jnp.int32)
    return pstarts.astype(jnp.int32), blk_e, blk_n


def kernel(x, norm1_w, w_in, ssd_conv_w, ssd_conv_b, dt_bias, a_log, d_skip, ssd_norm_w,
           sc_conv_w, sc_norm_w, w_out, norm2_w, w_router, b_router, w_gate_up, b_gate_up,
           w_down, b_down, final_norm_w):
    assert norm1_w.shape[0] == 1, "single-layer problem"
    batch, seq, d = x.shape
    t = batch * seq
    x2 = x.reshape(t, d)
    row = lambda a: a.reshape(1, -1)

    s0, s1, s2 = D_SSD, D_SSD + D_XBC, D_SSD + D_XBC + N_SSD_HEADS
    w = w_in[0]
    w_main = jnp.concatenate([w[:, :s1], w[:, s2:]], axis=1).astype(BF16)
    w_dt = w[:, s1:s2].astype(BF16)
    z, xbc, gb, gc, u, dt, dtt = _inproj(x2, row(norm1_w[0]), w_main, w_dt, w_dt.T)

    ymix = _mixer(z, xbc, gb, gc, u, dt, dtt, batch, seq,
                  ssd_conv_w[0], row(ssd_conv_b[0]), row(dt_bias[0]), row(a_log[0]),
                  row(d_skip[0]), row(ssd_norm_w[0]), sc_conv_w[0], row(sc_norm_w[0]))

    wr = w_router[0]
    wr_hi = wr.astype(BF16)
    wr_lo = (wr - wr_hi.astype(F32)).astype(BF16)
    h1, xn2, idx, gates, rank, counts = _outproj(
        ymix, x2, w_out[0].astype(BF16), row(norm2_w[0]), wr_hi, wr_lo, row(b_router[0]))

    n_assign = t * TOP_K
    n_rows = n_assign + N_EXPERTS * ROW_BLOCK
    pstarts, blk_e, blk_n = _block_plan(counts[0].astype(jnp.int32), n_rows // ROW_BLOCK)
    dest = _dest(pstarts, idx.reshape(n_assign // 128, 128), rank.reshape(n_assign // 128, 128))

    parts = jnp.arange(ROW_PARTS, dtype=jnp.int32)
    dest_parts = (dest.reshape(t, TOP_K).T[:, :, None] * ROW_PARTS + parts).reshape(TOP_K, -1)
    xs = _sc_scatter_rows(xn2, dest_parts, n_rows * ROW_PARTS)
    ys = _experts(blk_e, blk_n, xs, w_gate_up[0], b_gate_up[0][:, None, :],
                  w_down[0], b_down[0][:, None, :])
    y = _sc_gather_rows(ys, dest_parts.reshape(-1))
    out = _combine(gates, h1, row(final_norm_w), y)
    return out.reshape(batch, seq, d)
```

```python
import functools

import jax
import jax.numpy as jnp
from jax import lax
from jax.experimental import pallas as pl
from jax.experimental.pallas import tpu as pltpu
from jax.experimental.pallas import tpu_sc as plsc

D_MODEL = 1024
D_SSD = 1024
SSD_HEAD_DIM = 64
N_SSD_HEADS = 16
N_GROUPS = 2
HEADS_PER_GROUP = 8
D_STATE = 128
SSD_CONV = 4
CHUNK = 128
D_XBC = D_SSD + 2 * N_GROUPS * D_STATE
D_SC = 1024
SC_CONV = 3
D_MIX = D_SSD + D_SC
N_EXPERTS = 32
TOP_K = 4
EXPERT_FF = 1024
SWIGLU_LIMIT = 7.0
SWIGLU_ALPHA = 1.702
EPS = 1e-5

VMEM_LIMIT_BYTES = 56 * 1024 * 1024
SUBLANES = 8
ROW_BLOCK = 512
EXPERT_SUB = 256
IN_TILE = 512
OUT_TILE = 256
COMBINE_TILE = 512
LANES = 128
ROW_PARTS = D_MODEL // (2 * LANES)
SC_WINDOW = 128

F32 = jnp.float32
BF16 = jnp.bfloat16


def _dot(a, b):
    return jnp.dot(a, b, preferred_element_type=F32)


def _split3(v):
    hi = v.astype(BF16)
    r1 = v - hi.astype(F32)
    mid = r1.astype(BF16)
    lo = (r1 - mid.astype(F32)).astype(BF16)
    return hi, mid, lo


def _dot_exact_rhs01(v, m01):
    hi, mid, lo = _split3(v)
    return _dot(hi, m01) + _dot(mid, m01) + _dot(lo, m01)


def _dot_exact_lhs01(m01, v):
    hi, mid, lo = _split3(v)
    return _dot(m01, hi) + _dot(m01, mid) + _dot(m01, lo)


def _silu(v):
    return v * (1.0 / (1.0 + jnp.exp(-v)))


def _softplus(v):
    return jnp.maximum(v, 0.0) + jnp.log1p(jnp.exp(-jnp.abs(v)))


def _pack_rows(v):
    bits = lambda a: lax.bitcast_convert_type(a.astype(BF16).astype(F32), jnp.uint32)
    parts = []
    for c in range(ROW_PARTS):
        lo = v[:, (2 * c) * LANES:(2 * c + 1) * LANES]
        hi = v[:, (2 * c + 1) * LANES:(2 * c + 2) * LANES]
        parts.append(bits(hi) | (bits(lo) >> 16))
    return parts


def _unpack_rows(parts):
    cols = []
    for w in parts:
        cols.append(lax.bitcast_convert_type(w << 16, F32))
        cols.append(lax.bitcast_convert_type(w & jnp.uint32(0xFFFF0000), F32))
    return jnp.concatenate(cols, axis=-1)


def _shifted(cur, tail, s):
    if s == 0:
        return cur
    rc = pltpu.roll(cur, s, axis=0)
    rt = pltpu.roll(tail, s, axis=0)
    row = lax.broadcasted_iota(jnp.int32, tail.shape, 0)
    first = jnp.where(row < s, rt, rc[0:SUBLANES])
    return jnp.concatenate([first, rc[SUBLANES:]], axis=0)


def _inproj_kernel(x_ref, nw_ref, w_ref, wdt_ref, wdtt_ref,
                   z_ref, xbc_ref, gb_ref, gc_ref, u_ref, dt_ref, dtt_ref):
    x = x_ref[...]
    xn = x * lax.rsqrt(jnp.mean(x * x, axis=-1, keepdims=True) + EPS) * nw_ref[...]
    xb = xn.astype(BF16)
    o = 0
    for ref in (z_ref, xbc_ref, gb_ref, gc_ref, u_ref):
        n = ref.shape[-1]
        ref[...] = _dot(xb, w_ref[:, o:o + n]).astype(ref.dtype)
        o += n
    dt_ref[...] = _dot(xb, wdt_ref[...])
    dtt_ref[...] = lax.dot_general(wdtt_ref[...], xb, (((1,), (1,)), ((), ())),
                                   preferred_element_type=F32)


def _inproj(x2, nw, w_main, w_dt, w_dtt):
    t = x2.shape[0]
    tm = IN_TILE
    row = lambda n: pl.BlockSpec((tm, n), lambda i: (i, 0))
    full = lambda a: pl.BlockSpec(a.shape, lambda i: (0,) * a.ndim)
    return pl.pallas_call(
        _inproj_kernel,
        grid=(t // tm,),
        in_specs=[row(D_MODEL), full(nw), full(w_main), full(w_dt), full(w_dtt)],
        out_specs=[row(D_SSD), row(D_XBC), row(D_SC), row(D_SC), row(D_SC),
                   row(N_SSD_HEADS), pl.BlockSpec((N_SSD_HEADS, tm), lambda i: (0, i))],
        out_shape=[jax.ShapeDtypeStruct((t, D_SSD), BF16),
                   jax.ShapeDtypeStruct((t, D_XBC), BF16),
                   jax.ShapeDtypeStruct((t, D_SC), BF16),
                   jax.ShapeDtypeStruct((t, D_SC), BF16),
                   jax.ShapeDtypeStruct((t, D_SC), BF16),
                   jax.ShapeDtypeStruct((t, N_SSD_HEADS), F32),
                   jax.ShapeDtypeStruct((N_SSD_HEADS, t), F32)],
        compiler_params=pltpu.CompilerParams(
            dimension_semantics=("arbitrary",), vmem_limit_bytes=VMEM_LIMIT_BYTES),
    )(x2, nw, w_main, w_dt, w_dtt)


def _mixer_kernel(z_ref, xbc_ref, gb_ref, gc_ref, u_ref, dt_ref, dtt_ref,
                  cw_ref, cb_ref, dtb_ref, dtbt_ref, alog_ref, alogt_ref, dskip_ref,
                  nw_ref, scw_ref, scnw_ref, out_ref,
                  state_ref, tail_ref, sctail_ref):
    q = CHUNK

    @pl.when(pl.program_id(1) == 0)
    def _():
        state_ref[...] = jnp.zeros_like(state_ref)
        tail_ref[...] = jnp.zeros_like(tail_ref)
        sctail_ref[...] = jnp.zeros_like(sctail_ref)

    raw = xbc_ref[...].astype(F32)
    tail = tail_ref[...]
    conv = cb_ref[...]
    for k in range(SSD_CONV):
        conv = conv + cw_ref[k:k + 1, :] * _shifted(raw, tail, SSD_CONV - 1 - k)
    tail_ref[...] = raw[q - SUBLANES:q]
    act = _silu(conv)
    xs = act[:, :D_SSD]
    n_bc = N_GROUPS * D_STATE
    b_all = act[:, D_SSD:D_SSD + n_bc].astype(BF16)
    c_all = act[:, D_SSD + n_bc:].astype(BF16)

    dt = _softplus(dt_ref[...] + dtb_ref[...])
    dtt = _softplus(dtt_ref[...] + dtbt_ref[...])
    adt = dt * (-jnp.exp(alog_ref[...]))
    adtt = dtt * (-jnp.exp(alogt_ref[...]))
    ri = lax.broadcasted_iota(jnp.int32, (q, q), 0)
    ci = lax.broadcasted_iota(jnp.int32, (q, q), 1)
    causal = ri >= ci
    tri = jnp.where(causal, 1.0, 0.0).astype(BF16)
    trit = jnp.where(ri <= ci, 1.0, 0.0).astype(BF16)
    acum = _dot_exact_lhs01(tri, adt)
    acumt = _dot_exact_rhs01(adtt, trit)
    a_last = acum[q - 1:q, :]

    hh = lax.broadcasted_iota(jnp.int32, (N_SSD_HEADS, D_SSD), 0)
    hl = lax.broadcasted_iota(jnp.int32, (N_SSD_HEADS, D_SSD), 1) // SSD_HEAD_DIM
    expand = jnp.where(hh == hl, 1.0, 0.0).astype(BF16)

    dt_x = _dot_exact_rhs01(dt, expand)
    xdt = xs * dt_x
    xdt_b = xdt.astype(BF16)
    dec_in = _dot_exact_rhs01(jnp.exp(acum), expand)
    dec_out = _dot_exact_rhs01(jnp.exp(a_last - acum), expand)
    dec_all = _dot_exact_rhs01(jnp.exp(a_last), expand)
    xw_b = (xdt * dec_out).astype(BF16)

    lane_lo = lax.broadcasted_iota(jnp.int32, (q, 2 * SSD_HEAD_DIM), 1) < SSD_HEAD_DIM
    gw = HEADS_PER_GROUP * SSD_HEAD_DIM
    y_parts = []
    for g in range(N_GROUPS):
        b_g = b_all[:, g * D_STATE:(g + 1) * D_STATE]
        c_g = c_all[:, g * D_STATE:(g + 1) * D_STATE]
        cb = lax.dot_general(c_g, b_g, (((1,), (1,)), ((), ())), preferred_element_type=F32)
        st = state_ref[:, g * gw:(g + 1) * gw]
        y_inter = _dot(c_g, st.astype(BF16)) * dec_in[:, g * gw:(g + 1) * gw]
        new_st = lax.dot_general(b_g, xw_b[:, g * gw:(g + 1) * gw], (((0,), (0,)), ((), ())),
                                 preferred_element_type=F32)
        state_ref[:, g * gw:(g + 1) * gw] = st * dec_all[:, g * gw:(g + 1) * gw] + new_st
        pair_parts = []
        for pr in range(HEADS_PER_GROUP // 2):
            h0 = g * HEADS_PER_GROUP + 2 * pr
            lo = h0 * SSD_HEAD_DIM
            rhs = xdt_b[:, lo:lo + 2 * SSD_HEAD_DIM]
            ys = []
            for h in (h0, h0 + 1):
                seg = acum[:, h:h + 1] - acumt[h:h + 1, :]
                m = (cb * jnp.exp(jnp.where(causal, seg, -jnp.inf))).astype(BF16)
                ys.append(_dot(m, rhs))
            pair_parts.append(jnp.where(lane_lo, ys[0], ys[1]))
        y_parts.append(jnp.concatenate(pair_parts, axis=-1) + y_inter)
    y = jnp.concatenate(y_parts, axis=-1) + _dot_exact_rhs01(dskip_ref[...], expand) * xs

    gated = y * _silu(z_ref[...].astype(F32))
    outs = []
    for g in range(N_GROUPS):
        gg = gated[:, g * gw:(g + 1) * gw]
        outs.append(gg * lax.rsqrt(jnp.mean(gg * gg, axis=-1, keepdims=True) + EPS))
    y_ssd = jnp.concatenate(outs, axis=-1) * nw_ref[...]
    out_ref[:, :D_SSD] = y_ssd.astype(out_ref.dtype)

    cu = gc_ref[...].astype(F32) * u_ref[...].astype(F32)
    sctail = sctail_ref[...]
    v = jnp.zeros_like(cu)
    for k in range(SC_CONV):
        v = v + scw_ref[k:k + 1, :] * _shifted(cu, sctail, SC_CONV - 1 - k)
    sctail_ref[...] = cu[q - SUBLANES:q]
    gv = gb_ref[...].astype(F32) * v
    y_sc = gv * lax.rsqrt(jnp.mean(gv * gv, axis=-1, keepdims=True) + EPS) * scnw_ref[...]
    out_ref[:, D_SSD:] = y_sc.astype(out_ref.dtype)


def _mixer(z, xbc, gb, gc, u, dt, dtt, batch, seq, cw, cb, dtb, alog, dskip, nw, scw, scnw):
    q = CHUNK
    nc = seq // q
    t = batch * seq
    row = lambda n: pl.BlockSpec((q, n), lambda b, c: (b * nc + c, 0))
    full = lambda a: pl.BlockSpec(a.shape, lambda b, c: (0,) * a.ndim)
    dtbt, alogt = dtb.reshape(N_SSD_HEADS, 1), alog.reshape(N_SSD_HEADS, 1)
    params = (cw, cb, dtb, dtbt, alog, alogt, dskip, nw, scw, scnw)
    return pl.pallas_call(
        _mixer_kernel,
        grid=(batch, nc),
        in_specs=[row(D_SSD), row(D_XBC), row(D_SC), row(D_SC), row(D_SC), row(N_SSD_HEADS),
                  pl.BlockSpec((N_SSD_HEADS, q), lambda b, c: (0, b * nc + c))]
                 + [full(a) for a in params],
        out_specs=row(D_MIX),
        out_shape=jax.ShapeDtypeStruct((t, D_MIX), BF16),
        scratch_shapes=[pltpu.VMEM((D_STATE, D_SSD), F32),
                        pltpu.VMEM((SUBLANES, D_XBC), F32),
                        pltpu.VMEM((SUBLANES, D_SC), F32)],
        compiler_params=pltpu.CompilerParams(
            dimension_semantics=("arbitrary", "arbitrary"), vmem_limit_bytes=VMEM_LIMIT_BYTES),
    )(z, xbc, gb, gc, u, dt, dtt, *params)


def _outproj_kernel(ymix_ref, x_ref, wout_ref, nw_ref, wrh_ref, wrl_ref, br_ref,
                    h1_ref, xn_ref, idx_ref, gate_ref, rank_ref, cnt_ref, run_ref):
    tm = x_ref.shape[0]

    @pl.when(pl.program_id(0) == 0)
    def _():
        run_ref[...] = jnp.zeros_like(run_ref)

    h1 = x_ref[...] + _dot(ymix_ref[...], wout_ref[...])
    h1_ref[...] = h1
    xn = h1 * lax.rsqrt(jnp.mean(h1 * h1, axis=-1, keepdims=True) + EPS) * nw_ref[...]
    for c, words in enumerate(_pack_rows(xn)):
        xn_ref[c] = words

    xh = xn.astype(BF16)
    xl = (xn - xh.astype(F32)).astype(BF16)
    logits = (_dot(xh, wrh_ref[...]) + _dot(xh, wrl_ref[...]) + _dot(xl, wrh_ref[...])
              + br_ref[...])

    lane = lax.broadcasted_iota(jnp.int32, (tm, N_EXPERTS), 1).astype(F32)
    work = logits
    vals, sels = [], []
    for _ in range(TOP_K):
        m = jnp.max(work, axis=-1, keepdims=True)
        first = jnp.min(jnp.where(work == m, lane, float(N_EXPERTS)), axis=-1, keepdims=True)
        sel = lane == first
        vals.append(m)
        sels.append(sel)
        work = jnp.where(sel, -jnp.inf, work)
    exps = [jnp.exp(v - vals[0]) for v in vals]
    denom = exps[0] + exps[1] + exps[2] + exps[3]

    onehot = jnp.zeros((tm, N_EXPERTS), F32)
    for sel in sels:
        onehot = onehot + jnp.where(sel, 1.0, 0.0)
    ri = lax.broadcasted_iota(jnp.int32, (tm, tm), 0)
    ci = lax.broadcasted_iota(jnp.int32, (tm, tm), 1)
    strict = jnp.where(ri > ci, 1.0, 0.0).astype(BF16)
    before = run_ref[...] + _dot(strict, onehot.astype(BF16))
    run_ref[...] = run_ref[...] + jnp.sum(onehot, axis=0, keepdims=True)
    cnt_ref[...] = run_ref[...]

    slot = lax.broadcasted_iota(jnp.int32, (tm, TOP_K), 1)
    idx = jnp.zeros((tm, TOP_K), F32)
    gate = jnp.zeros((tm, TOP_K), F32)
    rank = jnp.zeros((tm, TOP_K), F32)
    for k in range(TOP_K):
        e_k = jnp.sum(jnp.where(sels[k], lane, 0.0), axis=-1, keepdims=True)
        r_k = jnp.sum(jnp.where(sels[k], before, 0.0), axis=-1, keepdims=True)
        idx = jnp.where(slot == k, e_k, idx)
        gate = jnp.where(slot == k, exps[k] / denom, gate)
        rank = jnp.where(slot == k, r_k, rank)
    idx_ref[...] = idx.astype(jnp.int32)
    gate_ref[...] = gate
    rank_ref[...] = rank.astype(jnp.int32)


def _outproj(ymix, x2, w_out, nw, wr_hi, wr_lo, br):
    t = x2.shape[0]
    tm = OUT_TILE
    row = lambda n: pl.BlockSpec((tm, n), lambda i: (i, 0))
    full = lambda a: pl.BlockSpec(a.shape, lambda i: (0,) * a.ndim)
    return pl.pallas_call(
        _outproj_kernel,
        grid=(t // tm,),
        in_specs=[row(D_MIX), row(D_MODEL), full(w_out), full(nw), full(wr_hi), full(wr_lo),
                  full(br)],
        out_specs=[row(D_MODEL), pl.BlockSpec((ROW_PARTS, tm, LANES), lambda i: (0, i, 0)),
                   row(TOP_K), row(TOP_K), row(TOP_K),
                   pl.BlockSpec((1, N_EXPERTS), lambda i: (0, 0))],
        out_shape=[jax.ShapeDtypeStruct((t, D_MODEL), F32),
                   jax.ShapeDtypeStruct((ROW_PARTS, t, LANES), jnp.uint32),
                   jax.ShapeDtypeStruct((t, TOP_K), jnp.int32),
                   jax.ShapeDtypeStruct((t, TOP_K), F32),
                   jax.ShapeDtypeStruct((t, TOP_K), jnp.int32),
                   jax.ShapeDtypeStruct((1, N_EXPERTS), F32)],
        scratch_shapes=[pltpu.VMEM((1, N_EXPERTS), F32)],
        compiler_params=pltpu.CompilerParams(
            dimension_semantics=("arbitrary",), vmem_limit_bytes=VMEM_LIMIT_BYTES),
    )(ymix, x2, w_out, nw, wr_hi, wr_lo, br)


def _dest_kernel(pstart_ref, idx_ref, rank_ref, dest_ref):
    idx = idx_ref[...]
    base = jnp.zeros_like(idx)
    for e in range(N_EXPERTS):
        base = jnp.where(idx == e, pstart_ref[e], base)
    dest_ref[...] = base + rank_ref[...]


def _dest(pstarts, idx_flat, rank_flat):
    full = lambda a: pl.BlockSpec(a.shape, lambda i, ps: (0,) * a.ndim)
    return pl.pallas_call(
        _dest_kernel,
        grid_spec=pltpu.PrefetchScalarGridSpec(
            num_scalar_prefetch=1, grid=(1,),
            in_specs=[full(idx_flat), full(rank_flat)], out_specs=full(idx_flat)),
        out_shape=jax.ShapeDtypeStruct(idx_flat.shape, jnp.int32),
    )(pstarts, idx_flat, rank_flat)


def _sc_mesh():
    return plsc.VectorSubcoreMesh(core_axis_name="core", subcore_axis_name="subcore")


def _part_rows(dest, n_rows):
    off = (jnp.arange(ROW_PARTS, dtype=jnp.int32) * n_rows)[:, None, None]
    return (dest[None] + off).reshape(ROW_PARTS * dest.shape[0], dest.shape[1])


def _sc_dispatch(x, dest, n_rows):
    parts, t, width = x.shape
    n_slots = dest.shape[0]
    win = SC_WINDOW
    nwin = t // win

    @functools.partial(pl.kernel,
                       out_type=jax.ShapeDtypeStruct((parts * n_rows, width), jnp.int32),
                       mesh=_sc_mesh(), scratch_types=[])
    def scatter(x_hbm, idx_hbm, out_hbm):
        def body(x_vmem, idx_vmem):
            for k in range(n_slots):
                pltpu.sync_copy(x_vmem, out_hbm.at[idx_vmem.at[k]])

        pltpu.emit_pipeline(
            body,
            grid=(parts * nwin,),
            in_specs=[pl.BlockSpec((win, width), lambda j: (j, 0)),
                      pl.BlockSpec((n_slots, win), lambda j: (j // nwin, j % nwin))],
            out_specs=[],
            core_axis_name=("core", "subcore"),
            dimension_semantics=(pltpu.PARALLEL,),
        )(x_hbm, idx_hbm)

    x_words = lax.bitcast_convert_type(x, jnp.int32).reshape(parts * t, width)
    xs = scatter(x_words, _part_rows(dest, n_rows))
    return lax.bitcast_convert_type(xs, x.dtype).reshape(parts, n_rows, width)


def _sc_collect(ys, dest):
    parts, n_rows, width = ys.shape
    n_slots, t = dest.shape
    win = SC_WINDOW
    nwin = t // win
    idx = _part_rows(dest, n_rows).reshape(parts, n_slots, t).swapaxes(0, 1)
    idx = idx.reshape(n_slots * parts, t)

    @functools.partial(pl.kernel,
                       out_type=jax.ShapeDtypeStruct((n_slots * parts * t, width), jnp.int32),
                       mesh=_sc_mesh(), scratch_types=[])
    def gather(y_hbm, idx_hbm, out_hbm):
        def body(idx_vmem, out_vmem):
            pltpu.sync_copy(y_hbm.at[idx_vmem.at[0]], out_vmem)

        pltpu.emit_pipeline(
            body,
            grid=(n_slots * parts * nwin,),
            in_specs=[pl.BlockSpec((1, win), lambda j: (j // nwin, j % nwin))],
            out_specs=[pl.BlockSpec((win, width), lambda j: (j, 0))],
            core_axis_name=("core", "subcore"),
            dimension_semantics=(pltpu.PARALLEL,),
        )(idx_hbm, out_hbm)

    y_words = lax.bitcast_convert_type(ys, jnp.int32).reshape(parts * n_rows, width)
    out = gather(y_words, idx)
    return lax.bitcast_convert_type(out, ys.dtype).reshape(n_slots, parts, t, width)


def _expert_kernel(blk_e_ref, blk_n_ref, xs_ref, wgu_ref, bgu_ref, wd_ref, bd_ref, out_ref,
                   wgu_b, wd_b):
    i = pl.program_id(0)
    n_valid = blk_n_ref[i]

    @pl.when((i == 0) | (blk_e_ref[i] != blk_e_ref[jnp.maximum(i - 1, 0)]))
    def _():
        wgu_b[...] = wgu_ref[0].astype(BF16)
        wd_b[...] = wd_ref[0].astype(BF16)

    for r in range(0, ROW_BLOCK, EXPERT_SUB):
        words = [xs_ref[c, r:r + EXPERT_SUB] for c in range(ROW_PARTS)]
        rows = r + lax.broadcasted_iota(jnp.int32, words[0].shape, 0)
        xb = _unpack_rows([jnp.where(rows < n_valid, w, 0) for w in words]).astype(BF16)
        gu = _dot(xb, wgu_b[...]) + bgu_ref[0]
        gate = jnp.minimum(gu[:, :EXPERT_FF], SWIGLU_LIMIT)
        up = jnp.clip(gu[:, EXPERT_FF:], -SWIGLU_LIMIT, SWIGLU_LIMIT)
        hid = (up + 1.0) * (gate * (1.0 / (1.0 + jnp.exp(-SWIGLU_ALPHA * gate))))
        out = _dot(hid.astype(BF16), wd_b[...]) + bd_ref[0]
        for c, w in enumerate(_pack_rows(out)):
            out_ref[c, r:r + EXPERT_SUB] = w


def _experts(blk_e, blk_n, xs, wgu, bgu, wd, bd):
    rb = ROW_BLOCK
    by_expert = lambda a: pl.BlockSpec((1,) + a.shape[1:], lambda i, be, bn: (be[i], 0, 0))
    return pl.pallas_call(
        _expert_kernel,
        grid_spec=pltpu.PrefetchScalarGridSpec(
            num_scalar_prefetch=2, grid=(xs.shape[1] // rb,),
            in_specs=[pl.BlockSpec((ROW_PARTS, rb, LANES), lambda i, be, bn: (0, i, 0)),
                      by_expert(wgu), by_expert(bgu), by_expert(wd), by_expert(bd)],
            out_specs=pl.BlockSpec((ROW_PARTS, rb, LANES), lambda i, be, bn: (0, i, 0)),
            scratch_shapes=[pltpu.VMEM(wgu.shape[1:], BF16), pltpu.VMEM(wd.shape[1:], BF16)]),
        out_shape=jax.ShapeDtypeStruct(xs.shape, xs.dtype),
        compiler_params=pltpu.CompilerParams(
            dimension_semantics=("arbitrary",), vmem_limit_bytes=VMEM_LIMIT_BYTES),
    )(blk_e, blk_n, xs, wgu, bgu, wd, bd)


def _combine_kernel(gate_ref, h1_ref, fw_ref, y0_ref, y1_ref, y2_ref, y3_ref, out_ref):
    gate = gate_ref[...]
    h2 = h1_ref[...]
    for k, y_ref in enumerate((y0_ref, y1_ref, y2_ref, y3_ref)):
        h2 = h2 + gate[:, k:k + 1] * _unpack_rows([y_ref[c] for c in range(ROW_PARTS)])
    out_ref[...] = h2 * lax.rsqrt(jnp.mean(h2 * h2, axis=-1, keepdims=True) + EPS) * fw_ref[...]


def _combine(gates, h1, fw, planes):
    t, d = h1.shape
    tm = COMBINE_TILE
    plane = lambda k: pl.BlockSpec((None, ROW_PARTS, tm, LANES), lambda i: (k, 0, i, 0))
    return pl.pallas_call(
        _combine_kernel,
        grid=(t // tm,),
        in_specs=[pl.BlockSpec((tm, TOP_K), lambda i: (i, 0)),
                  pl.BlockSpec((tm, d), lambda i: (i, 0)),
                  pl.BlockSpec((1, d), lambda i: (0, 0))] + [plane(k) for k in range(TOP_K)],
        out_specs=pl.BlockSpec((tm, d), lambda i: (i, 0)),
        out_shape=jax.ShapeDtypeStruct((t, d), F32),
        compiler_params=pltpu.CompilerParams(
            dimension_semantics=("arbitrary",), vmem_limit_bytes=VMEM_LIMIT_BYTES),
    )(gates, h1, fw, *([planes] * TOP_K))


def _block_plan(counts, n_blocks):
    rb = ROW_BLOCK
    pcounts = (counts + rb - 1) // rb * rb
    pends = jnp.cumsum(pcounts)
    pstarts = pends - pcounts
    block_start = jnp.arange(n_blocks, dtype=jnp.int32) * rb
    blk_e = jnp.minimum(jnp.sum(block_start[:, None] >= pends[None, :], axis=1),
                        N_EXPERTS - 1).astype(jnp.int32)
    blk_n = jnp.clip(counts[blk_e] - (block_start - pstarts[blk_e]), 0, rb).astype(jnp.int32)
    return pstarts.astype(jnp.int32), blk_e, blk_n


def kernel(x, norm1_w, w_in, ssd_conv_w, ssd_conv_b, dt_bias, a_log, d_skip, ssd_norm_w,
           sc_conv_w, sc_norm_w, w_out, norm2_w, w_router, b_router, w_gate_up, b_gate_up,
           w_down, b_down, final_norm_w):
    assert norm1_w.shape[0] == 1, "single-layer problem"
    batch, seq, d = x.shape
    t = batch * seq
    x2 = x.reshape(t, d)
    row = lambda a: a.reshape(1, -1)

    s0, s1, s2 = D_SSD, D_SSD + D_XBC, D_SSD + D_XBC + N_SSD_HEADS
    w = w_in[0]
    w_main = jnp.concatenate([w[:, :s1], w[:, s2:]], axis=1).astype(BF16)
    w_dt = w[:, s1:s2].astype(BF16)
    z, xbc, gb, gc, u, dt, dtt = _inproj(x2, row(norm1_w[0]), w_main, w_dt, w_dt.T)

    ymix = _mixer(z, xbc, gb, gc, u, dt, dtt, batch, seq,
                  ssd_conv_w[0], row(ssd_conv_b[0]), row(dt_bias[0]), row(a_log[0]),
                  row(d_skip[0]), row(ssd_norm_w[0]), sc_conv_w[0], row(sc_norm_w[0]))

    wr = w_router[0]
    wr_hi = wr.astype(BF16)
    wr_lo = (wr - wr_hi.astype(F32)).astype(BF16)
    h1, xn2, idx, gates, rank, counts = _outproj(
        ymix, x2, w_out[0].astype(BF16), row(norm2_w[0]), wr_hi, wr_lo, row(b_router[0]))

    n_assign = t * TOP_K
    n_rows = n_assign + N_EXPERTS * ROW_BLOCK
    pstarts, blk_e, blk_n = _block_plan(counts[0].astype(jnp.int32), n_rows // ROW_BLOCK)
    dest = _dest(pstarts, idx.reshape(n_assign // 128, 128), rank.reshape(n_assign // 128, 128))

    dest_slot_major = dest.reshape(t, TOP_K).T
    xs = _sc_dispatch(xn2, dest_slot_major, n_rows)
    ys = _experts(blk_e, blk_n, xs, w_gate_up[0], b_gate_up[0][:, None, :],
                  w_down[0], b_down[0][:, None, :])
    planes = _sc_collect(ys, dest_slot_major)
    out = _combine(gates, h1, row(final_norm_w), planes)
    return out.reshape(batch, seq, d)
```

```python
import functools

import jax
import jax.numpy as jnp
from jax import lax
from jax.experimental import pallas as pl
from jax.experimental.pallas import tpu as pltpu
from jax.experimental.pallas import tpu_sc as plsc

D_MODEL = 1024
D_SSD = 1024
SSD_HEAD_DIM = 64
N_SSD_HEADS = 16
N_GROUPS = 2
HEADS_PER_GROUP = 8
D_STATE = 128
SSD_CONV = 4
CHUNK = 128
D_XBC = D_SSD + 2 * N_GROUPS * D_STATE
D_SC = 1024
SC_CONV = 3
D_MIX = D_SSD + D_SC
N_EXPERTS = 32
TOP_K = 4
EXPERT_FF = 1024
SWIGLU_LIMIT = 7.0
SWIGLU_ALPHA = 1.702
EPS = 1e-5

VMEM_LIMIT_BYTES = 56 * 1024 * 1024
SUBLANES = 8
ROW_BLOCK = 512
EXPERT_SUB = 256
IN_TILE = 512
OUT_TILE = 512
COMBINE_TILE = 512
LANES = 128
ROW_PARTS = D_MODEL // (2 * LANES)
SC_WINDOW = 128
HIGH_HALF = -65536

F32 = jnp.float32
BF16 = jnp.bfloat16


def _dot(a, b):
    return jnp.dot(a, b, preferred_element_type=F32)


def _split3(v):
    hi = v.astype(BF16)
    r1 = v - hi.astype(F32)
    mid = r1.astype(BF16)
    lo = (r1 - mid.astype(F32)).astype(BF16)
    return hi, mid, lo


def _dot_exact_rhs01(v, m01):
    hi, mid, lo = _split3(v)
    return _dot(hi, m01) + _dot(mid, m01) + _dot(lo, m01)


def _dot_exact_lhs01(m01, v):
    hi, mid, lo = _split3(v)
    return _dot(m01, hi) + _dot(m01, mid) + _dot(m01, lo)


def _silu(v):
    return v * (1.0 / (1.0 + jnp.exp(-v)))


def _softplus(v):
    return jnp.maximum(v, 0.0) + jnp.log1p(jnp.exp(-jnp.abs(v)))


def _pack_rows(v):
    bits = lambda a: lax.bitcast_convert_type(a.astype(BF16).astype(F32), jnp.int32)
    parts = []
    for c in range(ROW_PARTS):
        lo = v[:, (2 * c) * LANES:(2 * c + 1) * LANES]
        hi = v[:, (2 * c + 1) * LANES:(2 * c + 2) * LANES]
        parts.append(bits(hi) | lax.shift_right_logical(bits(lo), 16))
    return parts


def _unpack_rows(parts):
    cols = []
    for w in parts:
        cols.append(lax.bitcast_convert_type(w << 16, F32))
        cols.append(lax.bitcast_convert_type(w & HIGH_HALF, F32))
    return jnp.concatenate(cols, axis=-1)


def _shifted(cur, tail, s):
    if s == 0:
        return cur
    rc = pltpu.roll(cur, s, axis=0)
    rt = pltpu.roll(tail, s, axis=0)
    row = lax.broadcasted_iota(jnp.int32, tail.shape, 0)
    first = jnp.where(row < s, rt, rc[0:SUBLANES])
    return jnp.concatenate([first, rc[SUBLANES:]], axis=0)


def _inproj_kernel(x_ref, nw_ref, w_ref, wdt_ref, wdtt_ref,
                   z_ref, xbc_ref, gb_ref, gc_ref, u_ref, dt_ref, dtt_ref):
    x = x_ref[...]
    xn = x * lax.rsqrt(jnp.mean(x * x, axis=-1, keepdims=True) + EPS) * nw_ref[...]
    xb = xn.astype(BF16)
    o = 0
    for ref in (z_ref, xbc_ref, gb_ref, gc_ref, u_ref):
        n = ref.shape[-1]
        ref[...] = _dot(xb, w_ref[:, o:o + n]).astype(ref.dtype)
        o += n
    dt_ref[...] = _dot(xb, wdt_ref[...])
    dtt_ref[...] = lax.dot_general(wdtt_ref[...], xb, (((1,), (1,)), ((), ())),
                                   preferred_element_type=F32)


def _inproj(x2, nw, w_main, w_dt, w_dtt):
    t = x2.shape[0]
    tm = IN_TILE
    row = lambda n: pl.BlockSpec((tm, n), lambda i: (i, 0))
    full = lambda a: pl.BlockSpec(a.shape, lambda i: (0,) * a.ndim)
    return pl.pallas_call(
        _inproj_kernel,
        grid=(t // tm,),
        in_specs=[row(D_MODEL), full(nw), full(w_main), full(w_dt), full(w_dtt)],
        out_specs=[row(D_SSD), row(D_XBC), row(D_SC), row(D_SC), row(D_SC),
                   row(N_SSD_HEADS), pl.BlockSpec((N_SSD_HEADS, tm), lambda i: (0, i))],
        out_shape=[jax.ShapeDtypeStruct((t, D_SSD), BF16),
                   jax.ShapeDtypeStruct((t, D_XBC), BF16),
                   jax.ShapeDtypeStruct((t, D_SC), BF16),
                   jax.ShapeDtypeStruct((t, D_SC), BF16),
                   jax.ShapeDtypeStruct((t, D_SC), BF16),
                   jax.ShapeDtypeStruct((t, N_SSD_HEADS), F32),
                   jax.ShapeDtypeStruct((N_SSD_HEADS, t), F32)],
        compiler_params=pltpu.CompilerParams(
            dimension_semantics=("arbitrary",), vmem_limit_bytes=VMEM_LIMIT_BYTES),
    )(x2, nw, w_main, w_dt, w_dtt)


def _mixer_kernel(z_ref, xbc_ref, gb_ref, gc_ref, u_ref, dt_ref, dtt_ref,
                  cw_ref, cb_ref, dtb_ref, dtbt_ref, alog_ref, alogt_ref, dskip_ref,
                  nw_ref, scw_ref, scnw_ref, out_ref,
                  state_ref, tail_ref, sctail_ref):
    q = CHUNK

    @pl.when(pl.program_id(1) == 0)
    def _():
        state_ref[...] = jnp.zeros_like(state_ref)
        tail_ref[...] = jnp.zeros_like(tail_ref)
        sctail_ref[...] = jnp.zeros_like(sctail_ref)

    raw = xbc_ref[...].astype(F32)
    tail = tail_ref[...]
    conv = cb_ref[...]
    for k in range(SSD_CONV):
        conv = conv + cw_ref[k:k + 1, :] * _shifted(raw, tail, SSD_CONV - 1 - k)
    tail_ref[...] = raw[q - SUBLANES:q]
    act = _silu(conv)
    xs = act[:, :D_SSD]
    n_bc = N_GROUPS * D_STATE
    b_all = act[:, D_SSD:D_SSD + n_bc].astype(BF16)
    c_all = act[:, D_SSD + n_bc:].astype(BF16)

    dt = _softplus(dt_ref[...] + dtb_ref[...])
    dtt = _softplus(dtt_ref[...] + dtbt_ref[...])
    adt = dt * (-jnp.exp(alog_ref[...]))
    adtt = dtt * (-jnp.exp(alogt_ref[...]))
    ri = lax.broadcasted_iota(jnp.int32, (q, q), 0)
    ci = lax.broadcasted_iota(jnp.int32, (q, q), 1)
    causal = ri >= ci
    tri = jnp.where(causal, 1.0, 0.0).astype(BF16)
    trit = jnp.where(ri <= ci, 1.0, 0.0).astype(BF16)
    acum = _dot_exact_lhs01(tri, adt)
    acumt = _dot_exact_rhs01(adtt, trit)
    a_last = acum[q - 1:q, :]

    hh = lax.broadcasted_iota(jnp.int32, (N_SSD_HEADS, D_SSD), 0)
    hl = lax.broadcasted_iota(jnp.int32, (N_SSD_HEADS, D_SSD), 1) // SSD_HEAD_DIM
    expand = jnp.where(hh == hl, 1.0, 0.0).astype(BF16)

    dt_x = _dot_exact_rhs01(dt, expand)
    xdt = xs * dt_x
    xdt_b = xdt.astype(BF16)
    dec_in = _dot_exact_rhs01(jnp.exp(acum), expand)
    dec_out = _dot_exact_rhs01(jnp.exp(a_last - acum), expand)
    dec_all = _dot_exact_rhs01(jnp.exp(a_last), expand)
    xw_b = (xdt * dec_out).astype(BF16)

    lane_lo = lax.broadcasted_iota(jnp.int32, (q, 2 * SSD_HEAD_DIM), 1) < SSD_HEAD_DIM
    gw = HEADS_PER_GROUP * SSD_HEAD_DIM
    y_parts = []
    for g in range(N_GROUPS):
        b_g = b_all[:, g * D_STATE:(g + 1) * D_STATE]
        c_g = c_all[:, g * D_STATE:(g + 1) * D_STATE]
        cb = lax.dot_general(c_g, b_g, (((1,), (1,)), ((), ())), preferred_element_type=F32)
        st = state_ref[:, g * gw:(g + 1) * gw]
        y_inter = _dot(c_g, st.astype(BF16)) * dec_in[:, g * gw:(g + 1) * gw]
        new_st = lax.dot_general(b_g, xw_b[:, g * gw:(g + 1) * gw], (((0,), (0,)), ((), ())),
                                 preferred_element_type=F32)
        state_ref[:, g * gw:(g + 1) * gw] = st * dec_all[:, g * gw:(g + 1) * gw] + new_st
        pair_parts = []
        for pr in range(HEADS_PER_GROUP // 2):
            h0 = g * HEADS_PER_GROUP + 2 * pr
            lo = h0 * SSD_HEAD_DIM
            rhs = xdt_b[:, lo:lo + 2 * SSD_HEAD_DIM]
            ys = []
            for h in (h0, h0 + 1):
                seg = acum[:, h:h + 1] - acumt[h:h + 1, :]
                m = (cb * jnp.exp(jnp.where(causal, seg, -jnp.inf))).astype(BF16)
                ys.append(_dot(m, rhs))
            pair_parts.append(jnp.where(lane_lo, ys[0], ys[1]))
        y_parts.append(jnp.concatenate(pair_parts, axis=-1) + y_inter)
    y = jnp.concatenate(y_parts, axis=-1) + _dot_exact_rhs01(dskip_ref[...], expand) * xs

    gated = y * _silu(z_ref[...].astype(F32))
    outs = []
    for g in range(N_GROUPS):
        gg = gated[:, g * gw:(g + 1) * gw]
        outs.append(gg * lax.rsqrt(jnp.mean(gg * gg, axis=-1, keepdims=True) + EPS))
    y_ssd = jnp.concatenate(outs, axis=-1) * nw_ref[...]
    out_ref[:, :D_SSD] = y_ssd.astype(out_ref.dtype)

    cu = gc_ref[...].astype(F32) * u_ref[...].astype(F32)
    sctail = sctail_ref[...]
    v = jnp.zeros_like(cu)
    for k in range(SC_CONV):
        v = v + scw_ref[k:k + 1, :] * _shifted(cu, sctail, SC_CONV - 1 - k)
    sctail_ref[...] = cu[q - SUBLANES:q]
    gv = gb_ref[...].astype(F32) * v
    y_sc = gv * lax.rsqrt(jnp.mean(gv * gv, axis=-1, keepdims=True) + EPS) * scnw_ref[...]
    out_ref[:, D_SSD:] = y_sc.astype(out_ref.dtype)


def _mixer(z, xbc, gb, gc, u, dt, dtt, batch, seq, cw, cb, dtb, alog, dskip, nw, scw, scnw):
    q = CHUNK
    nc = seq // q
    t = batch * seq
    row = lambda n: pl.BlockSpec((q, n), lambda b, c: (b * nc + c, 0))
    full = lambda a: pl.BlockSpec(a.shape, lambda b, c: (0,) * a.ndim)
    dtbt, alogt = dtb.reshape(N_SSD_HEADS, 1), alog.reshape(N_SSD_HEADS, 1)
    params = (cw, cb, dtb, dtbt, alog, alogt, dskip, nw, scw, scnw)
    return pl.pallas_call(
        _mixer_kernel,
        grid=(batch, nc),
        in_specs=[row(D_SSD), row(D_XBC), row(D_SC), row(D_SC), row(D_SC), row(N_SSD_HEADS),
                  pl.BlockSpec((N_SSD_HEADS, q), lambda b, c: (0, b * nc + c))]
                 + [full(a) for a in params],
        out_specs=row(D_MIX),
        out_shape=jax.ShapeDtypeStruct((t, D_MIX), BF16),
        scratch_shapes=[pltpu.VMEM((D_STATE, D_SSD), F32),
                        pltpu.VMEM((SUBLANES, D_XBC), F32),
                        pltpu.VMEM((SUBLANES, D_SC), F32)],
        compiler_params=pltpu.CompilerParams(
            dimension_semantics=("arbitrary", "arbitrary"), vmem_limit_bytes=VMEM_LIMIT_BYTES),
    )(z, xbc, gb, gc, u, dt, dtt, *params)


def _outproj_kernel(ymix_ref, x_ref, wout_ref, nw_ref, wrh_ref, wrl_ref, br_ref,
                    h1_ref, xn_ref, idx_ref, gate_ref, rank_ref, cnt_ref, run_ref):
    tm = x_ref.shape[0]

    @pl.when(pl.program_id(0) == 0)
    def _():
        run_ref[...] = jnp.zeros_like(run_ref)

    h1 = x_ref[...] + _dot(ymix_ref[...], wout_ref[...])
    h1_ref[...] = h1
    xn = h1 * lax.rsqrt(jnp.mean(h1 * h1, axis=-1, keepdims=True) + EPS) * nw_ref[...]
    for c, words in enumerate(_pack_rows(xn)):
        xn_ref[c] = words

    xh = xn.astype(BF16)
    xl = (xn - xh.astype(F32)).astype(BF16)
    nt = lambda w, a: lax.dot_general(w, a, (((1,), (1,)), ((), ())), preferred_element_type=F32)
    logits = (nt(wrh_ref[...], xh) + nt(wrl_ref[...], xh) + nt(wrh_ref[...], xl)
              + br_ref[...])

    expert = lax.broadcasted_iota(jnp.int32, (N_EXPERTS, tm), 0).astype(F32)
    work = logits
    vals, firsts, sels = [], [], []
    for _ in range(TOP_K):
        m = jnp.max(work, axis=0, keepdims=True)
        first = jnp.min(jnp.where(work == m, expert, float(N_EXPERTS)), axis=0, keepdims=True)
        sel = expert == first
        vals.append(m)
        firsts.append(first)
        sels.append(sel)
        work = jnp.where(sel, -jnp.inf, work)
    exps = [jnp.exp(v - vals[0]) for v in vals]
    denom = exps[0] + exps[1] + exps[2] + exps[3]

    onehot = jnp.zeros((N_EXPERTS, tm), F32)
    for sel in sels:
        onehot = onehot + jnp.where(sel, 1.0, 0.0)
    ri = lax.broadcasted_iota(jnp.int32, (tm, tm), 0)
    ci = lax.broadcasted_iota(jnp.int32, (tm, tm), 1)
    earlier = jnp.where(ri < ci, 1.0, 0.0).astype(BF16)
    before = run_ref[...] + _dot(onehot.astype(BF16), earlier)
    run_ref[...] = run_ref[...] + jnp.sum(onehot, axis=1, keepdims=True)
    cnt_ref[...] = run_ref[...]

    slot = lax.broadcasted_iota(jnp.int32, (TOP_K, tm), 0)
    idx = jnp.zeros((TOP_K, tm), F32)
    gate = jnp.zeros((TOP_K, tm), F32)
    rank = jnp.zeros((TOP_K, tm), F32)
    for k in range(TOP_K):
        r_k = jnp.sum(jnp.where(sels[k], before, 0.0), axis=0, keepdims=True)
        idx = jnp.where(slot == k, firsts[k], idx)
        gate = jnp.where(slot == k, exps[k] / denom, gate)
        rank = jnp.where(slot == k, r_k, rank)
    idx_ref[...] = idx.astype(jnp.int32)
    gate_ref[...] = gate
    rank_ref[...] = rank.astype(jnp.int32)


def _outproj(ymix, x2, w_out, nw, wr_hi, wr_lo, br):
    t = x2.shape[0]
    tm = OUT_TILE
    row = lambda n: pl.BlockSpec((tm, n), lambda i: (i, 0))
    slots = pl.BlockSpec((TOP_K, tm), lambda i: (0, i))
    full = lambda a: pl.BlockSpec(a.shape, lambda i: (0,) * a.ndim)
    return pl.pallas_call(
        _outproj_kernel,
        grid=(t // tm,),
        in_specs=[row(D_MIX), row(D_MODEL), full(w_out), full(nw), full(wr_hi), full(wr_lo),
                  full(br)],
        out_specs=[row(D_MODEL), pl.BlockSpec((ROW_PARTS, tm, LANES), lambda i: (0, i, 0)),
                   slots, slots, slots,
                   pl.BlockSpec((N_EXPERTS, 1), lambda i: (0, 0))],
        out_shape=[jax.ShapeDtypeStruct((t, D_MODEL), F32),
                   jax.ShapeDtypeStruct((ROW_PARTS, t, LANES), jnp.int32),
                   jax.ShapeDtypeStruct((TOP_K, t), jnp.int32),
                   jax.ShapeDtypeStruct((TOP_K, t), F32),
                   jax.ShapeDtypeStruct((TOP_K, t), jnp.int32),
                   jax.ShapeDtypeStruct((N_EXPERTS, 1), F32)],
        scratch_shapes=[pltpu.VMEM((N_EXPERTS, 1), F32)],
        compiler_params=pltpu.CompilerParams(
            dimension_semantics=("arbitrary",), vmem_limit_bytes=VMEM_LIMIT_BYTES),
    )(ymix, x2, w_out, nw, wr_hi, wr_lo, br)


def _dest_kernel(pstart_ref, idx_ref, rank_ref, dest_ref):
    idx = idx_ref[...]
    base = jnp.zeros_like(idx)
    for e in range(N_EXPERTS):
        base = jnp.where(idx == e, pstart_ref[e], base)
    dest_ref[...] = base + rank_ref[...]


def _dest(pstarts, idx, rank):
    full = lambda a: pl.BlockSpec(a.shape, lambda i, ps: (0,) * a.ndim)
    return pl.pallas_call(
        _dest_kernel,
        grid_spec=pltpu.PrefetchScalarGridSpec(
            num_scalar_prefetch=1, grid=(1,),
            in_specs=[full(idx), full(rank)], out_specs=full(idx)),
        out_shape=jax.ShapeDtypeStruct(idx.shape, jnp.int32),
    )(pstarts, idx, rank)


def _sc_mesh():
    return plsc.VectorSubcoreMesh(core_axis_name="core", subcore_axis_name="subcore")


def _part_rows(dest, n_rows):
    off = (jnp.arange(ROW_PARTS, dtype=jnp.int32) * n_rows)[:, None, None]
    return (dest[None] + off).reshape(ROW_PARTS * dest.shape[0], dest.shape[1])


def _sc_dispatch(x, dest, n_rows):
    parts, t, width = x.shape
    n_slots = dest.shape[0]
    win = SC_WINDOW
    nwin = t // win

    @functools.partial(pl.kernel,
                       out_type=jax.ShapeDtypeStruct((parts * n_rows, width), jnp.int32),
                       mesh=_sc_mesh(), scratch_types=[])
    def scatter(x_hbm, idx_hbm, out_hbm):
        def body(x_vmem, idx_vmem):
            for k in range(n_slots):
                pltpu.sync_copy(x_vmem, out_hbm.at[idx_vmem.at[k]])

        pltpu.emit_pipeline(
            body,
            grid=(parts * nwin,),
            in_specs=[pl.BlockSpec((win, width), lambda j: (j, 0)),
                      pl.BlockSpec((n_slots, win), lambda j: (j // nwin, j % nwin))],
            out_specs=[],
            core_axis_name=("core", "subcore"),
            dimension_semantics=(pltpu.PARALLEL,),
        )(x_hbm, idx_hbm)

    xs = scatter(x.reshape(parts * t, width), _part_rows(dest, n_rows))
    return xs.reshape(parts, n_rows, width)


def _sc_collect(ys, dest):
    parts, n_rows, width = ys.shape
    n_slots, t = dest.shape
    win = SC_WINDOW
    nwin = t // win
    idx = _part_rows(dest, n_rows).reshape(parts, n_slots, t).swapaxes(0, 1)
    idx = idx.reshape(n_slots * parts, t)

    @functools.partial(pl.kernel,
                       out_type=jax.ShapeDtypeStruct((n_slots * parts * t, width), jnp.int32),
                       mesh=_sc_mesh(), scratch_types=[])
    def gather(y_hbm, idx_hbm, out_hbm):
        def body(idx_vmem, out_vmem):
            pltpu.sync_copy(y_hbm.at[idx_vmem.at[0]], out_vmem)

        pltpu.emit_pipeline(
            body,
            grid=(n_slots * parts * nwin,),
            in_specs=[pl.BlockSpec((1, win), lambda j: (j // nwin, j % nwin))],
            out_specs=[pl.BlockSpec((win, width), lambda j: (j, 0))],
            core_axis_name=("core", "subcore"),
            dimension_semantics=(pltpu.PARALLEL,),
        )(idx_hbm, out_hbm)

    out = gather(ys.reshape(parts * n_rows, width), idx)
    return out.reshape(n_slots, parts, t, width)


def _expert_kernel(blk_e_ref, blk_n_ref, xs_ref, wgu_ref, bgu_ref, wd_ref, bd_ref, out_ref,
                   wgu_b, wd_b):
    i = pl.program_id(0)
    n_valid = blk_n_ref[i]

    @pl.when((i == 0) | (blk_e_ref[i] != blk_e_ref[jnp.maximum(i - 1, 0)]))
    def _():
        wgu_b[...] = wgu_ref[0].astype(BF16)
        wd_b[...] = wd_ref[0].astype(BF16)

    for r in range(0, ROW_BLOCK, EXPERT_SUB):
        words = [xs_ref[c, r:r + EXPERT_SUB] for c in range(ROW_PARTS)]
        rows = r + lax.broadcasted_iota(jnp.int32, words[0].shape, 0)
        xb = _unpack_rows([jnp.where(rows < n_valid, w, 0) for w in words]).astype(BF16)
        gu = _dot(xb, wgu_b[...]) + bgu_ref[0]
        gate = jnp.minimum(gu[:, :EXPERT_FF], SWIGLU_LIMIT)
        up = jnp.clip(gu[:, EXPERT_FF:], -SWIGLU_LIMIT, SWIGLU_LIMIT)
        hid = (up + 1.0) * (gate * (1.0 / (1.0 + jnp.exp(-SWIGLU_ALPHA * gate))))
        out = _dot(hid.astype(BF16), wd_b[...]) + bd_ref[0]
        for c, w in enumerate(_pack_rows(out)):
            out_ref[c, r:r + EXPERT_SUB] = w


def _experts(blk_e, blk_n, xs, wgu, bgu, wd, bd):
    rb = ROW_BLOCK
    by_expert = lambda a: pl.BlockSpec((1,) + a.shape[1:], lambda i, be, bn: (be[i], 0, 0))
    return pl.pallas_call(
        _expert_kernel,
        grid_spec=pltpu.PrefetchScalarGridSpec(
            num_scalar_prefetch=2, grid=(xs.shape[1] // rb,),
            in_specs=[pl.BlockSpec((ROW_PARTS, rb, LANES), lambda i, be, bn: (0, i, 0)),
                      by_expert(wgu), by_expert(bgu), by_expert(wd), by_expert(bd)],
            out_specs=pl.BlockSpec((ROW_PARTS, rb, LANES), lambda i, be, bn: (0, i, 0)),
            scratch_shapes=[pltpu.VMEM(wgu.shape[1:], BF16), pltpu.VMEM(wd.shape[1:], BF16)]),
        out_shape=jax.ShapeDtypeStruct(xs.shape, xs.dtype),
        compiler_params=pltpu.CompilerParams(
            dimension_semantics=("arbitrary",), vmem_limit_bytes=VMEM_LIMIT_BYTES),
    )(blk_e, blk_n, xs, wgu, bgu, wd, bd)


def _combine_kernel(gate_ref, h1_ref, fw_ref, y0_ref, y1_ref, y2_ref, y3_ref, out_ref):
    gate = gate_ref[...]
    h2 = h1_ref[...]
    for k, y_ref in enumerate((y0_ref, y1_ref, y2_ref, y3_ref)):
        h2 = h2 + gate[:, k:k + 1] * _unpack_rows([y_ref[c] for c in range(ROW_PARTS)])
    out_ref[...] = h2 * lax.rsqrt(jnp.mean(h2 * h2, axis=-1, keepdims=True) + EPS) * fw_ref[...]


def _combine(gates, h1, fw, planes):
    t, d = h1.shape
    tm = COMBINE_TILE
    plane = lambda k: pl.BlockSpec((None, ROW_PARTS, tm, LANES), lambda i: (k, 0, i, 0))
    return pl.pallas_call(
        _combine_kernel,
        grid=(t // tm,),
        in_specs=[pl.BlockSpec((tm, TOP_K), lambda i: (i, 0)),
                  pl.BlockSpec((tm, d), lambda i: (i, 0)),
                  pl.BlockSpec((1, d), lambda i: (0, 0))] + [plane(k) for k in range(TOP_K)],
        out_specs=pl.BlockSpec((tm, d), lambda i: (i, 0)),
        out_shape=jax.ShapeDtypeStruct((t, d), F32),
        compiler_params=pltpu.CompilerParams(
            dimension_semantics=("arbitrary",), vmem_limit_bytes=VMEM_LIMIT_BYTES),
    )(gates, h1, fw, *([planes] * TOP_K))


def _block_plan(counts, n_blocks):
    rb = ROW_BLOCK
    pcounts = (counts + rb - 1) // rb * rb
    pends = jnp.cumsum(pcounts)
    pstarts = pends - pcounts
    block_start = jnp.arange(n_blocks, dtype=jnp.int32) * rb
    blk_e = jnp.minimum(jnp.sum(block_start[:, None] >= pends[None, :], axis=1),
                        N_EXPERTS - 1).astype(jnp.int32)
    blk_n = jnp.clip(counts[blk_e] - (block_start - pstarts[blk_e]), 0, rb).astype(jnp.int32)
    return pstarts.astype(jnp.int32), blk_e, blk_n


def kernel(x, norm1_w, w_in, ssd_conv_w, ssd_conv_b, dt_bias, a_log, d_skip, ssd_norm_w,
           sc_conv_w, sc_norm_w, w_out, norm2_w, w_router, b_router, w_gate_up, b_gate_up,
           w_down, b_down, final_norm_w):
    assert norm1_w.shape[0] == 1, "single-layer problem"
    batch, seq, d = x.shape
    t = batch * seq
    x2 = x.reshape(t, d)
    row = lambda a: a.reshape(1, -1)

    s0, s1, s2 = D_SSD, D_SSD + D_XBC, D_SSD + D_XBC + N_SSD_HEADS
    w = w_in[0]
    w_main = jnp.concatenate([w[:, :s1], w[:, s2:]], axis=1).astype(BF16)
    w_dt = w[:, s1:s2].astype(BF16)
    z, xbc, gb, gc, u, dt, dtt = _inproj(x2, row(norm1_w[0]), w_main, w_dt, w_dt.T)

    ymix = _mixer(z, xbc, gb, gc, u, dt, dtt, batch, seq,
                  ssd_conv_w[0], row(ssd_conv_b[0]), row(dt_bias[0]), row(a_log[0]),
                  row(d_skip[0]), row(ssd_norm_w[0]), sc_conv_w[0], row(sc_norm_w[0]))

    wr = w_router[0].T
    wr_hi = wr.astype(BF16)
    wr_lo = (wr - wr_hi.astype(F32)).astype(BF16)
    h1, xn2, idx, gates, rank, counts = _outproj(
        ymix, x2, w_out[0].astype(BF16), row(norm2_w[0]), wr_hi, wr_lo,
        b_router[0].reshape(-1, 1))

    n_rows = t * TOP_K + N_EXPERTS * ROW_BLOCK
    pstarts, blk_e, blk_n = _block_plan(counts[:, 0].astype(jnp.int32), n_rows // ROW_BLOCK)
    dest = _dest(pstarts, idx, rank)

    xs = _sc_dispatch(xn2, dest, n_rows)
    ys = _experts(blk_e, blk_n, xs, w_gate_up[0], b_gate_up[0][:, None, :],
                  w_down[0], b_down[0][:, None, :])
    planes = _sc_collect(ys, dest)
    out = _combine(gates.T, h1, row(final_norm_w), planes)
    return out.reshape(batch, seq, d)
```

```python
import functools

import jax
import jax.numpy as jnp
from jax import lax
from jax.experimental import pallas as pl
from jax.experimental.pallas import tpu as pltpu
from jax.experimental.pallas import tpu_sc as plsc

D_MODEL = 1024
D_SSD = 1024
SSD_HEAD_DIM = 64
N_SSD_HEADS = 16
N_GROUPS = 2
HEADS_PER_GROUP = 8
D_STATE = 128
SSD_CONV = 4
CHUNK = 128
D_XBC = D_SSD + 2 * N_GROUPS * D_STATE
D_SC = 1024
SC_CONV = 3
D_MIX = D_SSD + D_SC
N_EXPERTS = 32
TOP_K = 4
EXPERT_FF = 1024
SWIGLU_LIMIT = 7.0
SWIGLU_ALPHA = 1.702
EPS = 1e-5

VMEM_LIMIT_BYTES = 56 * 1024 * 1024
SUBLANES = 8
ROW_BLOCK = 512
EXPERT_SUB = 256
IN_TILE = 512
MIX_CHUNKS = 2
OUT_TILE = 512
COMBINE_TILE = 512
LANES = 128
ROW_PARTS = D_MODEL // (2 * LANES)
SC_WINDOW = 128
HIGH_HALF = -65536

F32 = jnp.float32
BF16 = jnp.bfloat16


def _dot(a, b):
    return jnp.dot(a, b, preferred_element_type=F32)


def _split3(v):
    hi = v.astype(BF16)
    r1 = v - hi.astype(F32)
    mid = r1.astype(BF16)
    lo = (r1 - mid.astype(F32)).astype(BF16)
    return hi, mid, lo


def _dot_exact_rhs01(v, m01):
    hi, mid, lo = _split3(v)
    return _dot(hi, m01) + _dot(mid, m01) + _dot(lo, m01)


def _dot_exact_lhs01(m01, v):
    hi, mid, lo = _split3(v)
    return _dot(m01, hi) + _dot(m01, mid) + _dot(m01, lo)


def _silu(v):
    return v * (1.0 / (1.0 + jnp.exp(-v)))


def _softplus(v):
    return jnp.maximum(v, 0.0) + jnp.log1p(jnp.exp(-jnp.abs(v)))


def _pack_rows(v):
    bits = lambda a: lax.bitcast_convert_type(a.astype(BF16).astype(F32), jnp.int32)
    parts = []
    for c in range(ROW_PARTS):
        lo = v[:, (2 * c) * LANES:(2 * c + 1) * LANES]
        hi = v[:, (2 * c + 1) * LANES:(2 * c + 2) * LANES]
        parts.append(bits(hi) | lax.shift_right_logical(bits(lo), 16))
    return parts


def _unpack_rows(parts):
    cols = []
    for w in parts:
        cols.append(lax.bitcast_convert_type(w << 16, F32))
        cols.append(lax.bitcast_convert_type(w & HIGH_HALF, F32))
    return jnp.concatenate(cols, axis=-1)


def _shifted(cur, tail, s):
    if s == 0:
        return cur
    rc = pltpu.roll(cur, s, axis=0)
    rt = pltpu.roll(tail, s, axis=0)
    row = lax.broadcasted_iota(jnp.int32, tail.shape, 0)
    first = jnp.where(row < s, rt, rc[0:SUBLANES])
    return jnp.concatenate([first, rc[SUBLANES:]], axis=0)


def _causal_conv(tail_ref, cur, w_ref, acc):
    rows = cur.shape[0]
    n_taps = w_ref.shape[0]
    tail = tail_ref[...]
    for k in range(n_taps):
        acc = acc + w_ref[k:k + 1, :] * _shifted(cur, tail, n_taps - 1 - k)
    tail_ref[...] = cur[rows - SUBLANES:rows]
    return acc


def _inproj_kernel(tiles_per_seq, x_ref, nw_ref, w_ref, wdt_ref, wdtt_ref,
                   cw_ref, cb_ref, scw_ref, scnw_ref,
                   z_ref, xbc_ref, ysc_ref, dt_ref, dtt_ref, hist_ref, schist_ref):
    tm = x_ref.shape[0]

    @pl.when(pl.program_id(0) % tiles_per_seq == 0)
    def _():
        hist_ref[...] = jnp.zeros_like(hist_ref)
        schist_ref[...] = jnp.zeros_like(schist_ref)

    x = x_ref[...]
    xn = x * lax.rsqrt(jnp.mean(x * x, axis=-1, keepdims=True) + EPS) * nw_ref[...]
    xb = xn.astype(BF16)
    proj = lambda o, n: _dot(xb, w_ref[:, o:o + n])
    z_ref[...] = proj(0, D_SSD).astype(z_ref.dtype)

    conv = _causal_conv(hist_ref, proj(D_SSD, D_XBC), cw_ref, cb_ref[...])
    xbc_ref[...] = _silu(conv).astype(xbc_ref.dtype)

    o = D_SSD + D_XBC
    cu = proj(o + D_SC, D_SC) * proj(o + 2 * D_SC, D_SC)
    v = _causal_conv(schist_ref, cu, scw_ref, jnp.zeros_like(cu))
    gv = proj(o, D_SC) * v
    y_sc = gv * lax.rsqrt(jnp.mean(gv * gv, axis=-1, keepdims=True) + EPS) * scnw_ref[...]
    ysc_ref[...] = y_sc.astype(ysc_ref.dtype)

    dt_ref[...] = _dot(xb, wdt_ref[...])
    dtt_ref[...] = lax.dot_general(wdtt_ref[...], xb, (((1,), (1,)), ((), ())),
                                   preferred_element_type=F32)


def _inproj(x2, seq, nw, w_main, w_dt, w_dtt, cw, cb, scw, scnw):
    t = x2.shape[0]
    tm = IN_TILE
    assert seq % tm == 0, "in-proj tiles must not straddle sequences (causal conv carry)"
    row = lambda n: pl.BlockSpec((tm, n), lambda i: (i, 0))
    full = lambda a: pl.BlockSpec(a.shape, lambda i: (0,) * a.ndim)
    params = (nw, w_main, w_dt, w_dtt, cw, cb, scw, scnw)
    return pl.pallas_call(
        functools.partial(_inproj_kernel, seq // tm),
        grid=(t // tm,),
        in_specs=[row(D_MODEL)] + [full(a) for a in params],
        out_specs=[row(D_SSD), row(D_XBC), row(D_SC),
                   row(N_SSD_HEADS), pl.BlockSpec((N_SSD_HEADS, tm), lambda i: (0, i))],
        out_shape=[jax.ShapeDtypeStruct((t, D_SSD), BF16),
                   jax.ShapeDtypeStruct((t, D_XBC), BF16),
                   jax.ShapeDtypeStruct((t, D_SC), BF16),
                   jax.ShapeDtypeStruct((t, N_SSD_HEADS), F32),
                   jax.ShapeDtypeStruct((N_SSD_HEADS, t), F32)],
        scratch_shapes=[pltpu.VMEM((SUBLANES, D_XBC), F32), pltpu.VMEM((SUBLANES, D_SC), F32)],
        compiler_params=pltpu.CompilerParams(
            dimension_semantics=("arbitrary",), vmem_limit_bytes=VMEM_LIMIT_BYTES),
    )(x2, *params)


def _mixer_kernel(z_ref, xbc_ref, dt_ref, dtt_ref,
                  dtb_ref, dtbt_ref, alog_ref, alogt_ref, dskip_ref, nw_ref, out_ref, state_ref):
    q = CHUNK

    @pl.when(pl.program_id(1) == 0)
    def _():
        state_ref[...] = jnp.zeros_like(state_ref)

    ri = lax.broadcasted_iota(jnp.int32, (q, q), 0)
    ci = lax.broadcasted_iota(jnp.int32, (q, q), 1)
    causal = ri >= ci
    tri = jnp.where(causal, 1.0, 0.0).astype(BF16)
    trit = jnp.where(ri <= ci, 1.0, 0.0).astype(BF16)
    hh = lax.broadcasted_iota(jnp.int32, (N_SSD_HEADS, D_SSD), 0)
    hl = lax.broadcasted_iota(jnp.int32, (N_SSD_HEADS, D_SSD), 1) // SSD_HEAD_DIM
    expand = jnp.where(hh == hl, 1.0, 0.0).astype(BF16)
    lane_lo = lax.broadcasted_iota(jnp.int32, (q, 2 * SSD_HEAD_DIM), 1) < SSD_HEAD_DIM
    gw = HEADS_PER_GROUP * SSD_HEAD_DIM
    n_bc = N_GROUPS * D_STATE
    a_neg = -jnp.exp(alog_ref[...])
    a_negt = -jnp.exp(alogt_ref[...])
    d_x = _dot_exact_rhs01(dskip_ref[...], expand)

    for s in range(MIX_CHUNKS):
        rows = pl.ds(s * q, q)
        xs = xbc_ref[rows, :D_SSD].astype(F32)
        b_all = xbc_ref[rows, D_SSD:D_SSD + n_bc]
        c_all = xbc_ref[rows, D_SSD + n_bc:]

        dt = _softplus(dt_ref[rows, :] + dtb_ref[...])
        dtt = _softplus(dtt_ref[:, rows] + dtbt_ref[...])
        acum = _dot_exact_lhs01(tri, dt * a_neg)
        acumt = _dot_exact_rhs01(dtt * a_negt, trit)
        a_last = acum[q - 1:q, :]

        xdt = xs * _dot_exact_rhs01(dt, expand)
        xdt_b = xdt.astype(BF16)
        dec_in = _dot_exact_rhs01(jnp.exp(acum), expand)
        dec_out = _dot_exact_rhs01(jnp.exp(a_last - acum), expand)
        dec_all = _dot_exact_rhs01(jnp.exp(a_last), expand)
        xw_b = (xdt * dec_out).astype(BF16)

        y_parts = []
        for g in range(N_GROUPS):
            b_g = b_all[:, g * D_STATE:(g + 1) * D_STATE]
            c_g = c_all[:, g * D_STATE:(g + 1) * D_STATE]
            cb = lax.dot_general(c_g, b_g, (((1,), (1,)), ((), ())), preferred_element_type=F32)
            st = state_ref[:, g * gw:(g + 1) * gw]
            y_inter = _dot(c_g, st.astype(BF16)) * dec_in[:, g * gw:(g + 1) * gw]
            new_st = lax.dot_general(b_g, xw_b[:, g * gw:(g + 1) * gw], (((0,), (0,)), ((), ())),
                                     preferred_element_type=F32)
            state_ref[:, g * gw:(g + 1) * gw] = st * dec_all[:, g * gw:(g + 1) * gw] + new_st
            pair_parts = []
            for pr in range(HEADS_PER_GROUP // 2):
                h0 = g * HEADS_PER_GROUP + 2 * pr
                lo = h0 * SSD_HEAD_DIM
                rhs = xdt_b[:, lo:lo + 2 * SSD_HEAD_DIM]
                ys = []
                for h in (h0, h0 + 1):
                    seg = acum[:, h:h + 1] - acumt[h:h + 1, :]
                    m = (cb * jnp.exp(jnp.where(causal, seg, -jnp.inf))).astype(BF16)
                    ys.append(_dot(m, rhs))
                pair_parts.append(jnp.where(lane_lo, ys[0], ys[1]))
            y_parts.append(jnp.concatenate(pair_parts, axis=-1) + y_inter)
        y = jnp.concatenate(y_parts, axis=-1) + d_x * xs

        gated = y * _silu(z_ref[rows, :].astype(F32))
        outs = []
        for g in range(N_GROUPS):
            gg = gated[:, g * gw:(g + 1) * gw]
            outs.append(gg * lax.rsqrt(jnp.mean(gg * gg, axis=-1, keepdims=True) + EPS))
        out_ref[rows, :] = (jnp.concatenate(outs, axis=-1) * nw_ref[...]).astype(out_ref.dtype)


def _mixer(z, xbc, dt, dtt, batch, seq, dtb, alog, dskip, nw):
    rows = CHUNK * MIX_CHUNKS
    assert seq % rows == 0
    ns = seq // rows
    t = batch * seq
    row = lambda n: pl.BlockSpec((rows, n), lambda b, c: (b * ns + c, 0))
    full = lambda a: pl.BlockSpec(a.shape, lambda b, c: (0,) * a.ndim)
    dtbt, alogt = dtb.reshape(N_SSD_HEADS, 1), alog.reshape(N_SSD_HEADS, 1)
    params = (dtb, dtbt, alog, alogt, dskip, nw)
    return pl.pallas_call(
        _mixer_kernel,
        grid=(batch, ns),
        in_specs=[row(D_SSD), row(D_XBC), row(N_SSD_HEADS),
                  pl.BlockSpec((N_SSD_HEADS, rows), lambda b, c: (0, b * ns + c))]
                 + [full(a) for a in params],
        out_specs=row(D_SSD),
        out_shape=jax.ShapeDtypeStruct((t, D_SSD), BF16),
        scratch_shapes=[pltpu.VMEM((D_STATE, D_SSD), F32)],
        compiler_params=pltpu.CompilerParams(
            dimension_semantics=("arbitrary", "arbitrary"), vmem_limit_bytes=VMEM_LIMIT_BYTES),
    )(z, xbc, dt, dtt, *params)


def _outproj_kernel(yssd_ref, ysc_ref, x_ref, wout_ref, nw_ref, wrh_ref, wrl_ref, br_ref,
                    h1_ref, xn_ref, idx_ref, gate_ref, rank_ref, cnt_ref, run_ref):
    tm = x_ref.shape[0]

    @pl.when(pl.program_id(0) == 0)
    def _():
        run_ref[...] = jnp.zeros_like(run_ref)

    h1 = (x_ref[...] + _dot(yssd_ref[...], wout_ref[:D_SSD, :])
          + _dot(ysc_ref[...], wout_ref[D_SSD:, :]))
    h1_ref[...] = h1
    xn = h1 * lax.rsqrt(jnp.mean(h1 * h1, axis=-1, keepdims=True) + EPS) * nw_ref[...]
    for c, words in enumerate(_pack_rows(xn)):
        xn_ref[c] = words

    xh = xn.astype(BF16)
    xl = (xn - xh.astype(F32)).astype(BF16)
    nt = lambda w, a: lax.dot_general(w, a, (((1,), (1,)), ((), ())), preferred_element_type=F32)
    logits = (nt(wrh_ref[...], xh) + nt(wrl_ref[...], xh) + nt(wrh_ref[...], xl)
              + br_ref[...])

    expert = lax.broadcasted_iota(jnp.int32, (N_EXPERTS, tm), 0).astype(F32)
    work = logits
    vals, firsts, sels = [], [], []
    for _ in range(TOP_K):
        m = jnp.max(work, axis=0, keepdims=True)
        first = jnp.min(jnp.where(work == m, expert, float(N_EXPERTS)), axis=0, keepdims=True)
        sel = expert == first
        vals.append(m)
        firsts.append(first)
        sels.append(sel)
        work = jnp.where(sel, -jnp.inf, work)
    exps = [jnp.exp(v - vals[0]) for v in vals]
    denom = exps[0] + exps[1] + exps[2] + exps[3]

    onehot = jnp.zeros((N_EXPERTS, tm), F32)
    for sel in sels:
        onehot = onehot + jnp.where(sel, 1.0, 0.0)
    ri = lax.broadcasted_iota(jnp.int32, (tm, tm), 0)
    ci = lax.broadcasted_iota(jnp.int32, (tm, tm), 1)
    earlier = jnp.where(ri < ci, 1.0, 0.0).astype(BF16)
    before = run_ref[...] + _dot(onehot.astype(BF16), earlier)
    run_ref[...] = run_ref[...] + jnp.sum(onehot, axis=1, keepdims=True)
    cnt_ref[...] = run_ref[...]

    slot = lax.broadcasted_iota(jnp.int32, (TOP_K, tm), 0)
    idx = jnp.zeros((TOP_K, tm), F32)
    gate = jnp.zeros((TOP_K, tm), F32)
    rank = jnp.zeros((TOP_K, tm), F32)
    for k in range(TOP_K):
        r_k = jnp.sum(jnp.where(sels[k], before, 0.0), axis=0, keepdims=True)
        idx = jnp.where(slot == k, firsts[k], idx)
        gate = jnp.where(slot == k, exps[k] / denom, gate)
        rank = jnp.where(slot == k, r_k, rank)
    idx_ref[...] = idx.astype(jnp.int32)
    gate_ref[...] = gate
    rank_ref[...] = rank.astype(jnp.int32)


def _outproj(yssd, ysc, x2, w_out, nw, wr_hi, wr_lo, br):
    t = x2.shape[0]
    tm = OUT_TILE
    row = lambda n: pl.BlockSpec((tm, n), lambda i: (i, 0))
    slots = pl.BlockSpec((TOP_K, tm), lambda i: (0, i))
    full = lambda a: pl.BlockSpec(a.shape, lambda i: (0,) * a.ndim)
    return pl.pallas_call(
        _outproj_kernel,
        grid=(t // tm,),
        in_specs=[row(D_SSD), row(D_SC), row(D_MODEL), full(w_out), full(nw), full(wr_hi),
                  full(wr_lo), full(br)],
        out_specs=[row(D_MODEL), pl.BlockSpec((ROW_PARTS, tm, LANES), lambda i: (0, i, 0)),
                   slots, slots, slots,
                   pl.BlockSpec((N_EXPERTS, 1), lambda i: (0, 0))],
        out_shape=[jax.ShapeDtypeStruct((t, D_MODEL), F32),
                   jax.ShapeDtypeStruct((ROW_PARTS, t, LANES), jnp.int32),
                   jax.ShapeDtypeStruct((TOP_K, t), jnp.int32),
                   jax.ShapeDtypeStruct((TOP_K, t), F32),
                   jax.ShapeDtypeStruct((TOP_K, t), jnp.int32),
                   jax.ShapeDtypeStruct((N_EXPERTS, 1), F32)],
        scratch_shapes=[pltpu.VMEM((N_EXPERTS, 1), F32)],
        compiler_params=pltpu.CompilerParams(
            dimension_semantics=("arbitrary",), vmem_limit_bytes=VMEM_LIMIT_BYTES),
    )(yssd, ysc, x2, w_out, nw, wr_hi, wr_lo, br)


def _dest_kernel(pstart_ref, idx_ref, rank_ref, dest_ref):
    idx = idx_ref[...]
    base = jnp.zeros_like(idx)
    for e in range(N_EXPERTS):
        base = jnp.where(idx == e, pstart_ref[e], base)
    dest_ref[...] = base + rank_ref[...]


def _dest(pstarts, idx, rank):
    full = lambda a: pl.BlockSpec(a.shape, lambda i, ps: (0,) * a.ndim)
    return pl.pallas_call(
        _dest_kernel,
        grid_spec=pltpu.PrefetchScalarGridSpec(
            num_scalar_prefetch=1, grid=(1,),
            in_specs=[full(idx), full(rank)], out_specs=full(idx)),
        out_shape=jax.ShapeDtypeStruct(idx.shape, jnp.int32),
    )(pstarts, idx, rank)


def _sc_mesh():
    return plsc.VectorSubcoreMesh(core_axis_name="core", subcore_axis_name="subcore")


def _part_rows(dest, n_rows):
    off = (jnp.arange(ROW_PARTS, dtype=jnp.int32) * n_rows)[:, None, None]
    return (dest[None] + off).reshape(ROW_PARTS * dest.shape[0], dest.shape[1])


def _sc_dispatch(x, dest, n_rows):
    parts, t, width = x.shape
    n_slots = dest.shape[0]
    win = SC_WINDOW
    nwin = t // win

    @functools.partial(pl.kernel,
                       out_type=jax.ShapeDtypeStruct((parts * n_rows, width), jnp.int32),
                       mesh=_sc_mesh(), scratch_types=[])
    def scatter(x_hbm, idx_hbm, out_hbm):
        def body(x_vmem, idx_vmem):
            for k in range(n_slots):
                pltpu.sync_copy(x_vmem, out_hbm.at[idx_vmem.at[k]])

        pltpu.emit_pipeline(
            body,
            grid=(parts * nwin,),
            in_specs=[pl.BlockSpec((win, width), lambda j: (j, 0)),
                      pl.BlockSpec((n_slots, win), lambda j: (j // nwin, j % nwin))],
            out_specs=[],
            core_axis_name=("core", "subcore"),
            dimension_semantics=(pltpu.PARALLEL,),
        )(x_hbm, idx_hbm)

    xs = scatter(x.reshape(parts * t, width), _part_rows(dest, n_rows))
    return xs.reshape(parts, n_rows, width)


def _sc_collect(ys, dest):
    parts, n_rows, width = ys.shape
    n_slots, t = dest.shape
    win = SC_WINDOW
    nwin = t // win
    idx = _part_rows(dest, n_rows).reshape(parts, n_slots, t).swapaxes(0, 1)
    idx = idx.reshape(n_slots * parts, t)

    @functools.partial(pl.kernel,
                       out_type=jax.ShapeDtypeStruct((n_slots * parts * t, width), jnp.int32),
                       mesh=_sc_mesh(), scratch_types=[])
    def gather(y_hbm, idx_hbm, out_hbm):
        def body(idx_vmem, out_vmem):
            pltpu.sync_copy(y_hbm.at[idx_vmem.at[0]], out_vmem)

        pltpu.emit_pipeline(
            body,
            grid=(n_slots * parts * nwin,),
            in_specs=[pl.BlockSpec((1, win), lambda j: (j // nwin, j % nwin))],
            out_specs=[pl.BlockSpec((win, width), lambda j: (j, 0))],
            core_axis_name=("core", "subcore"),
            dimension_semantics=(pltpu.PARALLEL,),
        )(idx_hbm, out_hbm)

    out = gather(ys.reshape(parts * n_rows, width), idx)
    return out.reshape(n_slots, parts, t, width)


def _expert_kernel(blk_e_ref, blk_n_ref, xs_ref, wgu_ref, bgu_ref, wd_ref, bd_ref, out_ref,
                   wgu_b, wd_b):
    i = pl.program_id(0)
    n_valid = blk_n_ref[i]

    @pl.when((i == 0) | (blk_e_ref[i] != blk_e_ref[jnp.maximum(i - 1, 0)]))
    def _():
        wgu_b[...] = wgu_ref[0].astype(BF16)
        wd_b[...] = wd_ref[0].astype(BF16)

    @pl.when(n_valid == 0)
    def _():
        out_ref[...] = jnp.zeros_like(out_ref)

    @pl.when(n_valid > 0)
    def _():
        for r in range(0, ROW_BLOCK, EXPERT_SUB):
            words = [xs_ref[c, r:r + EXPERT_SUB] for c in range(ROW_PARTS)]
            rows = r + lax.broadcasted_iota(jnp.int32, words[0].shape, 0)
            xb = _unpack_rows([jnp.where(rows < n_valid, w, 0) for w in words]).astype(BF16)
            gu = _dot(xb, wgu_b[...]) + bgu_ref[0]
            gate = jnp.minimum(gu[:, :EXPERT_FF], SWIGLU_LIMIT)
            up = jnp.clip(gu[:, EXPERT_FF:], -SWIGLU_LIMIT, SWIGLU_LIMIT)
            hid = (up + 1.0) * (gate * (1.0 / (1.0 + jnp.exp(-SWIGLU_ALPHA * gate))))
            out = _dot(hid.astype(BF16), wd_b[...]) + bd_ref[0]
            for c, w in enumerate(_pack_rows(out)):
                out_ref[c, r:r + EXPERT_SUB] = w


def _experts(blk_e, blk_n, xs, wgu, bgu, wd, bd):
    rb = ROW_BLOCK
    by_expert = lambda a: pl.BlockSpec((1,) + a.shape[1:], lambda i, be, bn: (be[i], 0, 0))
    return pl.pallas_call(
        _expert_kernel,
        grid_spec=pltpu.PrefetchScalarGridSpec(
            num_scalar_prefetch=2, grid=(xs.shape[1] // rb,),
            in_specs=[pl.BlockSpec((ROW_PARTS, rb, LANES), lambda i, be, bn: (0, i, 0)),
                      by_expert(wgu), by_expert(bgu), by_expert(wd), by_expert(bd)],
            out_specs=pl.BlockSpec((ROW_PARTS, rb, LANES), lambda i, be, bn: (0, i, 0)),
            scratch_shapes=[pltpu.VMEM(wgu.shape[1:], BF16), pltpu.VMEM(wd.shape[1:], BF16)]),
        out_shape=jax.ShapeDtypeStruct(xs.shape, xs.dtype),
        compiler_params=pltpu.CompilerParams(
            dimension_semantics=("arbitrary",), vmem_limit_bytes=VMEM_LIMIT_BYTES),
    )(blk_e, blk_n, xs, wgu, bgu, wd, bd)


def _combine_kernel(gate_ref, h1_ref, fw_ref, y0_ref, y1_ref, y2_ref, y3_ref, out_ref):
    gate = gate_ref[...]
    h2 = h1_ref[...]
    for k, y_ref in enumerate((y0_ref, y1_ref, y2_ref, y3_ref)):
        h2 = h2 + gate[:, k:k + 1] * _unpack_rows([y_ref[c] for c in range(ROW_PARTS)])
    out_ref[...] = h2 * lax.rsqrt(jnp.mean(h2 * h2, axis=-1, keepdims=True) + EPS) * fw_ref[...]


def _combine(gates, h1, fw, planes):
    t, d = h1.shape
    tm = COMBINE_TILE
    plane = lambda k: pl.BlockSpec((None, ROW_PARTS, tm, LANES), lambda i: (k, 0, i, 0))
    return pl.pallas_call(
        _combine_kernel,
        grid=(t // tm,),
        in_specs=[pl.BlockSpec((tm, TOP_K), lambda i: (i, 0)),
                  pl.BlockSpec((tm, d), lambda i: (i, 0)),
                  pl.BlockSpec((1, d), lambda i: (0, 0))] + [plane(k) for k in range(TOP_K)],
        out_specs=pl.BlockSpec((tm, d), lambda i: (i, 0)),
        out_shape=jax.ShapeDtypeStruct((t, d), F32),
        compiler_params=pltpu.CompilerParams(
            dimension_semantics=("arbitrary",), vmem_limit_bytes=VMEM_LIMIT_BYTES),
    )(gates, h1, fw, *([planes] * TOP_K))


def _block_plan(counts, n_blocks):
    rb = ROW_BLOCK
    pcounts = (counts + rb - 1) // rb * rb
    pends = jnp.cumsum(pcounts)
    pstarts = pends - pcounts
    block_start = jnp.arange(n_blocks, dtype=jnp.int32) * rb
    blk_e = jnp.minimum(jnp.sum(block_start[:, None] >= pends[None, :], axis=1),
                        N_EXPERTS - 1).astype(jnp.int32)
    blk_n = jnp.clip(counts[blk_e] - (block_start - pstarts[blk_e]), 0, rb).astype(jnp.int32)
    return pstarts.astype(jnp.int32), blk_e, blk_n


def kernel(x, norm1_w, w_in, ssd_conv_w, ssd_conv_b, dt_bias, a_log, d_skip, ssd_norm_w,
           sc_conv_w, sc_norm_w, w_out, norm2_w, w_router, b_router, w_gate_up, b_gate_up,
           w_down, b_down, final_norm_w):
    assert norm1_w.shape[0] == 1, "single-layer problem"
    batch, seq, d = x.shape
    t = batch * seq
    x2 = x.reshape(t, d)
    row = lambda a: a.reshape(1, -1)

    s0, s1, s2 = D_SSD, D_SSD + D_XBC, D_SSD + D_XBC + N_SSD_HEADS
    w = w_in[0]
    w_main = jnp.concatenate([w[:, :s1], w[:, s2:]], axis=1).astype(BF16)
    w_dt = w[:, s1:s2].astype(BF16)
    z, xbc, ysc, dt, dtt = _inproj(x2, seq, row(norm1_w[0]), w_main, w_dt, w_dt.T,
                                   ssd_conv_w[0], row(ssd_conv_b[0]), sc_conv_w[0],
                                   row(sc_norm_w[0]))
    yssd = _mixer(z, xbc, dt, dtt, batch, seq, row(dt_bias[0]), row(a_log[0]), row(d_skip[0]),
                  row(ssd_norm_w[0]))

    wr = w_router[0].T
    wr_hi = wr.astype(BF16)
    wr_lo = (wr - wr_hi.astype(F32)).astype(BF16)
    h1, xn2, idx, gates, rank, counts = _outproj(
        yssd, ysc, x2, w_out[0].astype(BF16), row(norm2_w[0]), wr_hi, wr_lo,
        b_router[0].reshape(-1, 1))

    n_rows = t * TOP_K + N_EXPERTS * ROW_BLOCK
    pstarts, blk_e, blk_n = _block_plan(counts[:, 0].astype(jnp.int32), n_rows // ROW_BLOCK)
    dest = _dest(pstarts, idx, rank)

    xs = _sc_dispatch(xn2, dest, n_rows)
    ys = _experts(blk_e, blk_n, xs, w_gate_up[0], b_gate_up[0][:, None, :],
                  w_down[0], b_down[0][:, None, :])
    planes = _sc_collect(ys, dest)
    out = _combine(gates.T, h1, row(final_norm_w), planes)
    return out.reshape(batch, seq, d)
```

```python
import functools

import jax
import jax.numpy as jnp
from jax import lax
from jax.experimental import pallas as pl
from jax.experimental.pallas import tpu as pltpu
from jax.experimental.pallas import tpu_sc as plsc

D_MODEL = 1024
D_SSD = 1024
SSD_HEAD_DIM = 64
N_SSD_HEADS = 16
N_GROUPS = 2
HEADS_PER_GROUP = 8
D_STATE = 128
SSD_CONV = 4
CHUNK = 128
D_XBC = D_SSD + 2 * N_GROUPS * D_STATE
D_SC = 1024
SC_CONV = 3
D_MIX = D_SSD + D_SC
N_EXPERTS = 32
TOP_K = 4
EXPERT_FF = 1024
SWIGLU_LIMIT = 7.0
SWIGLU_ALPHA = 1.702
EPS = 1e-5

VMEM_LIMIT_BYTES = 56 * 1024 * 1024
SUBLANES = 8
ROW_BLOCK = 512
EXPERT_SUB = 256
IN_TILE = 512
IN_SUB = 256
MIX_CHUNKS = 2
OUT_TILE = 512
COMBINE_TILE = 512
COMBINE_CHUNKS = 4
LANES = 128
ROW_PARTS = D_MODEL // (2 * LANES)
SC_WINDOW = 128
HIGH_HALF = -65536

F32 = jnp.float32
BF16 = jnp.bfloat16


def _dot(a, b):
    return jnp.dot(a, b, preferred_element_type=F32)


def _split3(v):
    hi = v.astype(BF16)
    r1 = v - hi.astype(F32)
    mid = r1.astype(BF16)
    lo = (r1 - mid.astype(F32)).astype(BF16)
    return hi, mid, lo


def _dot_exact_rhs01(v, m01):
    hi, mid, lo = _split3(v)
    return _dot(hi, m01) + _dot(mid, m01) + _dot(lo, m01)


def _dot_exact_lhs01(m01, v):
    hi, mid, lo = _split3(v)
    return _dot(m01, hi) + _dot(m01, mid) + _dot(m01, lo)


def _silu(v):
    return v * (1.0 / (1.0 + jnp.exp(-v)))


def _softplus(v):
    return jnp.maximum(v, 0.0) + jnp.log1p(jnp.exp(-jnp.abs(v)))


def _pack_rows(v):
    bits = lambda a: lax.bitcast_convert_type(a.astype(BF16).astype(F32), jnp.int32)
    parts = []
    for c in range(ROW_PARTS):
        lo = v[:, (2 * c) * LANES:(2 * c + 1) * LANES]
        hi = v[:, (2 * c + 1) * LANES:(2 * c + 2) * LANES]
        parts.append(bits(hi) | lax.shift_right_logical(bits(lo), 16))
    return parts


def _unpack_rows(parts):
    cols = []
    for w in parts:
        cols.append(lax.bitcast_convert_type(w << 16, F32))
        cols.append(lax.bitcast_convert_type(w & HIGH_HALF, F32))
    return jnp.concatenate(cols, axis=-1)


def _shifted(cur, tail, s):
    if s == 0:
        return cur
    rc = pltpu.roll(cur, s, axis=0)
    rt = pltpu.roll(tail, s, axis=0)
    row = lax.broadcasted_iota(jnp.int32, tail.shape, 0)
    first = jnp.where(row < s, rt, rc[0:SUBLANES])
    return jnp.concatenate([first, rc[SUBLANES:]], axis=0)


def _causal_conv(tail_ref, cur, w_ref, acc):
    rows = cur.shape[0]
    n_taps = w_ref.shape[0]
    tail = tail_ref[...]
    for k in range(n_taps):
        acc = acc + w_ref[k:k + 1, :] * _shifted(cur, tail, n_taps - 1 - k)
    tail_ref[...] = cur[rows - SUBLANES:rows]
    return acc


def _inproj_kernel(tiles_per_seq, x_ref, nw_ref, w_ref, wdt_ref, wdtt_ref,
                   cw_ref, cb_ref, scw_ref, scnw_ref,
                   z_ref, xbc_ref, ysc_ref, dt_ref, dtt_ref, hist_ref, schist_ref):
    tm = x_ref.shape[0]

    @pl.when(pl.program_id(0) % tiles_per_seq == 0)
    def _():
        hist_ref[...] = jnp.zeros_like(hist_ref)
        schist_ref[...] = jnp.zeros_like(schist_ref)

    for r in range(0, tm, IN_SUB):
        rows = pl.ds(r, IN_SUB)
        x = x_ref[rows, :]
        xn = x * lax.rsqrt(jnp.mean(x * x, axis=-1, keepdims=True) + EPS) * nw_ref[...]
        xb = xn.astype(BF16)
        proj = lambda o, n, xb=xb: _dot(xb, w_ref[:, o:o + n])
        z_ref[rows, :] = proj(0, D_SSD).astype(z_ref.dtype)

        conv = _causal_conv(hist_ref, proj(D_SSD, D_XBC), cw_ref, cb_ref[...])
        xbc_ref[rows, :] = _silu(conv).astype(xbc_ref.dtype)

        o = D_SSD + D_XBC
        cu = proj(o + D_SC, D_SC) * proj(o + 2 * D_SC, D_SC)
        v = _causal_conv(schist_ref, cu, scw_ref, jnp.zeros_like(cu))
        gv = proj(o, D_SC) * v
        y_sc = gv * lax.rsqrt(jnp.mean(gv * gv, axis=-1, keepdims=True) + EPS) * scnw_ref[...]
        ysc_ref[rows, :] = y_sc.astype(ysc_ref.dtype)

        dt_ref[rows, :] = _dot(xb, wdt_ref[...])
        dtt_ref[:, rows] = lax.dot_general(wdtt_ref[...], xb, (((1,), (1,)), ((), ())),
                                           preferred_element_type=F32)


def _inproj(x2, seq, nw, w_main, w_dt, w_dtt, cw, cb, scw, scnw):
    t = x2.shape[0]
    tm = IN_TILE
    assert seq % tm == 0, "in-proj tiles must not straddle sequences (causal conv carry)"
    row = lambda n: pl.BlockSpec((tm, n), lambda i: (i, 0))
    full = lambda a: pl.BlockSpec(a.shape, lambda i: (0,) * a.ndim)
    params = (nw, w_main, w_dt, w_dtt, cw, cb, scw, scnw)
    return pl.pallas_call(
        functools.partial(_inproj_kernel, seq // tm),
        grid=(t // tm,),
        in_specs=[row(D_MODEL)] + [full(a) for a in params],
        out_specs=[row(D_SSD), row(D_XBC), row(D_SC),
                   row(N_SSD_HEADS), pl.BlockSpec((N_SSD_HEADS, tm), lambda i: (0, i))],
        out_shape=[jax.ShapeDtypeStruct((t, D_SSD), BF16),
                   jax.ShapeDtypeStruct((t, D_XBC), BF16),
                   jax.ShapeDtypeStruct((t, D_SC), BF16),
                   jax.ShapeDtypeStruct((t, N_SSD_HEADS), F32),
                   jax.ShapeDtypeStruct((N_SSD_HEADS, t), F32)],
        scratch_shapes=[pltpu.VMEM((SUBLANES, D_XBC), F32), pltpu.VMEM((SUBLANES, D_SC), F32)],
        compiler_params=pltpu.CompilerParams(
            dimension_semantics=("arbitrary",), vmem_limit_bytes=VMEM_LIMIT_BYTES),
    )(x2, *params)


def _mixer_kernel(z_ref, xbc_ref, dt_ref, dtt_ref,
                  dtb_ref, dtbt_ref, alog_ref, alogt_ref, dskip_ref, nw_ref, out_ref, state_ref):
    q = CHUNK

    @pl.when(pl.program_id(1) == 0)
    def _():
        state_ref[...] = jnp.zeros_like(state_ref)

    ri = lax.broadcasted_iota(jnp.int32, (q, q), 0)
    ci = lax.broadcasted_iota(jnp.int32, (q, q), 1)
    causal = ri >= ci
    tri = jnp.where(causal, 1.0, 0.0).astype(BF16)
    trit = jnp.where(ri <= ci, 1.0, 0.0).astype(BF16)
    hh = lax.broadcasted_iota(jnp.int32, (N_SSD_HEADS, D_SSD), 0)
    hl = lax.broadcasted_iota(jnp.int32, (N_SSD_HEADS, D_SSD), 1) // SSD_HEAD_DIM
    expand = jnp.where(hh == hl, 1.0, 0.0).astype(BF16)
    lane_lo = lax.broadcasted_iota(jnp.int32, (q, 2 * SSD_HEAD_DIM), 1) < SSD_HEAD_DIM
    gw = HEADS_PER_GROUP * SSD_HEAD_DIM
    n_bc = N_GROUPS * D_STATE
    a_neg = -jnp.exp(alog_ref[...])
    a_negt = -jnp.exp(alogt_ref[...])
    d_x = _dot_exact_rhs01(dskip_ref[...], expand)

    for s in range(MIX_CHUNKS):
        rows = pl.ds(s * q, q)
        xs = xbc_ref[rows, :D_SSD].astype(F32)
        b_all = xbc_ref[rows, D_SSD:D_SSD + n_bc]
        c_all = xbc_ref[rows, D_SSD + n_bc:]

        dt = _softplus(dt_ref[rows, :] + dtb_ref[...])
        dtt = _softplus(dtt_ref[:, rows] + dtbt_ref[...])
        acum = _dot_exact_lhs01(tri, dt * a_neg)
        acumt = _dot_exact_rhs01(dtt * a_negt, trit)
        a_last = acum[q - 1:q, :]

        xdt = xs * _dot_exact_rhs01(dt, expand)
        xdt_b = xdt.astype(BF16)
        dec_in = _dot_exact_rhs01(jnp.exp(acum), expand)
        dec_out = _dot_exact_rhs01(jnp.exp(a_last - acum), expand)
        dec_all = _dot_exact_rhs01(jnp.exp(a_last), expand)
        xw_b = (xdt * dec_out).astype(BF16)

        y_parts = []
        for g in range(N_GROUPS):
            b_g = b_all[:, g * D_STATE:(g + 1) * D_STATE]
            c_g = c_all[:, g * D_STATE:(g + 1) * D_STATE]
            cb = lax.dot_general(c_g, b_g, (((1,), (1,)), ((), ())), preferred_element_type=F32)
            st = state_ref[:, g * gw:(g + 1) * gw]
            y_inter = _dot(c_g, st.astype(BF16)) * dec_in[:, g * gw:(g + 1) * gw]
            new_st = lax.dot_general(b_g, xw_b[:, g * gw:(g + 1) * gw], (((0,), (0,)), ((), ())),
                                     preferred_element_type=F32)
            state_ref[:, g * gw:(g + 1) * gw] = st * dec_all[:, g * gw:(g + 1) * gw] + new_st
            pair_parts = []
            for pr in range(HEADS_PER_GROUP // 2):
                h0 = g * HEADS_PER_GROUP + 2 * pr
                lo = h0 * SSD_HEAD_DIM
                rhs = xdt_b[:, lo:lo + 2 * SSD_HEAD_DIM]
                ys = []
                for h in (h0, h0 + 1):
                    seg = acum[:, h:h + 1] - acumt[h:h + 1, :]
                    m = (cb * jnp.exp(jnp.where(causal, seg, -jnp.inf))).astype(BF16)
                    ys.append(_dot(m, rhs))
                pair_parts.append(jnp.where(lane_lo, ys[0], ys[1]))
            y_parts.append(jnp.concatenate(pair_parts, axis=-1) + y_inter)
        y = jnp.concatenate(y_parts, axis=-1) + d_x * xs

        gated = y * _silu(z_ref[rows, :].astype(F32))
        outs = []
        for g in range(N_GROUPS):
            gg = gated[:, g * gw:(g + 1) * gw]
            outs.append(gg * lax.rsqrt(jnp.mean(gg * gg, axis=-1, keepdims=True) + EPS))
        out_ref[rows, :] = (jnp.concatenate(outs, axis=-1) * nw_ref[...]).astype(out_ref.dtype)


def _mixer(z, xbc, dt, dtt, batch, seq, dtb, alog, dskip, nw):
    rows = CHUNK * MIX_CHUNKS
    assert seq % rows == 0
    ns = seq // rows
    t = batch * seq
    row = lambda n: pl.BlockSpec((rows, n), lambda b, c: (b * ns + c, 0))
    full = lambda a: pl.BlockSpec(a.shape, lambda b, c: (0,) * a.ndim)
    dtbt, alogt = dtb.reshape(N_SSD_HEADS, 1), alog.reshape(N_SSD_HEADS, 1)
    params = (dtb, dtbt, alog, alogt, dskip, nw)
    return pl.pallas_call(
        _mixer_kernel,
        grid=(batch, ns),
        in_specs=[row(D_SSD), row(D_XBC), row(N_SSD_HEADS),
                  pl.BlockSpec((N_SSD_HEADS, rows), lambda b, c: (0, b * ns + c))]
                 + [full(a) for a in params],
        out_specs=row(D_SSD),
        out_shape=jax.ShapeDtypeStruct((t, D_SSD), BF16),
        scratch_shapes=[pltpu.VMEM((D_STATE, D_SSD), F32)],
        compiler_params=pltpu.CompilerParams(
            dimension_semantics=("arbitrary", "arbitrary"), vmem_limit_bytes=VMEM_LIMIT_BYTES),
    )(z, xbc, dt, dtt, *params)


def _outproj_kernel(yssd_ref, ysc_ref, x_ref, wout_ref, nw_ref, wrh_ref, wrl_ref, br_ref,
                    h1_ref, xn_ref, idx_ref, gate_ref, rank_ref, cnt_ref, run_ref):
    tm = x_ref.shape[0]

    @pl.when(pl.program_id(0) == 0)
    def _():
        run_ref[...] = jnp.zeros_like(run_ref)

    h1 = (x_ref[...] + _dot(yssd_ref[...], wout_ref[:D_SSD, :])
          + _dot(ysc_ref[...], wout_ref[D_SSD:, :]))
    h1_ref[...] = h1
    xn = h1 * lax.rsqrt(jnp.mean(h1 * h1, axis=-1, keepdims=True) + EPS) * nw_ref[...]
    for c, words in enumerate(_pack_rows(xn)):
        xn_ref[c] = words

    xh = xn.astype(BF16)
    xl = (xn - xh.astype(F32)).astype(BF16)
    nt = lambda w, a: lax.dot_general(w, a, (((1,), (1,)), ((), ())), preferred_element_type=F32)
    logits = (nt(wrh_ref[...], xh) + nt(wrl_ref[...], xh) + nt(wrh_ref[...], xl)
              + br_ref[...])

    expert = lax.broadcasted_iota(jnp.int32, (N_EXPERTS, tm), 0).astype(F32)
    work = logits
    vals, firsts, sels = [], [], []
    for _ in range(TOP_K):
        m = jnp.max(work, axis=0, keepdims=True)
        first = jnp.min(jnp.where(work == m, expert, float(N_EXPERTS)), axis=0, keepdims=True)
        sel = expert == first
        vals.append(m)
        firsts.append(first)
        sels.append(sel)
        work = jnp.where(sel, -jnp.inf, work)
    exps = [jnp.exp(v - vals[0]) for v in vals]
    denom = exps[0] + exps[1] + exps[2] + exps[3]

    onehot = jnp.zeros((N_EXPERTS, tm), F32)
    for sel in sels:
        onehot = onehot + jnp.where(sel, 1.0, 0.0)
    ri = lax.broadcasted_iota(jnp.int32, (tm, tm), 0)
    ci = lax.broadcasted_iota(jnp.int32, (tm, tm), 1)
    earlier = jnp.where(ri < ci, 1.0, 0.0).astype(BF16)
    before = run_ref[...] + _dot(onehot.astype(BF16), earlier)
    run_ref[...] = run_ref[...] + jnp.sum(onehot, axis=1, keepdims=True)
    cnt_ref[...] = run_ref[...]

    slot = lax.broadcasted_iota(jnp.int32, (TOP_K, tm), 0)
    idx = jnp.zeros((TOP_K, tm), F32)
    gate = jnp.zeros((TOP_K, tm), F32)
    rank = jnp.zeros((TOP_K, tm), F32)
    for k in range(TOP_K):
        r_k = jnp.sum(jnp.where(sels[k], before, 0.0), axis=0, keepdims=True)
        idx = jnp.where(slot == k, firsts[k], idx)
        gate = jnp.where(slot == k, exps[k] / denom, gate)
        rank = jnp.where(slot == k, r_k, rank)
    idx_ref[...] = idx.astype(jnp.int32)
    gate_ref[...] = gate
    rank_ref[...] = rank.astype(jnp.int32)


def _outproj(yssd, ysc, x2, w_out, nw, wr_hi, wr_lo, br):
    t = x2.shape[0]
    tm = OUT_TILE
    row = lambda n: pl.BlockSpec((tm, n), lambda i: (i, 0))
    slots = pl.BlockSpec((TOP_K, tm), lambda i: (0, i))
    full = lambda a: pl.BlockSpec(a.shape, lambda i: (0,) * a.ndim)
    return pl.pallas_call(
        _outproj_kernel,
        grid=(t // tm,),
        in_specs=[row(D_SSD), row(D_SC), row(D_MODEL), full(w_out), full(nw), full(wr_hi),
                  full(wr_lo), full(br)],
        out_specs=[row(D_MODEL), pl.BlockSpec((ROW_PARTS, tm, LANES), lambda i: (0, i, 0)),
                   slots, slots, slots,
                   pl.BlockSpec((N_EXPERTS, 1), lambda i: (0, 0))],
        out_shape=[jax.ShapeDtypeStruct((t, D_MODEL), F32),
                   jax.ShapeDtypeStruct((ROW_PARTS, t, LANES), jnp.int32),
                   jax.ShapeDtypeStruct((TOP_K, t), jnp.int32),
                   jax.ShapeDtypeStruct((TOP_K, t), F32),
                   jax.ShapeDtypeStruct((TOP_K, t), jnp.int32),
                   jax.ShapeDtypeStruct((N_EXPERTS, 1), F32)],
        scratch_shapes=[pltpu.VMEM((N_EXPERTS, 1), F32)],
        compiler_params=pltpu.CompilerParams(
            dimension_semantics=("arbitrary",), vmem_limit_bytes=VMEM_LIMIT_BYTES),
    )(yssd, ysc, x2, w_out, nw, wr_hi, wr_lo, br)


def _dest_kernel(pstart_ref, idx_ref, rank_ref, dest_ref):
    idx = idx_ref[...]
    base = jnp.zeros_like(idx)
    for e in range(N_EXPERTS):
        base = jnp.where(idx == e, pstart_ref[e], base)
    dest_ref[...] = base + rank_ref[...]


def _dest(pstarts, idx, rank):
    full = lambda a: pl.BlockSpec(a.shape, lambda i, ps: (0,) * a.ndim)
    return pl.pallas_call(
        _dest_kernel,
        grid_spec=pltpu.PrefetchScalarGridSpec(
            num_scalar_prefetch=1, grid=(1,),
            in_specs=[full(idx), full(rank)], out_specs=full(idx)),
        out_shape=jax.ShapeDtypeStruct(idx.shape, jnp.int32),
    )(pstarts, idx, rank)


def _sc_mesh():
    return plsc.VectorSubcoreMesh(core_axis_name="core", subcore_axis_name="subcore")


def _part_rows(dest, n_rows):
    off = (jnp.arange(ROW_PARTS, dtype=jnp.int32) * n_rows)[:, None, None]
    return (dest[None] + off).reshape(ROW_PARTS * dest.shape[0], dest.shape[1])


def _sc_dispatch(x, dest, n_rows):
    parts, t, width = x.shape
    n_slots = dest.shape[0]
    win = SC_WINDOW
    nwin = t // win

    @functools.partial(pl.kernel,
                       out_type=jax.ShapeDtypeStruct((parts * n_rows, width), jnp.int32),
                       mesh=_sc_mesh(), scratch_types=[])
    def scatter(x_hbm, idx_hbm, out_hbm):
        def body(x_vmem, idx_vmem):
            for k in range(n_slots):
                pltpu.sync_copy(x_vmem, out_hbm.at[idx_vmem.at[k]])

        pltpu.emit_pipeline(
            body,
            grid=(parts * nwin,),
            in_specs=[pl.BlockSpec((win, width), lambda j: (j, 0)),
                      pl.BlockSpec((n_slots, win), lambda j: (j // nwin, j % nwin))],
            out_specs=[],
            core_axis_name=("core", "subcore"),
            dimension_semantics=(pltpu.PARALLEL,),
        )(x_hbm, idx_hbm)

    xs = scatter(x.reshape(parts * t, width), _part_rows(dest, n_rows))
    return xs.reshape(parts, n_rows, width)


def _sc_collect(ys, dest):
    parts, n_rows, width = ys.shape
    n_slots, t = dest.shape
    win = SC_WINDOW
    nwin = t // win
    idx = _part_rows(dest, n_rows).reshape(parts, n_slots, t).swapaxes(0, 1)
    idx = idx.reshape(n_slots * parts, t)

    @functools.partial(pl.kernel,
                       out_type=jax.ShapeDtypeStruct((n_slots * parts * t, width), jnp.int32),
                       mesh=_sc_mesh(), scratch_types=[])
    def gather(y_hbm, idx_hbm, out_hbm):
        def body(idx_vmem, out_vmem):
            pltpu.sync_copy(y_hbm.at[idx_vmem.at[0]], out_vmem)

        pltpu.emit_pipeline(
            body,
            grid=(n_slots * parts * nwin,),
            in_specs=[pl.BlockSpec((1, win), lambda j: (j // nwin, j % nwin))],
            out_specs=[pl.BlockSpec((win, width), lambda j: (j, 0))],
            core_axis_name=("core", "subcore"),
            dimension_semantics=(pltpu.PARALLEL,),
        )(idx_hbm, out_hbm)

    out = gather(ys.reshape(parts * n_rows, width), idx)
    return out.reshape(n_slots, parts, t, width)


def _expert_kernel(blk_e_ref, blk_n_ref, xs_ref, wgu_ref, bgu_ref, wd_ref, bd_ref, out_ref,
                   wgu_b, wd_b):
    i = pl.program_id(0)
    n_valid = blk_n_ref[i]

    @pl.when((i == 0) | (blk_e_ref[i] != blk_e_ref[jnp.maximum(i - 1, 0)]))
    def _():
        wgu_b[...] = wgu_ref[0].astype(BF16)
        wd_b[...] = wd_ref[0].astype(BF16)

    @pl.when(n_valid == 0)
    def _():
        out_ref[...] = jnp.zeros_like(out_ref)

    @pl.when(n_valid > 0)
    def _():
        for r in range(0, ROW_BLOCK, EXPERT_SUB):
            words = [xs_ref[c, r:r + EXPERT_SUB] for c in range(ROW_PARTS)]
            rows = r + lax.broadcasted_iota(jnp.int32, words[0].shape, 0)
            xb = _unpack_rows([jnp.where(rows < n_valid, w, 0) for w in words]).astype(BF16)
            gu = _dot(xb, wgu_b[...]) + bgu_ref[0]
            gate = jnp.minimum(gu[:, :EXPERT_FF], SWIGLU_LIMIT)
            up = jnp.clip(gu[:, EXPERT_FF:], -SWIGLU_LIMIT, SWIGLU_LIMIT)
            hid = (up + 1.0) * (gate * (1.0 / (1.0 + jnp.exp(-SWIGLU_ALPHA * gate))))
            out = _dot(hid.astype(BF16), wd_b[...]) + bd_ref[0]
            for c, w in enumerate(_pack_rows(out)):
                out_ref[c, r:r + EXPERT_SUB] = w


def _experts(blk_e, blk_n, xs, wgu, bgu, wd, bd):
    rb = ROW_BLOCK
    by_expert = lambda a: pl.BlockSpec((1,) + a.shape[1:], lambda i, be, bn: (be[i], 0, 0))
    return pl.pallas_call(
        _expert_kernel,
        grid_spec=pltpu.PrefetchScalarGridSpec(
            num_scalar_prefetch=2, grid=(xs.shape[1] // rb,),
            in_specs=[pl.BlockSpec((ROW_PARTS, rb, LANES), lambda i, be, bn: (0, i, 0)),
                      by_expert(wgu), by_expert(bgu), by_expert(wd), by_expert(bd)],
            out_specs=pl.BlockSpec((ROW_PARTS, rb, LANES), lambda i, be, bn: (0, i, 0)),
            scratch_shapes=[pltpu.VMEM(wgu.shape[1:], BF16), pltpu.VMEM(wd.shape[1:], BF16)]),
        out_shape=jax.ShapeDtypeStruct(xs.shape, xs.dtype),
        compiler_params=pltpu.CompilerParams(
            dimension_semantics=("arbitrary",), vmem_limit_bytes=VMEM_LIMIT_BYTES),
    )(blk_e, blk_n, xs, wgu, bgu, wd, bd)


def _combine_kernel(gate_ref, h1_ref, fw_ref, y0_ref, y1_ref, y2_ref, y3_ref, *rest):
    out_ref = rest[-1]
    gate = gate_ref[...]
    h2 = h1_ref[...]
    for k, y_ref in enumerate((y0_ref, y1_ref, y2_ref, y3_ref)):
        h2 = h2 + gate[:, k:k + 1] * _unpack_rows([y_ref[c] for c in range(ROW_PARTS)])
    out_ref[...] = h2 * lax.rsqrt(jnp.mean(h2 * h2, axis=-1, keepdims=True) + EPS) * fw_ref[...]


def _combine(gates, h1, fw, planes, first_tile, out_prev):
    t, d = h1.shape
    tm = COMBINE_TILE
    n = planes.shape[2]
    tok = lambda w: pl.BlockSpec((tm, w), lambda i: (first_tile + i, 0))
    plane = lambda k: pl.BlockSpec((None, ROW_PARTS, tm, LANES), lambda i: (k, 0, i, 0))
    in_specs = [tok(TOP_K), tok(d), pl.BlockSpec((1, d), lambda i: (0, 0))]
    in_specs += [plane(k) for k in range(TOP_K)]
    args = [gates, h1, fw] + [planes] * TOP_K
    aliases = {}
    if out_prev is not None:
        in_specs.append(pl.BlockSpec(memory_space=pl.ANY))
        aliases = {len(args): 0}
        args.append(out_prev)
    return pl.pallas_call(
        _combine_kernel,
        grid=(n // tm,),
        in_specs=in_specs,
        out_specs=tok(d),
        out_shape=jax.ShapeDtypeStruct((t, d), F32),
        input_output_aliases=aliases,
        compiler_params=pltpu.CompilerParams(
            dimension_semantics=("arbitrary",), vmem_limit_bytes=VMEM_LIMIT_BYTES),
    )(*args)


def _block_plan(counts, n_blocks):
    rb = ROW_BLOCK
    pcounts = (counts + rb - 1) // rb * rb
    pends = jnp.cumsum(pcounts)
    pstarts = pends - pcounts
    block_start = jnp.arange(n_blocks, dtype=jnp.int32) * rb
    blk_e = jnp.minimum(jnp.sum(block_start[:, None] >= pends[None, :], axis=1),
                        N_EXPERTS - 1).astype(jnp.int32)
    blk_n = jnp.clip(counts[blk_e] - (block_start - pstarts[blk_e]), 0, rb).astype(jnp.int32)
    return pstarts.astype(jnp.int32), blk_e, blk_n


def kernel(x, norm1_w, w_in, ssd_conv_w, ssd_conv_b, dt_bias, a_log, d_skip, ssd_norm_w,
           sc_conv_w, sc_norm_w, w_out, norm2_w, w_router, b_router, w_gate_up, b_gate_up,
           w_down, b_down, final_norm_w):
    assert norm1_w.shape[0] == 1, "single-layer problem"
    batch, seq, d = x.shape
    t = batch * seq
    x2 = x.reshape(t, d)
    row = lambda a: a.reshape(1, -1)

    s0, s1, s2 = D_SSD, D_SSD + D_XBC, D_SSD + D_XBC + N_SSD_HEADS
    w = w_in[0]
    w_main = jnp.concatenate([w[:, :s1], w[:, s2:]], axis=1).astype(BF16)
    w_dt = w[:, s1:s2].astype(BF16)
    z, xbc, ysc, dt, dtt = _inproj(x2, seq, row(norm1_w[0]), w_main, w_dt, w_dt.T,
                                   ssd_conv_w[0], row(ssd_conv_b[0]), sc_conv_w[0],
                                   row(sc_norm_w[0]))
    yssd = _mixer(z, xbc, dt, dtt, batch, seq, row(dt_bias[0]), row(a_log[0]), row(d_skip[0]),
                  row(ssd_norm_w[0]))

    wr = w_router[0].T
    wr_hi = wr.astype(BF16)
    wr_lo = (wr - wr_hi.astype(F32)).astype(BF16)
    h1, xn2, idx, gates, rank, counts = _outproj(
        yssd, ysc, x2, w_out[0].astype(BF16), row(norm2_w[0]), wr_hi, wr_lo,
        b_router[0].reshape(-1, 1))

    n_rows = t * TOP_K + N_EXPERTS * ROW_BLOCK
    pstarts, blk_e, blk_n = _block_plan(counts[:, 0].astype(jnp.int32), n_rows // ROW_BLOCK)
    dest = _dest(pstarts, idx, rank)

    xs = _sc_dispatch(xn2, dest, n_rows)
    ys = _experts(blk_e, blk_n, xs, w_gate_up[0], b_gate_up[0][:, None, :],
                  w_down[0], b_down[0][:, None, :])
    assert t % (COMBINE_CHUNKS * COMBINE_TILE) == 0
    tc = t // COMBINE_CHUNKS
    gates_t, fw, out = gates.T, row(final_norm_w), None
    for c in range(COMBINE_CHUNKS):
        planes = _sc_collect(ys, dest[:, c * tc:(c + 1) * tc])
        out = _combine(gates_t, h1, fw, planes, c * (tc // COMBINE_TILE), out)
    return out.reshape(batch, seq, d)
```

```python
import functools

import jax
import jax.numpy as jnp
from jax import lax
from jax.experimental import pallas as pl
from jax.experimental.pallas import tpu as pltpu
from jax.experimental.pallas import tpu_sc as plsc

D_MODEL = 1024
D_SSD = 1024
SSD_HEAD_DIM = 64
N_SSD_HEADS = 16
N_GROUPS = 2
HEADS_PER_GROUP = 8
D_STATE = 128
SSD_CONV = 4
CHUNK = 128
D_XBC = D_SSD + 2 * N_GROUPS * D_STATE
D_SC = 1024
SC_CONV = 3
D_MIX = D_SSD + D_SC
N_EXPERTS = 32
TOP_K = 4
EXPERT_FF = 1024
SWIGLU_LIMIT = 7.0
SWIGLU_ALPHA = 1.702
EPS = 1e-5

VMEM_LIMIT_BYTES = 56 * 1024 * 1024
SUBLANES = 8
ROW_BLOCK = 1024
EXPERT_SUB = 256
IN_TILE = 512
IN_SUB = 256
MIX_CHUNKS = 4
OUT_TILE = 1024
OUT_SUB = 1024
COMBINE_TILE = 512
COMBINE_CHUNKS = 4
LANES = 128
ROW_PARTS = D_MODEL // (2 * LANES)
SC_WINDOW = 128
HIGH_HALF = -65536

F32 = jnp.float32
BF16 = jnp.bfloat16


def _dot(a, b):
    return jnp.dot(a, b, preferred_element_type=F32)


def _split3(v):
    hi = v.astype(BF16)
    r1 = v - hi.astype(F32)
    mid = r1.astype(BF16)
    lo = (r1 - mid.astype(F32)).astype(BF16)
    return hi, mid, lo


def _dot_exact_rhs01(v, m01):
    hi, mid, lo = _split3(v)
    return _dot(hi, m01) + _dot(mid, m01) + _dot(lo, m01)


def _dot_exact_lhs01(m01, v):
    hi, mid, lo = _split3(v)
    return _dot(m01, hi) + _dot(m01, mid) + _dot(m01, lo)


def _silu(v):
    return v * (1.0 / (1.0 + jnp.exp(-v)))


def _softplus(v):
    return jnp.maximum(v, 0.0) + jnp.log1p(jnp.exp(-jnp.abs(v)))


def _pack_rows(v):
    bits = lambda a: lax.bitcast_convert_type(a.astype(BF16).astype(F32), jnp.int32)
    parts = []
    for c in range(ROW_PARTS):
        lo = v[:, (2 * c) * LANES:(2 * c + 1) * LANES]
        hi = v[:, (2 * c + 1) * LANES:(2 * c + 2) * LANES]
        parts.append(bits(hi) | lax.shift_right_logical(bits(lo), 16))
    return parts


def _unpack_rows(parts):
    cols = []
    for w in parts:
        cols.append(lax.bitcast_convert_type(w << 16, F32))
        cols.append(lax.bitcast_convert_type(w & HIGH_HALF, F32))
    return jnp.concatenate(cols, axis=-1)


def _shifted(cur, tail, s):
    if s == 0:
        return cur
    rc = pltpu.roll(cur, s, axis=0)
    rt = pltpu.roll(tail, s, axis=0)
    row = lax.broadcasted_iota(jnp.int32, tail.shape, 0)
    first = jnp.where(row < s, rt, rc[0:SUBLANES])
    return jnp.concatenate([first, rc[SUBLANES:]], axis=0)


def _causal_conv(tail_ref, cur, w_ref, acc):
    rows = cur.shape[0]
    n_taps = w_ref.shape[0]
    tail = tail_ref[...]
    for k in range(n_taps):
        acc = acc + w_ref[k:k + 1, :] * _shifted(cur, tail, n_taps - 1 - k)
    tail_ref[...] = cur[rows - SUBLANES:rows]
    return acc


def _inproj_kernel(tiles_per_seq, x_ref, nw_ref, w_ref, wdt_ref, wdtt_ref,
                   cw_ref, cb_ref, scw_ref, scnw_ref,
                   z_ref, xbc_ref, ysc_ref, dt_ref, dtt_ref, hist_ref, schist_ref):
    tm = x_ref.shape[0]

    @pl.when(pl.program_id(0) % tiles_per_seq == 0)
    def _():
        hist_ref[...] = jnp.zeros_like(hist_ref)
        schist_ref[...] = jnp.zeros_like(schist_ref)

    for r in range(0, tm, IN_SUB):
        rows = pl.ds(r, IN_SUB)
        x = x_ref[rows, :]
        xn = x * lax.rsqrt(jnp.mean(x * x, axis=-1, keepdims=True) + EPS) * nw_ref[...]
        xb = xn.astype(BF16)
        proj = lambda o, n, xb=xb: _dot(xb, w_ref[:, o:o + n])
        z_ref[rows, :] = proj(0, D_SSD).astype(z_ref.dtype)

        conv = _causal_conv(hist_ref, proj(D_SSD, D_XBC), cw_ref, cb_ref[...])
        xbc_ref[rows, :] = _silu(conv).astype(xbc_ref.dtype)

        o = D_SSD + D_XBC
        cu = proj(o + D_SC, D_SC) * proj(o + 2 * D_SC, D_SC)
        v = _causal_conv(schist_ref, cu, scw_ref, jnp.zeros_like(cu))
        gv = proj(o, D_SC) * v
        y_sc = gv * lax.rsqrt(jnp.mean(gv * gv, axis=-1, keepdims=True) + EPS) * scnw_ref[...]
        ysc_ref[rows, :] = y_sc.astype(ysc_ref.dtype)

        dt_ref[rows, :] = _dot(xb, wdt_ref[...])
        dtt_ref[:, rows] = lax.dot_general(wdtt_ref[...], xb, (((1,), (1,)), ((), ())),
                                           preferred_element_type=F32)


def _inproj(x2, seq, nw, w_main, w_dt, w_dtt, cw, cb, scw, scnw):
    t = x2.shape[0]
    tm = IN_TILE
    assert seq % tm == 0, "in-proj tiles must not straddle sequences (causal conv carry)"
    row = lambda n: pl.BlockSpec((tm, n), lambda i: (i, 0))
    full = lambda a: pl.BlockSpec(a.shape, lambda i: (0,) * a.ndim)
    params = (nw, w_main, w_dt, w_dtt, cw, cb, scw, scnw)
    return pl.pallas_call(
        functools.partial(_inproj_kernel, seq // tm),
        grid=(t // tm,),
        in_specs=[row(D_MODEL)] + [full(a) for a in params],
        out_specs=[row(D_SSD), row(D_XBC), row(D_SC),
                   row(N_SSD_HEADS), pl.BlockSpec((N_SSD_HEADS, tm), lambda i: (0, i))],
        out_shape=[jax.ShapeDtypeStruct((t, D_SSD), BF16),
                   jax.ShapeDtypeStruct((t, D_XBC), BF16),
                   jax.ShapeDtypeStruct((t, D_SC), BF16),
                   jax.ShapeDtypeStruct((t, N_SSD_HEADS), F32),
                   jax.ShapeDtypeStruct((N_SSD_HEADS, t), F32)],
        scratch_shapes=[pltpu.VMEM((SUBLANES, D_XBC), F32), pltpu.VMEM((SUBLANES, D_SC), F32)],
        compiler_params=pltpu.CompilerParams(
            dimension_semantics=("arbitrary",), vmem_limit_bytes=VMEM_LIMIT_BYTES),
    )(x2, *params)


def _mixer_kernel(z_ref, xbc_ref, dt_ref, dtt_ref,
                  dtb_ref, dtbt_ref, alog_ref, alogt_ref, dskip_ref, nw_ref, out_ref, state_ref):
    q = CHUNK

    @pl.when(pl.program_id(1) == 0)
    def _():
        state_ref[...] = jnp.zeros_like(state_ref)

    ri = lax.broadcasted_iota(jnp.int32, (q, q), 0)
    ci = lax.broadcasted_iota(jnp.int32, (q, q), 1)
    causal = ri >= ci
    tri = jnp.where(causal, 1.0, 0.0).astype(BF16)
    trit = jnp.where(ri <= ci, 1.0, 0.0).astype(BF16)
    hh = lax.broadcasted_iota(jnp.int32, (N_SSD_HEADS, D_SSD), 0)
    hl = lax.broadcasted_iota(jnp.int32, (N_SSD_HEADS, D_SSD), 1) // SSD_HEAD_DIM
    expand = jnp.where(hh == hl, 1.0, 0.0).astype(BF16)
    lane_lo = lax.broadcasted_iota(jnp.int32, (q, 2 * SSD_HEAD_DIM), 1) < SSD_HEAD_DIM
    gw = HEADS_PER_GROUP * SSD_HEAD_DIM
    n_bc = N_GROUPS * D_STATE
    a_neg = -jnp.exp(alog_ref[...])
    a_negt = -jnp.exp(alogt_ref[...])
    d_x = _dot_exact_rhs01(dskip_ref[...], expand)

    for s in range(MIX_CHUNKS):
        rows = pl.ds(s * q, q)
        xs = xbc_ref[rows, :D_SSD].astype(F32)
        b_all = xbc_ref[rows, D_SSD:D_SSD + n_bc]
        c_all = xbc_ref[rows, D_SSD + n_bc:]

        dt = _softplus(dt_ref[rows, :] + dtb_ref[...])
        dtt = _softplus(dtt_ref[:, rows] + dtbt_ref[...])
        acum = _dot_exact_lhs01(tri, dt * a_neg)
        acumt = _dot_exact_rhs01(dtt * a_negt, trit)
        a_last = acum[q - 1:q, :]

        xdt = xs * _dot_exact_rhs01(dt, expand)
        xdt_b = xdt.astype(BF16)
        dec_in = _dot_exact_rhs01(jnp.exp(acum), expand)
        dec_out = _dot_exact_rhs01(jnp.exp(a_last - acum), expand)
        dec_all = _dot_exact_rhs01(jnp.exp(a_last), expand)
        xw_b = (xdt * dec_out).astype(BF16)

        y_parts = []
        for g in range(N_GROUPS):
            b_g = b_all[:, g * D_STATE:(g + 1) * D_STATE]
            c_g = c_all[:, g * D_STATE:(g + 1) * D_STATE]
            cb = lax.dot_general(c_g, b_g, (((1,), (1,)), ((), ())), preferred_element_type=F32)
            st = state_ref[:, g * gw:(g + 1) * gw]
            y_inter = _dot(c_g, st.astype(BF16)) * dec_in[:, g * gw:(g + 1) * gw]
            new_st = lax.dot_general(b_g, xw_b[:, g * gw:(g + 1) * gw], (((0,), (0,)), ((), ())),
                                     preferred_element_type=F32)
            state_ref[:, g * gw:(g + 1) * gw] = st * dec_all[:, g * gw:(g + 1) * gw] + new_st
            pair_parts = []
            for pr in range(HEADS_PER_GROUP // 2):
                h0 = g * HEADS_PER_GROUP + 2 * pr
                lo = h0 * SSD_HEAD_DIM
                rhs = xdt_b[:, lo:lo + 2 * SSD_HEAD_DIM]
                ys = []
                for h in (h0, h0 + 1):
                    seg = acum[:, h:h + 1] - acumt[h:h + 1, :]
                    m = (cb * jnp.exp(jnp.where(causal, seg, -jnp.inf))).astype(BF16)
                    ys.append(_dot(m, rhs))
                pair_parts.append(jnp.where(lane_lo, ys[0], ys[1]))
            y_parts.append(jnp.concatenate(pair_parts, axis=-1) + y_inter)
        y = jnp.concatenate(y_parts, axis=-1) + d_x * xs

        gated = y * _silu(z_ref[rows, :].astype(F32))
        outs = []
        for g in range(N_GROUPS):
            gg = gated[:, g * gw:(g + 1) * gw]
            outs.append(gg * lax.rsqrt(jnp.mean(gg * gg, axis=-1, keepdims=True) + EPS))
        out_ref[rows, :] = (jnp.concatenate(outs, axis=-1) * nw_ref[...]).astype(out_ref.dtype)


def _mixer(z, xbc, dt, dtt, batch, seq, dtb, alog, dskip, nw):
    rows = CHUNK * MIX_CHUNKS
    assert seq % rows == 0
    ns = seq // rows
    t = batch * seq
    row = lambda n: pl.BlockSpec((rows, n), lambda b, c: (b * ns + c, 0))
    full = lambda a: pl.BlockSpec(a.shape, lambda b, c: (0,) * a.ndim)
    dtbt, alogt = dtb.reshape(N_SSD_HEADS, 1), alog.reshape(N_SSD_HEADS, 1)
    params = (dtb, dtbt, alog, alogt, dskip, nw)
    return pl.pallas_call(
        _mixer_kernel,
        grid=(batch, ns),
        in_specs=[row(D_SSD), row(D_XBC), row(N_SSD_HEADS),
                  pl.BlockSpec((N_SSD_HEADS, rows), lambda b, c: (0, b * ns + c))]
                 + [full(a) for a in params],
        out_specs=row(D_SSD),
        out_shape=jax.ShapeDtypeStruct((t, D_SSD), BF16),
        scratch_shapes=[pltpu.VMEM((D_STATE, D_SSD), F32)],
        compiler_params=pltpu.CompilerParams(
            dimension_semantics=("arbitrary", "arbitrary"), vmem_limit_bytes=VMEM_LIMIT_BYTES),
    )(z, xbc, dt, dtt, *params)


def _outproj_kernel(yssd_ref, ysc_ref, x_ref, wout_ref, nw_ref, wrh_ref, wrl_ref, br_ref,
                    h1_ref, xn_ref, idx_ref, gate_ref, rank_ref, cnt_ref, run_ref):
    tm = x_ref.shape[0]

    @pl.when(pl.program_id(0) == 0)
    def _():
        run_ref[...] = jnp.zeros_like(run_ref)

    sub = OUT_SUB
    nt = lambda w, a: lax.dot_general(w, a, (((1,), (1,)), ((), ())), preferred_element_type=F32)
    expert = lax.broadcasted_iota(jnp.int32, (N_EXPERTS, sub), 0).astype(F32)
    slot = lax.broadcasted_iota(jnp.int32, (TOP_K, sub), 0)
    ri = lax.broadcasted_iota(jnp.int32, (sub, sub), 0)
    ci = lax.broadcasted_iota(jnp.int32, (sub, sub), 1)
    earlier = jnp.where(ri < ci, 1.0, 0.0).astype(BF16)

    run = run_ref[...]
    for r in range(0, tm, sub):
        rows = pl.ds(r, sub)
        h1 = (x_ref[rows, :] + _dot(yssd_ref[rows, :], wout_ref[:D_SSD, :])
              + _dot(ysc_ref[rows, :], wout_ref[D_SSD:, :]))
        h1_ref[rows, :] = h1
        xn = h1 * lax.rsqrt(jnp.mean(h1 * h1, axis=-1, keepdims=True) + EPS) * nw_ref[...]
        for c, words in enumerate(_pack_rows(xn)):
            xn_ref[c, rows, :] = words

        xh = xn.astype(BF16)
        xl = (xn - xh.astype(F32)).astype(BF16)
        logits = (nt(wrh_ref[...], xh) + nt(wrl_ref[...], xh) + nt(wrh_ref[...], xl)
                  + br_ref[...])

        work = logits
        vals, firsts, sels = [], [], []
        for _ in range(TOP_K):
            m = jnp.max(work, axis=0, keepdims=True)
            first = jnp.min(jnp.where(work == m, expert, float(N_EXPERTS)), axis=0,
                            keepdims=True)
            sel = expert == first
            vals.append(m)
            firsts.append(first)
            sels.append(sel)
            work = jnp.where(sel, -jnp.inf, work)
        exps = [jnp.exp(v - vals[0]) for v in vals]
        denom = exps[0] + exps[1] + exps[2] + exps[3]

        onehot = jnp.zeros((N_EXPERTS, sub), F32)
        for sel in sels:
            onehot = onehot + jnp.where(sel, 1.0, 0.0)
        before = run + _dot(onehot.astype(BF16), earlier)
        run = run + jnp.sum(onehot, axis=1, keepdims=True)

        idx = jnp.zeros((TOP_K, sub), F32)
        gate = jnp.zeros((TOP_K, sub), F32)
        rank = jnp.zeros((TOP_K, sub), F32)
        for k in range(TOP_K):
            r_k = jnp.sum(jnp.where(sels[k], before, 0.0), axis=0, keepdims=True)
            idx = jnp.where(slot == k, firsts[k], idx)
            gate = jnp.where(slot == k, exps[k] / denom, gate)
            rank = jnp.where(slot == k, r_k, rank)
        idx_ref[:, rows] = idx.astype(jnp.int32)
        gate_ref[:, rows] = gate
        rank_ref[:, rows] = rank.astype(jnp.int32)
    run_ref[...] = run
    cnt_ref[...] = run


def _outproj(yssd, ysc, x2, w_out, nw, wr_hi, wr_lo, br):
    t = x2.shape[0]
    tm = OUT_TILE
    row = lambda n: pl.BlockSpec((tm, n), lambda i: (i, 0))
    slots = pl.BlockSpec((TOP_K, tm), lambda i: (0, i))
    full = lambda a: pl.BlockSpec(a.shape, lambda i: (0,) * a.ndim)
    return pl.pallas_call(
        _outproj_kernel,
        grid=(t // tm,),
        in_specs=[row(D_SSD), row(D_SC), row(D_MODEL), full(w_out), full(nw), full(wr_hi),
                  full(wr_lo), full(br)],
        out_specs=[row(D_MODEL), pl.BlockSpec((ROW_PARTS, tm, LANES), lambda i: (0, i, 0)),
                   slots, slots, slots,
                   pl.BlockSpec((N_EXPERTS, 1), lambda i: (0, 0))],
        out_shape=[jax.ShapeDtypeStruct((t, D_MODEL), F32),
                   jax.ShapeDtypeStruct((ROW_PARTS, t, LANES), jnp.int32),
                   jax.ShapeDtypeStruct((TOP_K, t), jnp.int32),
                   jax.ShapeDtypeStruct((TOP_K, t), F32),
                   jax.ShapeDtypeStruct((TOP_K, t), jnp.int32),
                   jax.ShapeDtypeStruct((N_EXPERTS, 1), F32)],
        scratch_shapes=[pltpu.VMEM((N_EXPERTS, 1), F32)],
        compiler_params=pltpu.CompilerParams(
            dimension_semantics=("arbitrary",), vmem_limit_bytes=VMEM_LIMIT_BYTES),
    )(yssd, ysc, x2, w_out, nw, wr_hi, wr_lo, br)


def _dest_kernel(pstart_ref, idx_ref, rank_ref, dest_ref):
    idx = idx_ref[...]
    base = jnp.zeros_like(idx)
    for e in range(N_EXPERTS):
        base = jnp.where(idx == e, pstart_ref[e], base)
    dest_ref[...] = base + rank_ref[...]


def _dest(pstarts, idx, rank):
    full = lambda a: pl.BlockSpec(a.shape, lambda i, ps: (0,) * a.ndim)
    return pl.pallas_call(
        _dest_kernel,
        grid_spec=pltpu.PrefetchScalarGridSpec(
            num_scalar_prefetch=1, grid=(1,),
            in_specs=[full(idx), full(rank)], out_specs=full(idx)),
        out_shape=jax.ShapeDtypeStruct(idx.shape, jnp.int32),
    )(pstarts, idx, rank)


def _sc_mesh():
    return plsc.VectorSubcoreMesh(core_axis_name="core", subcore_axis_name="subcore")


def _part_rows(dest, n_rows):
    off = (jnp.arange(ROW_PARTS, dtype=jnp.int32) * n_rows)[:, None, None]
    return (dest[None] + off).reshape(ROW_PARTS * dest.shape[0], dest.shape[1])


def _sc_dispatch(x, dest, n_rows):
    parts, t, width = x.shape
    n_slots = dest.shape[0]
    win = SC_WINDOW
    nwin = t // win

    @functools.partial(pl.kernel,
                       out_type=jax.ShapeDtypeStruct((parts * n_rows, width), jnp.int32),
                       mesh=_sc_mesh(), scratch_types=[])
    def scatter(x_hbm, idx_hbm, out_hbm):
        def body(x_vmem, idx_vmem):
            for k in range(n_slots):
                pltpu.sync_copy(x_vmem, out_hbm.at[idx_vmem.at[k]])

        pltpu.emit_pipeline(
            body,
            grid=(parts * nwin,),
            in_specs=[pl.BlockSpec((win, width), lambda j: (j, 0)),
                      pl.BlockSpec((n_slots, win), lambda j: (j // nwin, j % nwin))],
            out_specs=[],
            core_axis_name=("core", "subcore"),
            dimension_semantics=(pltpu.PARALLEL,),
        )(x_hbm, idx_hbm)

    xs = scatter(x.reshape(parts * t, width), _part_rows(dest, n_rows))
    return xs.reshape(parts, n_rows, width)


def _sc_collect(ys, dest):
    parts, n_rows, width = ys.shape
    n_slots, t = dest.shape
    win = SC_WINDOW
    nwin = t // win
    idx = _part_rows(dest, n_rows).reshape(parts, n_slots, t).swapaxes(0, 1)
    idx = idx.reshape(n_slots * parts, t)

    @functools.partial(pl.kernel,
                       out_type=jax.ShapeDtypeStruct((n_slots * parts * t, width), jnp.int32),
                       mesh=_sc_mesh(), scratch_types=[])
    def gather(y_hbm, idx_hbm, out_hbm):
        def body(idx_vmem, out_vmem):
            pltpu.sync_copy(y_hbm.at[idx_vmem.at[0]], out_vmem)

        pltpu.emit_pipeline(
            body,
            grid=(n_slots * parts * nwin,),
            in_specs=[pl.BlockSpec((1, win), lambda j: (j // nwin, j % nwin))],
            out_specs=[pl.BlockSpec((win, width), lambda j: (j, 0))],
            core_axis_name=("core", "subcore"),
            dimension_semantics=(pltpu.PARALLEL,),
        )(idx_hbm, out_hbm)

    out = gather(ys.reshape(parts * n_rows, width), idx)
    return out.reshape(n_slots, parts, t, width)


def _expert_kernel(blk_e_ref, blk_n_ref, xs_ref, wgu_ref, bgu_ref, wd_ref, bd_ref, out_ref,
                   wgu_b, wd_b):
    i = pl.program_id(0)
    n_valid = blk_n_ref[i]

    @pl.when((i == 0) | (blk_e_ref[i] != blk_e_ref[jnp.maximum(i - 1, 0)]))
    def _():
        wgu_b[...] = wgu_ref[0].astype(BF16)
        wd_b[...] = wd_ref[0].astype(BF16)

    for r in range(0, ROW_BLOCK, EXPERT_SUB):
        @pl.when(n_valid <= r)
        def _():
            out_ref[:, r:r + EXPERT_SUB, :] = jnp.zeros((ROW_PARTS, EXPERT_SUB, LANES), jnp.int32)

        @pl.when(n_valid > r)
        def _():
            words = [xs_ref[c, r:r + EXPERT_SUB] for c in range(ROW_PARTS)]
            rows = r + lax.broadcasted_iota(jnp.int32, words[0].shape, 0)
            xb = _unpack_rows([jnp.where(rows < n_valid, w, 0) for w in words]).astype(BF16)
            gu = _dot(xb, wgu_b[...]) + bgu_ref[0]
            gate = jnp.minimum(gu[:, :EXPERT_FF], SWIGLU_LIMIT)
            up = jnp.clip(gu[:, EXPERT_FF:], -SWIGLU_LIMIT, SWIGLU_LIMIT)
            hid = (up + 1.0) * (gate * (1.0 / (1.0 + jnp.exp(-SWIGLU_ALPHA * gate))))
            out = _dot(hid.astype(BF16), wd_b[...]) + bd_ref[0]
            for c, w in enumerate(_pack_rows(out)):
                out_ref[c, r:r + EXPERT_SUB] = w


def _experts(blk_e, blk_n, xs, wgu, bgu, wd, bd):
    rb = ROW_BLOCK
    by_expert = lambda a: pl.BlockSpec((1,) + a.shape[1:], lambda i, be, bn: (be[i], 0, 0))
    return pl.pallas_call(
        _expert_kernel,
        grid_spec=pltpu.PrefetchScalarGridSpec(
            num_scalar_prefetch=2, grid=(xs.shape[1] // rb,),
            in_specs=[pl.BlockSpec((ROW_PARTS, rb, LANES), lambda i, be, bn: (0, i, 0)),
                      by_expert(wgu), by_expert(bgu), by_expert(wd), by_expert(bd)],
            out_specs=pl.BlockSpec((ROW_PARTS, rb, LANES), lambda i, be, bn: (0, i, 0)),
            scratch_shapes=[pltpu.VMEM(wgu.shape[1:], BF16), pltpu.VMEM(wd.shape[1:], BF16)]),
        out_shape=jax.ShapeDtypeStruct(xs.shape, xs.dtype),
        compiler_params=pltpu.CompilerParams(
            dimension_semantics=("arbitrary",), vmem_limit_bytes=VMEM_LIMIT_BYTES),
    )(blk_e, blk_n, xs, wgu, bgu, wd, bd)


def _combine_kernel(gate_ref, h1_ref, fw_ref, y0_ref, y1_ref, y2_ref, y3_ref, *rest):
    out_ref = rest[-1]
    gate = gate_ref[...]
    h2 = h1_ref[...]
    for k, y_ref in enumerate((y0_ref, y1_ref, y2_ref, y3_ref)):
        h2 = h2 + gate[:, k:k + 1] * _unpack_rows([y_ref[c] for c in range(ROW_PARTS)])
    out_ref[...] = h2 * lax.rsqrt(jnp.mean(h2 * h2, axis=-1, keepdims=True) + EPS) * fw_ref[...]


def _combine(gates, h1, fw, planes, first_tile, out_prev):
    t, d = h1.shape
    tm = COMBINE_TILE
    n = planes.shape[2]
    tok = lambda w: pl.BlockSpec((tm, w), lambda i: (first_tile + i, 0))
    plane = lambda k: pl.BlockSpec((None, ROW_PARTS, tm, LANES), lambda i: (k, 0, i, 0))
    in_specs = [tok(TOP_K), tok(d), pl.BlockSpec((1, d), lambda i: (0, 0))]
    in_specs += [plane(k) for k in range(TOP_K)]
    args = [gates, h1, fw] + [planes] * TOP_K
    aliases = {}
    if out_prev is not None:
        in_specs.append(pl.BlockSpec(memory_space=pl.ANY))
        aliases = {len(args): 0}
        args.append(out_prev)
    return pl.pallas_call(
        _combine_kernel,
        grid=(n // tm,),
        in_specs=in_specs,
        out_specs=tok(d),
        out_shape=jax.ShapeDtypeStruct((t, d), F32),
        input_output_aliases=aliases,
        compiler_params=pltpu.CompilerParams(
            dimension_semantics=("arbitrary",), vmem_limit_bytes=VMEM_LIMIT_BYTES),
    )(*args)


def _block_plan(counts, n_blocks):
    rb = ROW_BLOCK
    pcounts = (counts + rb - 1) // rb * rb
    pends = jnp.cumsum(pcounts)
    pstarts = pends - pcounts
    block_start = jnp.arange(n_blocks, dtype=jnp.int32) * rb
    blk_e = jnp.minimum(jnp.sum(block_start[:, None] >= pends[None, :], axis=1),
                        N_EXPERTS - 1).astype(jnp.int32)
    blk_n = jnp.clip(counts[blk_e] - (block_start - pstarts[blk_e]), 0, rb).astype(jnp.int32)
    return pstarts.astype(jnp.int32), blk_e, blk_n


def kernel(x, norm1_w, w_in, ssd_conv_w, ssd_conv_b, dt_bias, a_log, d_skip, ssd_norm_w,
           sc_conv_w, sc_norm_w, w_out, norm2_w, w_router, b_router, w_gate_up, b_gate_up,
           w_down, b_down, final_norm_w):
    assert norm1_w.shape[0] == 1, "single-layer problem"
    batch, seq, d = x.shape
    t = batch * seq
    x2 = x.reshape(t, d)
    row = lambda a: a.reshape(1, -1)

    s0, s1, s2 = D_SSD, D_SSD + D_XBC, D_SSD + D_XBC + N_SSD_HEADS
    w = w_in[0]
    w_main = jnp.concatenate([w[:, :s1], w[:, s2:]], axis=1).astype(BF16)
    w_dt = w[:, s1:s2].astype(BF16)
    z, xbc, ysc, dt, dtt = _inproj(x2, seq, row(norm1_w[0]), w_main, w_dt, w_dt.T,
                                   ssd_conv_w[0], row(ssd_conv_b[0]), sc_conv_w[0],
                                   row(sc_norm_w[0]))
    yssd = _mixer(z, xbc, dt, dtt, batch, seq, row(dt_bias[0]), row(a_log[0]), row(d_skip[0]),
                  row(ssd_norm_w[0]))

    wr = w_router[0].T
    wr_hi = wr.astype(BF16)
    wr_lo = (wr - wr_hi.astype(F32)).astype(BF16)
    h1, xn2, idx, gates, rank, counts = _outproj(
        yssd, ysc, x2, w_out[0].astype(BF16), row(norm2_w[0]), wr_hi, wr_lo,
        b_router[0].reshape(-1, 1))

    n_rows = t * TOP_K + N_EXPERTS * ROW_BLOCK
    pstarts, blk_e, blk_n = _block_plan(counts[:, 0].astype(jnp.int32), n_rows // ROW_BLOCK)
    dest = _dest(pstarts, idx, rank)

    xs = _sc_dispatch(xn2, dest, n_rows)
    ys = _experts(blk_e, blk_n, xs, w_gate_up[0], b_gate_up[0][:, None, :],
                  w_down[0], b_down[0][:, None, :])
    assert t % (COMBINE_CHUNKS * COMBINE_TILE) == 0
    tc = t // COMBINE_CHUNKS
    gates_t, fw, out = gates.T, row(final_norm_w), None
    for c in range(COMBINE_CHUNKS):
        planes = _sc_collect(ys, dest[:, c * tc:(c + 1) * tc])
        out = _combine(gates_t, h1, fw, planes, c * (tc // COMBINE_TILE), out)
    return out.reshape(batch, seq, d)
```

```python
import functools

import jax
import jax.numpy as jnp
from jax import lax
from jax.experimental import pallas as pl
from jax.experimental.pallas import tpu as pltpu
from jax.experimental.pallas import tpu_sc as plsc

D_MODEL = 1024
D_SSD = 1024
SSD_HEAD_DIM = 64
N_SSD_HEADS = 16
N_GROUPS = 2
HEADS_PER_GROUP = 8
D_STATE = 128
SSD_CONV = 4
CHUNK = 128
D_XBC = D_SSD + 2 * N_GROUPS * D_STATE
D_SC = 1024
SC_CONV = 3
D_MIX = D_SSD + D_SC
N_EXPERTS = 32
TOP_K = 4
EXPERT_FF = 1024
SWIGLU_LIMIT = 7.0
SWIGLU_ALPHA = 1.702
EPS = 1e-5

VMEM_LIMIT_BYTES = 56 * 1024 * 1024
SUBLANES = 8
ROW_BLOCK = 512
EXPERT_SUB = 256
IN_TILE = 512
IN_SUB = 256
MIX_CHUNKS = 4
OUT_TILE = 1024
OUT_SUB = 1024
COMBINE_TILE = 512
MOE_GROUPS = 2
COMBINE_CHUNKS = 4
LANES = 128
ROW_PARTS = D_MODEL // (2 * LANES)
SC_WINDOW = 128
HIGH_HALF = -65536

F32 = jnp.float32
BF16 = jnp.bfloat16


def _dot(a, b):
    return jnp.dot(a, b, preferred_element_type=F32)


def _split3(v):
    hi = v.astype(BF16)
    r1 = v - hi.astype(F32)
    mid = r1.astype(BF16)
    lo = (r1 - mid.astype(F32)).astype(BF16)
    return hi, mid, lo


def _dot_exact_rhs01(v, m01):
    hi, mid, lo = _split3(v)
    return _dot(hi, m01) + _dot(mid, m01) + _dot(lo, m01)


def _dot_exact_lhs01(m01, v):
    hi, mid, lo = _split3(v)
    return _dot(m01, hi) + _dot(m01, mid) + _dot(m01, lo)


def _silu(v):
    return v * (1.0 / (1.0 + jnp.exp(-v)))


def _softplus(v):
    return jnp.maximum(v, 0.0) + jnp.log1p(jnp.exp(-jnp.abs(v)))


def _pack_rows(v):
    bits = lambda a: lax.bitcast_convert_type(a.astype(BF16).astype(F32), jnp.int32)
    parts = []
    for c in range(ROW_PARTS):
        lo = v[:, (2 * c) * LANES:(2 * c + 1) * LANES]
        hi = v[:, (2 * c + 1) * LANES:(2 * c + 2) * LANES]
        parts.append(bits(hi) | lax.shift_right_logical(bits(lo), 16))
    return parts


def _unpack_rows(parts):
    cols = []
    for w in parts:
        cols.append(lax.bitcast_convert_type(w << 16, F32))
        cols.append(lax.bitcast_convert_type(w & HIGH_HALF, F32))
    return jnp.concatenate(cols, axis=-1)


def _shifted(cur, tail, s):
    if s == 0:
        return cur
    rc = pltpu.roll(cur, s, axis=0)
    rt = pltpu.roll(tail, s, axis=0)
    row = lax.broadcasted_iota(jnp.int32, tail.shape, 0)
    first = jnp.where(row < s, rt, rc[0:SUBLANES])
    return jnp.concatenate([first, rc[SUBLANES:]], axis=0)


def _causal_conv(tail_ref, cur, w_ref, acc):
    rows = cur.shape[0]
    n_taps = w_ref.shape[0]
    tail = tail_ref[...]
    for k in range(n_taps):
        acc = acc + w_ref[k:k + 1, :] * _shifted(cur, tail, n_taps - 1 - k)
    tail_ref[...] = cur[rows - SUBLANES:rows]
    return acc


def _inproj_kernel(tiles_per_seq, x_ref, nw_ref, wzx_ref, wsc_ref, wdt_ref, wdtt_ref,
                   cw_ref, cb_ref, scw_ref, scnw_ref,
                   z_ref, xbc_ref, ysc_ref, dt_ref, dtt_ref, hist_ref, schist_ref):
    tm = x_ref.shape[0]

    @pl.when(pl.program_id(0) % tiles_per_seq == 0)
    def _():
        hist_ref[...] = jnp.zeros_like(hist_ref)
        schist_ref[...] = jnp.zeros_like(schist_ref)

    for r in range(0, tm, IN_SUB):
        rows = pl.ds(r, IN_SUB)
        x = x_ref[rows, :]
        xn = x * lax.rsqrt(jnp.mean(x * x, axis=-1, keepdims=True) + EPS) * nw_ref[...]
        xb = xn.astype(BF16)
        proj = lambda w_ref, o, n, xb=xb: _dot(xb, w_ref[:, o:o + n])
        z_ref[rows, :] = proj(wzx_ref, 0, D_SSD).astype(z_ref.dtype)

        conv = _causal_conv(hist_ref, proj(wzx_ref, D_SSD, D_XBC), cw_ref, cb_ref[...])
        xbc_ref[rows, :] = _silu(conv).astype(xbc_ref.dtype)

        cu = proj(wsc_ref, D_SC, D_SC) * proj(wsc_ref, 2 * D_SC, D_SC)
        v = _causal_conv(schist_ref, cu, scw_ref, jnp.zeros_like(cu))
        gv = proj(wsc_ref, 0, D_SC) * v
        y_sc = gv * lax.rsqrt(jnp.mean(gv * gv, axis=-1, keepdims=True) + EPS) * scnw_ref[...]
        ysc_ref[rows, :] = y_sc.astype(ysc_ref.dtype)

        dt_ref[rows, :] = _dot(xb, wdt_ref[...])
        dtt_ref[:, rows] = lax.dot_general(wdtt_ref[...], xb, (((1,), (1,)), ((), ())),
                                           preferred_element_type=F32)


def _inproj(x2, seq, nw, w_zx, w_sc, w_dt, w_dtt, cw, cb, scw, scnw):
    t = x2.shape[0]
    tm = IN_TILE
    assert seq % tm == 0, "in-proj tiles must not straddle sequences (causal conv carry)"
    row = lambda n: pl.BlockSpec((tm, n), lambda i: (i, 0))
    full = lambda a: pl.BlockSpec(a.shape, lambda i: (0,) * a.ndim)
    params = (nw, w_zx, w_sc, w_dt, w_dtt, cw, cb, scw, scnw)
    return pl.pallas_call(
        functools.partial(_inproj_kernel, seq // tm),
        grid=(t // tm,),
        in_specs=[row(D_MODEL)] + [full(a) for a in params],
        out_specs=[row(D_SSD), row(D_XBC), row(D_SC),
                   row(N_SSD_HEADS), pl.BlockSpec((N_SSD_HEADS, tm), lambda i: (0, i))],
        out_shape=[jax.ShapeDtypeStruct((t, D_SSD), BF16),
                   jax.ShapeDtypeStruct((t, D_XBC), BF16),
                   jax.ShapeDtypeStruct((t, D_SC), BF16),
                   jax.ShapeDtypeStruct((t, N_SSD_HEADS), F32),
                   jax.ShapeDtypeStruct((N_SSD_HEADS, t), F32)],
        scratch_shapes=[pltpu.VMEM((SUBLANES, D_XBC), F32), pltpu.VMEM((SUBLANES, D_SC), F32)],
        compiler_params=pltpu.CompilerParams(
            dimension_semantics=("arbitrary",), vmem_limit_bytes=VMEM_LIMIT_BYTES),
    )(x2, *params)


def _mixer_kernel(z_ref, xbc_ref, dt_ref, dtt_ref,
                  dtb_ref, dtbt_ref, alog_ref, alogt_ref, dskip_ref, nw_ref, out_ref, state_ref):
    q = CHUNK

    @pl.when(pl.program_id(1) == 0)
    def _():
        state_ref[...] = jnp.zeros_like(state_ref)

    ri = lax.broadcasted_iota(jnp.int32, (q, q), 0)
    ci = lax.broadcasted_iota(jnp.int32, (q, q), 1)
    causal = ri >= ci
    tri = jnp.where(causal, 1.0, 0.0).astype(BF16)
    trit = jnp.where(ri <= ci, 1.0, 0.0).astype(BF16)
    hh = lax.broadcasted_iota(jnp.int32, (N_SSD_HEADS, D_SSD), 0)
    hl = lax.broadcasted_iota(jnp.int32, (N_SSD_HEADS, D_SSD), 1) // SSD_HEAD_DIM
    expand = jnp.where(hh == hl, 1.0, 0.0).astype(BF16)
    lane_lo = lax.broadcasted_iota(jnp.int32, (q, 2 * SSD_HEAD_DIM), 1) < SSD_HEAD_DIM
    gw = HEADS_PER_GROUP * SSD_HEAD_DIM
    n_bc = N_GROUPS * D_STATE
    a_neg = -jnp.exp(alog_ref[...])
    a_negt = -jnp.exp(alogt_ref[...])
    d_x = _dot_exact_rhs01(dskip_ref[...], expand)

    for s in range(MIX_CHUNKS):
        rows = pl.ds(s * q, q)
        xs = xbc_ref[rows, :D_SSD].astype(F32)
        b_all = xbc_ref[rows, D_SSD:D_SSD + n_bc]
        c_all = xbc_ref[rows, D_SSD + n_bc:]

        dt = _softplus(dt_ref[rows, :] + dtb_ref[...])
        dtt = _softplus(dtt_ref[:, rows] + dtbt_ref[...])
        acum = _dot_exact_lhs01(tri, dt * a_neg)
        acumt = _dot_exact_rhs01(dtt * a_negt, trit)
        a_last = acum[q - 1:q, :]

        xdt = xs * _dot_exact_rhs01(dt, expand)
        xdt_b = xdt.astype(BF16)
        dec_in = _dot_exact_rhs01(jnp.exp(acum), expand)
        dec_out = _dot_exact_rhs01(jnp.exp(a_last - acum), expand)
        dec_all = _dot_exact_rhs01(jnp.exp(a_last), expand)
        xw_b = (xdt * dec_out).astype(BF16)

        y_parts = []
        for g in range(N_GROUPS):
            b_g = b_all[:, g * D_STATE:(g + 1) * D_STATE]
            c_g = c_all[:, g * D_STATE:(g + 1) * D_STATE]
            cb = lax.dot_general(c_g, b_g, (((1,), (1,)), ((), ())), preferred_element_type=F32)
            st = state_ref[:, g * gw:(g + 1) * gw]
            y_inter = _dot(c_g, st.astype(BF16)) * dec_in[:, g * gw:(g + 1) * gw]
            new_st = lax.dot_general(b_g, xw_b[:, g * gw:(g + 1) * gw], (((0,), (0,)), ((), ())),
                                     preferred_element_type=F32)
            state_ref[:, g * gw:(g + 1) * gw] = st * dec_all[:, g * gw:(g + 1) * gw] + new_st
            pair_parts = []
            for pr in range(HEADS_PER_GROUP // 2):
                h0 = g * HEADS_PER_GROUP + 2 * pr
                lo = h0 * SSD_HEAD_DIM
                rhs = xdt_b[:, lo:lo + 2 * SSD_HEAD_DIM]
                ys = []
                for h in (h0, h0 + 1):
                    seg = acum[:, h:h + 1] - acumt[h:h + 1, :]
                    m = (cb * jnp.exp(jnp.where(causal, seg, -jnp.inf))).astype(BF16)
                    ys.append(_dot(m, rhs))
                pair_parts.append(jnp.where(lane_lo, ys[0], ys[1]))
            y_parts.append(jnp.concatenate(pair_parts, axis=-1) + y_inter)
        y = jnp.concatenate(y_parts, axis=-1) + d_x * xs

        gated = y * _silu(z_ref[rows, :].astype(F32))
        outs = []
        for g in range(N_GROUPS):
            gg = gated[:, g * gw:(g + 1) * gw]
            outs.append(gg * lax.rsqrt(jnp.mean(gg * gg, axis=-1, keepdims=True) + EPS))
        out_ref[rows, :] = (jnp.concatenate(outs, axis=-1) * nw_ref[...]).astype(out_ref.dtype)


def _mixer(z, xbc, dt, dtt, batch, seq, dtb, alog, dskip, nw):
    rows = CHUNK * MIX_CHUNKS
    assert seq % rows == 0
    ns = seq // rows
    t = batch * seq
    row = lambda n: pl.BlockSpec((rows, n), lambda b, c: (b * ns + c, 0))
    full = lambda a: pl.BlockSpec(a.shape, lambda b, c: (0,) * a.ndim)
    dtbt, alogt = dtb.reshape(N_SSD_HEADS, 1), alog.reshape(N_SSD_HEADS, 1)
    params = (dtb, dtbt, alog, alogt, dskip, nw)
    return pl.pallas_call(
        _mixer_kernel,
        grid=(batch, ns),
        in_specs=[row(D_SSD), row(D_XBC), row(N_SSD_HEADS),
                  pl.BlockSpec((N_SSD_HEADS, rows), lambda b, c: (0, b * ns + c))]
                 + [full(a) for a in params],
        out_specs=row(D_SSD),
        out_shape=jax.ShapeDtypeStruct((t, D_SSD), BF16),
        scratch_shapes=[pltpu.VMEM((D_STATE, D_SSD), F32)],
        compiler_params=pltpu.CompilerParams(
            dimension_semantics=("arbitrary", "arbitrary"), vmem_limit_bytes=VMEM_LIMIT_BYTES),
    )(z, xbc, dt, dtt, *params)


def _outproj_kernel(yssd_ref, ysc_ref, x_ref, wout_ref, nw_ref, wrh_ref, wrl_ref, br_ref,
                    h1_ref, xn_ref, idx_ref, gate_ref, rank_ref, cnt_ref, run_ref):
    tm = x_ref.shape[0]

    @pl.when(pl.program_id(0) == 0)
    def _():
        run_ref[...] = jnp.zeros_like(run_ref)

    sub = OUT_SUB
    nt = lambda w, a: lax.dot_general(w, a, (((1,), (1,)), ((), ())), preferred_element_type=F32)
    expert = lax.broadcasted_iota(jnp.int32, (N_EXPERTS, sub), 0).astype(F32)
    slot = lax.broadcasted_iota(jnp.int32, (TOP_K, sub), 0)
    ri = lax.broadcasted_iota(jnp.int32, (sub, sub), 0)
    ci = lax.broadcasted_iota(jnp.int32, (sub, sub), 1)
    earlier = jnp.where(ri < ci, 1.0, 0.0).astype(BF16)

    run = run_ref[...]
    for r in range(0, tm, sub):
        rows = pl.ds(r, sub)
        h1 = (x_ref[rows, :] + _dot(yssd_ref[rows, :], wout_ref[:D_SSD, :])
              + _dot(ysc_ref[rows, :], wout_ref[D_SSD:, :]))
        h1_ref[rows, :] = h1
        xn = h1 * lax.rsqrt(jnp.mean(h1 * h1, axis=-1, keepdims=True) + EPS) * nw_ref[...]
        for c, words in enumerate(_pack_rows(xn)):
            xn_ref[c, rows, :] = words

        xh = xn.astype(BF16)
        xl = (xn - xh.astype(F32)).astype(BF16)
        logits = (nt(wrh_ref[...], xh) + nt(wrl_ref[...], xh) + nt(wrh_ref[...], xl)
                  + br_ref[...])

        work = logits
        vals, firsts, sels = [], [], []
        for _ in range(TOP_K):
            m = jnp.max(work, axis=0, keepdims=True)
            first = jnp.min(jnp.where(work == m, expert, float(N_EXPERTS)), axis=0,
                            keepdims=True)
            sel = expert == first
            vals.append(m)
            firsts.append(first)
            sels.append(sel)
            work = jnp.where(sel, -jnp.inf, work)
        exps = [jnp.exp(v - vals[0]) for v in vals]
        denom = exps[0] + exps[1] + exps[2] + exps[3]

        onehot = jnp.zeros((N_EXPERTS, sub), F32)
        for sel in sels:
            onehot = onehot + jnp.where(sel, 1.0, 0.0)
        before = run + _dot(onehot.astype(BF16), earlier)
        run = run + jnp.sum(onehot, axis=1, keepdims=True)

        idx = jnp.zeros((TOP_K, sub), F32)
        gate = jnp.zeros((TOP_K, sub), F32)
        rank = jnp.zeros((TOP_K, sub), F32)
        for k in range(TOP_K):
            r_k = jnp.sum(jnp.where(sels[k], before, 0.0), axis=0, keepdims=True)
            idx = jnp.where(slot == k, firsts[k], idx)
            gate = jnp.where(slot == k, exps[k] / denom, gate)
            rank = jnp.where(slot == k, r_k, rank)
        idx_ref[:, rows] = idx.astype(jnp.int32)
        gate_ref[:, rows] = gate
        rank_ref[:, rows] = rank.astype(jnp.int32)
    run_ref[...] = run
    cnt_ref[...] = run


def _outproj(yssd, ysc, x2, w_out, nw, wr_hi, wr_lo, br, first_tile, t):
    tm = OUT_TILE
    src = lambda n: pl.BlockSpec((tm, n), lambda i: (first_tile + i, 0))
    row = lambda n: pl.BlockSpec((tm, n), lambda i: (i, 0))
    slots = pl.BlockSpec((TOP_K, tm), lambda i: (0, i))
    full = lambda a: pl.BlockSpec(a.shape, lambda i: (0,) * a.ndim)
    return pl.pallas_call(
        _outproj_kernel,
        grid=(t // tm,),
        in_specs=[src(D_SSD), src(D_SC), src(D_MODEL), full(w_out), full(nw), full(wr_hi),
                  full(wr_lo), full(br)],
        out_specs=[row(D_MODEL), pl.BlockSpec((ROW_PARTS, tm, LANES), lambda i: (0, i, 0)),
                   slots, slots, slots,
                   pl.BlockSpec((N_EXPERTS, 1), lambda i: (0, 0))],
        out_shape=[jax.ShapeDtypeStruct((t, D_MODEL), F32),
                   jax.ShapeDtypeStruct((ROW_PARTS, t, LANES), jnp.int32),
                   jax.ShapeDtypeStruct((TOP_K, t), jnp.int32),
                   jax.ShapeDtypeStruct((TOP_K, t), F32),
                   jax.ShapeDtypeStruct((TOP_K, t), jnp.int32),
                   jax.ShapeDtypeStruct((N_EXPERTS, 1), F32)],
        scratch_shapes=[pltpu.VMEM((N_EXPERTS, 1), F32)],
        compiler_params=pltpu.CompilerParams(
            dimension_semantics=("arbitrary",), vmem_limit_bytes=VMEM_LIMIT_BYTES),
    )(yssd, ysc, x2, w_out, nw, wr_hi, wr_lo, br)


def _dest_kernel(pstart_ref, idx_ref, rank_ref, dest_ref):
    idx = idx_ref[...]
    base = jnp.zeros_like(idx)
    for e in range(N_EXPERTS):
        base = jnp.where(idx == e, pstart_ref[e], base)
    dest_ref[...] = base + rank_ref[...]


def _dest(pstarts, idx, rank):
    full = lambda a: pl.BlockSpec(a.shape, lambda i, ps: (0,) * a.ndim)
    return pl.pallas_call(
        _dest_kernel,
        grid_spec=pltpu.PrefetchScalarGridSpec(
            num_scalar_prefetch=1, grid=(1,),
            in_specs=[full(idx), full(rank)], out_specs=full(idx)),
        out_shape=jax.ShapeDtypeStruct(idx.shape, jnp.int32),
    )(pstarts, idx, rank)


def _sc_mesh():
    return plsc.VectorSubcoreMesh(core_axis_name="core", subcore_axis_name="subcore")


def _part_rows(dest, n_rows):
    off = (jnp.arange(ROW_PARTS, dtype=jnp.int32) * n_rows)[:, None, None]
    return (dest[None] + off).reshape(ROW_PARTS * dest.shape[0], dest.shape[1])


def _sc_dispatch(x, dest, n_rows):
    parts, t, width = x.shape
    n_slots = dest.shape[0]
    win = SC_WINDOW
    nwin = t // win

    @functools.partial(pl.kernel,
                       out_type=jax.ShapeDtypeStruct((parts * n_rows, width), jnp.int32),
                       mesh=_sc_mesh(), scratch_types=[])
    def scatter(x_hbm, idx_hbm, out_hbm):
        def body(x_vmem, idx_vmem):
            for k in range(n_slots):
                pltpu.sync_copy(x_vmem, out_hbm.at[idx_vmem.at[k]])

        pltpu.emit_pipeline(
            body,
            grid=(parts * nwin,),
            in_specs=[pl.BlockSpec((win, width), lambda j: (j, 0)),
                      pl.BlockSpec((n_slots, win), lambda j: (j // nwin, j % nwin))],
            out_specs=[],
            core_axis_name=("core", "subcore"),
            dimension_semantics=(pltpu.PARALLEL,),
        )(x_hbm, idx_hbm)

    xs = scatter(x.reshape(parts * t, width), _part_rows(dest, n_rows))
    return xs.reshape(parts, n_rows, width)


def _sc_collect(ys, dest):
    parts, n_rows, width = ys.shape
    n_slots, t = dest.shape
    win = SC_WINDOW
    nwin = t // win
    idx = _part_rows(dest, n_rows).reshape(parts, n_slots, t).swapaxes(0, 1)
    idx = idx.reshape(n_slots * parts, t)

    @functools.partial(pl.kernel,
                       out_type=jax.ShapeDtypeStruct((n_slots * parts * t, width), jnp.int32),
                       mesh=_sc_mesh(), scratch_types=[])
    def gather(y_hbm, idx_hbm, out_hbm):
        def body(idx_vmem, out_vmem):
            pltpu.sync_copy(y_hbm.at[idx_vmem.at[0]], out_vmem)

        pltpu.emit_pipeline(
            body,
            grid=(n_slots * parts * nwin,),
            in_specs=[pl.BlockSpec((1, win), lambda j: (j // nwin, j % nwin))],
            out_specs=[pl.BlockSpec((win, width), lambda j: (j, 0))],
            core_axis_name=("core", "subcore"),
            dimension_semantics=(pltpu.PARALLEL,),
        )(idx_hbm, out_hbm)

    out = gather(ys.reshape(parts * n_rows, width), idx)
    return out.reshape(n_slots, parts, t, width)


def _expert_kernel(blk_e_ref, blk_n_ref, xs_ref, wgu_ref, bgu_ref, wd_ref, bd_ref, out_ref,
                   wgu_b, wd_b):
    i = pl.program_id(0)
    n_valid = blk_n_ref[i]

    @pl.when((i == 0) | (blk_e_ref[i] != blk_e_ref[jnp.maximum(i - 1, 0)]))
    def _():
        wgu_b[...] = wgu_ref[0].astype(BF16)
        wd_b[...] = wd_ref[0].astype(BF16)

    @pl.when(n_valid == 0)
    def _():
        out_ref[...] = jnp.zeros_like(out_ref)

    @pl.when(n_valid > 0)
    def _():
        for r in range(0, ROW_BLOCK, EXPERT_SUB):
            words = [xs_ref[c, r:r + EXPERT_SUB] for c in range(ROW_PARTS)]
            rows = r + lax.broadcasted_iota(jnp.int32, words[0].shape, 0)
            xb = _unpack_rows([jnp.where(rows < n_valid, w, 0) for w in words]).astype(BF16)
            gu = _dot(xb, wgu_b[...]) + bgu_ref[0]
            gate = jnp.minimum(gu[:, :EXPERT_FF], SWIGLU_LIMIT)
            up = jnp.clip(gu[:, EXPERT_FF:], -SWIGLU_LIMIT, SWIGLU_LIMIT)
            hid = (up + 1.0) * (gate * (1.0 / (1.0 + jnp.exp(-SWIGLU_ALPHA * gate))))
            out = _dot(hid.astype(BF16), wd_b[...]) + bd_ref[0]
            for c, w in enumerate(_pack_rows(out)):
                out_ref[c, r:r + EXPERT_SUB] = w


def _experts(blk_e, blk_n, xs, wgu, bgu, wd, bd):
    rb = ROW_BLOCK
    by_expert = lambda a: pl.BlockSpec((1,) + a.shape[1:], lambda i, be, bn: (be[i], 0, 0))
    return pl.pallas_call(
        _expert_kernel,
        grid_spec=pltpu.PrefetchScalarGridSpec(
            num_scalar_prefetch=2, grid=(xs.shape[1] // rb,),
            in_specs=[pl.BlockSpec((ROW_PARTS, rb, LANES), lambda i, be, bn: (0, i, 0)),
                      by_expert(wgu), by_expert(bgu), by_expert(wd), by_expert(bd)],
            out_specs=pl.BlockSpec((ROW_PARTS, rb, LANES), lambda i, be, bn: (0, i, 0)),
            scratch_shapes=[pltpu.VMEM(wgu.shape[1:], BF16), pltpu.VMEM(wd.shape[1:], BF16)]),
        out_shape=jax.ShapeDtypeStruct(xs.shape, xs.dtype),
        compiler_params=pltpu.CompilerParams(
            dimension_semantics=("arbitrary",), vmem_limit_bytes=VMEM_LIMIT_BYTES),
    )(blk_e, blk_n, xs, wgu, bgu, wd, bd)


def _combine_kernel(gate_ref, h1_ref, fw_ref, y0_ref, y1_ref, y2_ref, y3_ref, *rest):
    out_ref = rest[-1]
    parts = _split3(gate_ref[...])
    slot = lax.broadcasted_iota(jnp.int32, (TOP_K, LANES), 0)
    h2 = h1_ref[...]
    for k, y_ref in enumerate((y0_ref, y1_ref, y2_ref, y3_ref)):
        pick = jnp.where(slot == k, 1.0, 0.0).astype(BF16)
        g = sum(lax.dot_general(p, pick, (((0,), (0,)), ((), ())), preferred_element_type=F32)
                for p in parts)
        y = _unpack_rows([y_ref[c] for c in range(ROW_PARTS)])
        h2 = h2 + jnp.concatenate([g] * (D_MODEL // LANES), axis=-1) * y
    out_ref[...] = h2 * lax.rsqrt(jnp.mean(h2 * h2, axis=-1, keepdims=True) + EPS) * fw_ref[...]


def _combine(gates, h1, fw, planes, first_tile, out_tile, t, out_prev):
    d = h1.shape[1]
    tm = COMBINE_TILE
    n = planes.shape[2]
    plane = lambda k: pl.BlockSpec((None, ROW_PARTS, tm, LANES), lambda i: (k, 0, i, 0))
    in_specs = [pl.BlockSpec((TOP_K, tm), lambda i: (0, first_tile + i)),
                pl.BlockSpec((tm, d), lambda i: (first_tile + i, 0)),
                pl.BlockSpec((1, d), lambda i: (0, 0))]
    in_specs += [plane(k) for k in range(TOP_K)]
    args = [gates, h1, fw] + [planes] * TOP_K
    aliases = {}
    if out_prev is not None:
        in_specs.append(pl.BlockSpec(memory_space=pl.ANY))
        aliases = {len(args): 0}
        args.append(out_prev)
    return pl.pallas_call(
        _combine_kernel,
        grid=(n // tm,),
        in_specs=in_specs,
        out_specs=pl.BlockSpec((tm, d), lambda i: (out_tile + i, 0)),
        out_shape=jax.ShapeDtypeStruct((t, d), F32),
        input_output_aliases=aliases,
        compiler_params=pltpu.CompilerParams(
            dimension_semantics=("arbitrary",), vmem_limit_bytes=VMEM_LIMIT_BYTES),
    )(*args)


def _block_plan(counts, n_blocks):
    rb = ROW_BLOCK
    pcounts = (counts + rb - 1) // rb * rb
    pends = jnp.cumsum(pcounts)
    pstarts = pends - pcounts
    block_start = jnp.arange(n_blocks, dtype=jnp.int32) * rb
    blk_e = jnp.minimum(jnp.sum(block_start[:, None] >= pends[None, :], axis=1),
                        N_EXPERTS - 1).astype(jnp.int32)
    blk_n = jnp.clip(counts[blk_e] - (block_start - pstarts[blk_e]), 0, rb).astype(jnp.int32)
    return pstarts.astype(jnp.int32), blk_e, blk_n


def kernel(x, norm1_w, w_in, ssd_conv_w, ssd_conv_b, dt_bias, a_log, d_skip, ssd_norm_w,
           sc_conv_w, sc_norm_w, w_out, norm2_w, w_router, b_router, w_gate_up, b_gate_up,
           w_down, b_down, final_norm_w):
    assert norm1_w.shape[0] == 1, "single-layer problem"
    batch, seq, d = x.shape
    t = batch * seq
    x2 = x.reshape(t, d)
    row = lambda a: a.reshape(1, -1)

    s1, s2 = D_SSD + D_XBC, D_SSD + D_XBC + N_SSD_HEADS
    w = w_in[0]
    w_zx, w_dt, w_sc = w[:, :s1].astype(BF16), w[:, s1:s2].astype(BF16), w[:, s2:].astype(BF16)
    z, xbc, ysc, dt, dtt = _inproj(x2, seq, row(norm1_w[0]), w_zx, w_sc, w_dt, w_dt.T,
                                   ssd_conv_w[0], row(ssd_conv_b[0]), sc_conv_w[0],
                                   row(sc_norm_w[0]))
    yssd = _mixer(z, xbc, dt, dtt, batch, seq, row(dt_bias[0]), row(a_log[0]), row(d_skip[0]),
                  row(ssd_norm_w[0]))

    wr = w_router[0].T
    wr_hi = wr.astype(BF16)
    wr_lo = (wr - wr_hi.astype(F32)).astype(BF16)
    assert t % (MOE_GROUPS * COMBINE_CHUNKS * COMBINE_TILE) == 0 and t % (MOE_GROUPS * OUT_TILE) == 0
    tg = t // MOE_GROUPS
    n_rows = tg * TOP_K + N_EXPERTS * ROW_BLOCK
    w_out_b, nw2, br = w_out[0].astype(BF16), row(norm2_w[0]), b_router[0].reshape(-1, 1)
    routed = []
    for g in range(MOE_GROUPS):
        h1, xn2, idx, gates, rank, counts = _outproj(
            yssd, ysc, x2, w_out_b, nw2, wr_hi, wr_lo, br, g * (tg // OUT_TILE), tg)
        pstarts, blk_e, blk_n = _block_plan(counts[:, 0].astype(jnp.int32),
                                            n_rows // ROW_BLOCK)
        dest = _dest(pstarts, idx, rank)
        xs = _sc_dispatch(xn2, dest, n_rows)
        routed.append((h1, gates, dest, blk_e, blk_n, xs))
    expert_out = [_experts(blk_e, blk_n, xs, w_gate_up[0], b_gate_up[0][:, None, :],
                           w_down[0], b_down[0][:, None, :])
                  for (_, _, _, blk_e, blk_n, xs) in routed]
    tc = tg // COMBINE_CHUNKS
    fw, out = row(final_norm_w), None
    for g, ((h1, gates, dest, _, _, _), ys) in enumerate(zip(routed, expert_out)):
        for c in range(COMBINE_CHUNKS):
            planes = _sc_collect(ys, dest[:, c * tc:(c + 1) * tc])
            out = _combine(gates, h1, fw, planes, c * (tc // COMBINE_TILE),
                           (g * tg + c * tc) // COMBINE_TILE, t, out)
    return out.reshape(batch, seq, d)
```

```python
import functools

import jax
import jax.numpy as jnp
from jax import lax
from jax.experimental import pallas as pl
from jax.experimental.pallas import tpu as pltpu
from jax.experimental.pallas import tpu_sc as plsc

D_MODEL = 1024
D_SSD = 1024
SSD_HEAD_DIM = 64
N_SSD_HEADS = 16
N_GROUPS = 2
HEADS_PER_GROUP = 8
D_STATE = 128
SSD_CONV = 4
CHUNK = 128
D_XBC = D_SSD + 2 * N_GROUPS * D_STATE
D_SC = 1024
SC_CONV = 3
D_MIX = D_SSD + D_SC
N_EXPERTS = 32
TOP_K = 4
EXPERT_FF = 1024
SWIGLU_LIMIT = 7.0
SWIGLU_ALPHA = 1.702
EPS = 1e-5

VMEM_LIMIT_BYTES = 56 * 1024 * 1024
SUBLANES = 8
ROW_BLOCK = 512
EXPERT_SUB = 256
IN_TILE = 512
IN_SUB = 256
MIX_CHUNKS = 4
OUT_TILE = 1024
OUT_SUB = 1024
COMBINE_TILE = 512
MOE_SPLIT = (5, 3)
COMBINE_CHUNKS = 4
LANES = 128
ROW_PARTS = D_MODEL // (2 * LANES)
SC_WINDOW = 128
HIGH_HALF = -65536

F32 = jnp.float32
BF16 = jnp.bfloat16


def _dot(a, b):
    return jnp.dot(a, b, preferred_element_type=F32)


def _split3(v):
    hi = v.astype(BF16)
    r1 = v - hi.astype(F32)
    mid = r1.astype(BF16)
    lo = (r1 - mid.astype(F32)).astype(BF16)
    return hi, mid, lo


def _dot_exact_rhs01(v, m01):
    hi, mid, lo = _split3(v)
    return _dot(hi, m01) + _dot(mid, m01) + _dot(lo, m01)


def _dot_exact_lhs01(m01, v):
    hi, mid, lo = _split3(v)
    return _dot(m01, hi) + _dot(m01, mid) + _dot(m01, lo)


def _silu(v):
    return v * (1.0 / (1.0 + jnp.exp(-v)))


def _softplus(v):
    return jnp.maximum(v, 0.0) + jnp.log1p(jnp.exp(-jnp.abs(v)))


def _pack_rows(v):
    bits = lambda a: lax.bitcast_convert_type(a.astype(BF16).astype(F32), jnp.int32)
    parts = []
    for c in range(ROW_PARTS):
        lo = v[:, (2 * c) * LANES:(2 * c + 1) * LANES]
        hi = v[:, (2 * c + 1) * LANES:(2 * c + 2) * LANES]
        parts.append(bits(hi) | lax.shift_right_logical(bits(lo), 16))
    return parts


def _unpack_rows(parts):
    cols = []
    for w in parts:
        cols.append(lax.bitcast_convert_type(w << 16, F32))
        cols.append(lax.bitcast_convert_type(w & HIGH_HALF, F32))
    return jnp.concatenate(cols, axis=-1)


def _shifted(cur, tail, s):
    if s == 0:
        return cur
    rc = pltpu.roll(cur, s, axis=0)
    rt = pltpu.roll(tail, s, axis=0)
    row = lax.broadcasted_iota(jnp.int32, tail.shape, 0)
    first = jnp.where(row < s, rt, rc[0:SUBLANES])
    return jnp.concatenate([first, rc[SUBLANES:]], axis=0)


def _causal_conv(tail_ref, cur, w_ref, acc):
    rows = cur.shape[0]
    n_taps = w_ref.shape[0]
    tail = tail_ref[...]
    for k in range(n_taps):
        acc = acc + w_ref[k:k + 1, :] * _shifted(cur, tail, n_taps - 1 - k)
    tail_ref[...] = cur[rows - SUBLANES:rows]
    return acc


def _inproj_kernel(tiles_per_seq, x_ref, nw_ref, wzx_ref, wsc_ref, wdt_ref, wdtt_ref,
                   cw_ref, cb_ref, scw_ref, scnw_ref,
                   z_ref, xbc_ref, ysc_ref, dt_ref, dtt_ref, hist_ref, schist_ref):
    tm = x_ref.shape[0]

    @pl.when(pl.program_id(0) % tiles_per_seq == 0)
    def _():
        hist_ref[...] = jnp.zeros_like(hist_ref)
        schist_ref[...] = jnp.zeros_like(schist_ref)

    for r in range(0, tm, IN_SUB):
        rows = pl.ds(r, IN_SUB)
        x = x_ref[rows, :]
        xn = x * lax.rsqrt(jnp.mean(x * x, axis=-1, keepdims=True) + EPS) * nw_ref[...]
        xb = xn.astype(BF16)
        proj = lambda w_ref, o, n, xb=xb: _dot(xb, w_ref[:, o:o + n])
        z_ref[rows, :] = proj(wzx_ref, 0, D_SSD).astype(z_ref.dtype)

        conv = _causal_conv(hist_ref, proj(wzx_ref, D_SSD, D_XBC), cw_ref, cb_ref[...])
        xbc_ref[rows, :] = _silu(conv).astype(xbc_ref.dtype)

        cu = proj(wsc_ref, D_SC, D_SC) * proj(wsc_ref, 2 * D_SC, D_SC)
        v = _causal_conv(schist_ref, cu, scw_ref, jnp.zeros_like(cu))
        gv = proj(wsc_ref, 0, D_SC) * v
        y_sc = gv * lax.rsqrt(jnp.mean(gv * gv, axis=-1, keepdims=True) + EPS) * scnw_ref[...]
        ysc_ref[rows, :] = y_sc.astype(ysc_ref.dtype)

        dt_ref[rows, :] = _dot(xb, wdt_ref[...])
        dtt_ref[:, rows] = lax.dot_general(wdtt_ref[...], xb, (((1,), (1,)), ((), ())),
                                           preferred_element_type=F32)


def _inproj(x2, seq, nw, w_zx, w_sc, w_dt, w_dtt, cw, cb, scw, scnw):
    t = x2.shape[0]
    tm = IN_TILE
    assert seq % tm == 0, "in-proj tiles must not straddle sequences (causal conv carry)"
    row = lambda n: pl.BlockSpec((tm, n), lambda i: (i, 0))
    full = lambda a: pl.BlockSpec(a.shape, lambda i: (0,) * a.ndim)
    params = (nw, w_zx, w_sc, w_dt, w_dtt, cw, cb, scw, scnw)
    return pl.pallas_call(
        functools.partial(_inproj_kernel, seq // tm),
        grid=(t // tm,),
        in_specs=[row(D_MODEL)] + [full(a) for a in params],
        out_specs=[row(D_SSD), row(D_XBC), row(D_SC),
                   row(N_SSD_HEADS), pl.BlockSpec((N_SSD_HEADS, tm), lambda i: (0, i))],
        out_shape=[jax.ShapeDtypeStruct((t, D_SSD), BF16),
                   jax.ShapeDtypeStruct((t, D_XBC), BF16),
                   jax.ShapeDtypeStruct((t, D_SC), BF16),
                   jax.ShapeDtypeStruct((t, N_SSD_HEADS), F32),
                   jax.ShapeDtypeStruct((N_SSD_HEADS, t), F32)],
        scratch_shapes=[pltpu.VMEM((SUBLANES, D_XBC), F32), pltpu.VMEM((SUBLANES, D_SC), F32)],
        compiler_params=pltpu.CompilerParams(
            dimension_semantics=("arbitrary",), vmem_limit_bytes=VMEM_LIMIT_BYTES),
    )(x2, *params)


def _mixer_kernel(z_ref, xbc_ref, dt_ref, dtt_ref,
                  dtb_ref, dtbt_ref, alog_ref, alogt_ref, dskip_ref, nw_ref, out_ref, state_ref):
    q = CHUNK

    @pl.when(pl.program_id(1) == 0)
    def _():
        state_ref[...] = jnp.zeros_like(state_ref)

    ri = lax.broadcasted_iota(jnp.int32, (q, q), 0)
    ci = lax.broadcasted_iota(jnp.int32, (q, q), 1)
    causal = ri >= ci
    tri = jnp.where(causal, 1.0, 0.0).astype(BF16)
    trit = jnp.where(ri <= ci, 1.0, 0.0).astype(BF16)
    hh = lax.broadcasted_iota(jnp.int32, (N_SSD_HEADS, D_SSD), 0)
    hl = lax.broadcasted_iota(jnp.int32, (N_SSD_HEADS, D_SSD), 1) // SSD_HEAD_DIM
    expand = jnp.where(hh == hl, 1.0, 0.0).astype(BF16)
    lane_lo = lax.broadcasted_iota(jnp.int32, (q, 2 * SSD_HEAD_DIM), 1) < SSD_HEAD_DIM
    gw = HEADS_PER_GROUP * SSD_HEAD_DIM
    n_bc = N_GROUPS * D_STATE
    a_neg = -jnp.exp(alog_ref[...])
    a_negt = -jnp.exp(alogt_ref[...])
    d_x = _dot_exact_rhs01(dskip_ref[...], expand)

    for s in range(MIX_CHUNKS):
        rows = pl.ds(s * q, q)
        xs = xbc_ref[rows, :D_SSD].astype(F32)
        b_all = xbc_ref[rows, D_SSD:D_SSD + n_bc]
        c_all = xbc_ref[rows, D_SSD + n_bc:]

        dt = _softplus(dt_ref[rows, :] + dtb_ref[...])
        dtt = _softplus(dtt_ref[:, rows] + dtbt_ref[...])
        acum = _dot_exact_lhs01(tri, dt * a_neg)
        acumt = _dot_exact_rhs01(dtt * a_negt, trit)
        a_last = acum[q - 1:q, :]

        xdt = xs * _dot_exact_rhs01(dt, expand)
        xdt_b = xdt.astype(BF16)
        dec_in = _dot_exact_rhs01(jnp.exp(acum), expand)
        dec_out = _dot_exact_rhs01(jnp.exp(a_last - acum), expand)
        dec_all = _dot_exact_rhs01(jnp.exp(a_last), expand)
        xw_b = (xdt * dec_out).astype(BF16)

        y_parts = []
        for g in range(N_GROUPS):
            b_g = b_all[:, g * D_STATE:(g + 1) * D_STATE]
            c_g = c_all[:, g * D_STATE:(g + 1) * D_STATE]
            cb = lax.dot_general(c_g, b_g, (((1,), (1,)), ((), ())), preferred_element_type=F32)
            st = state_ref[:, g * gw:(g + 1) * gw]
            y_inter = _dot(c_g, st.astype(BF16)) * dec_in[:, g * gw:(g + 1) * gw]
            new_st = lax.dot_general(b_g, xw_b[:, g * gw:(g + 1) * gw], (((0,), (0,)), ((), ())),
                                     preferred_element_type=F32)
            state_ref[:, g * gw:(g + 1) * gw] = st * dec_all[:, g * gw:(g + 1) * gw] + new_st
            pair_parts = []
            for pr in range(HEADS_PER_GROUP // 2):
                h0 = g * HEADS_PER_GROUP + 2 * pr
                lo = h0 * SSD_HEAD_DIM
                rhs = xdt_b[:, lo:lo + 2 * SSD_HEAD_DIM]
                ys = []
                for h in (h0, h0 + 1):
                    seg = acum[:, h:h + 1] - acumt[h:h + 1, :]
                    m = (cb * jnp.exp(jnp.where(causal, seg, -jnp.inf))).astype(BF16)
                    ys.append(_dot(m, rhs))
                pair_parts.append(jnp.where(lane_lo, ys[0], ys[1]))
            y_parts.append(jnp.concatenate(pair_parts, axis=-1) + y_inter)
        y = jnp.concatenate(y_parts, axis=-1) + d_x * xs

        gated = y * _silu(z_ref[rows, :].astype(F32))
        outs = []
        for g in range(N_GROUPS):
            gg = gated[:, g * gw:(g + 1) * gw]
            outs.append(gg * lax.rsqrt(jnp.mean(gg * gg, axis=-1, keepdims=True) + EPS))
        out_ref[rows, :] = (jnp.concatenate(outs, axis=-1) * nw_ref[...]).astype(out_ref.dtype)


def _mixer(z, xbc, dt, dtt, batch, seq, dtb, alog, dskip, nw):
    rows = CHUNK * MIX_CHUNKS
    assert seq % rows == 0
    ns = seq // rows
    t = batch * seq
    row = lambda n: pl.BlockSpec((rows, n), lambda b, c: (b * ns + c, 0))
    full = lambda a: pl.BlockSpec(a.shape, lambda b, c: (0,) * a.ndim)
    dtbt, alogt = dtb.reshape(N_SSD_HEADS, 1), alog.reshape(N_SSD_HEADS, 1)
    params = (dtb, dtbt, alog, alogt, dskip, nw)
    return pl.pallas_call(
        _mixer_kernel,
        grid=(batch, ns),
        in_specs=[row(D_SSD), row(D_XBC), row(N_SSD_HEADS),
                  pl.BlockSpec((N_SSD_HEADS, rows), lambda b, c: (0, b * ns + c))]
                 + [full(a) for a in params],
        out_specs=row(D_SSD),
        out_shape=jax.ShapeDtypeStruct((t, D_SSD), BF16),
        scratch_shapes=[pltpu.VMEM((D_STATE, D_SSD), F32)],
        compiler_params=pltpu.CompilerParams(
            dimension_semantics=("arbitrary", "arbitrary"), vmem_limit_bytes=VMEM_LIMIT_BYTES),
    )(z, xbc, dt, dtt, *params)


def _outproj_kernel(yssd_ref, ysc_ref, x_ref, wout_ref, nw_ref, wrh_ref, wrl_ref, br_ref,
                    h1_ref, xn_ref, idx_ref, gate_ref, rank_ref, cnt_ref, run_ref):
    tm = x_ref.shape[0]

    @pl.when(pl.program_id(0) == 0)
    def _():
        run_ref[...] = jnp.zeros_like(run_ref)

    sub = OUT_SUB
    nt = lambda w, a: lax.dot_general(w, a, (((1,), (1,)), ((), ())), preferred_element_type=F32)
    expert = lax.broadcasted_iota(jnp.int32, (N_EXPERTS, sub), 0).astype(F32)
    slot = lax.broadcasted_iota(jnp.int32, (TOP_K, sub), 0)
    ri = lax.broadcasted_iota(jnp.int32, (sub, sub), 0)
    ci = lax.broadcasted_iota(jnp.int32, (sub, sub), 1)
    earlier = jnp.where(ri < ci, 1.0, 0.0).astype(BF16)

    run = run_ref[...]
    for r in range(0, tm, sub):
        rows = pl.ds(r, sub)
        h1 = (x_ref[rows, :] + _dot(yssd_ref[rows, :], wout_ref[:D_SSD, :])
              + _dot(ysc_ref[rows, :], wout_ref[D_SSD:, :]))
        h1_ref[rows, :] = h1
        xn = h1 * lax.rsqrt(jnp.mean(h1 * h1, axis=-1, keepdims=True) + EPS) * nw_ref[...]
        for c, words in enumerate(_pack_rows(xn)):
            xn_ref[c, rows, :] = words

        xh = xn.astype(BF16)
        xl = (xn - xh.astype(F32)).astype(BF16)
        logits = (nt(wrh_ref[...], xh) + nt(wrl_ref[...], xh) + nt(wrh_ref[...], xl)
                  + br_ref[...])

        work = logits
        vals, firsts, sels = [], [], []
        for _ in range(TOP_K):
            m = jnp.max(work, axis=0, keepdims=True)
            first = jnp.min(jnp.where(work == m, expert, float(N_EXPERTS)), axis=0,
                            keepdims=True)
            sel = expert == first
            vals.append(m)
            firsts.append(first)
            sels.append(sel)
            work = jnp.where(sel, -jnp.inf, work)
        exps = [jnp.exp(v - vals[0]) for v in vals]
        denom = exps[0] + exps[1] + exps[2] + exps[3]

        onehot = jnp.zeros((N_EXPERTS, sub), F32)
        for sel in sels:
            onehot = onehot + jnp.where(sel, 1.0, 0.0)
        before = run + _dot(onehot.astype(BF16), earlier)
        run = run + jnp.sum(onehot, axis=1, keepdims=True)

        idx = jnp.zeros((TOP_K, sub), F32)
        gate = jnp.zeros((TOP_K, sub), F32)
        rank = jnp.zeros((TOP_K, sub), F32)
        for k in range(TOP_K):
            r_k = jnp.sum(jnp.where(sels[k], before, 0.0), axis=0, keepdims=True)
            idx = jnp.where(slot == k, firsts[k], idx)
            gate = jnp.where(slot == k, exps[k] / denom, gate)
            rank = jnp.where(slot == k, r_k, rank)
        idx_ref[:, rows] = idx.astype(jnp.int32)
        gate_ref[:, rows] = gate
        rank_ref[:, rows] = rank.astype(jnp.int32)
    run_ref[...] = run
    cnt_ref[...] = run


def _outproj(yssd, ysc, x2, w_out, nw, wr_hi, wr_lo, br, first_tile, t):
    tm = OUT_TILE
    src = lambda n: pl.BlockSpec((tm, n), lambda i: (first_tile + i, 0))
    row = lambda n: pl.BlockSpec((tm, n), lambda i: (i, 0))
    slots = pl.BlockSpec((TOP_K, tm), lambda i: (0, i))
    full = lambda a: pl.BlockSpec(a.shape, lambda i: (0,) * a.ndim)
    return pl.pallas_call(
        _outproj_kernel,
        grid=(t // tm,),
        in_specs=[src(D_SSD), src(D_SC), src(D_MODEL), full(w_out), full(nw), full(wr_hi),
                  full(wr_lo), full(br)],
        out_specs=[row(D_MODEL), pl.BlockSpec((ROW_PARTS, tm, LANES), lambda i: (0, i, 0)),
                   slots, slots, slots,
                   pl.BlockSpec((N_EXPERTS, 1), lambda i: (0, 0))],
        out_shape=[jax.ShapeDtypeStruct((t, D_MODEL), F32),
                   jax.ShapeDtypeStruct((ROW_PARTS, t, LANES), jnp.int32),
                   jax.ShapeDtypeStruct((TOP_K, t), jnp.int32),
                   jax.ShapeDtypeStruct((TOP_K, t), F32),
                   jax.ShapeDtypeStruct((TOP_K, t), jnp.int32),
                   jax.ShapeDtypeStruct((N_EXPERTS, 1), F32)],
        scratch_shapes=[pltpu.VMEM((N_EXPERTS, 1), F32)],
        compiler_params=pltpu.CompilerParams(
            dimension_semantics=("arbitrary",), vmem_limit_bytes=VMEM_LIMIT_BYTES),
    )(yssd, ysc, x2, w_out, nw, wr_hi, wr_lo, br)


def _dest_kernel(pstart_ref, idx_ref, rank_ref, dest_ref):
    idx = idx_ref[...]
    base = jnp.zeros_like(idx)
    for e in range(N_EXPERTS):
        base = jnp.where(idx == e, pstart_ref[e], base)
    dest_ref[...] = base + rank_ref[...]


def _dest(pstarts, idx, rank):
    full = lambda a: pl.BlockSpec(a.shape, lambda i, ps: (0,) * a.ndim)
    return pl.pallas_call(
        _dest_kernel,
        grid_spec=pltpu.PrefetchScalarGridSpec(
            num_scalar_prefetch=1, grid=(1,),
            in_specs=[full(idx), full(rank)], out_specs=full(idx)),
        out_shape=jax.ShapeDtypeStruct(idx.shape, jnp.int32),
    )(pstarts, idx, rank)


def _sc_mesh():
    return plsc.VectorSubcoreMesh(core_axis_name="core", subcore_axis_name="subcore")


def _part_rows(dest, n_rows):
    off = (jnp.arange(ROW_PARTS, dtype=jnp.int32) * n_rows)[:, None, None]
    return (dest[None] + off).reshape(ROW_PARTS * dest.shape[0], dest.shape[1])


def _sc_dispatch(x, dest, n_rows):
    parts, t, width = x.shape
    n_slots = dest.shape[0]
    win = SC_WINDOW
    nwin = t // win

    @functools.partial(pl.kernel,
                       out_type=jax.ShapeDtypeStruct((parts * n_rows, width), jnp.int32),
                       mesh=_sc_mesh(), scratch_types=[])
    def scatter(x_hbm, idx_hbm, out_hbm):
        def body(x_vmem, idx_vmem):
            for k in range(n_slots):
                pltpu.sync_copy(x_vmem, out_hbm.at[idx_vmem.at[k]])

        pltpu.emit_pipeline(
            body,
            grid=(parts * nwin,),
            in_specs=[pl.BlockSpec((win, width), lambda j: (j, 0)),
                      pl.BlockSpec((n_slots, win), lambda j: (j // nwin, j % nwin))],
            out_specs=[],
            core_axis_name=("core", "subcore"),
            dimension_semantics=(pltpu.PARALLEL,),
        )(x_hbm, idx_hbm)

    xs = scatter(x.reshape(parts * t, width), _part_rows(dest, n_rows))
    return xs.reshape(parts, n_rows, width)


def _sc_collect(ys, dest):
    parts, n_rows, width = ys.shape
    n_slots, t = dest.shape
    win = SC_WINDOW
    nwin = t // win
    idx = _part_rows(dest, n_rows).reshape(parts, n_slots, t).swapaxes(0, 1)
    idx = idx.reshape(n_slots * parts, t)

    @functools.partial(pl.kernel,
                       out_type=jax.ShapeDtypeStruct((n_slots * parts * t, width), jnp.int32),
                       mesh=_sc_mesh(), scratch_types=[])
    def gather(y_hbm, idx_hbm, out_hbm):
        def body(idx_vmem, out_vmem):
            pltpu.sync_copy(y_hbm.at[idx_vmem.at[0]], out_vmem)

        pltpu.emit_pipeline(
            body,
            grid=(n_slots * parts * nwin,),
            in_specs=[pl.BlockSpec((1, win), lambda j: (j // nwin, j % nwin))],
            out_specs=[pl.BlockSpec((win, width), lambda j: (j, 0))],
            core_axis_name=("core", "subcore"),
            dimension_semantics=(pltpu.PARALLEL,),
        )(idx_hbm, out_hbm)

    out = gather(ys.reshape(parts * n_rows, width), idx)
    return out.reshape(n_slots, parts, t, width)


def _expert_kernel(blk_e_ref, blk_n_ref, xs_ref, wgu_ref, bgu_ref, wd_ref, bd_ref, out_ref,
                   wgu_b, wd_b):
    i = pl.program_id(0)
    n_valid = blk_n_ref[i]

    @pl.when((i == 0) | (blk_e_ref[i] != blk_e_ref[jnp.maximum(i - 1, 0)]))
    def _():
        wgu_b[...] = wgu_ref[0].astype(BF16)
        wd_b[...] = wd_ref[0].astype(BF16)

    @pl.when(n_valid == 0)
    def _():
        out_ref[...] = jnp.zeros_like(out_ref)

    @pl.when(n_valid > 0)
    def _():
        for r in range(0, ROW_BLOCK, EXPERT_SUB):
            words = [xs_ref[c, r:r + EXPERT_SUB] for c in range(ROW_PARTS)]
            rows = r + lax.broadcasted_iota(jnp.int32, words[0].shape, 0)
            xb = _unpack_rows([jnp.where(rows < n_valid, w, 0) for w in words]).astype(BF16)
            gu = _dot(xb, wgu_b[...]) + bgu_ref[0]
            gate = jnp.minimum(gu[:, :EXPERT_FF], SWIGLU_LIMIT)
            up = jnp.clip(gu[:, EXPERT_FF:], -SWIGLU_LIMIT, SWIGLU_LIMIT)
            hid = (up + 1.0) * (gate * (1.0 / (1.0 + jnp.exp(-SWIGLU_ALPHA * gate))))
            out = _dot(hid.astype(BF16), wd_b[...]) + bd_ref[0]
            for c, w in enumerate(_pack_rows(out)):
                out_ref[c, r:r + EXPERT_SUB] = w


def _experts(blk_e, blk_n, xs, wgu, bgu, wd, bd):
    rb = ROW_BLOCK
    by_expert = lambda a: pl.BlockSpec((1,) + a.shape[1:], lambda i, be, bn: (be[i], 0, 0))
    return pl.pallas_call(
        _expert_kernel,
        grid_spec=pltpu.PrefetchScalarGridSpec(
            num_scalar_prefetch=2, grid=(xs.shape[1] // rb,),
            in_specs=[pl.BlockSpec((ROW_PARTS, rb, LANES), lambda i, be, bn: (0, i, 0)),
                      by_expert(wgu), by_expert(bgu), by_expert(wd), by_expert(bd)],
            out_specs=pl.BlockSpec((ROW_PARTS, rb, LANES), lambda i, be, bn: (0, i, 0)),
            scratch_shapes=[pltpu.VMEM(wgu.shape[1:], BF16), pltpu.VMEM(wd.shape[1:], BF16)]),
        out_shape=jax.ShapeDtypeStruct(xs.shape, xs.dtype),
        compiler_params=pltpu.CompilerParams(
            dimension_semantics=("arbitrary",), vmem_limit_bytes=VMEM_LIMIT_BYTES),
    )(blk_e, blk_n, xs, wgu, bgu, wd, bd)


def _combine_kernel(gate_ref, h1_ref, fw_ref, y0_ref, y1_ref, y2_ref, y3_ref, *rest):
    out_ref = rest[-1]
    parts = _split3(gate_ref[...])
    slot = lax.broadcasted_iota(jnp.int32, (TOP_K, LANES), 0)
    h2 = h1_ref[...]
    for k, y_ref in enumerate((y0_ref, y1_ref, y2_ref, y3_ref)):
        pick = jnp.where(slot == k, 1.0, 0.0).astype(BF16)
        g = sum(lax.dot_general(p, pick, (((0,), (0,)), ((), ())), preferred_element_type=F32)
                for p in parts)
        y = _unpack_rows([y_ref[c] for c in range(ROW_PARTS)])
        h2 = h2 + jnp.concatenate([g] * (D_MODEL // LANES), axis=-1) * y
    out_ref[...] = h2 * lax.rsqrt(jnp.mean(h2 * h2, axis=-1, keepdims=True) + EPS) * fw_ref[...]


def _combine(gates, h1, fw, planes, first_tile, out_tile, t, out_prev):
    d = h1.shape[1]
    tm = COMBINE_TILE
    n = planes.shape[2]
    plane = lambda k: pl.BlockSpec((None, ROW_PARTS, tm, LANES), lambda i: (k, 0, i, 0))
    in_specs = [pl.BlockSpec((TOP_K, tm), lambda i: (0, first_tile + i)),
                pl.BlockSpec((tm, d), lambda i: (first_tile + i, 0)),
                pl.BlockSpec((1, d), lambda i: (0, 0))]
    in_specs += [plane(k) for k in range(TOP_K)]
    args = [gates, h1, fw] + [planes] * TOP_K
    aliases = {}
    if out_prev is not None:
        in_specs.append(pl.BlockSpec(memory_space=pl.ANY))
        aliases = {len(args): 0}
        args.append(out_prev)
    return pl.pallas_call(
        _combine_kernel,
        grid=(n // tm,),
        in_specs=in_specs,
        out_specs=pl.BlockSpec((tm, d), lambda i: (out_tile + i, 0)),
        out_shape=jax.ShapeDtypeStruct((t, d), F32),
        input_output_aliases=aliases,
        compiler_params=pltpu.CompilerParams(
            dimension_semantics=("arbitrary",), vmem_limit_bytes=VMEM_LIMIT_BYTES),
    )(*args)


def _block_plan(counts, n_blocks):
    rb = ROW_BLOCK
    pcounts = (counts + rb - 1) // rb * rb
    pends = jnp.cumsum(pcounts)
    pstarts = pends - pcounts
    block_start = jnp.arange(n_blocks, dtype=jnp.int32) * rb
    blk_e = jnp.minimum(jnp.sum(block_start[:, None] >= pends[None, :], axis=1),
                        N_EXPERTS - 1).astype(jnp.int32)
    mine = blk_e[:, None] == jnp.arange(N_EXPERTS, dtype=jnp.int32)[None, :]
    pick = lambda table: jnp.sum(jnp.where(mine, table[None, :], 0), axis=1)
    blk_n = jnp.clip(pick(counts) - (block_start - pick(pstarts)), 0, rb).astype(jnp.int32)
    return pstarts.astype(jnp.int32), blk_e, blk_n


def kernel(x, norm1_w, w_in, ssd_conv_w, ssd_conv_b, dt_bias, a_log, d_skip, ssd_norm_w,
           sc_conv_w, sc_norm_w, w_out, norm2_w, w_router, b_router, w_gate_up, b_gate_up,
           w_down, b_down, final_norm_w):
    assert norm1_w.shape[0] == 1, "single-layer problem"
    batch, seq, d = x.shape
    t = batch * seq
    x2 = x.reshape(t, d)
    row = lambda a: a.reshape(1, -1)

    s1, s2 = D_SSD + D_XBC, D_SSD + D_XBC + N_SSD_HEADS
    w = w_in[0]
    w_zx, w_dt, w_sc = w[:, :s1].astype(BF16), w[:, s1:s2].astype(BF16), w[:, s2:].astype(BF16)
    z, xbc, ysc, dt, dtt = _inproj(x2, seq, row(norm1_w[0]), w_zx, w_sc, w_dt, w_dt.T,
                                   ssd_conv_w[0], row(ssd_conv_b[0]), sc_conv_w[0],
                                   row(sc_norm_w[0]))
    yssd = _mixer(z, xbc, dt, dtt, batch, seq, row(dt_bias[0]), row(a_log[0]), row(d_skip[0]),
                  row(ssd_norm_w[0]))

    wr = w_router[0].T
    wr_hi = wr.astype(BF16)
    wr_lo = (wr - wr_hi.astype(F32)).astype(BF16)
    unit = t // sum(MOE_SPLIT)
    sizes = [unit * s for s in MOE_SPLIT]
    starts = [sum(sizes[:g]) for g in range(len(sizes))]
    assert sum(sizes) == t and all(n % (COMBINE_CHUNKS * COMBINE_TILE) == 0 and n % OUT_TILE == 0
                                   for n in sizes)
    w_out_b, nw2, br = w_out[0].astype(BF16), row(norm2_w[0]), b_router[0].reshape(-1, 1)
    routed = []
    for t0, tg in zip(starts, sizes):
        n_rows = tg * TOP_K + N_EXPERTS * ROW_BLOCK
        h1, xn2, idx, gates, rank, counts = _outproj(
            yssd, ysc, x2, w_out_b, nw2, wr_hi, wr_lo, br, t0 // OUT_TILE, tg)
        pstarts, blk_e, blk_n = _block_plan(counts[:, 0].astype(jnp.int32),
                                            n_rows // ROW_BLOCK)
        dest = _dest(pstarts, idx, rank)
        xs = _sc_dispatch(xn2, dest, n_rows)
        routed.append((h1, gates, dest, blk_e, blk_n, xs))
    expert_out = [_experts(blk_e, blk_n, xs, w_gate_up[0], b_gate_up[0][:, None, :],
                           w_down[0], b_down[0][:, None, :])
                  for (_, _, _, blk_e, blk_n, xs) in routed]
    fw, out = row(final_norm_w), None
    for t0, tg, (h1, gates, dest, _, _, _), ys in zip(starts, sizes, routed, expert_out):
        tc = tg // COMBINE_CHUNKS
        for c in range(COMBINE_CHUNKS):
            planes = _sc_collect(ys, dest[:, c * tc:(c + 1) * tc])
            out = _combine(gates, h1, fw, planes, c * (tc // COMBINE_TILE),
                           (t0 + c * tc) // COMBINE_TILE, t, out)
    return out.reshape(batch, seq, d)
```

```python
import functools

import jax
import jax.numpy as jnp
from jax import lax
from jax.experimental import pallas as pl
from jax.experimental.pallas import tpu as pltpu
from jax.experimental.pallas import tpu_sc as plsc

D_MODEL = 1024
D_SSD = 1024
SSD_HEAD_DIM = 64
N_SSD_HEADS = 16
N_GROUPS = 2
HEADS_PER_GROUP = 8
D_STATE = 128
SSD_CONV = 4
CHUNK = 128
D_XBC = D_SSD + 2 * N_GROUPS * D_STATE
D_SC = 1024
SC_CONV = 3
D_MIX = D_SSD + D_SC
N_EXPERTS = 32
TOP_K = 4
EXPERT_FF = 1024
SWIGLU_LIMIT = 7.0
SWIGLU_ALPHA = 1.702
EPS = 1e-5

VMEM_LIMIT_BYTES = 56 * 1024 * 1024
SUBLANES = 8
ROW_BLOCK = 512
EXPERT_SUB = 256
IN_TILE = 512
IN_SUB = 256
MIX_CHUNKS = 4
OUT_TILE = 1024
OUT_SUB = 1024
COMBINE_TILE = 512
MOE_SPLIT = (3, 1)
COMBINE_CHUNKS = 4
LANES = 128
ROW_PARTS = D_MODEL // (2 * LANES)
SC_WINDOW = 128
HIGH_HALF = -65536

F32 = jnp.float32
BF16 = jnp.bfloat16


def _dot(a, b):
    return jnp.dot(a, b, preferred_element_type=F32)


def _split3(v):
    hi = v.astype(BF16)
    r1 = v - hi.astype(F32)
    mid = r1.astype(BF16)
    lo = (r1 - mid.astype(F32)).astype(BF16)
    return hi, mid, lo


def _dot_exact_rhs01(v, m01):
    hi, mid, lo = _split3(v)
    return _dot(hi, m01) + _dot(mid, m01) + _dot(lo, m01)


def _dot_exact_lhs01(m01, v):
    hi, mid, lo = _split3(v)
    return _dot(m01, hi) + _dot(m01, mid) + _dot(m01, lo)


def _silu(v):
    return v * (1.0 / (1.0 + jnp.exp(-v)))


def _softplus(v):
    return jnp.maximum(v, 0.0) + jnp.log1p(jnp.exp(-jnp.abs(v)))


def _pack_rows(v):
    bits = lambda a: lax.bitcast_convert_type(a.astype(BF16).astype(F32), jnp.int32)
    parts = []
    for c in range(ROW_PARTS):
        lo = v[:, (2 * c) * LANES:(2 * c + 1) * LANES]
        hi = v[:, (2 * c + 1) * LANES:(2 * c + 2) * LANES]
        parts.append(bits(hi) | lax.shift_right_logical(bits(lo), 16))
    return parts


def _unpack_rows(parts):
    cols = []
    for w in parts:
        cols.append(lax.bitcast_convert_type(w << 16, F32))
        cols.append(lax.bitcast_convert_type(w & HIGH_HALF, F32))
    return jnp.concatenate(cols, axis=-1)


def _shifted(cur, tail, s):
    if s == 0:
        return cur
    rc = pltpu.roll(cur, s, axis=0)
    rt = pltpu.roll(tail, s, axis=0)
    row = lax.broadcasted_iota(jnp.int32, tail.shape, 0)
    first = jnp.where(row < s, rt, rc[0:SUBLANES])
    return jnp.concatenate([first, rc[SUBLANES:]], axis=0)


def _causal_conv(tail_ref, cur, w_ref, acc):
    rows = cur.shape[0]
    n_taps = w_ref.shape[0]
    tail = tail_ref[...]
    for k in range(n_taps):
        acc = acc + w_ref[k:k + 1, :] * _shifted(cur, tail, n_taps - 1 - k)
    tail_ref[...] = cur[rows - SUBLANES:rows]
    return acc


def _inproj_kernel(tiles_per_seq, x_ref, nw_ref, wzx_ref, wsc_ref, wdt_ref, wdtt_ref,
                   cw_ref, cb_ref, scw_ref, scnw_ref,
                   z_ref, xbc_ref, ysc_ref, dt_ref, dtt_ref, hist_ref, schist_ref):
    tm = x_ref.shape[0]

    @pl.when(pl.program_id(0) % tiles_per_seq == 0)
    def _():
        hist_ref[...] = jnp.zeros_like(hist_ref)
        schist_ref[...] = jnp.zeros_like(schist_ref)

    for r in range(0, tm, IN_SUB):
        rows = pl.ds(r, IN_SUB)
        x = x_ref[rows, :]
        xn = x * lax.rsqrt(jnp.mean(x * x, axis=-1, keepdims=True) + EPS) * nw_ref[...]
        xb = xn.astype(BF16)
        proj = lambda w_ref, o, n, xb=xb: _dot(xb, w_ref[:, o:o + n])
        z_ref[rows, :] = proj(wzx_ref, 0, D_SSD).astype(z_ref.dtype)

        conv = _causal_conv(hist_ref, proj(wzx_ref, D_SSD, D_XBC), cw_ref, cb_ref[...])
        xbc_ref[rows, :] = _silu(conv).astype(xbc_ref.dtype)

        cu = proj(wsc_ref, D_SC, D_SC) * proj(wsc_ref, 2 * D_SC, D_SC)
        v = _causal_conv(schist_ref, cu, scw_ref, jnp.zeros_like(cu))
        gv = proj(wsc_ref, 0, D_SC) * v
        y_sc = gv * lax.rsqrt(jnp.mean(gv * gv, axis=-1, keepdims=True) + EPS) * scnw_ref[...]
        ysc_ref[rows, :] = y_sc.astype(ysc_ref.dtype)

        dt_ref[rows, :] = _dot(xb, wdt_ref[...])
        dtt_ref[:, rows] = lax.dot_general(wdtt_ref[...], xb, (((1,), (1,)), ((), ())),
                                           preferred_element_type=F32)


def _inproj(x2, seq, nw, w_zx, w_sc, w_dt, w_dtt, cw, cb, scw, scnw):
    t = x2.shape[0]
    tm = IN_TILE
    assert seq % tm == 0, "in-proj tiles must not straddle sequences (causal conv carry)"
    row = lambda n: pl.BlockSpec((tm, n), lambda i: (i, 0))
    full = lambda a: pl.BlockSpec(a.shape, lambda i: (0,) * a.ndim)
    params = (nw, w_zx, w_sc, w_dt, w_dtt, cw, cb, scw, scnw)
    return pl.pallas_call(
        functools.partial(_inproj_kernel, seq // tm),
        grid=(t // tm,),
        in_specs=[row(D_MODEL)] + [full(a) for a in params],
        out_specs=[row(D_SSD), row(D_XBC), row(D_SC),
                   row(N_SSD_HEADS), pl.BlockSpec((N_SSD_HEADS, tm), lambda i: (0, i))],
        out_shape=[jax.ShapeDtypeStruct((t, D_SSD), BF16),
                   jax.ShapeDtypeStruct((t, D_XBC), BF16),
                   jax.ShapeDtypeStruct((t, D_SC), BF16),
                   jax.ShapeDtypeStruct((t, N_SSD_HEADS), F32),
                   jax.ShapeDtypeStruct((N_SSD_HEADS, t), F32)],
        scratch_shapes=[pltpu.VMEM((SUBLANES, D_XBC), F32), pltpu.VMEM((SUBLANES, D_SC), F32)],
        compiler_params=pltpu.CompilerParams(
            dimension_semantics=("arbitrary",), vmem_limit_bytes=VMEM_LIMIT_BYTES),
    )(x2, *params)


def _mixer_kernel(z_ref, xbc_ref, dt_ref, dtt_ref,
                  dtb_ref, dtbt_ref, alog_ref, alogt_ref, dskip_ref, nw_ref, out_ref, state_ref):
    q = CHUNK

    @pl.when(pl.program_id(1) == 0)
    def _():
        state_ref[...] = jnp.zeros_like(state_ref)

    ri = lax.broadcasted_iota(jnp.int32, (q, q), 0)
    ci = lax.broadcasted_iota(jnp.int32, (q, q), 1)
    causal = ri >= ci
    tri = jnp.where(causal, 1.0, 0.0).astype(BF16)
    trit = jnp.where(ri <= ci, 1.0, 0.0).astype(BF16)
    hh = lax.broadcasted_iota(jnp.int32, (N_SSD_HEADS, D_SSD), 0)
    hl = lax.broadcasted_iota(jnp.int32, (N_SSD_HEADS, D_SSD), 1) // SSD_HEAD_DIM
    expand = jnp.where(hh == hl, 1.0, 0.0).astype(BF16)
    lane_lo = lax.broadcasted_iota(jnp.int32, (q, 2 * SSD_HEAD_DIM), 1) < SSD_HEAD_DIM
    gw = HEADS_PER_GROUP * SSD_HEAD_DIM
    n_bc = N_GROUPS * D_STATE
    a_neg = -jnp.exp(alog_ref[...])
    a_negt = -jnp.exp(alogt_ref[...])
    d_x = _dot_exact_rhs01(dskip_ref[...], expand)

    for s in range(MIX_CHUNKS):
        rows = pl.ds(s * q, q)
        xs = xbc_ref[rows, :D_SSD].astype(F32)
        b_all = xbc_ref[rows, D_SSD:D_SSD + n_bc]
        c_all = xbc_ref[rows, D_SSD + n_bc:]

        dt = _softplus(dt_ref[rows, :] + dtb_ref[...])
        dtt = _softplus(dtt_ref[:, rows] + dtbt_ref[...])
        acum = _dot_exact_lhs01(tri, dt * a_neg)
        acumt = _dot_exact_rhs01(dtt * a_negt, trit)
        a_last = acum[q - 1:q, :]

        xdt = xs * _dot_exact_rhs01(dt, expand)
        xdt_b = xdt.astype(BF16)
        dec_in = _dot_exact_rhs01(jnp.exp(acum), expand)
        dec_out = _dot_exact_rhs01(jnp.exp(a_last - acum), expand)
        dec_all = _dot_exact_rhs01(jnp.exp(a_last), expand)
        xw_b = (xdt * dec_out).astype(BF16)

        y_parts = []
        for g in range(N_GROUPS):
            b_g = b_all[:, g * D_STATE:(g + 1) * D_STATE]
            c_g = c_all[:, g * D_STATE:(g + 1) * D_STATE]
            cb = lax.dot_general(c_g, b_g, (((1,), (1,)), ((), ())), preferred_element_type=F32)
            st = state_ref[:, g * gw:(g + 1) * gw]
            y_inter = _dot(c_g, st.astype(BF16)) * dec_in[:, g * gw:(g + 1) * gw]
            new_st = lax.dot_general(b_g, xw_b[:, g * gw:(g + 1) * gw], (((0,), (0,)), ((), ())),
                                     preferred_element_type=F32)
            state_ref[:, g * gw:(g + 1) * gw] = st * dec_all[:, g * gw:(g + 1) * gw] + new_st
            pair_parts = []
            for pr in range(HEADS_PER_GROUP // 2):
                h0 = g * HEADS_PER_GROUP + 2 * pr
                lo = h0 * SSD_HEAD_DIM
                rhs = xdt_b[:, lo:lo + 2 * SSD_HEAD_DIM]
                ys = []
                for h in (h0, h0 + 1):
                    seg = acum[:, h:h + 1] - acumt[h:h + 1, :]
                    m = (cb * jnp.exp(jnp.where(causal, seg, -jnp.inf))).astype(BF16)
                    ys.append(_dot(m, rhs))
                pair_parts.append(jnp.where(lane_lo, ys[0], ys[1]))
            y_parts.append(jnp.concatenate(pair_parts, axis=-1) + y_inter)
        y = jnp.concatenate(y_parts, axis=-1) + d_x * xs

        gated = y * _silu(z_ref[rows, :].astype(F32))
        outs = []
        for g in range(N_GROUPS):
            gg = gated[:, g * gw:(g + 1) * gw]
            outs.append(gg * lax.rsqrt(jnp.mean(gg * gg, axis=-1, keepdims=True) + EPS))
        out_ref[rows, :] = (jnp.concatenate(outs, axis=-1) * nw_ref[...]).astype(out_ref.dtype)


def _mixer(z, xbc, dt, dtt, batch, seq, dtb, alog, dskip, nw):
    rows = CHUNK * MIX_CHUNKS
    assert seq % rows == 0
    ns = seq // rows
    t = batch * seq
    row = lambda n: pl.BlockSpec((rows, n), lambda b, c: (b * ns + c, 0))
    full = lambda a: pl.BlockSpec(a.shape, lambda b, c: (0,) * a.ndim)
    dtbt, alogt = dtb.reshape(N_SSD_HEADS, 1), alog.reshape(N_SSD_HEADS, 1)
    params = (dtb, dtbt, alog, alogt, dskip, nw)
    return pl.pallas_call(
        _mixer_kernel,
        grid=(batch, ns),
        in_specs=[row(D_SSD), row(D_XBC), row(N_SSD_HEADS),
                  pl.BlockSpec((N_SSD_HEADS, rows), lambda b, c: (0, b * ns + c))]
                 + [full(a) for a in params],
        out_specs=row(D_SSD),
        out_shape=jax.ShapeDtypeStruct((t, D_SSD), BF16),
        scratch_shapes=[pltpu.VMEM((D_STATE, D_SSD), F32)],
        compiler_params=pltpu.CompilerParams(
            dimension_semantics=("arbitrary", "arbitrary"), vmem_limit_bytes=VMEM_LIMIT_BYTES),
    )(z, xbc, dt, dtt, *params)


def _outproj_kernel(yssd_ref, ysc_ref, x_ref, wout_ref, nw_ref, wrh_ref, wrl_ref, br_ref,
                    h1_ref, xn_ref, idx_ref, gate_ref, rank_ref, cnt_ref, run_ref):
    tm = x_ref.shape[0]

    @pl.when(pl.program_id(0) == 0)
    def _():
        run_ref[...] = jnp.zeros_like(run_ref)

    sub = OUT_SUB
    nt = lambda w, a: lax.dot_general(w, a, (((1,), (1,)), ((), ())), preferred_element_type=F32)
    expert = lax.broadcasted_iota(jnp.int32, (N_EXPERTS, sub), 0).astype(F32)
    slot = lax.broadcasted_iota(jnp.int32, (TOP_K, sub), 0)
    ri = lax.broadcasted_iota(jnp.int32, (sub, sub), 0)
    ci = lax.broadcasted_iota(jnp.int32, (sub, sub), 1)
    earlier = jnp.where(ri < ci, 1.0, 0.0).astype(BF16)

    run = run_ref[...]
    for r in range(0, tm, sub):
        rows = pl.ds(r, sub)
        h1 = (x_ref[rows, :] + _dot(yssd_ref[rows, :], wout_ref[:D_SSD, :])
              + _dot(ysc_ref[rows, :], wout_ref[D_SSD:, :]))
        h1_ref[rows, :] = h1
        xn = h1 * lax.rsqrt(jnp.mean(h1 * h1, axis=-1, keepdims=True) + EPS) * nw_ref[...]
        for c, words in enumerate(_pack_rows(xn)):
            xn_ref[c, rows, :] = words

        xh = xn.astype(BF16)
        xl = (xn - xh.astype(F32)).astype(BF16)
        logits = (nt(wrh_ref[...], xh) + nt(wrl_ref[...], xh) + nt(wrh_ref[...], xl)
                  + br_ref[...])

        work = logits
        vals, firsts, sels = [], [], []
        for _ in range(TOP_K):
            m = jnp.max(work, axis=0, keepdims=True)
            first = jnp.min(jnp.where(work == m, expert, float(N_EXPERTS)), axis=0,
                            keepdims=True)
            sel = expert == first
            vals.append(m)
            firsts.append(first)
            sels.append(sel)
            work = jnp.where(sel, -jnp.inf, work)
        exps = [jnp.exp(v - vals[0]) for v in vals]
        denom = exps[0] + exps[1] + exps[2] + exps[3]

        onehot = jnp.zeros((N_EXPERTS, sub), F32)
        for sel in sels:
            onehot = onehot + jnp.where(sel, 1.0, 0.0)
        before = run + _dot(onehot.astype(BF16), earlier)
        run = run + jnp.sum(onehot, axis=1, keepdims=True)

        idx = jnp.zeros((TOP_K, sub), F32)
        gate = jnp.zeros((TOP_K, sub), F32)
        rank = jnp.zeros((TOP_K, sub), F32)
        for k in range(TOP_K):
            r_k = jnp.sum(jnp.where(sels[k], before, 0.0), axis=0, keepdims=True)
            idx = jnp.where(slot == k, firsts[k], idx)
            gate = jnp.where(slot == k, exps[k] / denom, gate)
            rank = jnp.where(slot == k, r_k, rank)
        idx_ref[:, rows] = idx.astype(jnp.int32)
        gate_ref[:, rows] = gate
        rank_ref[:, rows] = rank.astype(jnp.int32)
    run_ref[...] = run
    cnt_ref[...] = run


def _outproj(yssd, ysc, x2, w_out, nw, wr_hi, wr_lo, br, first_tile, t):
    tm = OUT_TILE
    src = lambda n: pl.BlockSpec((tm, n), lambda i: (first_tile + i, 0))
    row = lambda n: pl.BlockSpec((tm, n), lambda i: (i, 0))
    slots = pl.BlockSpec((TOP_K, tm), lambda i: (0, i))
    full = lambda a: pl.BlockSpec(a.shape, lambda i: (0,) * a.ndim)
    return pl.pallas_call(
        _outproj_kernel,
        grid=(t // tm,),
        in_specs=[src(D_SSD), src(D_SC), src(D_MODEL), full(w_out), full(nw), full(wr_hi),
                  full(wr_lo), full(br)],
        out_specs=[row(D_MODEL), pl.BlockSpec((ROW_PARTS, tm, LANES), lambda i: (0, i, 0)),
                   slots, slots, slots,
                   pl.BlockSpec((N_EXPERTS, 1), lambda i: (0, 0))],
        out_shape=[jax.ShapeDtypeStruct((t, D_MODEL), F32),
                   jax.ShapeDtypeStruct((ROW_PARTS, t, LANES), jnp.int32),
                   jax.ShapeDtypeStruct((TOP_K, t), jnp.int32),
                   jax.ShapeDtypeStruct((TOP_K, t), F32),
                   jax.ShapeDtypeStruct((TOP_K, t), jnp.int32),
                   jax.ShapeDtypeStruct((N_EXPERTS, 1), F32)],
        scratch_shapes=[pltpu.VMEM((N_EXPERTS, 1), F32)],
        compiler_params=pltpu.CompilerParams(
            dimension_semantics=("arbitrary",), vmem_limit_bytes=VMEM_LIMIT_BYTES),
    )(yssd, ysc, x2, w_out, nw, wr_hi, wr_lo, br)


def _dest_kernel(pstart_ref, idx_ref, rank_ref, dest_ref):
    idx = idx_ref[...]
    base = jnp.zeros_like(idx)
    for e in range(N_EXPERTS):
        base = jnp.where(idx == e, pstart_ref[e], base)
    dest_ref[...] = base + rank_ref[...]


def _dest(pstarts, idx, rank):
    full = lambda a: pl.BlockSpec(a.shape, lambda i, ps: (0,) * a.ndim)
    return pl.pallas_call(
        _dest_kernel,
        grid_spec=pltpu.PrefetchScalarGridSpec(
            num_scalar_prefetch=1, grid=(1,),
            in_specs=[full(idx), full(rank)], out_specs=full(idx)),
        out_shape=jax.ShapeDtypeStruct(idx.shape, jnp.int32),
    )(pstarts, idx, rank)


def _sc_mesh():
    return plsc.VectorSubcoreMesh(core_axis_name="core", subcore_axis_name="subcore")


def _part_rows(dest, n_rows):
    off = (jnp.arange(ROW_PARTS, dtype=jnp.int32) * n_rows)[:, None, None]
    return (dest[None] + off).reshape(ROW_PARTS * dest.shape[0], dest.shape[1])


def _sc_dispatch(x, dest, n_rows):
    parts, t, width = x.shape
    n_slots = dest.shape[0]
    win = SC_WINDOW
    nwin = t // win

    @functools.partial(pl.kernel,
                       out_type=jax.ShapeDtypeStruct((parts * n_rows, width), jnp.int32),
                       mesh=_sc_mesh(), scratch_types=[])
    def scatter(x_hbm, idx_hbm, out_hbm):
        def body(x_vmem, idx_vmem):
            for k in range(n_slots):
                pltpu.sync_copy(x_vmem, out_hbm.at[idx_vmem.at[k]])

        pltpu.emit_pipeline(
            body,
            grid=(parts * nwin,),
            in_specs=[pl.BlockSpec((win, width), lambda j: (j, 0)),
                      pl.BlockSpec((n_slots, win), lambda j: (j // nwin, j % nwin))],
            out_specs=[],
            core_axis_name=("core", "subcore"),
            dimension_semantics=(pltpu.PARALLEL,),
        )(x_hbm, idx_hbm)

    xs = scatter(x.reshape(parts * t, width), _part_rows(dest, n_rows))
    return xs.reshape(parts, n_rows, width)


def _sc_collect(ys, dest):
    parts, n_rows, width = ys.shape
    n_slots, t = dest.shape
    win = SC_WINDOW
    nwin = t // win
    idx = _part_rows(dest, n_rows).reshape(parts, n_slots, t).swapaxes(0, 1)
    idx = idx.reshape(n_slots * parts, t)

    @functools.partial(pl.kernel,
                       out_type=jax.ShapeDtypeStruct((n_slots * parts * t, width), jnp.int32),
                       mesh=_sc_mesh(), scratch_types=[])
    def gather(y_hbm, idx_hbm, out_hbm):
        def body(idx_vmem, out_vmem):
            pltpu.sync_copy(y_hbm.at[idx_vmem.at[0]], out_vmem)

        pltpu.emit_pipeline(
            body,
            grid=(n_slots * parts * nwin,),
            in_specs=[pl.BlockSpec((1, win), lambda j: (j // nwin, j % nwin))],
            out_specs=[pl.BlockSpec((win, width), lambda j: (j, 0))],
            core_axis_name=("core", "subcore"),
            dimension_semantics=(pltpu.PARALLEL,),
        )(idx_hbm, out_hbm)

    out = gather(ys.reshape(parts * n_rows, width), idx)
    return out.reshape(n_slots, parts, t, width)


def _expert_kernel(blk_e_ref, blk_n_ref, xs_ref, wgu_ref, bgu_ref, wd_ref, bd_ref, out_ref,
                   wgu_b, wd_b):
    i = pl.program_id(0)
    n_valid = blk_n_ref[i]

    @pl.when((i == 0) | (blk_e_ref[i] != blk_e_ref[jnp.maximum(i - 1, 0)]))
    def _():
        wgu_b[...] = wgu_ref[0].astype(BF16)
        wd_b[...] = wd_ref[0].astype(BF16)

    @pl.when(n_valid == 0)
    def _():
        out_ref[...] = jnp.zeros_like(out_ref)

    @pl.when(n_valid > 0)
    def _():
        for r in range(0, ROW_BLOCK, EXPERT_SUB):
            words = [xs_ref[c, r:r + EXPERT_SUB] for c in range(ROW_PARTS)]
            rows = r + lax.broadcasted_iota(jnp.int32, words[0].shape, 0)
            xb = _unpack_rows([jnp.where(rows < n_valid, w, 0) for w in words]).astype(BF16)
            gu = _dot(xb, wgu_b[...]) + bgu_ref[0]
            gate = jnp.minimum(gu[:, :EXPERT_FF], SWIGLU_LIMIT)
            up = jnp.clip(gu[:, EXPERT_FF:], -SWIGLU_LIMIT, SWIGLU_LIMIT)
            hid = (up + 1.0) * (gate * (1.0 / (1.0 + jnp.exp(-SWIGLU_ALPHA * gate))))
            out = _dot(hid.astype(BF16), wd_b[...]) + bd_ref[0]
            for c, w in enumerate(_pack_rows(out)):
                out_ref[c, r:r + EXPERT_SUB] = w


def _experts(blk_e, blk_n, xs, wgu, bgu, wd, bd):
    rb = ROW_BLOCK
    by_expert = lambda a: pl.BlockSpec((1,) + a.shape[1:], lambda i, be, bn: (be[i], 0, 0))
    return pl.pallas_call(
        _expert_kernel,
        grid_spec=pltpu.PrefetchScalarGridSpec(
            num_scalar_prefetch=2, grid=(xs.shape[1] // rb,),
            in_specs=[pl.BlockSpec((ROW_PARTS, rb, LANES), lambda i, be, bn: (0, i, 0)),
                      by_expert(wgu), by_expert(bgu), by_expert(wd), by_expert(bd)],
            out_specs=pl.BlockSpec((ROW_PARTS, rb, LANES), lambda i, be, bn: (0, i, 0)),
            scratch_shapes=[pltpu.VMEM(wgu.shape[1:], BF16), pltpu.VMEM(wd.shape[1:], BF16)]),
        out_shape=jax.ShapeDtypeStruct(xs.shape, xs.dtype),
        compiler_params=pltpu.CompilerParams(
            dimension_semantics=("arbitrary",), vmem_limit_bytes=VMEM_LIMIT_BYTES),
    )(blk_e, blk_n, xs, wgu, bgu, wd, bd)


def _combine_kernel(gate_ref, h1_ref, fw_ref, y0_ref, y1_ref, y2_ref, y3_ref, *rest):
    out_ref = rest[-1]
    parts = _split3(gate_ref[...])
    slot = lax.broadcasted_iota(jnp.int32, (TOP_K, LANES), 0)
    h2 = h1_ref[...]
    for k, y_ref in enumerate((y0_ref, y1_ref, y2_ref, y3_ref)):
        pick = jnp.where(slot == k, 1.0, 0.0).astype(BF16)
        g = sum(lax.dot_general(p, pick, (((0,), (0,)), ((), ())), preferred_element_type=F32)
                for p in parts)
        y = _unpack_rows([y_ref[c] for c in range(ROW_PARTS)])
        h2 = h2 + jnp.concatenate([g] * (D_MODEL // LANES), axis=-1) * y
    out_ref[...] = h2 * lax.rsqrt(jnp.mean(h2 * h2, axis=-1, keepdims=True) + EPS) * fw_ref[...]


def _combine(gates, h1, fw, planes, first_tile, out_tile, t, out_prev):
    d = h1.shape[1]
    tm = COMBINE_TILE
    n = planes.shape[2]
    plane = lambda k: pl.BlockSpec((None, ROW_PARTS, tm, LANES), lambda i: (k, 0, i, 0))
    in_specs = [pl.BlockSpec((TOP_K, tm), lambda i: (0, first_tile + i)),
                pl.BlockSpec((tm, d), lambda i: (first_tile + i, 0)),
                pl.BlockSpec((1, d), lambda i: (0, 0))]
    in_specs += [plane(k) for k in range(TOP_K)]
    args = [gates, h1, fw] + [planes] * TOP_K
    aliases = {}
    if out_prev is not None:
        in_specs.append(pl.BlockSpec(memory_space=pl.ANY))
        aliases = {len(args): 0}
        args.append(out_prev)
    return pl.pallas_call(
        _combine_kernel,
        grid=(n // tm,),
        in_specs=in_specs,
        out_specs=pl.BlockSpec((tm, d), lambda i: (out_tile + i, 0)),
        out_shape=jax.ShapeDtypeStruct((t, d), F32),
        input_output_aliases=aliases,
        compiler_params=pltpu.CompilerParams(
            dimension_semantics=("arbitrary",), vmem_limit_bytes=VMEM_LIMIT_BYTES),
    )(*args)


def _block_plan(counts, n_blocks):
    rb = ROW_BLOCK
    pcounts = (counts + rb - 1) // rb * rb
    pends = jnp.cumsum(pcounts)
    pstarts = pends - pcounts
    block_start = jnp.arange(n_blocks, dtype=jnp.int32) * rb
    blk_e = jnp.minimum(jnp.sum(block_start[:, None] >= pends[None, :], axis=1),
                        N_EXPERTS - 1).astype(jnp.int32)
    mine = blk_e[:, None] == jnp.arange(N_EXPERTS, dtype=jnp.int32)[None, :]
    pick = lambda table: jnp.sum(jnp.where(mine, table[None, :], 0), axis=1)
    blk_n = jnp.clip(pick(counts) - (block_start - pick(pstarts)), 0, rb).astype(jnp.int32)
    return pstarts.astype(jnp.int32), blk_e, blk_n


def kernel(x, norm1_w, w_in, ssd_conv_w, ssd_conv_b, dt_bias, a_log, d_skip, ssd_norm_w,
           sc_conv_w, sc_norm_w, w_out, norm2_w, w_router, b_router, w_gate_up, b_gate_up,
           w_down, b_down, final_norm_w):
    assert norm1_w.shape[0] == 1, "single-layer problem"
    batch, seq, d = x.shape
    t = batch * seq
    x2 = x.reshape(t, d)
    row = lambda a: a.reshape(1, -1)

    s1, s2 = D_SSD + D_XBC, D_SSD + D_XBC + N_SSD_HEADS
    w = w_in[0]
    w_zx, w_dt, w_sc = w[:, :s1].astype(BF16), w[:, s1:s2].astype(BF16), w[:, s2:].astype(BF16)
    z, xbc, ysc, dt, dtt = _inproj(x2, seq, row(norm1_w[0]), w_zx, w_sc, w_dt, w_dt.T,
                                   ssd_conv_w[0], row(ssd_conv_b[0]), sc_conv_w[0],
                                   row(sc_norm_w[0]))
    yssd = _mixer(z, xbc, dt, dtt, batch, seq, row(dt_bias[0]), row(a_log[0]), row(d_skip[0]),
                  row(ssd_norm_w[0]))

    wr = w_router[0].T
    wr_hi = wr.astype(BF16)
    wr_lo = (wr - wr_hi.astype(F32)).astype(BF16)
    unit = t // sum(MOE_SPLIT)
    sizes = [unit * s for s in MOE_SPLIT]
    starts = [sum(sizes[:g]) for g in range(len(sizes))]
    assert sum(sizes) == t and all(n % (COMBINE_CHUNKS * COMBINE_TILE) == 0 and n % OUT_TILE == 0
                                   for n in sizes)
    w_out_b, nw2, br = w_out[0].astype(BF16), row(norm2_w[0]), b_router[0].reshape(-1, 1)
    routed = []
    for t0, tg in zip(starts, sizes):
        n_rows = tg * TOP_K + N_EXPERTS * ROW_BLOCK
        h1, xn2, idx, gates, rank, counts = _outproj(
            yssd, ysc, x2, w_out_b, nw2, wr_hi, wr_lo, br, t0 // OUT_TILE, tg)
        pstarts, blk_e, blk_n = _block_plan(counts[:, 0].astype(jnp.int32),
                                            n_rows // ROW_BLOCK)
        dest = _dest(pstarts, idx, rank)
        xs = _sc_dispatch(xn2, dest, n_rows)
        routed.append((h1, gates, dest, blk_e, blk_n, xs))
    expert_out = [_experts(blk_e, blk_n, xs, w_gate_up[0], b_gate_up[0][:, None, :],
                           w_down[0], b_down[0][:, None, :])
                  for (_, _, _, blk_e, blk_n, xs) in routed]
    fw, out = row(final_norm_w), None
    for t0, tg, (h1, gates, dest, _, _, _), ys in zip(starts, sizes, routed, expert_out):
        tc = tg // COMBINE_CHUNKS
        for c in range(COMBINE_CHUNKS):
            planes = _sc_collect(ys, dest[:, c * tc:(c + 1) * tc])
            out = _combine(gates, h1, fw, planes, c * (tc // COMBINE_TILE),
                           (t0 + c * tc) // COMBINE_TILE, t, out)
    return out.reshape(batch, seq, d)
```

```python
import functools

import jax
import jax.numpy as jnp
from jax import lax
from jax.experimental import pallas as pl
from jax.experimental.pallas import tpu as pltpu
from jax.experimental.pallas import tpu_sc as plsc

D_MODEL = 1024
D_SSD = 1024
SSD_HEAD_DIM = 64
N_SSD_HEADS = 16
N_GROUPS = 2
HEADS_PER_GROUP = 8
D_STATE = 128
SSD_CONV = 4
CHUNK = 128
D_XBC = D_SSD + 2 * N_GROUPS * D_STATE
D_SC = 1024
SC_CONV = 3
D_MIX = D_SSD + D_SC
N_EXPERTS = 32
TOP_K = 4
EXPERT_FF = 1024
SWIGLU_LIMIT = 7.0
SWIGLU_ALPHA = 1.702
EPS = 1e-5

VMEM_LIMIT_BYTES = 56 * 1024 * 1024
SUBLANES = 8
ROW_BLOCK = 512
EXPERT_SUB = 256
IN_TILE = 512
IN_SUB = 256
MIX_CHUNKS = 4
OUT_TILE = 1024
OUT_SUB = 1024
COMBINE_TILE = 512
MOE_SPLIT = (5, 3)
COMBINE_CHUNKS = 4
LANES = 128
ROW_PARTS = D_MODEL // (2 * LANES)
SC_WINDOW = 128
HIGH_HALF = -65536

F32 = jnp.float32
BF16 = jnp.bfloat16


def _dot(a, b):
    return jnp.dot(a, b, preferred_element_type=F32)


def _split3(v):
    hi = v.astype(BF16)
    r1 = v - hi.astype(F32)
    mid = r1.astype(BF16)
    lo = (r1 - mid.astype(F32)).astype(BF16)
    return hi, mid, lo


def _dot_exact_rhs01(v, m01):
    hi, mid, lo = _split3(v)
    return _dot(hi, m01) + _dot(mid, m01) + _dot(lo, m01)


def _dot_exact_lhs01(m01, v):
    hi, mid, lo = _split3(v)
    return _dot(m01, hi) + _dot(m01, mid) + _dot(m01, lo)


def _silu(v):
    return v * (1.0 / (1.0 + jnp.exp(-v)))


def _softplus(v):
    return jnp.maximum(v, 0.0) + jnp.log1p(jnp.exp(-jnp.abs(v)))


def _pack_rows(v):
    bits = lambda a: lax.bitcast_convert_type(a.astype(BF16).astype(F32), jnp.int32)
    parts = []
    for c in range(ROW_PARTS):
        lo = v[:, (2 * c) * LANES:(2 * c + 1) * LANES]
        hi = v[:, (2 * c + 1) * LANES:(2 * c + 2) * LANES]
        parts.append(bits(hi) | lax.shift_right_logical(bits(lo), 16))
    return parts


def _unpack_rows(parts):
    cols = []
    for w in parts:
        cols.append(lax.bitcast_convert_type(w << 16, F32))
        cols.append(lax.bitcast_convert_type(w & HIGH_HALF, F32))
    return jnp.concatenate(cols, axis=-1)


def _shifted(cur, tail, s):
    if s == 0:
        return cur
    rc = pltpu.roll(cur, s, axis=0)
    rt = pltpu.roll(tail, s, axis=0)
    row = lax.broadcasted_iota(jnp.int32, tail.shape, 0)
    first = jnp.where(row < s, rt, rc[0:SUBLANES])
    return jnp.concatenate([first, rc[SUBLANES:]], axis=0)


def _causal_conv(tail_ref, cur, w_ref, acc):
    rows = cur.shape[0]
    n_taps = w_ref.shape[0]
    tail = tail_ref[...]
    for k in range(n_taps):
        acc = acc + w_ref[k:k + 1, :] * _shifted(cur, tail, n_taps - 1 - k)
    tail_ref[...] = cur[rows - SUBLANES:rows]
    return acc


def _inproj_kernel(tiles_per_seq, x_ref, nw_ref, wzx_ref, wsc_ref, wdt_ref, wdtt_ref,
                   cw_ref, cb_ref, scw_ref, scnw_ref,
                   z_ref, xbc_ref, ysc_ref, dt_ref, dtt_ref, hist_ref, schist_ref):
    tm = x_ref.shape[0]

    @pl.when(pl.program_id(0) % tiles_per_seq == 0)
    def _():
        hist_ref[...] = jnp.zeros_like(hist_ref)
        schist_ref[...] = jnp.zeros_like(schist_ref)

    for r in range(0, tm, IN_SUB):
        rows = pl.ds(r, IN_SUB)
        x = x_ref[rows, :]
        xn = x * lax.rsqrt(jnp.mean(x * x, axis=-1, keepdims=True) + EPS) * nw_ref[...]
        xb = xn.astype(BF16)
        proj = lambda w_ref, o, n, xb=xb: _dot(xb, w_ref[:, o:o + n])
        z_ref[rows, :] = proj(wzx_ref, 0, D_SSD).astype(z_ref.dtype)

        conv = _causal_conv(hist_ref, proj(wzx_ref, D_SSD, D_XBC), cw_ref, cb_ref[...])
        xbc_ref[rows, :] = _silu(conv).astype(xbc_ref.dtype)

        cu = proj(wsc_ref, D_SC, D_SC) * proj(wsc_ref, 2 * D_SC, D_SC)
        v = _causal_conv(schist_ref, cu, scw_ref, jnp.zeros_like(cu))
        gv = proj(wsc_ref, 0, D_SC) * v
        y_sc = gv * lax.rsqrt(jnp.mean(gv * gv, axis=-1, keepdims=True) + EPS) * scnw_ref[...]
        ysc_ref[rows, :] = y_sc.astype(ysc_ref.dtype)

        dt_ref[rows, :] = _dot(xb, wdt_ref[...])
        dtt_ref[:, rows] = lax.dot_general(wdtt_ref[...], xb, (((1,), (1,)), ((), ())),
                                           preferred_element_type=F32)


def _inproj(x2, seq, nw, w_zx, w_sc, w_dt, w_dtt, cw, cb, scw, scnw):
    t = x2.shape[0]
    tm = IN_TILE
    assert seq % tm == 0, "in-proj tiles must not straddle sequences (causal conv carry)"
    row = lambda n: pl.BlockSpec((tm, n), lambda i: (i, 0))
    full = lambda a: pl.BlockSpec(a.shape, lambda i: (0,) * a.ndim)
    params = (nw, w_zx, w_sc, w_dt, w_dtt, cw, cb, scw, scnw)
    return pl.pallas_call(
        functools.partial(_inproj_kernel, seq // tm),
        grid=(t // tm,),
        in_specs=[row(D_MODEL)] + [full(a) for a in params],
        out_specs=[row(D_SSD), row(D_XBC), row(D_SC),
                   row(N_SSD_HEADS), pl.BlockSpec((N_SSD_HEADS, tm), lambda i: (0, i))],
        out_shape=[jax.ShapeDtypeStruct((t, D_SSD), BF16),
                   jax.ShapeDtypeStruct((t, D_XBC), BF16),
                   jax.ShapeDtypeStruct((t, D_SC), BF16),
                   jax.ShapeDtypeStruct((t, N_SSD_HEADS), F32),
                   jax.ShapeDtypeStruct((N_SSD_HEADS, t), F32)],
        scratch_shapes=[pltpu.VMEM((SUBLANES, D_XBC), F32), pltpu.VMEM((SUBLANES, D_SC), F32)],
        compiler_params=pltpu.CompilerParams(
            dimension_semantics=("arbitrary",), vmem_limit_bytes=VMEM_LIMIT_BYTES),
    )(x2, *params)


def _mixer_kernel(z_ref, xbc_ref, dt_ref, dtt_ref,
                  dtb_ref, dtbt_ref, alog_ref, alogt_ref, dskip_ref, nw_ref, out_ref, state_ref):
    q = CHUNK

    @pl.when(pl.program_id(1) == 0)
    def _():
        state_ref[...] = jnp.zeros_like(state_ref)

    ri = lax.broadcasted_iota(jnp.int32, (q, q), 0)
    ci = lax.broadcasted_iota(jnp.int32, (q, q), 1)
    causal = ri >= ci
    tri = jnp.where(causal, 1.0, 0.0).astype(BF16)
    trit = jnp.where(ri <= ci, 1.0, 0.0).astype(BF16)
    hh = lax.broadcasted_iota(jnp.int32, (N_SSD_HEADS, D_SSD), 0)
    hl = lax.broadcasted_iota(jnp.int32, (N_SSD_HEADS, D_SSD), 1) // SSD_HEAD_DIM
    expand = jnp.where(hh == hl, 1.0, 0.0).astype(BF16)
    lane_lo = lax.broadcasted_iota(jnp.int32, (q, 2 * SSD_HEAD_DIM), 1) < SSD_HEAD_DIM
    gw = HEADS_PER_GROUP * SSD_HEAD_DIM
    n_bc = N_GROUPS * D_STATE
    a_neg = -jnp.exp(alog_ref[...])
    a_negt = -jnp.exp(alogt_ref[...])
    d_x = _dot_exact_rhs01(dskip_ref[...], expand)

    for s in range(MIX_CHUNKS):
        rows = pl.ds(s * q, q)
        xs = xbc_ref[rows, :D_SSD].astype(F32)
        b_all = xbc_ref[rows, D_SSD:D_SSD + n_bc]
        c_all = xbc_ref[rows, D_SSD + n_bc:]

        dt = _softplus(dt_ref[rows, :] + dtb_ref[...])
        dtt = _softplus(dtt_ref[:, rows] + dtbt_ref[...])
        acum = _dot_exact_lhs01(tri, dt * a_neg)
        acumt = _dot_exact_rhs01(dtt * a_negt, trit)
        a_last = acum[q - 1:q, :]

        xdt = xs * _dot_exact_rhs01(dt, expand)
        xdt_b = xdt.astype(BF16)
        dec_in = _dot_exact_rhs01(jnp.exp(acum), expand)
        dec_out = _dot_exact_rhs01(jnp.exp(a_last - acum), expand)
        dec_all = _dot_exact_rhs01(jnp.exp(a_last), expand)
        xw_b = (xdt * dec_out).astype(BF16)

        y_parts = []
        for g in range(N_GROUPS):
            b_g = b_all[:, g * D_STATE:(g + 1) * D_STATE]
            c_g = c_all[:, g * D_STATE:(g + 1) * D_STATE]
            cb = lax.dot_general(c_g, b_g, (((1,), (1,)), ((), ())), preferred_element_type=F32)
            st = state_ref[:, g * gw:(g + 1) * gw]
            y_inter = _dot(c_g, st.astype(BF16)) * dec_in[:, g * gw:(g + 1) * gw]
            new_st = lax.dot_general(b_g, xw_b[:, g * gw:(g + 1) * gw], (((0,), (0,)), ((), ())),
                                     preferred_element_type=F32)
            state_ref[:, g * gw:(g + 1) * gw] = st * dec_all[:, g * gw:(g + 1) * gw] + new_st
            pair_parts = []
            for pr in range(HEADS_PER_GROUP // 2):
                h0 = g * HEADS_PER_GROUP + 2 * pr
                lo = h0 * SSD_HEAD_DIM
                rhs = xdt_b[:, lo:lo + 2 * SSD_HEAD_DIM]
                ys = []
                for h in (h0, h0 + 1):
                    seg = acum[:, h:h + 1] - acumt[h:h + 1, :]
                    m = (cb * jnp.exp(jnp.where(causal, seg, -jnp.inf))).astype(BF16)
                    ys.append(_dot(m, rhs))
                pair_parts.append(jnp.where(lane_lo, ys[0], ys[1]))
            y_parts.append(jnp.concatenate(pair_parts, axis=-1) + y_inter)
        y = jnp.concatenate(y_parts, axis=-1) + d_x * xs

        gated = y * _silu(z_ref[rows, :].astype(F32))
        outs = []
        for g in range(N_GROUPS):
            gg = gated[:, g * gw:(g + 1) * gw]
            outs.append(gg * lax.rsqrt(jnp.mean(gg * gg, axis=-1, keepdims=True) + EPS))
        out_ref[rows, :] = (jnp.concatenate(outs, axis=-1) * nw_ref[...]).astype(out_ref.dtype)


def _mixer(z, xbc, dt, dtt, batch, seq, dtb, alog, dskip, nw):
    rows = CHUNK * MIX_CHUNKS
    assert seq % rows == 0
    ns = seq // rows
    t = batch * seq
    row = lambda n: pl.BlockSpec((rows, n), lambda b, c: (b * ns + c, 0))
    full = lambda a: pl.BlockSpec(a.shape, lambda b, c: (0,) * a.ndim)
    dtbt, alogt = dtb.reshape(N_SSD_HEADS, 1), alog.reshape(N_SSD_HEADS, 1)
    params = (dtb, dtbt, alog, alogt, dskip, nw)
    return pl.pallas_call(
        _mixer_kernel,
        grid=(batch, ns),
        in_specs=[row(D_SSD), row(D_XBC), row(N_SSD_HEADS),
                  pl.BlockSpec((N_SSD_HEADS, rows), lambda b, c: (0, b * ns + c))]
                 + [full(a) for a in params],
        out_specs=row(D_SSD),
        out_shape=jax.ShapeDtypeStruct((t, D_SSD), BF16),
        scratch_shapes=[pltpu.VMEM((D_STATE, D_SSD), F32)],
        compiler_params=pltpu.CompilerParams(
            dimension_semantics=("arbitrary", "arbitrary"), vmem_limit_bytes=VMEM_LIMIT_BYTES),
    )(z, xbc, dt, dtt, *params)


def _outproj_kernel(yssd_ref, ysc_ref, x_ref, wout_ref, nw_ref, wrh_ref, wrl_ref, br_ref,
                    h1_ref, xn_ref, idx_ref, gate_ref, rank_ref, cnt_ref, run_ref):
    tm = x_ref.shape[0]

    @pl.when(pl.program_id(0) == 0)
    def _():
        run_ref[...] = jnp.zeros_like(run_ref)

    sub = OUT_SUB
    nt = lambda w, a: lax.dot_general(w, a, (((1,), (1,)), ((), ())), preferred_element_type=F32)
    expert = lax.broadcasted_iota(jnp.int32, (N_EXPERTS, sub), 0).astype(F32)
    slot = lax.broadcasted_iota(jnp.int32, (TOP_K, sub), 0)
    ri = lax.broadcasted_iota(jnp.int32, (sub, sub), 0)
    ci = lax.broadcasted_iota(jnp.int32, (sub, sub), 1)
    earlier = jnp.where(ri < ci, 1.0, 0.0).astype(BF16)

    run = run_ref[...]
    for r in range(0, tm, sub):
        rows = pl.ds(r, sub)
        h1 = (x_ref[rows, :] + _dot(yssd_ref[rows, :], wout_ref[:D_SSD, :])
              + _dot(ysc_ref[rows, :], wout_ref[D_SSD:, :]))
        h1_ref[rows, :] = h1
        xn = h1 * lax.rsqrt(jnp.mean(h1 * h1, axis=-1, keepdims=True) + EPS) * nw_ref[...]
        for c, words in enumerate(_pack_rows(xn)):
            xn_ref[c, rows, :] = words

        xh = xn.astype(BF16)
        xl = (xn - xh.astype(F32)).astype(BF16)
        logits = (nt(wrh_ref[...], xh) + nt(wrl_ref[...], xh) + nt(wrh_ref[...], xl)
                  + br_ref[...])

        work = logits
        vals, firsts, sels = [], [], []
        for _ in range(TOP_K):
            m = jnp.max(work, axis=0, keepdims=True)
            first = jnp.min(jnp.where(work == m, expert, float(N_EXPERTS)), axis=0,
                            keepdims=True)
            sel = expert == first
            vals.append(m)
            firsts.append(first)
            sels.append(sel)
            work = jnp.where(sel, -jnp.inf, work)
        exps = [jnp.exp(v - vals[0]) for v in vals]
        denom = exps[0] + exps[1] + exps[2] + exps[3]

        onehot = jnp.zeros((N_EXPERTS, sub), F32)
        for sel in sels:
            onehot = onehot + jnp.where(sel, 1.0, 0.0)
        before = run + _dot(onehot.astype(BF16), earlier)
        run = run + jnp.sum(onehot, axis=1, keepdims=True)

        idx = jnp.zeros((TOP_K, sub), F32)
        gate = jnp.zeros((TOP_K, sub), F32)
        rank = jnp.zeros((TOP_K, sub), F32)
        for k in range(TOP_K):
            r_k = jnp.sum(jnp.where(sels[k], before, 0.0), axis=0, keepdims=True)
            idx = jnp.where(slot == k, firsts[k], idx)
            gate = jnp.where(slot == k, exps[k] / denom, gate)
            rank = jnp.where(slot == k, r_k, rank)
        idx_ref[:, rows] = idx.astype(jnp.int32)
        gate_ref[:, rows] = gate
        rank_ref[:, rows] = rank.astype(jnp.int32)
    run_ref[...] = run
    cnt_ref[...] = run


def _outproj(yssd, ysc, x2, w_out, nw, wr_hi, wr_lo, br, first_tile, t):
    tm = OUT_TILE
    src = lambda n: pl.BlockSpec((tm, n), lambda i: (first_tile + i, 0))
    row = lambda n: pl.BlockSpec((tm, n), lambda i: (i, 0))
    slots = pl.BlockSpec((TOP_K, tm), lambda i: (0, i))
    full = lambda a: pl.BlockSpec(a.shape, lambda i: (0,) * a.ndim)
    return pl.pallas_call(
        _outproj_kernel,
        grid=(t // tm,),
        in_specs=[src(D_SSD), src(D_SC), src(D_MODEL), full(w_out), full(nw), full(wr_hi),
                  full(wr_lo), full(br)],
        out_specs=[row(D_MODEL), pl.BlockSpec((ROW_PARTS, tm, LANES), lambda i: (0, i, 0)),
                   slots, slots, slots,
                   pl.BlockSpec((N_EXPERTS, 1), lambda i: (0, 0))],
        out_shape=[jax.ShapeDtypeStruct((t, D_MODEL), F32),
                   jax.ShapeDtypeStruct((ROW_PARTS, t, LANES), jnp.int32),
                   jax.ShapeDtypeStruct((TOP_K, t), jnp.int32),
                   jax.ShapeDtypeStruct((TOP_K, t), F32),
                   jax.ShapeDtypeStruct((TOP_K, t), jnp.int32),
                   jax.ShapeDtypeStruct((N_EXPERTS, 1), F32)],
        scratch_shapes=[pltpu.VMEM((N_EXPERTS, 1), F32)],
        compiler_params=pltpu.CompilerParams(
            dimension_semantics=("arbitrary",), vmem_limit_bytes=VMEM_LIMIT_BYTES),
    )(yssd, ysc, x2, w_out, nw, wr_hi, wr_lo, br)


def _dest_kernel(pstart_ref, idx_ref, rank_ref, dest_ref):
    idx = idx_ref[...]
    base = jnp.zeros_like(idx)
    for e in range(N_EXPERTS):
        base = jnp.where(idx == e, pstart_ref[e], base)
    dest_ref[...] = base + rank_ref[...]


def _dest(pstarts, idx, rank):
    full = lambda a: pl.BlockSpec(a.shape, lambda i, ps: (0,) * a.ndim)
    return pl.pallas_call(
        _dest_kernel,
        grid_spec=pltpu.PrefetchScalarGridSpec(
            num_scalar_prefetch=1, grid=(1,),
            in_specs=[full(idx), full(rank)], out_specs=full(idx)),
        out_shape=jax.ShapeDtypeStruct(idx.shape, jnp.int32),
    )(pstarts, idx, rank)


def _sc_mesh():
    return plsc.VectorSubcoreMesh(core_axis_name="core", subcore_axis_name="subcore")


def _part_rows(dest, n_rows):
    off = (jnp.arange(ROW_PARTS, dtype=jnp.int32) * n_rows)[:, None, None]
    return (dest[None] + off).reshape(ROW_PARTS * dest.shape[0], dest.shape[1])


def _sc_dispatch(x, dest, n_rows):
    parts, t, width = x.shape
    n_slots = dest.shape[0]
    win = SC_WINDOW
    nwin = t // win

    @functools.partial(pl.kernel,
                       out_type=jax.ShapeDtypeStruct((parts * n_rows, width), jnp.int32),
                       mesh=_sc_mesh(), scratch_types=[])
    def scatter(x_hbm, idx_hbm, out_hbm):
        def body(x_vmem, idx_vmem):
            for k in range(n_slots):
                pltpu.sync_copy(x_vmem, out_hbm.at[idx_vmem.at[k]])

        pltpu.emit_pipeline(
            body,
            grid=(parts * nwin,),
            in_specs=[pl.BlockSpec((win, width), lambda j: (j, 0)),
                      pl.BlockSpec((n_slots, win), lambda j: (j // nwin, j % nwin))],
            out_specs=[],
            core_axis_name=("core", "subcore"),
            dimension_semantics=(pltpu.PARALLEL,),
        )(x_hbm, idx_hbm)

    xs = scatter(x.reshape(parts * t, width), _part_rows(dest, n_rows))
    return xs.reshape(parts, n_rows, width)


def _sc_collect(ys, dest):
    parts, n_rows, width = ys.shape
    n_slots, t = dest.shape
    win = SC_WINDOW
    nwin = t // win
    idx = _part_rows(dest, n_rows).reshape(parts, n_slots, t).swapaxes(0, 1)
    idx = idx.reshape(n_slots * parts, t)

    @functools.partial(pl.kernel,
                       out_type=jax.ShapeDtypeStruct((n_slots * parts * t, width), jnp.int32),
                       mesh=_sc_mesh(), scratch_types=[])
    def gather(y_hbm, idx_hbm, out_hbm):
        def body(idx_vmem, out_vmem):
            pltpu.sync_copy(y_hbm.at[idx_vmem.at[0]], out_vmem)

        pltpu.emit_pipeline(
            body,
            grid=(n_slots * parts * nwin,),
            in_specs=[pl.BlockSpec((1, win), lambda j: (j // nwin, j % nwin))],
            out_specs=[pl.BlockSpec((win, width), lambda j: (j, 0))],
            core_axis_name=("core", "subcore"),
            dimension_semantics=(pltpu.PARALLEL,),
        )(idx_hbm, out_hbm)

    out = gather(ys.reshape(parts * n_rows, width), idx)
    return out.reshape(n_slots, parts, t, width)


def _expert_kernel(blk_e_ref, blk_n_ref, xs_ref, wgu_ref, bgu_ref, wd_ref, bd_ref, out_ref,
                   wgu_b, wd_b):
    i = pl.program_id(0)
    n_valid = blk_n_ref[i]

    @pl.when((i == 0) | (blk_e_ref[i] != blk_e_ref[jnp.maximum(i - 1, 0)]))
    def _():
        wgu_b[...] = wgu_ref[0].astype(BF16)
        wd_b[...] = wd_ref[0].astype(BF16)

    def run_rows(r):
        words = [xs_ref[c, r:r + EXPERT_SUB] for c in range(ROW_PARTS)]
        rows = r + lax.broadcasted_iota(jnp.int32, words[0].shape, 0)
        xb = _unpack_rows([jnp.where(rows < n_valid, w, 0) for w in words]).astype(BF16)
        gu = _dot(xb, wgu_b[...]) + bgu_ref[0]
        gate = jnp.minimum(gu[:, :EXPERT_FF], SWIGLU_LIMIT)
        up = jnp.clip(gu[:, EXPERT_FF:], -SWIGLU_LIMIT, SWIGLU_LIMIT)
        hid = (up + 1.0) * (gate * (1.0 / (1.0 + jnp.exp(-SWIGLU_ALPHA * gate))))
        out = _dot(hid.astype(BF16), wd_b[...]) + bd_ref[0]
        for c, w in enumerate(_pack_rows(out)):
            out_ref[c, r:r + EXPERT_SUB] = w

    def zero_rows(r, n):
        out_ref[:, r:r + n, :] = jnp.zeros((ROW_PARTS, n, LANES), jnp.int32)

    @pl.when(n_valid == 0)
    def _():
        zero_rows(0, ROW_BLOCK)

    @pl.when((n_valid > 0) & (n_valid <= EXPERT_SUB))
    def _():
        run_rows(0)
        zero_rows(EXPERT_SUB, ROW_BLOCK - EXPERT_SUB)

    @pl.when(n_valid > EXPERT_SUB)
    def _():
        for r in range(0, ROW_BLOCK, EXPERT_SUB):
            run_rows(r)


def _experts(blk_e, blk_n, xs, wgu, bgu, wd, bd):
    rb = ROW_BLOCK
    by_expert = lambda a: pl.BlockSpec((1,) + a.shape[1:], lambda i, be, bn: (be[i], 0, 0))
    return pl.pallas_call(
        _expert_kernel,
        grid_spec=pltpu.PrefetchScalarGridSpec(
            num_scalar_prefetch=2, grid=(xs.shape[1] // rb,),
            in_specs=[pl.BlockSpec((ROW_PARTS, rb, LANES), lambda i, be, bn: (0, i, 0)),
                      by_expert(wgu), by_expert(bgu), by_expert(wd), by_expert(bd)],
            out_specs=pl.BlockSpec((ROW_PARTS, rb, LANES), lambda i, be, bn: (0, i, 0)),
            scratch_shapes=[pltpu.VMEM(wgu.shape[1:], BF16), pltpu.VMEM(wd.shape[1:], BF16)]),
        out_shape=jax.ShapeDtypeStruct(xs.shape, xs.dtype),
        compiler_params=pltpu.CompilerParams(
            dimension_semantics=("arbitrary",), vmem_limit_bytes=VMEM_LIMIT_BYTES),
    )(blk_e, blk_n, xs, wgu, bgu, wd, bd)


def _combine_kernel(gate_ref, h1_ref, fw_ref, y0_ref, y1_ref, y2_ref, y3_ref, *rest):
    out_ref = rest[-1]
    parts = _split3(gate_ref[...])
    slot = lax.broadcasted_iota(jnp.int32, (TOP_K, LANES), 0)
    h2 = h1_ref[...]
    for k, y_ref in enumerate((y0_ref, y1_ref, y2_ref, y3_ref)):
        pick = jnp.where(slot == k, 1.0, 0.0).astype(BF16)
        g = sum(lax.dot_general(p, pick, (((0,), (0,)), ((), ())), preferred_element_type=F32)
                for p in parts)
        y = _unpack_rows([y_ref[c] for c in range(ROW_PARTS)])
        h2 = h2 + jnp.concatenate([g] * (D_MODEL // LANES), axis=-1) * y
    out_ref[...] = h2 * lax.rsqrt(jnp.mean(h2 * h2, axis=-1, keepdims=True) + EPS) * fw_ref[...]


def _combine(gates, h1, fw, planes, first_tile, out_tile, t, out_prev):
    d = h1.shape[1]
    tm = COMBINE_TILE
    n = planes.shape[2]
    plane = lambda k: pl.BlockSpec((None, ROW_PARTS, tm, LANES), lambda i: (k, 0, i, 0))
    in_specs = [pl.BlockSpec((TOP_K, tm), lambda i: (0, first_tile + i)),
                pl.BlockSpec((tm, d), lambda i: (first_tile + i, 0)),
                pl.BlockSpec((1, d), lambda i: (0, 0))]
    in_specs += [plane(k) for k in range(TOP_K)]
    args = [gates, h1, fw] + [planes] * TOP_K
    aliases = {}
    if out_prev is not None:
        in_specs.append(pl.BlockSpec(memory_space=pl.ANY))
        aliases = {len(args): 0}
        args.append(out_prev)
    return pl.pallas_call(
        _combine_kernel,
        grid=(n // tm,),
        in_specs=in_specs,
        out_specs=pl.BlockSpec((tm, d), lambda i: (out_tile + i, 0)),
        out_shape=jax.ShapeDtypeStruct((t, d), F32),
        input_output_aliases=aliases,
        compiler_params=pltpu.CompilerParams(
            dimension_semantics=("arbitrary",), vmem_limit_bytes=VMEM_LIMIT_BYTES),
    )(*args)


def _block_plan(counts, n_blocks):
    rb = ROW_BLOCK
    pcounts = (counts + rb - 1) // rb * rb
    pends = jnp.cumsum(pcounts)
    pstarts = pends - pcounts
    block_start = jnp.arange(n_blocks, dtype=jnp.int32) * rb
    blk_e = jnp.minimum(jnp.sum(block_start[:, None] >= pends[None, :], axis=1),
                        N_EXPERTS - 1).astype(jnp.int32)
    mine = blk_e[:, None] == jnp.arange(N_EXPERTS, dtype=jnp.int32)[None, :]
    pick = lambda table: jnp.sum(jnp.where(mine, table[None, :], 0), axis=1)
    blk_n = jnp.clip(pick(counts) - (block_start - pick(pstarts)), 0, rb).astype(jnp.int32)
    return pstarts.astype(jnp.int32), blk_e, blk_n


def kernel(x, norm1_w, w_in, ssd_conv_w, ssd_conv_b, dt_bias, a_log, d_skip, ssd_norm_w,
           sc_conv_w, sc_norm_w, w_out, norm2_w, w_router, b_router, w_gate_up, b_gate_up,
           w_down, b_down, final_norm_w):
    assert norm1_w.shape[0] == 1, "single-layer problem"
    batch, seq, d = x.shape
    t = batch * seq
    x2 = x.reshape(t, d)
    row = lambda a: a.reshape(1, -1)

    s1, s2 = D_SSD + D_XBC, D_SSD + D_XBC + N_SSD_HEADS
    w = w_in[0]
    w_zx, w_dt, w_sc = w[:, :s1].astype(BF16), w[:, s1:s2].astype(BF16), w[:, s2:].astype(BF16)
    z, xbc, ysc, dt, dtt = _inproj(x2, seq, row(norm1_w[0]), w_zx, w_sc, w_dt, w_dt.T,
                                   ssd_conv_w[0], row(ssd_conv_b[0]), sc_conv_w[0],
                                   row(sc_norm_w[0]))
    yssd = _mixer(z, xbc, dt, dtt, batch, seq, row(dt_bias[0]), row(a_log[0]), row(d_skip[0]),
                  row(ssd_norm_w[0]))

    wr = w_router[0].T
    wr_hi = wr.astype(BF16)
    wr_lo = (wr - wr_hi.astype(F32)).astype(BF16)
    unit = t // sum(MOE_SPLIT)
    sizes = [unit * s for s in MOE_SPLIT]
    starts = [sum(sizes[:g]) for g in range(len(sizes))]
    assert sum(sizes) == t and all(n % (COMBINE_CHUNKS * COMBINE_TILE) == 0 and n % OUT_TILE == 0
                                   for n in sizes)
    w_out_b, nw2, br = w_out[0].astype(BF16), row(norm2_w[0]), b_router[0].reshape(-1, 1)
    routed = []
    for t0, tg in zip(starts, sizes):
        n_rows = tg * TOP_K + N_EXPERTS * ROW_BLOCK
        h1, xn2, idx, gates, rank, counts = _outproj(
            yssd, ysc, x2, w_out_b, nw2, wr_hi, wr_lo, br, t0 // OUT_TILE, tg)
        pstarts, blk_e, blk_n = _block_plan(counts[:, 0].astype(jnp.int32),
                                            n_rows // ROW_BLOCK)
        dest = _dest(pstarts, idx, rank)
        xs = _sc_dispatch(xn2, dest, n_rows)
        routed.append((h1, gates, dest, blk_e, blk_n, xs))
    expert_out = [_experts(blk_e, blk_n, xs, w_gate_up[0], b_gate_up[0][:, None, :],
                           w_down[0], b_down[0][:, None, :])
                  for (_, _, _, blk_e, blk_n, xs) in routed]
    fw, out = row(final_norm_w), None
    for t0, tg, (h1, gates, dest, _, _, _), ys in zip(starts, sizes, routed, expert_out):
        tc = tg // COMBINE_CHUNKS
        for c in range(COMBINE_CHUNKS):
            planes = _sc_collect(ys, dest[:, c * tc:(c + 1) * tc])
            out = _combine(gates, h1, fw, planes, c * (tc // COMBINE_TILE),
                           (t0 + c * tc) // COMBINE_TILE, t, out)
    return out.reshape(batch, seq, d)
```

```python
import functools

import jax
import jax.numpy as jnp
from jax import lax
from jax.experimental import pallas as pl
from jax.experimental.pallas import tpu as pltpu
from jax.experimental.pallas import tpu_sc as plsc

D_MODEL = 1024
D_SSD = 1024
SSD_HEAD_DIM = 64
N_SSD_HEADS = 16
N_GROUPS = 2
HEADS_PER_GROUP = 8
D_STATE = 128
SSD_CONV = 4
CHUNK = 128
D_XBC = D_SSD + 2 * N_GROUPS * D_STATE
D_SC = 1024
SC_CONV = 3
D_MIX = D_SSD + D_SC
N_EXPERTS = 32
TOP_K = 4
EXPERT_FF = 1024
SWIGLU_LIMIT = 7.0
SWIGLU_ALPHA = 1.702
EPS = 1e-5

VMEM_LIMIT_BYTES = 56 * 1024 * 1024
SUBLANES = 8
ROW_BLOCK = 512
EXPERT_SUB = 256
IN_TILE = 512
IN_SUB = 256
MIX_CHUNKS = 4
OUT_TILE = 1024
COMBINE_TILE = 512
MOE_SPLIT = (5, 3)
COMBINE_CHUNKS = 4
LANES = 128
ROW_PARTS = D_MODEL // (2 * LANES)
SC_WINDOW = 128
HIGH_HALF = -65536

F32 = jnp.float32
BF16 = jnp.bfloat16


def _dot(a, b):
    return jnp.dot(a, b, preferred_element_type=F32)


def _split3(v):
    hi = v.astype(BF16)
    r1 = v - hi.astype(F32)
    mid = r1.astype(BF16)
    lo = (r1 - mid.astype(F32)).astype(BF16)
    return hi, mid, lo


def _dot_exact_rhs01(v, m01):
    hi, mid, lo = _split3(v)
    return _dot(hi, m01) + _dot(mid, m01) + _dot(lo, m01)


def _dot_exact_lhs01(m01, v):
    hi, mid, lo = _split3(v)
    return _dot(m01, hi) + _dot(m01, mid) + _dot(m01, lo)


def _spread(v, m01):
    hi = v.astype(BF16)
    mid = (v - hi.astype(F32)).astype(BF16)
    return _dot(hi, m01) + _dot(mid, m01)


def _silu(v):
    return v * (1.0 / (1.0 + jnp.exp(-v)))


def _softplus(v):
    return jnp.maximum(v, 0.0) + jnp.log1p(jnp.exp(-jnp.abs(v)))


def _pack_rows(v):
    bits = lambda a: lax.bitcast_convert_type(a.astype(BF16).astype(F32), jnp.int32)
    parts = []
    for c in range(ROW_PARTS):
        lo = v[:, (2 * c) * LANES:(2 * c + 1) * LANES]
        hi = v[:, (2 * c + 1) * LANES:(2 * c + 2) * LANES]
        parts.append(bits(hi) | lax.shift_right_logical(bits(lo), 16))
    return parts


def _unpack_rows(parts):
    cols = []
    for w in parts:
        cols.append(lax.bitcast_convert_type(w << 16, F32))
        cols.append(lax.bitcast_convert_type(w & HIGH_HALF, F32))
    return jnp.concatenate(cols, axis=-1)


def _shifted(cur, tail, s):
    if s == 0:
        return cur
    rc = pltpu.roll(cur, s, axis=0)
    rt = pltpu.roll(tail, s, axis=0)
    row = lax.broadcasted_iota(jnp.int32, tail.shape, 0)
    first = jnp.where(row < s, rt, rc[0:SUBLANES])
    return jnp.concatenate([first, rc[SUBLANES:]], axis=0)


def _causal_conv(tail_ref, cur, w_ref, acc):
    rows = cur.shape[0]
    n_taps = w_ref.shape[0]
    tail = tail_ref[...]
    for k in range(n_taps):
        acc = acc + w_ref[k:k + 1, :] * _shifted(cur, tail, n_taps - 1 - k)
    tail_ref[...] = cur[rows - SUBLANES:rows]
    return acc


def _inproj_kernel(tiles_per_seq, x_ref, nw_ref, wzx_ref, wsc_ref, wdt_ref, wdtt_ref,
                   cw_ref, cb_ref, scw_ref, scnw_ref,
                   z_ref, xbc_ref, ysc_ref, dt_ref, dtt_ref, hist_ref, schist_ref):
    tm = x_ref.shape[0]

    @pl.when(pl.program_id(0) % tiles_per_seq == 0)
    def _():
        hist_ref[...] = jnp.zeros_like(hist_ref)
        schist_ref[...] = jnp.zeros_like(schist_ref)

    for r in range(0, tm, IN_SUB):
        rows = pl.ds(r, IN_SUB)
        x = x_ref[rows, :]
        xn = x * lax.rsqrt(jnp.mean(x * x, axis=-1, keepdims=True) + EPS) * nw_ref[...]
        xb = xn.astype(BF16)
        proj = lambda w_ref, o, n, xb=xb: _dot(xb, w_ref[:, o:o + n])
        z_ref[rows, :] = proj(wzx_ref, 0, D_SSD).astype(z_ref.dtype)

        conv = _causal_conv(hist_ref, proj(wzx_ref, D_SSD, D_XBC), cw_ref, cb_ref[...])
        xbc_ref[rows, :] = _silu(conv).astype(xbc_ref.dtype)

        cu = proj(wsc_ref, D_SC, D_SC) * proj(wsc_ref, 2 * D_SC, D_SC)
        v = _causal_conv(schist_ref, cu, scw_ref, jnp.zeros_like(cu))
        gv = proj(wsc_ref, 0, D_SC) * v
        y_sc = gv * lax.rsqrt(jnp.mean(gv * gv, axis=-1, keepdims=True) + EPS) * scnw_ref[...]
        ysc_ref[rows, :] = y_sc.astype(ysc_ref.dtype)

        dt_ref[rows, :] = _dot(xb, wdt_ref[...])
        dtt_ref[:, rows] = lax.dot_general(wdtt_ref[...], xb, (((1,), (1,)), ((), ())),
                                           preferred_element_type=F32)


def _inproj(x2, seq, nw, w_zx, w_sc, w_dt, w_dtt, cw, cb, scw, scnw):
    t = x2.shape[0]
    tm = IN_TILE
    assert seq % tm == 0, "in-proj tiles must not straddle sequences (causal conv carry)"
    row = lambda n: pl.BlockSpec((tm, n), lambda i: (i, 0))
    full = lambda a: pl.BlockSpec(a.shape, lambda i: (0,) * a.ndim)
    params = (nw, w_zx, w_sc, w_dt, w_dtt, cw, cb, scw, scnw)
    return pl.pallas_call(
        functools.partial(_inproj_kernel, seq // tm),
        grid=(t // tm,),
        in_specs=[row(D_MODEL)] + [full(a) for a in params],
        out_specs=[row(D_SSD), row(D_XBC), row(D_SC),
                   row(N_SSD_HEADS), pl.BlockSpec((N_SSD_HEADS, tm), lambda i: (0, i))],
        out_shape=[jax.ShapeDtypeStruct((t, D_SSD), BF16),
                   jax.ShapeDtypeStruct((t, D_XBC), BF16),
                   jax.ShapeDtypeStruct((t, D_SC), BF16),
                   jax.ShapeDtypeStruct((t, N_SSD_HEADS), F32),
                   jax.ShapeDtypeStruct((N_SSD_HEADS, t), F32)],
        scratch_shapes=[pltpu.VMEM((SUBLANES, D_XBC), F32), pltpu.VMEM((SUBLANES, D_SC), F32)],
        compiler_params=pltpu.CompilerParams(
            dimension_semantics=("arbitrary",), vmem_limit_bytes=VMEM_LIMIT_BYTES),
    )(x2, *params)


def _mixer_kernel(z_ref, xbc_ref, dt_ref, dtt_ref,
                  dtb_ref, dtbt_ref, alog_ref, alogt_ref, dskip_ref, nw_ref, out_ref, state_ref):
    q = CHUNK

    @pl.when(pl.program_id(1) == 0)
    def _():
        state_ref[...] = jnp.zeros_like(state_ref)

    ri = lax.broadcasted_iota(jnp.int32, (q, q), 0)
    ci = lax.broadcasted_iota(jnp.int32, (q, q), 1)
    causal = ri >= ci
    tri = jnp.where(causal, 1.0, 0.0).astype(BF16)
    trit = jnp.where(ri <= ci, 1.0, 0.0).astype(BF16)
    hh = lax.broadcasted_iota(jnp.int32, (N_SSD_HEADS, D_SSD), 0)
    hl = lax.broadcasted_iota(jnp.int32, (N_SSD_HEADS, D_SSD), 1) // SSD_HEAD_DIM
    expand = jnp.where(hh == hl, 1.0, 0.0).astype(BF16)
    lane_lo = lax.broadcasted_iota(jnp.int32, (q, 2 * SSD_HEAD_DIM), 1) < SSD_HEAD_DIM
    gw = HEADS_PER_GROUP * SSD_HEAD_DIM
    n_bc = N_GROUPS * D_STATE
    a_neg = -jnp.exp(alog_ref[...])
    a_negt = -jnp.exp(alogt_ref[...])
    d_x = _dot_exact_rhs01(dskip_ref[...], expand)

    for s in range(MIX_CHUNKS):
        rows = pl.ds(s * q, q)
        xs = xbc_ref[rows, :D_SSD].astype(F32)
        b_all = xbc_ref[rows, D_SSD:D_SSD + n_bc]
        c_all = xbc_ref[rows, D_SSD + n_bc:]

        dt = _softplus(dt_ref[rows, :] + dtb_ref[...])
        dtt = _softplus(dtt_ref[:, rows] + dtbt_ref[...])
        acum = _dot_exact_lhs01(tri, dt * a_neg)
        acumt = _dot_exact_rhs01(dtt * a_negt, trit)
        a_last = acum[q - 1:q, :]

        xdt = xs * _spread(dt, expand)
        xdt_b = xdt.astype(BF16)
        dec_in = _spread(jnp.exp(acum), expand)
        dec_out = _spread(jnp.exp(a_last - acum), expand)
        dec_all = _dot_exact_rhs01(jnp.exp(a_last), expand)
        xw_b = (xdt * dec_out).astype(BF16)

        y_parts = []
        for g in range(N_GROUPS):
            b_g = b_all[:, g * D_STATE:(g + 1) * D_STATE]
            c_g = c_all[:, g * D_STATE:(g + 1) * D_STATE]
            cb = lax.dot_general(c_g, b_g, (((1,), (1,)), ((), ())), preferred_element_type=F32)
            st = state_ref[:, g * gw:(g + 1) * gw]
            y_inter = _dot(c_g, st.astype(BF16)) * dec_in[:, g * gw:(g + 1) * gw]
            new_st = lax.dot_general(b_g, xw_b[:, g * gw:(g + 1) * gw], (((0,), (0,)), ((), ())),
                                     preferred_element_type=F32)
            state_ref[:, g * gw:(g + 1) * gw] = st * dec_all[:, g * gw:(g + 1) * gw] + new_st
            pair_parts = []
            for pr in range(HEADS_PER_GROUP // 2):
                h0 = g * HEADS_PER_GROUP + 2 * pr
                lo = h0 * SSD_HEAD_DIM
                rhs = xdt_b[:, lo:lo + 2 * SSD_HEAD_DIM]
                ys = []
                for h in (h0, h0 + 1):
                    seg = acum[:, h:h + 1] - acumt[h:h + 1, :]
                    m = (cb * jnp.exp(jnp.where(causal, seg, -jnp.inf))).astype(BF16)
                    ys.append(_dot(m, rhs))
                pair_parts.append(jnp.where(lane_lo, ys[0], ys[1]))
            y_parts.append(jnp.concatenate(pair_parts, axis=-1) + y_inter)
        y = jnp.concatenate(y_parts, axis=-1) + d_x * xs

        gated = y * _silu(z_ref[rows, :].astype(F32))
        outs = []
        for g in range(N_GROUPS):
            gg = gated[:, g * gw:(g + 1) * gw]
            outs.append(gg * lax.rsqrt(jnp.mean(gg * gg, axis=-1, keepdims=True) + EPS))
        out_ref[rows, :] = (jnp.concatenate(outs, axis=-1) * nw_ref[...]).astype(out_ref.dtype)


def _mixer(z, xbc, dt, dtt, batch, seq, dtb, alog, dskip, nw):
    rows = CHUNK * MIX_CHUNKS
    assert seq % rows == 0
    ns = seq // rows
    t = batch * seq
    row = lambda n: pl.BlockSpec((rows, n), lambda b, c: (b * ns + c, 0))
    full = lambda a: pl.BlockSpec(a.shape, lambda b, c: (0,) * a.ndim)
    dtbt, alogt = dtb.reshape(N_SSD_HEADS, 1), alog.reshape(N_SSD_HEADS, 1)
    params = (dtb, dtbt, alog, alogt, dskip, nw)
    return pl.pallas_call(
        _mixer_kernel,
        grid=(batch, ns),
        in_specs=[row(D_SSD), row(D_XBC), row(N_SSD_HEADS),
                  pl.BlockSpec((N_SSD_HEADS, rows), lambda b, c: (0, b * ns + c))]
                 + [full(a) for a in params],
        out_specs=row(D_SSD),
        out_shape=jax.ShapeDtypeStruct((t, D_SSD), BF16),
        scratch_shapes=[pltpu.VMEM((D_STATE, D_SSD), F32)],
        compiler_params=pltpu.CompilerParams(
            dimension_semantics=("arbitrary", "arbitrary"), vmem_limit_bytes=VMEM_LIMIT_BYTES),
    )(z, xbc, dt, dtt, *params)


def _outproj_kernel(yssd_ref, ysc_ref, x_ref, wout_ref, nw_ref, wrh_ref, wrl_ref, br_ref,
                    h1_ref, xn_ref, idx_ref, gate_ref, rank_ref, cnt_ref, run_ref):
    tm = x_ref.shape[0]

    @pl.when(pl.program_id(0) == 0)
    def _():
        run_ref[...] = jnp.zeros_like(run_ref)

    h1 = (x_ref[...] + _dot(yssd_ref[...], wout_ref[:D_SSD, :])
          + _dot(ysc_ref[...], wout_ref[D_SSD:, :]))
    h1_ref[...] = h1
    xn = h1 * lax.rsqrt(jnp.mean(h1 * h1, axis=-1, keepdims=True) + EPS) * nw_ref[...]
    for c, words in enumerate(_pack_rows(xn)):
        xn_ref[c] = words

    xh = xn.astype(BF16)
    xl = (xn - xh.astype(F32)).astype(BF16)
    nt = lambda w, a: lax.dot_general(w, a, (((1,), (1,)), ((), ())), preferred_element_type=F32)
    logits = (nt(wrh_ref[...], xh) + nt(wrl_ref[...], xh) + nt(wrh_ref[...], xl)
              + br_ref[...])

    expert = lax.broadcasted_iota(jnp.int32, (N_EXPERTS, tm), 0).astype(F32)
    work = logits
    vals, firsts, sels = [], [], []
    for _ in range(TOP_K):
        m = jnp.max(work, axis=0, keepdims=True)
        first = jnp.min(jnp.where(work == m, expert, float(N_EXPERTS)), axis=0, keepdims=True)
        sel = expert == first
        vals.append(m)
        firsts.append(first)
        sels.append(sel)
        work = jnp.where(sel, -jnp.inf, work)
    exps = [jnp.exp(v - vals[0]) for v in vals]
    denom = exps[0] + exps[1] + exps[2] + exps[3]

    onehot = jnp.zeros((N_EXPERTS, tm), F32)
    for sel in sels:
        onehot = onehot + jnp.where(sel, 1.0, 0.0)
    ri = lax.broadcasted_iota(jnp.int32, (tm, tm), 0)
    ci = lax.broadcasted_iota(jnp.int32, (tm, tm), 1)
    earlier = jnp.where(ri < ci, 1.0, 0.0).astype(BF16)
    before = run_ref[...] + _dot(onehot.astype(BF16), earlier)
    run_ref[...] = run_ref[...] + jnp.sum(onehot, axis=1, keepdims=True)
    cnt_ref[...] = run_ref[...]

    slot = lax.broadcasted_iota(jnp.int32, (TOP_K, tm), 0)
    idx = jnp.zeros((TOP_K, tm), F32)
    gate = jnp.zeros((TOP_K, tm), F32)
    rank = jnp.zeros((TOP_K, tm), F32)
    for k in range(TOP_K):
        r_k = jnp.sum(jnp.where(sels[k], before, 0.0), axis=0, keepdims=True)
        idx = jnp.where(slot == k, firsts[k], idx)
        gate = jnp.where(slot == k, exps[k] / denom, gate)
        rank = jnp.where(slot == k, r_k, rank)
    idx_ref[...] = idx.astype(jnp.int32)
    gate_ref[...] = gate
    rank_ref[...] = rank.astype(jnp.int32)


def _outproj(yssd, ysc, x2, w_out, nw, wr_hi, wr_lo, br, first_tile, t):
    tm = OUT_TILE
    src = lambda n: pl.BlockSpec((tm, n), lambda i: (first_tile + i, 0))
    row = lambda n: pl.BlockSpec((tm, n), lambda i: (i, 0))
    slots = pl.BlockSpec((TOP_K, tm), lambda i: (0, i))
    full = lambda a: pl.BlockSpec(a.shape, lambda i: (0,) * a.ndim)
    return pl.pallas_call(
        _outproj_kernel,
        grid=(t // tm,),
        in_specs=[src(D_SSD), src(D_SC), src(D_MODEL), full(w_out), full(nw), full(wr_hi),
                  full(wr_lo), full(br)],
        out_specs=[row(D_MODEL), pl.BlockSpec((ROW_PARTS, tm, LANES), lambda i: (0, i, 0)),
                   slots, slots, slots,
                   pl.BlockSpec((N_EXPERTS, 1), lambda i: (0, 0))],
        out_shape=[jax.ShapeDtypeStruct((t, D_MODEL), F32),
                   jax.ShapeDtypeStruct((ROW_PARTS, t, LANES), jnp.int32),
                   jax.ShapeDtypeStruct((TOP_K, t), jnp.int32),
                   jax.ShapeDtypeStruct((TOP_K, t), F32),
                   jax.ShapeDtypeStruct((TOP_K, t), jnp.int32),
                   jax.ShapeDtypeStruct((N_EXPERTS, 1), F32)],
        scratch_shapes=[pltpu.VMEM((N_EXPERTS, 1), F32)],
        compiler_params=pltpu.CompilerParams(
            dimension_semantics=("arbitrary",), vmem_limit_bytes=VMEM_LIMIT_BYTES),
    )(yssd, ysc, x2, w_out, nw, wr_hi, wr_lo, br)


def _dest_kernel(pstart_ref, idx_ref, rank_ref, dest_ref):
    idx = idx_ref[...]
    base = jnp.zeros_like(idx)
    for e in range(N_EXPERTS):
        base = jnp.where(idx == e, pstart_ref[e], base)
    dest_ref[...] = base + rank_ref[...]


def _dest(pstarts, idx, rank):
    full = lambda a: pl.BlockSpec(a.shape, lambda i, ps: (0,) * a.ndim)
    return pl.pallas_call(
        _dest_kernel,
        grid_spec=pltpu.PrefetchScalarGridSpec(
            num_scalar_prefetch=1, grid=(1,),
            in_specs=[full(idx), full(rank)], out_specs=full(idx)),
        out_shape=jax.ShapeDtypeStruct(idx.shape, jnp.int32),
    )(pstarts, idx, rank)


def _sc_mesh():
    return plsc.VectorSubcoreMesh(core_axis_name="core", subcore_axis_name="subcore")


def _part_rows(dest, n_rows):
    off = (jnp.arange(ROW_PARTS, dtype=jnp.int32) * n_rows)[:, None, None]
    return (dest[None] + off).reshape(ROW_PARTS * dest.shape[0], dest.shape[1])


def _sc_dispatch(x, dest, n_rows):
    parts, t, width = x.shape
    n_slots = dest.shape[0]
    win = SC_WINDOW
    nwin = t // win

    @functools.partial(pl.kernel,
                       out_type=jax.ShapeDtypeStruct((parts * n_rows, width), jnp.int32),
                       mesh=_sc_mesh(), scratch_types=[])
    def scatter(x_hbm, idx_hbm, out_hbm):
        def body(x_vmem, idx_vmem):
            for k in range(n_slots):
                pltpu.sync_copy(x_vmem, out_hbm.at[idx_vmem.at[k]])

        pltpu.emit_pipeline(
            body,
            grid=(parts * nwin,),
            in_specs=[pl.BlockSpec((win, width), lambda j: (j, 0)),
                      pl.BlockSpec((n_slots, win), lambda j: (j // nwin, j % nwin))],
            out_specs=[],
            core_axis_name=("core", "subcore"),
            dimension_semantics=(pltpu.PARALLEL,),
        )(x_hbm, idx_hbm)

    xs = scatter(x.reshape(parts * t, width), _part_rows(dest, n_rows))
    return xs.reshape(parts, n_rows, width)


def _sc_collect(ys, dest):
    parts, n_rows, width = ys.shape
    n_slots, t = dest.shape
    win = SC_WINDOW
    nwin = t // win
    idx = _part_rows(dest, n_rows).reshape(parts, n_slots, t).swapaxes(0, 1)
    idx = idx.reshape(n_slots * parts, t)

    @functools.partial(pl.kernel,
                       out_type=jax.ShapeDtypeStruct((n_slots * parts * t, width), jnp.int32),
                       mesh=_sc_mesh(), scratch_types=[])
    def gather(y_hbm, idx_hbm, out_hbm):
        def body(idx_vmem, out_vmem):
            pltpu.sync_copy(y_hbm.at[idx_vmem.at[0]], out_vmem)

        pltpu.emit_pipeline(
            body,
            grid=(n_slots * parts * nwin,),
            in_specs=[pl.BlockSpec((1, win), lambda j: (j // nwin, j % nwin))],
            out_specs=[pl.BlockSpec((win, width), lambda j: (j, 0))],
            core_axis_name=("core", "subcore"),
            dimension_semantics=(pltpu.PARALLEL,),
        )(idx_hbm, out_hbm)

    out = gather(ys.reshape(parts * n_rows, width), idx)
    return out.reshape(n_slots, parts, t, width)


def _expert_kernel(blk_e_ref, blk_n_ref, xs_ref, wgu_ref, bgu_ref, wd_ref, bd_ref, out_ref,
                   wgu_b, wd_b):
    i = pl.program_id(0)
    n_valid = blk_n_ref[i]

    @pl.when((i == 0) | (blk_e_ref[i] != blk_e_ref[jnp.maximum(i - 1, 0)]))
    def _():
        wgu_b[...] = wgu_ref[0].astype(BF16)
        wd_b[...] = wd_ref[0].astype(BF16)

    @pl.when(n_valid == 0)
    def _():
        out_ref[...] = jnp.zeros_like(out_ref)

    @pl.when(n_valid > 0)
    def _():
        for r in range(0, ROW_BLOCK, EXPERT_SUB):
            words = [xs_ref[c, r:r + EXPERT_SUB] for c in range(ROW_PARTS)]
            rows = r + lax.broadcasted_iota(jnp.int32, words[0].shape, 0)
            xb = _unpack_rows([jnp.where(rows < n_valid, w, 0) for w in words]).astype(BF16)
            gu = _dot(xb, wgu_b[...]) + bgu_ref[0]
            gate = jnp.minimum(gu[:, :EXPERT_FF], SWIGLU_LIMIT)
            up = jnp.clip(gu[:, EXPERT_FF:], -SWIGLU_LIMIT, SWIGLU_LIMIT)
            hid = (up + 1.0) * (gate * (1.0 / (1.0 + jnp.exp(-SWIGLU_ALPHA * gate))))
            out = _dot(hid.astype(BF16), wd_b[...]) + bd_ref[0]
            for c, w in enumerate(_pack_rows(out)):
                out_ref[c, r:r + EXPERT_SUB] = w


def _experts(blk_e, blk_n, xs, wgu, bgu, wd, bd):
    rb = ROW_BLOCK
    by_expert = lambda a: pl.BlockSpec((1,) + a.shape[1:], lambda i, be, bn: (be[i], 0, 0))
    return pl.pallas_call(
        _expert_kernel,
        grid_spec=pltpu.PrefetchScalarGridSpec(
            num_scalar_prefetch=2, grid=(xs.shape[1] // rb,),
            in_specs=[pl.BlockSpec((ROW_PARTS, rb, LANES), lambda i, be, bn: (0, i, 0)),
                      by_expert(wgu), by_expert(bgu), by_expert(wd), by_expert(bd)],
            out_specs=pl.BlockSpec((ROW_PARTS, rb, LANES), lambda i, be, bn: (0, i, 0)),
            scratch_shapes=[pltpu.VMEM(wgu.shape[1:], BF16), pltpu.VMEM(wd.shape[1:], BF16)]),
        out_shape=jax.ShapeDtypeStruct(xs.shape, xs.dtype),
        compiler_params=pltpu.CompilerParams(
            dimension_semantics=("arbitrary",), vmem_limit_bytes=VMEM_LIMIT_BYTES),
    )(blk_e, blk_n, xs, wgu, bgu, wd, bd)


def _combine_kernel(gate_ref, h1_ref, fw_ref, y0_ref, y1_ref, y2_ref, y3_ref, *rest):
    out_ref = rest[-1]
    parts = _split3(gate_ref[...])
    slot = lax.broadcasted_iota(jnp.int32, (TOP_K, LANES), 0)
    h2 = h1_ref[...]
    for k, y_ref in enumerate((y0_ref, y1_ref, y2_ref, y3_ref)):
        pick = jnp.where(slot == k, 1.0, 0.0).astype(BF16)
        g = sum(lax.dot_general(p, pick, (((0,), (0,)), ((), ())), preferred_element_type=F32)
                for p in parts)
        y = _unpack_rows([y_ref[c] for c in range(ROW_PARTS)])
        h2 = h2 + jnp.concatenate([g] * (D_MODEL // LANES), axis=-1) * y
    out_ref[...] = h2 * lax.rsqrt(jnp.mean(h2 * h2, axis=-1, keepdims=True) + EPS) * fw_ref[...]


def _combine(gates, h1, fw, planes, first_tile, out_tile, t, out_prev):
    d = h1.shape[1]
    tm = COMBINE_TILE
    n = planes.shape[2]
    plane = lambda k: pl.BlockSpec((None, ROW_PARTS, tm, LANES), lambda i: (k, 0, i, 0))
    in_specs = [pl.BlockSpec((TOP_K, tm), lambda i: (0, first_tile + i)),
                pl.BlockSpec((tm, d), lambda i: (first_tile + i, 0)),
                pl.BlockSpec((1, d), lambda i: (0, 0))]
    in_specs += [plane(k) for k in range(TOP_K)]
    args = [gates, h1, fw] + [planes] * TOP_K
    aliases = {}
    if out_prev is not None:
        in_specs.append(pl.BlockSpec(memory_space=pl.ANY))
        aliases = {len(args): 0}
        args.append(out_prev)
    return pl.pallas_call(
        _combine_kernel,
        grid=(n // tm,),
        in_specs=in_specs,
        out_specs=pl.BlockSpec((tm, d), lambda i: (out_tile + i, 0)),
        out_shape=jax.ShapeDtypeStruct((t, d), F32),
        input_output_aliases=aliases,
        compiler_params=pltpu.CompilerParams(
            dimension_semantics=("arbitrary",), vmem_limit_bytes=VMEM_LIMIT_BYTES),
    )(*args)


def _block_plan(counts, n_blocks):
    rb = ROW_BLOCK
    pcounts = (counts + rb - 1) // rb * rb
    pends = jnp.cumsum(pcounts)
    pstarts = pends - pcounts
    block_start = jnp.arange(n_blocks, dtype=jnp.int32) * rb
    blk_e = jnp.minimum(jnp.sum(block_start[:, None] >= pends[None, :], axis=1),
                        N_EXPERTS - 1).astype(jnp.int32)
    mine = blk_e[:, None] == jnp.arange(N_EXPERTS, dtype=jnp.int32)[None, :]
    pick = lambda table: jnp.sum(jnp.where(mine, table[None, :], 0), axis=1)
    blk_n = jnp.clip(pick(counts) - (block_start - pick(pstarts)), 0, rb).astype(jnp.int32)
    return pstarts.astype(jnp.int32), blk_e, blk_n


def kernel(x, norm1_w, w_in, ssd_conv_w, ssd_conv_b, dt_bias, a_log, d_skip, ssd_norm_w,
           sc_conv_w, sc_norm_w, w_out, norm2_w, w_router, b_router, w_gate_up, b_gate_up,
           w_down, b_down, final_norm_w):
    assert norm1_w.shape[0] == 1, "single-layer problem"
    batch, seq, d = x.shape
    t = batch * seq
    x2 = x.reshape(t, d)
    row = lambda a: a.reshape(1, -1)

    s1, s2 = D_SSD + D_XBC, D_SSD + D_XBC + N_SSD_HEADS
    w = w_in[0]
    w_zx, w_dt, w_sc = w[:, :s1].astype(BF16), w[:, s1:s2].astype(BF16), w[:, s2:].astype(BF16)
    z, xbc, ysc, dt, dtt = _inproj(x2, seq, row(norm1_w[0]), w_zx, w_sc, w_dt, w_dt.T,
                                   ssd_conv_w[0], row(ssd_conv_b[0]), sc_conv_w[0],
                                   row(sc_norm_w[0]))
    yssd = _mixer(z, xbc, dt, dtt, batch, seq, row(dt_bias[0]), row(a_log[0]), row(d_skip[0]),
                  row(ssd_norm_w[0]))

    wr = w_router[0].T
    wr_hi = wr.astype(BF16)
    wr_lo = (wr - wr_hi.astype(F32)).astype(BF16)
    unit = t // sum(MOE_SPLIT)
    sizes = [unit * s for s in MOE_SPLIT]
    starts = [sum(sizes[:g]) for g in range(len(sizes))]
    assert sum(sizes) == t and all(n % (COMBINE_CHUNKS * COMBINE_TILE) == 0 and n % OUT_TILE == 0
                                   for n in sizes)
    w_out_b, nw2, br = w_out[0].astype(BF16), row(norm2_w[0]), b_router[0].reshape(-1, 1)
    routed = []
    for t0, tg in zip(starts, sizes):
        n_rows = tg * TOP_K + N_EXPERTS * ROW_BLOCK
        h1, xn2, idx, gates, rank, counts = _outproj(
            yssd, ysc, x2, w_out_b, nw2, wr_hi, wr_lo, br, t0 // OUT_TILE, tg)
        pstarts, blk_e, blk_n = _block_plan(counts[:, 0].astype(jnp.int32),
                                            n_rows // ROW_BLOCK)
        dest = _dest(pstarts, idx, rank)
        xs = _sc_dispatch(xn2, dest, n_rows)
        routed.append((h1, gates, dest, blk_e, blk_n, xs))
    expert_out = [_experts(blk_e, blk_n, xs, w_gate_up[0], b_gate_up[0][:, None, :],
                           w_down[0], b_down[0][:, None, :])
                  for (_, _, _, blk_e, blk_n, xs) in routed]
    fw, out = row(final_norm_w), None
    for t0, tg, (h1, gates, dest, _, _, _), ys in zip(starts, sizes, routed, expert_out):
        tc = tg // COMBINE_CHUNKS
        for c in range(COMBINE_CHUNKS):
            planes = _sc_collect(ys, dest[:, c * tc:(c + 1) * tc])
            out = _combine(gates, h1, fw, planes, c * (tc // COMBINE_TILE),
                           (t0 + c * tc) // COMBINE_TILE, t, out)
    return out.reshape(batch, seq, d)
```

```python
import functools

import jax
import jax.numpy as jnp
from jax import lax
from jax.experimental import pallas as pl
from jax.experimental.pallas import tpu as pltpu
from jax.experimental.pallas import tpu_sc as plsc

D_MODEL = 1024
D_SSD = 1024
SSD_HEAD_DIM = 64
N_SSD_HEADS = 16
N_GROUPS = 2
HEADS_PER_GROUP = 8
D_STATE = 128
SSD_CONV = 4
CHUNK = 128
D_XBC = D_SSD + 2 * N_GROUPS * D_STATE
D_SC = 1024
SC_CONV = 3
N_EXPERTS = 32
TOP_K = 4
EXPERT_FF = 1024
SWIGLU_LIMIT = 7.0
SWIGLU_ALPHA = 1.702
EPS = 1e-5

VMEM_LIMIT_BYTES = 56 * 1024 * 1024
SUBLANES = 8
ROW_BLOCK = 512
EXPERT_SUB = 256
IN_TILE = 512
IN_SUB = 256
MIX_CHUNKS = 8
OUT_TILE = 1024
COMBINE_TILE = 512
MOE_SPLIT = (5, 3)
COMBINE_CHUNKS = 4
LANES = 128
ROW_PARTS = D_MODEL // (2 * LANES)
SC_WINDOW = 128
HIGH_HALF = -65536

F32 = jnp.float32
BF16 = jnp.bfloat16


def _dot(a, b):
    return jnp.dot(a, b, preferred_element_type=F32)


def _split3(v):
    hi = v.astype(BF16)
    r1 = v - hi.astype(F32)
    mid = r1.astype(BF16)
    lo = (r1 - mid.astype(F32)).astype(BF16)
    return hi, mid, lo


def _dot_exact_rhs01(v, m01):
    hi, mid, lo = _split3(v)
    return _dot(hi, m01) + _dot(mid, m01) + _dot(lo, m01)


def _dot_exact_lhs01(m01, v):
    hi, mid, lo = _split3(v)
    return _dot(m01, hi) + _dot(m01, mid) + _dot(m01, lo)


def _spread(v, m01):
    hi = v.astype(BF16)
    mid = (v - hi.astype(F32)).astype(BF16)
    return _dot(hi, m01) + _dot(mid, m01)


def _silu(v):
    return v * (1.0 / (1.0 + jnp.exp(-v)))


def _softplus(v):
    return jnp.maximum(v, 0.0) + jnp.log1p(jnp.exp(-jnp.abs(v)))


def _pack_rows(v):
    bits = lambda a: lax.bitcast_convert_type(a.astype(BF16).astype(F32), jnp.int32)
    parts = []
    for c in range(ROW_PARTS):
        lo = v[:, (2 * c) * LANES:(2 * c + 1) * LANES]
        hi = v[:, (2 * c + 1) * LANES:(2 * c + 2) * LANES]
        parts.append(bits(hi) | lax.shift_right_logical(bits(lo), 16))
    return parts


def _unpack_rows(parts):
    cols = []
    for w in parts:
        cols.append(lax.bitcast_convert_type(w << 16, F32))
        cols.append(lax.bitcast_convert_type(w & HIGH_HALF, F32))
    return jnp.concatenate(cols, axis=-1)


def _shifted(cur, tail, s):
    if s == 0:
        return cur
    rc = pltpu.roll(cur, s, axis=0)
    rt = pltpu.roll(tail, s, axis=0)
    row = lax.broadcasted_iota(jnp.int32, tail.shape, 0)
    first = jnp.where(row < s, rt, rc[0:SUBLANES])
    return jnp.concatenate([first, rc[SUBLANES:]], axis=0)


def _causal_conv(tail_ref, cur, w_ref, acc):
    rows = cur.shape[0]
    n_taps = w_ref.shape[0]
    tail = tail_ref[...]
    for k in range(n_taps):
        acc = acc + w_ref[k:k + 1, :] * _shifted(cur, tail, n_taps - 1 - k)
    tail_ref[...] = cur[rows - SUBLANES:rows]
    return acc


def _inproj_kernel(tiles_per_seq, x_ref, nw_ref, wzx_ref, wsc_ref, wdt_ref, wdtt_ref,
                   cw_ref, cb_ref, scw_ref, scnw_ref,
                   z_ref, xbc_ref, ysc_ref, dt_ref, dtt_ref, hist_ref, schist_ref):
    tm = x_ref.shape[0]

    @pl.when(pl.program_id(0) % tiles_per_seq == 0)
    def _():
        hist_ref[...] = jnp.zeros_like(hist_ref)
        schist_ref[...] = jnp.zeros_like(schist_ref)

    for r in range(0, tm, IN_SUB):
        rows = pl.ds(r, IN_SUB)
        x = x_ref[rows, :]
        xn = x * lax.rsqrt(jnp.mean(x * x, axis=-1, keepdims=True) + EPS) * nw_ref[...]
        xb = xn.astype(BF16)
        proj = lambda w_ref, o, n, xb=xb: _dot(xb, w_ref[:, o:o + n])
        z_ref[rows, :] = proj(wzx_ref, 0, D_SSD).astype(z_ref.dtype)

        conv = _causal_conv(hist_ref, proj(wzx_ref, D_SSD, D_XBC), cw_ref, cb_ref[...])
        xbc_ref[rows, :] = _silu(conv).astype(xbc_ref.dtype)

        cu = proj(wsc_ref, D_SC, D_SC) * proj(wsc_ref, 2 * D_SC, D_SC)
        v = _causal_conv(schist_ref, cu, scw_ref, jnp.zeros_like(cu))
        gv = proj(wsc_ref, 0, D_SC) * v
        y_sc = gv * lax.rsqrt(jnp.mean(gv * gv, axis=-1, keepdims=True) + EPS) * scnw_ref[...]
        ysc_ref[rows, :] = y_sc.astype(ysc_ref.dtype)

        dt_ref[rows, :] = _dot(xb, wdt_ref[...])
        dtt_ref[:, rows] = lax.dot_general(wdtt_ref[...], xb, (((1,), (1,)), ((), ())),
                                           preferred_element_type=F32)


def _inproj(x2, seq, nw, w_zx, w_sc, w_dt, w_dtt, cw, cb, scw, scnw):
    t = x2.shape[0]
    tm = IN_TILE
    assert seq % tm == 0, "in-proj tiles must not straddle sequences (causal conv carry)"
    row = lambda n: pl.BlockSpec((tm, n), lambda i: (i, 0))
    full = lambda a: pl.BlockSpec(a.shape, lambda i: (0,) * a.ndim)
    params = (nw, w_zx, w_sc, w_dt, w_dtt, cw, cb, scw, scnw)
    return pl.pallas_call(
        functools.partial(_inproj_kernel, seq // tm),
        grid=(t // tm,),
        in_specs=[row(D_MODEL)] + [full(a) for a in params],
        out_specs=[row(D_SSD), row(D_XBC), row(D_SC),
                   row(N_SSD_HEADS), pl.BlockSpec((N_SSD_HEADS, tm), lambda i: (0, i))],
        out_shape=[jax.ShapeDtypeStruct((t, D_SSD), BF16),
                   jax.ShapeDtypeStruct((t, D_XBC), BF16),
                   jax.ShapeDtypeStruct((t, D_SC), BF16),
                   jax.ShapeDtypeStruct((t, N_SSD_HEADS), F32),
                   jax.ShapeDtypeStruct((N_SSD_HEADS, t), F32)],
        scratch_shapes=[pltpu.VMEM((SUBLANES, D_XBC), F32), pltpu.VMEM((SUBLANES, D_SC), F32)],
        compiler_params=pltpu.CompilerParams(
            dimension_semantics=("arbitrary",), vmem_limit_bytes=VMEM_LIMIT_BYTES),
    )(x2, *params)


def _mixer_kernel(z_ref, xbc_ref, dt_ref, dtt_ref,
                  dtb_ref, dtbt_ref, alog_ref, alogt_ref, dskip_ref, nw_ref, out_ref, state_ref):
    q = CHUNK

    @pl.when(pl.program_id(1) == 0)
    def _():
        state_ref[...] = jnp.zeros_like(state_ref)

    ri = lax.broadcasted_iota(jnp.int32, (q, q), 0)
    ci = lax.broadcasted_iota(jnp.int32, (q, q), 1)
    causal = ri >= ci
    tri = jnp.where(causal, 1.0, 0.0).astype(BF16)
    trit = jnp.where(ri <= ci, 1.0, 0.0).astype(BF16)
    hh = lax.broadcasted_iota(jnp.int32, (N_SSD_HEADS, D_SSD), 0)
    hl = lax.broadcasted_iota(jnp.int32, (N_SSD_HEADS, D_SSD), 1) // SSD_HEAD_DIM
    expand = jnp.where(hh == hl, 1.0, 0.0).astype(BF16)
    lane_lo = lax.broadcasted_iota(jnp.int32, (q, 2 * SSD_HEAD_DIM), 1) < SSD_HEAD_DIM
    gw = HEADS_PER_GROUP * SSD_HEAD_DIM
    n_bc = N_GROUPS * D_STATE
    a_neg = -jnp.exp(alog_ref[...])
    a_negt = -jnp.exp(alogt_ref[...])
    d_x = _dot_exact_rhs01(dskip_ref[...], expand)

    for s in range(MIX_CHUNKS):
        rows = pl.ds(s * q, q)
        xs = xbc_ref[rows, :D_SSD].astype(F32)
        b_all = xbc_ref[rows, D_SSD:D_SSD + n_bc]
        c_all = xbc_ref[rows, D_SSD + n_bc:]

        dt = _softplus(dt_ref[rows, :] + dtb_ref[...])
        dtt = _softplus(dtt_ref[:, rows] + dtbt_ref[...])
        acum = _dot_exact_lhs01(tri, dt * a_neg)
        acumt = _dot_exact_rhs01(dtt * a_negt, trit)
        a_last = acum[q - 1:q, :]

        xdt = xs * _spread(dt, expand)
        xdt_b = xdt.astype(BF16)
        dec_in = _spread(jnp.exp(acum), expand)
        dec_out = _spread(jnp.exp(a_last - acum), expand)
        dec_all = _dot_exact_rhs01(jnp.exp(a_last), expand)
        xw_b = (xdt * dec_out).astype(BF16)

        y_parts = []
        for g in range(N_GROUPS):
            b_g = b_all[:, g * D_STATE:(g + 1) * D_STATE]
            c_g = c_all[:, g * D_STATE:(g + 1) * D_STATE]
            cb = lax.dot_general(c_g, b_g, (((1,), (1,)), ((), ())), preferred_element_type=F32)
            st = state_ref[:, g * gw:(g + 1) * gw]
            y_inter = _dot(c_g, st.astype(BF16)) * dec_in[:, g * gw:(g + 1) * gw]
            new_st = lax.dot_general(b_g, xw_b[:, g * gw:(g + 1) * gw], (((0,), (0,)), ((), ())),
                                     preferred_element_type=F32)
            state_ref[:, g * gw:(g + 1) * gw] = st * dec_all[:, g * gw:(g + 1) * gw] + new_st
            pair_parts = []
            for pr in range(HEADS_PER_GROUP // 2):
                h0 = g * HEADS_PER_GROUP + 2 * pr
                lo = h0 * SSD_HEAD_DIM
                rhs = xdt_b[:, lo:lo + 2 * SSD_HEAD_DIM]
                ys = []
                for h in (h0, h0 + 1):
                    seg = acum[:, h:h + 1] - acumt[h:h + 1, :]
                    m = (cb * jnp.exp(jnp.where(causal, seg, -jnp.inf))).astype(BF16)
                    ys.append(_dot(m, rhs))
                pair_parts.append(jnp.where(lane_lo, ys[0], ys[1]))
            y_parts.append(jnp.concatenate(pair_parts, axis=-1) + y_inter)
        y = jnp.concatenate(y_parts, axis=-1) + d_x * xs

        gated = y * _silu(z_ref[rows, :].astype(F32))
        outs = []
        for g in range(N_GROUPS):
            gg = gated[:, g * gw:(g + 1) * gw]
            outs.append(gg * lax.rsqrt(jnp.mean(gg * gg, axis=-1, keepdims=True) + EPS))
        out_ref[rows, :] = (jnp.concatenate(outs, axis=-1) * nw_ref[...]).astype(out_ref.dtype)


def _mixer(z, xbc, dt, dtt, batch, seq, dtb, alog, dskip, nw):
    rows = CHUNK * MIX_CHUNKS
    assert seq % rows == 0
    ns = seq // rows
    t = batch * seq
    row = lambda n: pl.BlockSpec((rows, n), lambda b, c: (b * ns + c, 0))
    full = lambda a: pl.BlockSpec(a.shape, lambda b, c: (0,) * a.ndim)
    dtbt, alogt = dtb.reshape(N_SSD_HEADS, 1), alog.reshape(N_SSD_HEADS, 1)
    params = (dtb, dtbt, alog, alogt, dskip, nw)
    return pl.pallas_call(
        _mixer_kernel,
        grid=(batch, ns),
        in_specs=[row(D_SSD), row(D_XBC), row(N_SSD_HEADS),
                  pl.BlockSpec((N_SSD_HEADS, rows), lambda b, c: (0, b * ns + c))]
                 + [full(a) for a in params],
        out_specs=row(D_SSD),
        out_shape=jax.ShapeDtypeStruct((t, D_SSD), BF16),
        scratch_shapes=[pltpu.VMEM((D_STATE, D_SSD), F32)],
        compiler_params=pltpu.CompilerParams(
            dimension_semantics=("arbitrary", "arbitrary"), vmem_limit_bytes=VMEM_LIMIT_BYTES),
    )(z, xbc, dt, dtt, *params)


def _outproj_kernel(yssd_ref, ysc_ref, x_ref, wout_ref, nw_ref, wrh_ref, wrl_ref, br_ref,
                    h1_ref, xn_ref, idx_ref, gate_ref, rank_ref, cnt_ref, run_ref, earlier_ref):
    tm = x_ref.shape[0]

    @pl.when(pl.program_id(0) == 0)
    def _():
        run_ref[...] = jnp.zeros_like(run_ref)
        ri = lax.broadcasted_iota(jnp.int32, (tm, tm), 0)
        ci = lax.broadcasted_iota(jnp.int32, (tm, tm), 1)
        earlier_ref[...] = jnp.where(ri < ci, 1.0, 0.0).astype(BF16)

    h1 = (x_ref[...] + _dot(yssd_ref[...], wout_ref[:D_SSD, :])
          + _dot(ysc_ref[...], wout_ref[D_SSD:, :]))
    h1_ref[...] = h1
    xn = h1 * lax.rsqrt(jnp.mean(h1 * h1, axis=-1, keepdims=True) + EPS) * nw_ref[...]
    for c, words in enumerate(_pack_rows(xn)):
        xn_ref[c] = words

    xh = xn.astype(BF16)
    xl = (xn - xh.astype(F32)).astype(BF16)
    nt = lambda w, a: lax.dot_general(w, a, (((1,), (1,)), ((), ())), preferred_element_type=F32)
    logits = (nt(wrh_ref[...], xh) + nt(wrl_ref[...], xh) + nt(wrh_ref[...], xl)
              + br_ref[...])

    expert = lax.broadcasted_iota(jnp.int32, (N_EXPERTS, tm), 0).astype(F32)
    work = logits
    vals, firsts, sels = [], [], []
    for _ in range(TOP_K):
        m = jnp.max(work, axis=0, keepdims=True)
        first = jnp.min(jnp.where(work == m, expert, float(N_EXPERTS)), axis=0, keepdims=True)
        sel = expert == first
        vals.append(m)
        firsts.append(first)
        sels.append(sel)
        work = jnp.where(sel, -jnp.inf, work)
    exps = [jnp.exp(v - vals[0]) for v in vals]
    denom = exps[0] + exps[1] + exps[2] + exps[3]

    onehot = jnp.zeros((N_EXPERTS, tm), F32)
    for sel in sels:
        onehot = onehot + jnp.where(sel, 1.0, 0.0)
    before = run_ref[...] + _dot(onehot.astype(BF16), earlier_ref[...])
    run_ref[...] = run_ref[...] + jnp.sum(onehot, axis=1, keepdims=True)
    cnt_ref[...] = run_ref[...]

    slot = lax.broadcasted_iota(jnp.int32, (TOP_K, tm), 0)
    idx = jnp.zeros((TOP_K, tm), F32)
    gate = jnp.zeros((TOP_K, tm), F32)
    rank = jnp.zeros((TOP_K, tm), F32)
    for k in range(TOP_K):
        r_k = jnp.sum(jnp.where(sels[k], before, 0.0), axis=0, keepdims=True)
        idx = jnp.where(slot == k, firsts[k], idx)
        gate = jnp.where(slot == k, exps[k] / denom, gate)
        rank = jnp.where(slot == k, r_k, rank)
    idx_ref[...] = idx.astype(jnp.int32)
    gate_ref[...] = gate
    rank_ref[...] = rank.astype(jnp.int32)


def _outproj(yssd, ysc, x2, w_out, nw, wr_hi, wr_lo, br, first_tile, t):
    tm = OUT_TILE
    src = lambda n: pl.BlockSpec((tm, n), lambda i: (first_tile + i, 0))
    row = lambda n: pl.BlockSpec((tm, n), lambda i: (i, 0))
    slots = pl.BlockSpec((TOP_K, tm), lambda i: (0, i))
    full = lambda a: pl.BlockSpec(a.shape, lambda i: (0,) * a.ndim)
    return pl.pallas_call(
        _outproj_kernel,
        grid=(t // tm,),
        in_specs=[src(D_SSD), src(D_SC), src(D_MODEL), full(w_out), full(nw), full(wr_hi),
                  full(wr_lo), full(br)],
        out_specs=[row(D_MODEL), pl.BlockSpec((ROW_PARTS, tm, LANES), lambda i: (0, i, 0)),
                   slots, slots, slots,
                   pl.BlockSpec((N_EXPERTS, 1), lambda i: (0, 0))],
        out_shape=[jax.ShapeDtypeStruct((t, D_MODEL), F32),
                   jax.ShapeDtypeStruct((ROW_PARTS, t, LANES), jnp.int32),
                   jax.ShapeDtypeStruct((TOP_K, t), jnp.int32),
                   jax.ShapeDtypeStruct((TOP_K, t), F32),
                   jax.ShapeDtypeStruct((TOP_K, t), jnp.int32),
                   jax.ShapeDtypeStruct((N_EXPERTS, 1), F32)],
        scratch_shapes=[pltpu.VMEM((N_EXPERTS, 1), F32), pltpu.VMEM((tm, tm), BF16)],
        compiler_params=pltpu.CompilerParams(
            dimension_semantics=("arbitrary",), vmem_limit_bytes=VMEM_LIMIT_BYTES),
    )(yssd, ysc, x2, w_out, nw, wr_hi, wr_lo, br)


def _dest_kernel(pstart_ref, idx_ref, rank_ref, dest_ref):
    idx = idx_ref[...]
    base = jnp.zeros_like(idx)
    for e in range(N_EXPERTS):
        base = jnp.where(idx == e, pstart_ref[e], base)
    dest_ref[...] = base + rank_ref[...]


def _dest(pstarts, idx, rank):
    full = lambda a: pl.BlockSpec(a.shape, lambda i, ps: (0,) * a.ndim)
    return pl.pallas_call(
        _dest_kernel,
        grid_spec=pltpu.PrefetchScalarGridSpec(
            num_scalar_prefetch=1, grid=(1,),
            in_specs=[full(idx), full(rank)], out_specs=full(idx)),
        out_shape=jax.ShapeDtypeStruct(idx.shape, jnp.int32),
    )(pstarts, idx, rank)


def _sc_mesh():
    return plsc.VectorSubcoreMesh(core_axis_name="core", subcore_axis_name="subcore")


def _part_rows(dest, n_rows):
    off = (jnp.arange(ROW_PARTS, dtype=jnp.int32) * n_rows)[:, None, None]
    return (dest[None] + off).reshape(ROW_PARTS * dest.shape[0], dest.shape[1])


def _sc_dispatch(x, dest, n_rows):
    parts, t, width = x.shape
    n_slots = dest.shape[0]
    win = SC_WINDOW
    nwin = t // win

    @functools.partial(pl.kernel,
                       out_type=jax.ShapeDtypeStruct((parts * n_rows, width), jnp.int32),
                       mesh=_sc_mesh(), scratch_types=[])
    def scatter(x_hbm, idx_hbm, out_hbm):
        def body(x_vmem, idx_vmem):
            for k in range(n_slots):
                pltpu.sync_copy(x_vmem, out_hbm.at[idx_vmem.at[k]])

        pltpu.emit_pipeline(
            body,
            grid=(parts * nwin,),
            in_specs=[pl.BlockSpec((win, width), lambda j: (j, 0)),
                      pl.BlockSpec((n_slots, win), lambda j: (j // nwin, j % nwin))],
            out_specs=[],
            core_axis_name=("core", "subcore"),
            dimension_semantics=(pltpu.PARALLEL,),
        )(x_hbm, idx_hbm)

    xs = scatter(x.reshape(parts * t, width), _part_rows(dest, n_rows))
    return xs.reshape(parts, n_rows, width)


def _sc_collect(ys, dest):
    parts, n_rows, width = ys.shape
    n_slots, t = dest.shape
    win = SC_WINDOW
    nwin = t // win
    idx = _part_rows(dest, n_rows).reshape(parts, n_slots, t).swapaxes(0, 1)
    idx = idx.reshape(n_slots * parts, t)

    @functools.partial(pl.kernel,
                       out_type=jax.ShapeDtypeStruct((n_slots * parts * t, width), jnp.int32),
                       mesh=_sc_mesh(), scratch_types=[])
    def gather(y_hbm, idx_hbm, out_hbm):
        def body(idx_vmem, out_vmem):
            pltpu.sync_copy(y_hbm.at[idx_vmem.at[0]], out_vmem)

        pltpu.emit_pipeline(
            body,
            grid=(n_slots * parts * nwin,),
            in_specs=[pl.BlockSpec((1, win), lambda j: (j // nwin, j % nwin))],
            out_specs=[pl.BlockSpec((win, width), lambda j: (j, 0))],
            core_axis_name=("core", "subcore"),
            dimension_semantics=(pltpu.PARALLEL,),
        )(idx_hbm, out_hbm)

    out = gather(ys.reshape(parts * n_rows, width), idx)
    return out.reshape(n_slots, parts, t, width)


def _expert_kernel(blk_e_ref, blk_n_ref, xs_ref, wgu_ref, bgu_ref, wd_ref, bd_ref, out_ref,
                   wgu_b, wd_b):
    i = pl.program_id(0)
    n_valid = blk_n_ref[i]

    @pl.when((i == 0) | (blk_e_ref[i] != blk_e_ref[jnp.maximum(i - 1, 0)]))
    def _():
        wgu_b[...] = wgu_ref[0].astype(BF16)
        wd_b[...] = wd_ref[0].astype(BF16)

    @pl.when(n_valid == 0)
    def _():
        out_ref[...] = jnp.zeros_like(out_ref)

    @pl.when(n_valid > 0)
    def _():
        for r in range(0, ROW_BLOCK, EXPERT_SUB):
            words = [xs_ref[c, r:r + EXPERT_SUB] for c in range(ROW_PARTS)]
            rows = r + lax.broadcasted_iota(jnp.int32, words[0].shape, 0)
            xb = _unpack_rows([jnp.where(rows < n_valid, w, 0) for w in words]).astype(BF16)
            gu = _dot(xb, wgu_b[...]) + bgu_ref[0]
            gate = jnp.minimum(gu[:, :EXPERT_FF], SWIGLU_LIMIT)
            up = jnp.clip(gu[:, EXPERT_FF:], -SWIGLU_LIMIT, SWIGLU_LIMIT)
            hid = (up + 1.0) * (gate * (1.0 / (1.0 + jnp.exp(-SWIGLU_ALPHA * gate))))
            out = _dot(hid.astype(BF16), wd_b[...]) + bd_ref[0]
            for c, w in enumerate(_pack_rows(out)):
                out_ref[c, r:r + EXPERT_SUB] = w


def _experts(blk_e, blk_n, xs, wgu, bgu, wd, bd):
    rb = ROW_BLOCK
    by_expert = lambda a: pl.BlockSpec((1,) + a.shape[1:], lambda i, be, bn: (be[i], 0, 0))
    return pl.pallas_call(
        _expert_kernel,
        grid_spec=pltpu.PrefetchScalarGridSpec(
            num_scalar_prefetch=2, grid=(xs.shape[1] // rb,),
            in_specs=[pl.BlockSpec((ROW_PARTS, rb, LANES), lambda i, be, bn: (0, i, 0)),
                      by_expert(wgu), by_expert(bgu), by_expert(wd), by_expert(bd)],
            out_specs=pl.BlockSpec((ROW_PARTS, rb, LANES), lambda i, be, bn: (0, i, 0)),
            scratch_shapes=[pltpu.VMEM(wgu.shape[1:], BF16), pltpu.VMEM(wd.shape[1:], BF16)]),
        out_shape=jax.ShapeDtypeStruct(xs.shape, xs.dtype),
        compiler_params=pltpu.CompilerParams(
            dimension_semantics=("arbitrary",), vmem_limit_bytes=VMEM_LIMIT_BYTES),
    )(blk_e, blk_n, xs, wgu, bgu, wd, bd)


def _combine_kernel(gate_ref, h1_ref, fw_ref, y0_ref, y1_ref, y2_ref, y3_ref, *rest):
    out_ref = rest[-1]
    parts = _split3(gate_ref[...])
    slot = lax.broadcasted_iota(jnp.int32, (TOP_K, LANES), 0)
    h2 = h1_ref[...]
    for k, y_ref in enumerate((y0_ref, y1_ref, y2_ref, y3_ref)):
        pick = jnp.where(slot == k, 1.0, 0.0).astype(BF16)
        g = sum(lax.dot_general(p, pick, (((0,), (0,)), ((), ())), preferred_element_type=F32)
                for p in parts)
        y = _unpack_rows([y_ref[c] for c in range(ROW_PARTS)])
        h2 = h2 + jnp.concatenate([g] * (D_MODEL // LANES), axis=-1) * y
    out_ref[...] = h2 * lax.rsqrt(jnp.mean(h2 * h2, axis=-1, keepdims=True) + EPS) * fw_ref[...]


def _combine(gates, h1, fw, planes, first_tile, out_tile, t, out_prev):
    d = h1.shape[1]
    tm = COMBINE_TILE
    n = planes.shape[2]
    plane = lambda k: pl.BlockSpec((None, ROW_PARTS, tm, LANES), lambda i: (k, 0, i, 0))
    in_specs = [pl.BlockSpec((TOP_K, tm), lambda i: (0, first_tile + i)),
                pl.BlockSpec((tm, d), lambda i: (first_tile + i, 0)),
                pl.BlockSpec((1, d), lambda i: (0, 0))]
    in_specs += [plane(k) for k in range(TOP_K)]
    args = [gates, h1, fw] + [planes] * TOP_K
    aliases = {}
    if out_prev is not None:
        in_specs.append(pl.BlockSpec(memory_space=pl.ANY))
        aliases = {len(args): 0}
        args.append(out_prev)
    return pl.pallas_call(
        _combine_kernel,
        grid=(n // tm,),
        in_specs=in_specs,
        out_specs=pl.BlockSpec((tm, d), lambda i: (out_tile + i, 0)),
        out_shape=jax.ShapeDtypeStruct((t, d), F32),
        input_output_aliases=aliases,
        compiler_params=pltpu.CompilerParams(
            dimension_semantics=("arbitrary",), vmem_limit_bytes=VMEM_LIMIT_BYTES),
    )(*args)


def _block_plan(counts, n_blocks):
    rb = ROW_BLOCK
    pcounts = (counts + rb - 1) // rb * rb
    pends = jnp.cumsum(pcounts)
    pstarts = pends - pcounts
    block_start = jnp.arange(n_blocks, dtype=jnp.int32) * rb
    blk_e = jnp.minimum(jnp.sum(block_start[:, None] >= pends[None, :], axis=1),
                        N_EXPERTS - 1).astype(jnp.int32)
    mine = blk_e[:, None] == jnp.arange(N_EXPERTS, dtype=jnp.int32)[None, :]
    pick = lambda table: jnp.sum(jnp.where(mine, table[None, :], 0), axis=1)
    blk_n = jnp.clip(pick(counts) - (block_start - pick(pstarts)), 0, rb).astype(jnp.int32)
    return pstarts.astype(jnp.int32), blk_e, blk_n


def kernel(x, norm1_w, w_in, ssd_conv_w, ssd_conv_b, dt_bias, a_log, d_skip, ssd_norm_w,
           sc_conv_w, sc_norm_w, w_out, norm2_w, w_router, b_router, w_gate_up, b_gate_up,
           w_down, b_down, final_norm_w):
    assert norm1_w.shape[0] == 1, "single-layer problem"
    batch, seq, d = x.shape
    t = batch * seq
    x2 = x.reshape(t, d)
    row = lambda a: a.reshape(1, -1)

    s1, s2 = D_SSD + D_XBC, D_SSD + D_XBC + N_SSD_HEADS
    w = w_in[0]
    w_zx, w_dt, w_sc = w[:, :s1].astype(BF16), w[:, s1:s2].astype(BF16), w[:, s2:].astype(BF16)
    z, xbc, ysc, dt, dtt = _inproj(x2, seq, row(norm1_w[0]), w_zx, w_sc, w_dt, w_dt.T,
                                   ssd_conv_w[0], row(ssd_conv_b[0]), sc_conv_w[0],
                                   row(sc_norm_w[0]))
    yssd = _mixer(z, xbc, dt, dtt, batch, seq, row(dt_bias[0]), row(a_log[0]), row(d_skip[0]),
                  row(ssd_norm_w[0]))

    wr = w_router[0].T
    wr_hi = wr.astype(BF16)
    wr_lo = (wr - wr_hi.astype(F32)).astype(BF16)
    unit = t // sum(MOE_SPLIT)
    sizes = [unit * s for s in MOE_SPLIT]
    starts = [sum(sizes[:g]) for g in range(len(sizes))]
    assert sum(sizes) == t and all(n % (COMBINE_CHUNKS * COMBINE_TILE) == 0 and n % OUT_TILE == 0
                                   for n in sizes)
    w_out_b, nw2, br = w_out[0].astype(BF16), row(norm2_w[0]), b_router[0].reshape(-1, 1)
    routed = []
    for t0, tg in zip(starts, sizes):
        n_rows = tg * TOP_K + N_EXPERTS * ROW_BLOCK
        h1, xn2, idx, gates, rank, counts = _outproj(
            yssd, ysc, x2, w_out_b, nw2, wr_hi, wr_lo, br, t0 // OUT_TILE, tg)
        pstarts, blk_e, blk_n = _block_plan(counts[:, 0].astype(jnp.int32),
                                            n_rows // ROW_BLOCK)
        dest = _dest(pstarts, idx, rank)
        xs = _sc_dispatch(xn2, dest, n_rows)
        routed.append((h1, gates, dest, blk_e, blk_n, xs))
    expert_out = [_experts(blk_e, blk_n, xs, w_gate_up[0], b_gate_up[0][:, None, :],
                           w_down[0], b_down[0][:, None, :])
                  for (_, _, _, blk_e, blk_n, xs) in routed]
    fw, out = row(final_norm_w), None
    for t0, tg, (h1, gates, dest, _, _, _), ys in zip(starts, sizes, routed, expert_out):
        tc = tg // COMBINE_CHUNKS
        for c in range(COMBINE_CHUNKS):
            planes = _sc_collect(ys, dest[:, c * tc:(c + 1) * tc])
            out = _combine(gates, h1, fw, planes, c * (tc // COMBINE_TILE),
                           (t0 + c * tc) // COMBINE_TILE, t, out)
    return out.reshape(batch, seq, d)
```

```python
import functools

import jax
import jax.numpy as jnp
from jax import lax
from jax.experimental import pallas as pl
from jax.experimental.pallas import tpu as pltpu
from jax.experimental.pallas import tpu_sc as plsc

D_MODEL = 1024
D_SSD = 1024
SSD_HEAD_DIM = 64
N_SSD_HEADS = 16
N_GROUPS = 2
HEADS_PER_GROUP = 8
D_STATE = 128
SSD_CONV = 4
CHUNK = 128
D_XBC = D_SSD + 2 * N_GROUPS * D_STATE
D_SC = 1024
SC_CONV = 3
N_EXPERTS = 32
TOP_K = 4
EXPERT_FF = 1024
SWIGLU_LIMIT = 7.0
SWIGLU_ALPHA = 1.702
EPS = 1e-5

VMEM_LIMIT_BYTES = 56 * 1024 * 1024
SUBLANES = 8
ROW_BLOCK = 512
EXPERT_SUB = 256
IN_TILE = 512
IN_SUB = 256
MIX_CHUNKS = 8
OUT_TILE = 1024
COMBINE_TILE = 512
MOE_SPLIT = (5, 3)
COMBINE_CHUNKS = 4
LANES = 128
ROW_PARTS = D_MODEL // (2 * LANES)
SC_WINDOW = 128
HIGH_HALF = -65536

F32 = jnp.float32
BF16 = jnp.bfloat16


def _dot(a, b):
    return jnp.dot(a, b, preferred_element_type=F32)


def _split3(v):
    hi = v.astype(BF16)
    r1 = v - hi.astype(F32)
    mid = r1.astype(BF16)
    lo = (r1 - mid.astype(F32)).astype(BF16)
    return hi, mid, lo


def _dot_exact_rhs01(v, m01):
    hi, mid, lo = _split3(v)
    return _dot(hi, m01) + _dot(mid, m01) + _dot(lo, m01)


def _split2(v):
    hi = v.astype(BF16)
    return hi, (v - hi.astype(F32)).astype(BF16)


def _spread(v, m01):
    hi, mid = _split2(v)
    return _dot(hi, m01) + _dot(mid, m01)


def _spread_rows(m01, v):
    hi, mid = _split2(v)
    return _dot(m01, hi) + _dot(m01, mid)


def _silu(v):
    return v * (1.0 / (1.0 + jnp.exp(-v)))


def _softplus(v):
    return jnp.maximum(v, 0.0) + jnp.log1p(jnp.exp(-jnp.abs(v)))


def _pack_rows(v):
    bits = lambda a: lax.bitcast_convert_type(a.astype(BF16).astype(F32), jnp.int32)
    parts = []
    for c in range(ROW_PARTS):
        lo = v[:, (2 * c) * LANES:(2 * c + 1) * LANES]
        hi = v[:, (2 * c + 1) * LANES:(2 * c + 2) * LANES]
        parts.append(bits(hi) | lax.shift_right_logical(bits(lo), 16))
    return parts


def _unpack_rows(parts):
    cols = []
    for w in parts:
        cols.append(lax.bitcast_convert_type(w << 16, F32))
        cols.append(lax.bitcast_convert_type(w & HIGH_HALF, F32))
    return jnp.concatenate(cols, axis=-1)


def _shifted(cur, tail, s):
    if s == 0:
        return cur
    rc = pltpu.roll(cur, s, axis=0)
    rt = pltpu.roll(tail, s, axis=0)
    row = lax.broadcasted_iota(jnp.int32, tail.shape, 0)
    first = jnp.where(row < s, rt, rc[0:SUBLANES])
    return jnp.concatenate([first, rc[SUBLANES:]], axis=0)


def _causal_conv(tail_ref, cur, w_ref, acc):
    rows = cur.shape[0]
    n_taps = w_ref.shape[0]
    tail = tail_ref[...]
    for k in range(n_taps):
        acc = acc + w_ref[k:k + 1, :] * _shifted(cur, tail, n_taps - 1 - k)
    tail_ref[...] = cur[rows - SUBLANES:rows]
    return acc


def _inproj_kernel(tiles_per_seq, x_ref, nw_ref, wzx_ref, wsc_ref, wdt_ref, wdtt_ref,
                   cw_ref, cb_ref, scw_ref, scnw_ref,
                   z_ref, xbc_ref, ysc_ref, dt_ref, dtt_ref, hist_ref, schist_ref):
    tm = x_ref.shape[0]

    @pl.when(pl.program_id(0) % tiles_per_seq == 0)
    def _():
        hist_ref[...] = jnp.zeros_like(hist_ref)
        schist_ref[...] = jnp.zeros_like(schist_ref)

    for r in range(0, tm, IN_SUB):
        rows = pl.ds(r, IN_SUB)
        x = x_ref[rows, :]
        xn = x * lax.rsqrt(jnp.mean(x * x, axis=-1, keepdims=True) + EPS) * nw_ref[...]
        xb = xn.astype(BF16)
        proj = lambda w_ref, o, n, xb=xb: _dot(xb, w_ref[:, o:o + n])
        z_ref[rows, :] = proj(wzx_ref, 0, D_SSD).astype(z_ref.dtype)

        conv = _causal_conv(hist_ref, proj(wzx_ref, D_SSD, D_XBC), cw_ref, cb_ref[...])
        xbc_ref[rows, :] = _silu(conv).astype(xbc_ref.dtype)

        cu = proj(wsc_ref, D_SC, D_SC) * proj(wsc_ref, 2 * D_SC, D_SC)
        v = _causal_conv(schist_ref, cu, scw_ref, jnp.zeros_like(cu))
        gv = proj(wsc_ref, 0, D_SC) * v
        y_sc = gv * lax.rsqrt(jnp.mean(gv * gv, axis=-1, keepdims=True) + EPS) * scnw_ref[...]
        ysc_ref[rows, :] = y_sc.astype(ysc_ref.dtype)

        dt_ref[rows, :] = _dot(xb, wdt_ref[...])
        dtt_ref[:, rows] = lax.dot_general(wdtt_ref[...], xb, (((1,), (1,)), ((), ())),
                                           preferred_element_type=F32)


def _inproj(x2, seq, nw, w_zx, w_sc, w_dt, w_dtt, cw, cb, scw, scnw):
    t = x2.shape[0]
    tm = IN_TILE
    assert seq % tm == 0, "in-proj tiles must not straddle sequences (causal conv carry)"
    row = lambda n: pl.BlockSpec((tm, n), lambda i: (i, 0))
    full = lambda a: pl.BlockSpec(a.shape, lambda i: (0,) * a.ndim)
    params = (nw, w_zx, w_sc, w_dt, w_dtt, cw, cb, scw, scnw)
    return pl.pallas_call(
        functools.partial(_inproj_kernel, seq // tm),
        grid=(t // tm,),
        in_specs=[row(D_MODEL)] + [full(a) for a in params],
        out_specs=[row(D_SSD), row(D_XBC), row(D_SC),
                   row(N_SSD_HEADS), pl.BlockSpec((N_SSD_HEADS, tm), lambda i: (0, i))],
        out_shape=[jax.ShapeDtypeStruct((t, D_SSD), BF16),
                   jax.ShapeDtypeStruct((t, D_XBC), BF16),
                   jax.ShapeDtypeStruct((t, D_SC), BF16),
                   jax.ShapeDtypeStruct((t, N_SSD_HEADS), F32),
                   jax.ShapeDtypeStruct((N_SSD_HEADS, t), F32)],
        scratch_shapes=[pltpu.VMEM((SUBLANES, D_XBC), F32), pltpu.VMEM((SUBLANES, D_SC), F32)],
        compiler_params=pltpu.CompilerParams(
            dimension_semantics=("arbitrary",), vmem_limit_bytes=VMEM_LIMIT_BYTES),
    )(x2, *params)


def _mixer_kernel(z_ref, xbc_ref, dt_ref, dtt_ref,
                  dtb_ref, dtbt_ref, alog_ref, alogt_ref, dskip_ref, nw_ref, out_ref, state_ref):
    q = CHUNK

    @pl.when(pl.program_id(1) == 0)
    def _():
        state_ref[...] = jnp.zeros_like(state_ref)

    ri = lax.broadcasted_iota(jnp.int32, (q, q), 0)
    ci = lax.broadcasted_iota(jnp.int32, (q, q), 1)
    causal = ri >= ci
    tri = jnp.where(causal, 1.0, 0.0).astype(BF16)
    trit = jnp.where(ri <= ci, 1.0, 0.0).astype(BF16)
    hh = lax.broadcasted_iota(jnp.int32, (N_SSD_HEADS, D_SSD), 0)
    hl = lax.broadcasted_iota(jnp.int32, (N_SSD_HEADS, D_SSD), 1) // SSD_HEAD_DIM
    expand = jnp.where(hh == hl, 1.0, 0.0).astype(BF16)
    lane_lo = lax.broadcasted_iota(jnp.int32, (q, 2 * SSD_HEAD_DIM), 1) < SSD_HEAD_DIM
    gw = HEADS_PER_GROUP * SSD_HEAD_DIM
    n_bc = N_GROUPS * D_STATE
    a_neg = -jnp.exp(alog_ref[...])
    a_negt = -jnp.exp(alogt_ref[...])
    d_x = _dot_exact_rhs01(dskip_ref[...], expand)

    for s in range(MIX_CHUNKS):
        rows = pl.ds(s * q, q)
        xs = xbc_ref[rows, :D_SSD].astype(F32)
        b_all = xbc_ref[rows, D_SSD:D_SSD + n_bc]
        c_all = xbc_ref[rows, D_SSD + n_bc:]

        dt = _softplus(dt_ref[rows, :] + dtb_ref[...])
        dtt = _softplus(dtt_ref[:, rows] + dtbt_ref[...])
        acum = _spread_rows(tri, dt * a_neg)
        acumt = _spread(dtt * a_negt, trit)
        a_last = acum[q - 1:q, :]

        xdt = xs * _spread(dt, expand)
        xdt_b = xdt.astype(BF16)
        dec_in = _spread(jnp.exp(acum), expand)
        dec_out = _spread(jnp.exp(a_last - acum), expand)
        dec_all = _dot_exact_rhs01(jnp.exp(a_last), expand)
        xw_b = (xdt * dec_out).astype(BF16)

        y_parts = []
        for g in range(N_GROUPS):
            b_g = b_all[:, g * D_STATE:(g + 1) * D_STATE]
            c_g = c_all[:, g * D_STATE:(g + 1) * D_STATE]
            cb = lax.dot_general(c_g, b_g, (((1,), (1,)), ((), ())), preferred_element_type=F32)
            st = state_ref[:, g * gw:(g + 1) * gw]
            y_inter = _dot(c_g, st.astype(BF16)) * dec_in[:, g * gw:(g + 1) * gw]
            new_st = lax.dot_general(b_g, xw_b[:, g * gw:(g + 1) * gw], (((0,), (0,)), ((), ())),
                                     preferred_element_type=F32)
            state_ref[:, g * gw:(g + 1) * gw] = st * dec_all[:, g * gw:(g + 1) * gw] + new_st
            pair_parts = []
            for pr in range(HEADS_PER_GROUP // 2):
                h0 = g * HEADS_PER_GROUP + 2 * pr
                lo = h0 * SSD_HEAD_DIM
                rhs = xdt_b[:, lo:lo + 2 * SSD_HEAD_DIM]
                ys = []
                for h in (h0, h0 + 1):
                    seg = acum[:, h:h + 1] - acumt[h:h + 1, :]
                    m = (cb * jnp.exp(jnp.where(causal, seg, -jnp.inf))).astype(BF16)
                    ys.append(_dot(m, rhs))
                pair_parts.append(jnp.where(lane_lo, ys[0], ys[1]))
            y_parts.append(jnp.concatenate(pair_parts, axis=-1) + y_inter)
        y = jnp.concatenate(y_parts, axis=-1) + d_x * xs

        gated = y * _silu(z_ref[rows, :].astype(F32))
        outs = []
        for g in range(N_GROUPS):
            gg = gated[:, g * gw:(g + 1) * gw]
            outs.append(gg * lax.rsqrt(jnp.mean(gg * gg, axis=-1, keepdims=True) + EPS))
        out_ref[rows, :] = (jnp.concatenate(outs, axis=-1) * nw_ref[...]).astype(out_ref.dtype)


def _mixer(z, xbc, dt, dtt, batch, seq, dtb, alog, dskip, nw):
    rows = CHUNK * MIX_CHUNKS
    assert seq % rows == 0
    ns = seq // rows
    t = batch * seq
    row = lambda n: pl.BlockSpec((rows, n), lambda b, c: (b * ns + c, 0))
    full = lambda a: pl.BlockSpec(a.shape, lambda b, c: (0,) * a.ndim)
    dtbt, alogt = dtb.reshape(N_SSD_HEADS, 1), alog.reshape(N_SSD_HEADS, 1)
    params = (dtb, dtbt, alog, alogt, dskip, nw)
    return pl.pallas_call(
        _mixer_kernel,
        grid=(batch, ns),
        in_specs=[row(D_SSD), row(D_XBC), row(N_SSD_HEADS),
                  pl.BlockSpec((N_SSD_HEADS, rows), lambda b, c: (0, b * ns + c))]
                 + [full(a) for a in params],
        out_specs=row(D_SSD),
        out_shape=jax.ShapeDtypeStruct((t, D_SSD), BF16),
        scratch_shapes=[pltpu.VMEM((D_STATE, D_SSD), F32)],
        compiler_params=pltpu.CompilerParams(
            dimension_semantics=("arbitrary", "arbitrary"), vmem_limit_bytes=VMEM_LIMIT_BYTES),
    )(z, xbc, dt, dtt, *params)


def _outproj_kernel(yssd_ref, ysc_ref, x_ref, wout_ref, nw_ref, wrh_ref, wrl_ref, br_ref,
                    h1_ref, xn_ref, idx_ref, gate_ref, rank_ref, cnt_ref, run_ref, earlier_ref):
    tm = x_ref.shape[0]

    @pl.when(pl.program_id(0) == 0)
    def _():
        run_ref[...] = jnp.zeros_like(run_ref)
        ri = lax.broadcasted_iota(jnp.int32, (tm, tm), 0)
        ci = lax.broadcasted_iota(jnp.int32, (tm, tm), 1)
        earlier_ref[...] = jnp.where(ri < ci, 1.0, 0.0).astype(BF16)

    h1 = (x_ref[...] + _dot(yssd_ref[...], wout_ref[:D_SSD, :])
          + _dot(ysc_ref[...], wout_ref[D_SSD:, :]))
    h1_ref[...] = h1
    xn = h1 * lax.rsqrt(jnp.mean(h1 * h1, axis=-1, keepdims=True) + EPS) * nw_ref[...]
    for c, words in enumerate(_pack_rows(xn)):
        xn_ref[c] = words

    xh = xn.astype(BF16)
    xl = (xn - xh.astype(F32)).astype(BF16)
    nt = lambda w, a: lax.dot_general(w, a, (((1,), (1,)), ((), ())), preferred_element_type=F32)
    logits = (nt(wrh_ref[...], xh) + nt(wrl_ref[...], xh) + nt(wrh_ref[...], xl)
              + br_ref[...])

    expert = lax.broadcasted_iota(jnp.int32, (N_EXPERTS, tm), 0).astype(F32)
    work = logits
    vals, firsts, sels = [], [], []
    for _ in range(TOP_K):
        m = jnp.max(work, axis=0, keepdims=True)
        first = jnp.min(jnp.where(work == m, expert, float(N_EXPERTS)), axis=0, keepdims=True)
        sel = expert == first
        vals.append(m)
        firsts.append(first)
        sels.append(sel)
        work = jnp.where(sel, -jnp.inf, work)
    exps = [jnp.exp(v - vals[0]) for v in vals]
    denom = exps[0] + exps[1] + exps[2] + exps[3]

    onehot = jnp.zeros((N_EXPERTS, tm), F32)
    for sel in sels:
        onehot = onehot + jnp.where(sel, 1.0, 0.0)
    before = run_ref[...] + _dot(onehot.astype(BF16), earlier_ref[...])
    run_ref[...] = run_ref[...] + jnp.sum(onehot, axis=1, keepdims=True)
    cnt_ref[...] = run_ref[...]

    slot = lax.broadcasted_iota(jnp.int32, (TOP_K, tm), 0)
    idx = jnp.zeros((TOP_K, tm), F32)
    gate = jnp.zeros((TOP_K, tm), F32)
    rank = jnp.zeros((TOP_K, tm), F32)
    for k in range(TOP_K):
        r_k = jnp.sum(jnp.where(sels[k], before, 0.0), axis=0, keepdims=True)
        idx = jnp.where(slot == k, firsts[k], idx)
        gate = jnp.where(slot == k, exps[k] / denom, gate)
        rank = jnp.where(slot == k, r_k, rank)
    idx_ref[...] = idx.astype(jnp.int32)
    gate_ref[...] = gate
    rank_ref[...] = rank.astype(jnp.int32)


def _outproj(yssd, ysc, x2, w_out, nw, wr_hi, wr_lo, br, first_tile, t):
    tm = OUT_TILE
    src = lambda n: pl.BlockSpec((tm, n), lambda i: (first_tile + i, 0))
    row = lambda n: pl.BlockSpec((tm, n), lambda i: (i, 0))
    slots = pl.BlockSpec((TOP_K, tm), lambda i: (0, i))
    full = lambda a: pl.BlockSpec(a.shape, lambda i: (0,) * a.ndim)
    return pl.pallas_call(
        _outproj_kernel,
        grid=(t // tm,),
        in_specs=[src(D_SSD), src(D_SC), src(D_MODEL), full(w_out), full(nw), full(wr_hi),
                  full(wr_lo), full(br)],
        out_specs=[row(D_MODEL), pl.BlockSpec((ROW_PARTS, tm, LANES), lambda i: (0, i, 0)),
                   slots, slots, slots,
                   pl.BlockSpec((N_EXPERTS, 1), lambda i: (0, 0))],
        out_shape=[jax.ShapeDtypeStruct((t, D_MODEL), F32),
                   jax.ShapeDtypeStruct((ROW_PARTS, t, LANES), jnp.int32),
                   jax.ShapeDtypeStruct((TOP_K, t), jnp.int32),
                   jax.ShapeDtypeStruct((TOP_K, t), F32),
                   jax.ShapeDtypeStruct((TOP_K, t), jnp.int32),
                   jax.ShapeDtypeStruct((N_EXPERTS, 1), F32)],
        scratch_shapes=[pltpu.VMEM((N_EXPERTS, 1), F32), pltpu.VMEM((tm, tm), BF16)],
        compiler_params=pltpu.CompilerParams(
            dimension_semantics=("arbitrary",), vmem_limit_bytes=VMEM_LIMIT_BYTES),
    )(yssd, ysc, x2, w_out, nw, wr_hi, wr_lo, br)


def _dest_kernel(pstart_ref, idx_ref, rank_ref, dest_ref):
    idx = idx_ref[...]
    base = jnp.zeros_like(idx)
    for e in range(N_EXPERTS):
        base = jnp.where(idx == e, pstart_ref[e], base)
    dest_ref[...] = base + rank_ref[...]


def _dest(pstarts, idx, rank):
    full = lambda a: pl.BlockSpec(a.shape, lambda i, ps: (0,) * a.ndim)
    return pl.pallas_call(
        _dest_kernel,
        grid_spec=pltpu.PrefetchScalarGridSpec(
            num_scalar_prefetch=1, grid=(1,),
            in_specs=[full(idx), full(rank)], out_specs=full(idx)),
        out_shape=jax.ShapeDtypeStruct(idx.shape, jnp.int32),
    )(pstarts, idx, rank)


def _sc_mesh():
    return plsc.VectorSubcoreMesh(core_axis_name="core", subcore_axis_name="subcore")


def _part_rows(dest, n_rows):
    off = (jnp.arange(ROW_PARTS, dtype=jnp.int32) * n_rows)[:, None, None]
    return (dest[None] + off).reshape(ROW_PARTS * dest.shape[0], dest.shape[1])


def _sc_dispatch(x, dest, n_rows):
    parts, t, width = x.shape
    n_slots = dest.shape[0]
    win = SC_WINDOW
    nwin = t // win

    @functools.partial(pl.kernel,
                       out_type=jax.ShapeDtypeStruct((parts * n_rows, width), jnp.int32),
                       mesh=_sc_mesh(), scratch_types=[])
    def scatter(x_hbm, idx_hbm, out_hbm):
        def body(x_vmem, idx_vmem):
            for k in range(n_slots):
                pltpu.sync_copy(x_vmem, out_hbm.at[idx_vmem.at[k]])

        pltpu.emit_pipeline(
            body,
            grid=(parts * nwin,),
            in_specs=[pl.BlockSpec((win, width), lambda j: (j, 0)),
                      pl.BlockSpec((n_slots, win), lambda j: (j // nwin, j % nwin))],
            out_specs=[],
            core_axis_name=("core", "subcore"),
            dimension_semantics=(pltpu.PARALLEL,),
        )(x_hbm, idx_hbm)

    xs = scatter(x.reshape(parts * t, width), _part_rows(dest, n_rows))
    return xs.reshape(parts, n_rows, width)


def _sc_collect(ys, dest):
    parts, n_rows, width = ys.shape
    n_slots, t = dest.shape
    win = SC_WINDOW
    nwin = t // win
    idx = _part_rows(dest, n_rows).reshape(parts, n_slots, t).swapaxes(0, 1)
    idx = idx.reshape(n_slots * parts, t)

    @functools.partial(pl.kernel,
                       out_type=jax.ShapeDtypeStruct((n_slots * parts * t, width), jnp.int32),
                       mesh=_sc_mesh(), scratch_types=[])
    def gather(y_hbm, idx_hbm, out_hbm):
        def body(idx_vmem, out_vmem):
            pltpu.sync_copy(y_hbm.at[idx_vmem.at[0]], out_vmem)

        pltpu.emit_pipeline(
            body,
            grid=(n_slots * parts * nwin,),
            in_specs=[pl.BlockSpec((1, win), lambda j: (j // nwin, j % nwin))],
            out_specs=[pl.BlockSpec((win, width), lambda j: (j, 0))],
            core_axis_name=("core", "subcore"),
            dimension_semantics=(pltpu.PARALLEL,),
        )(idx_hbm, out_hbm)

    out = gather(ys.reshape(parts * n_rows, width), idx)
    return out.reshape(n_slots, parts, t, width)


def _expert_kernel(blk_e_ref, blk_n_ref, xs_ref, wgu_ref, bgu_ref, wd_ref, bd_ref, out_ref,
                   wgu_b, wd_b):
    i = pl.program_id(0)
    n_valid = blk_n_ref[i]

    @pl.when((i == 0) | (blk_e_ref[i] != blk_e_ref[jnp.maximum(i - 1, 0)]))
    def _():
        wgu_b[...] = wgu_ref[0].astype(BF16)
        wd_b[...] = wd_ref[0].astype(BF16)

    @pl.when(n_valid == 0)
    def _():
        out_ref[...] = jnp.zeros_like(out_ref)

    @pl.when(n_valid > 0)
    def _():
        for r in range(0, ROW_BLOCK, EXPERT_SUB):
            words = [xs_ref[c, r:r + EXPERT_SUB] for c in range(ROW_PARTS)]
            rows = r + lax.broadcasted_iota(jnp.int32, words[0].shape, 0)
            xb = _unpack_rows([jnp.where(rows < n_valid, w, 0) for w in words]).astype(BF16)
            gu = _dot(xb, wgu_b[...]) + bgu_ref[0]
            gate = jnp.minimum(gu[:, :EXPERT_FF], SWIGLU_LIMIT)
            up = jnp.clip(gu[:, EXPERT_FF:], -SWIGLU_LIMIT, SWIGLU_LIMIT)
            hid = (up + 1.0) * (gate * (1.0 / (1.0 + jnp.exp(-SWIGLU_ALPHA * gate))))
            out = _dot(hid.astype(BF16), wd_b[...]) + bd_ref[0]
            for c, w in enumerate(_pack_rows(out)):
                out_ref[c, r:r + EXPERT_SUB] = w


def _experts(blk_e, blk_n, xs, wgu, bgu, wd, bd):
    rb = ROW_BLOCK
    by_expert = lambda a: pl.BlockSpec((1,) + a.shape[1:], lambda i, be, bn: (be[i], 0, 0))
    return pl.pallas_call(
        _expert_kernel,
        grid_spec=pltpu.PrefetchScalarGridSpec(
            num_scalar_prefetch=2, grid=(xs.shape[1] // rb,),
            in_specs=[pl.BlockSpec((ROW_PARTS, rb, LANES), lambda i, be, bn: (0, i, 0)),
                      by_expert(wgu), by_expert(bgu), by_expert(wd), by_expert(bd)],
            out_specs=pl.BlockSpec((ROW_PARTS, rb, LANES), lambda i, be, bn: (0, i, 0)),
            scratch_shapes=[pltpu.VMEM(wgu.shape[1:], BF16), pltpu.VMEM(wd.shape[1:], BF16)]),
        out_shape=jax.ShapeDtypeStruct(xs.shape, xs.dtype),
        compiler_params=pltpu.CompilerParams(
            dimension_semantics=("arbitrary",), vmem_limit_bytes=VMEM_LIMIT_BYTES),
    )(blk_e, blk_n, xs, wgu, bgu, wd, bd)


def _combine_kernel(gate_ref, h1_ref, fw_ref, y0_ref, y1_ref, y2_ref, y3_ref, *rest):
    out_ref = rest[-1]
    parts = _split3(gate_ref[...])
    slot = lax.broadcasted_iota(jnp.int32, (TOP_K, LANES), 0)
    h2 = h1_ref[...]
    for k, y_ref in enumerate((y0_ref, y1_ref, y2_ref, y3_ref)):
        pick = jnp.where(slot == k, 1.0, 0.0).astype(BF16)
        g = sum(lax.dot_general(p, pick, (((0,), (0,)), ((), ())), preferred_element_type=F32)
                for p in parts)
        y = _unpack_rows([y_ref[c] for c in range(ROW_PARTS)])
        h2 = h2 + jnp.concatenate([g] * (D_MODEL // LANES), axis=-1) * y
    out_ref[...] = h2 * lax.rsqrt(jnp.mean(h2 * h2, axis=-1, keepdims=True) + EPS) * fw_ref[...]


def _combine(gates, h1, fw, planes, first_tile, out_tile, t, out_prev):
    d = h1.shape[1]
    tm = COMBINE_TILE
    n = planes.shape[2]
    plane = lambda k: pl.BlockSpec((None, ROW_PARTS, tm, LANES), lambda i: (k, 0, i, 0))
    in_specs = [pl.BlockSpec((TOP_K, tm), lambda i: (0, first_tile + i)),
                pl.BlockSpec((tm, d), lambda i: (first_tile + i, 0)),
                pl.BlockSpec((1, d), lambda i: (0, 0))]
    in_specs += [plane(k) for k in range(TOP_K)]
    args = [gates, h1, fw] + [planes] * TOP_K
    aliases = {}
    if out_prev is not None:
        in_specs.append(pl.BlockSpec(memory_space=pl.ANY))
        aliases = {len(args): 0}
        args.append(out_prev)
    return pl.pallas_call(
        _combine_kernel,
        grid=(n // tm,),
        in_specs=in_specs,
        out_specs=pl.BlockSpec((tm, d), lambda i: (out_tile + i, 0)),
        out_shape=jax.ShapeDtypeStruct((t, d), F32),
        input_output_aliases=aliases,
        compiler_params=pltpu.CompilerParams(
            dimension_semantics=("arbitrary",), vmem_limit_bytes=VMEM_LIMIT_BYTES),
    )(*args)


def _block_plan(counts, n_blocks):
    rb = ROW_BLOCK
    pcounts = (counts + rb - 1) // rb * rb
    pends = jnp.cumsum(pcounts)
    pstarts = pends - pcounts
    block_start = jnp.arange(n_blocks, dtype=jnp.int32) * rb
    blk_e = jnp.minimum(jnp.sum(block_start[:, None] >= pends[None, :], axis=1),
                        N_EXPERTS - 1).astype(jnp.int32)
    mine = blk_e[:, None] == jnp.arange(N_EXPERTS, dtype=jnp.int32)[None, :]
    pick = lambda table: jnp.sum(jnp.where(mine, table[None, :], 0), axis=1)
    blk_n = jnp.clip(pick(counts) - (block_start - pick(pstarts)), 0, rb).astype(jnp.int32)
    return pstarts.astype(jnp.int32), blk_e, blk_n


def kernel(x, norm1_w, w_in, ssd_conv_w, ssd_conv_b, dt_bias, a_log, d_skip, ssd_norm_w,
           sc_conv_w, sc_norm_w, w_out, norm2_w, w_router, b_router, w_gate_up, b_gate_up,
           w_down, b_down, final_norm_w):
    assert norm1_w.shape[0] == 1, "single-layer problem"
    batch, seq, d = x.shape
    t = batch * seq
    x2 = x.reshape(t, d)
    row = lambda a: a.reshape(1, -1)

    s1, s2 = D_SSD + D_XBC, D_SSD + D_XBC + N_SSD_HEADS
    w = w_in[0]
    w_zx, w_dt, w_sc = w[:, :s1].astype(BF16), w[:, s1:s2].astype(BF16), w[:, s2:].astype(BF16)
    z, xbc, ysc, dt, dtt = _inproj(x2, seq, row(norm1_w[0]), w_zx, w_sc, w_dt, w_dt.T,
                                   ssd_conv_w[0], row(ssd_conv_b[0]), sc_conv_w[0],
                                   row(sc_norm_w[0]))
    yssd = _mixer(z, xbc, dt, dtt, batch, seq, row(dt_bias[0]), row(a_log[0]), row(d_skip[0]),
                  row(ssd_norm_w[0]))

    wr = w_router[0].T
    wr_hi = wr.astype(BF16)
    wr_lo = (wr - wr_hi.astype(F32)).astype(BF16)
    unit = t // sum(MOE_SPLIT)
    sizes = [unit * s for s in MOE_SPLIT]
    starts = [sum(sizes[:g]) for g in range(len(sizes))]
    assert sum(sizes) == t and all(n % (COMBINE_CHUNKS * COMBINE_TILE) == 0 and n % OUT_TILE == 0
                                   for n in sizes)
    w_out_b, nw2, br = w_out[0].astype(BF16), row(norm2_w[0]), b_router[0].reshape(-1, 1)
    routed = []
    for t0, tg in zip(starts, sizes):
        n_rows = tg * TOP_K + N_EXPERTS * ROW_BLOCK
        h1, xn2, idx, gates, rank, counts = _outproj(
            yssd, ysc, x2, w_out_b, nw2, wr_hi, wr_lo, br, t0 // OUT_TILE, tg)
        pstarts, blk_e, blk_n = _block_plan(counts[:, 0].astype(jnp.int32),
                                            n_rows // ROW_BLOCK)
        dest = _dest(pstarts, idx, rank)
        xs = _sc_dispatch(xn2, dest, n_rows)
        routed.append((h1, gates, dest, blk_e, blk_n, xs))
    expert_out = [_experts(blk_e, blk_n, xs, w_gate_up[0], b_gate_up[0][:, None, :],
                           w_down[0], b_down[0][:, None, :])
                  for (_, _, _, blk_e, blk_n, xs) in routed]
    fw, out = row(final_norm_w), None
    for t0, tg, (h1, gates, dest, _, _, _), ys in zip(starts, sizes, routed, expert_out):
        tc = tg // COMBINE_CHUNKS
        for c in range(COMBINE_CHUNKS):
            planes = _sc_collect(ys, dest[:, c * tc:(c + 1) * tc])
            out = _combine(gates, h1, fw, planes, c * (tc // COMBINE_TILE),
                           (t0 + c * tc) // COMBINE_TILE, t, out)
    return out.reshape(batch, seq, d)
```

```python
import functools

import jax
import jax.numpy as jnp
from jax import lax
from jax.experimental import pallas as pl
from jax.experimental.pallas import tpu as pltpu
from jax.experimental.pallas import tpu_sc as plsc

D_MODEL = 1024
D_SSD = 1024
SSD_HEAD_DIM = 64
N_SSD_HEADS = 16
N_GROUPS = 2
HEADS_PER_GROUP = 8
D_STATE = 128
SSD_CONV = 4
CHUNK = 128
D_XBC = D_SSD + 2 * N_GROUPS * D_STATE
D_SC = 1024
SC_CONV = 3
N_EXPERTS = 32
TOP_K = 4
EXPERT_FF = 1024
SWIGLU_LIMIT = 7.0
SWIGLU_ALPHA = 1.702
EPS = 1e-5

VMEM_LIMIT_BYTES = 56 * 1024 * 1024
SUBLANES = 8
ROW_BLOCK = 512
EXPERT_SUB = 256
IN_TILE = 512
IN_SUB = 256
MIX_CHUNKS = 8
OUT_TILE = 1024
COMBINE_TILE = 512
MOE_SPLIT = (5, 3)
COMBINE_CHUNKS = 4
LANES = 128
ROW_PARTS = D_MODEL // (2 * LANES)
SC_WINDOW = 128
HIGH_HALF = -65536

F32 = jnp.float32
BF16 = jnp.bfloat16


def _dot(a, b):
    return jnp.dot(a, b, preferred_element_type=F32)


def _split3(v):
    hi = v.astype(BF16)
    r1 = v - hi.astype(F32)
    mid = r1.astype(BF16)
    lo = (r1 - mid.astype(F32)).astype(BF16)
    return hi, mid, lo


def _dot_exact_rhs01(v, m01):
    hi, mid, lo = _split3(v)
    return _dot(hi, m01) + _dot(mid, m01) + _dot(lo, m01)


def _split2(v):
    hi = v.astype(BF16)
    return hi, (v - hi.astype(F32)).astype(BF16)


def _spread(v, m01):
    hi, mid = _split2(v)
    return _dot(hi, m01) + _dot(mid, m01)


def _spread_rows(m01, v):
    hi, mid = _split2(v)
    return _dot(m01, hi) + _dot(m01, mid)


def _silu(v):
    return v * (1.0 / (1.0 + jnp.exp(-v)))


def _softplus(v):
    return jnp.maximum(v, 0.0) + jnp.log1p(jnp.exp(-jnp.abs(v)))


def _pack_rows(v):
    bits = lambda a: lax.bitcast_convert_type(a.astype(BF16).astype(F32), jnp.int32)
    parts = []
    for c in range(ROW_PARTS):
        lo = v[:, (2 * c) * LANES:(2 * c + 1) * LANES]
        hi = v[:, (2 * c + 1) * LANES:(2 * c + 2) * LANES]
        parts.append(bits(hi) | lax.shift_right_logical(bits(lo), 16))
    return parts


def _unpack_rows(parts):
    cols = []
    for w in parts:
        cols.append(lax.bitcast_convert_type(w << 16, F32))
        cols.append(lax.bitcast_convert_type(w & HIGH_HALF, F32))
    return jnp.concatenate(cols, axis=-1)


def _shifted(cur, tail, s):
    if s == 0:
        return cur
    rc = pltpu.roll(cur, s, axis=0)
    rt = pltpu.roll(tail, s, axis=0)
    row = lax.broadcasted_iota(jnp.int32, tail.shape, 0)
    first = jnp.where(row < s, rt, rc[0:SUBLANES])
    return jnp.concatenate([first, rc[SUBLANES:]], axis=0)


def _causal_conv(tail_ref, cur, w_ref, acc):
    rows = cur.shape[0]
    n_taps = w_ref.shape[0]
    tail = tail_ref[...]
    for k in range(n_taps):
        acc = acc + w_ref[k:k + 1, :] * _shifted(cur, tail, n_taps - 1 - k)
    tail_ref[...] = cur[rows - SUBLANES:rows]
    return acc


def _inproj_kernel(tiles_per_seq, x_ref, nw_ref, wzx_ref, wsc_ref, wdt_ref, wdtt_ref,
                   cw_ref, cb_ref, scw_ref, scnw_ref,
                   z_ref, xbc_ref, ysc_ref, dt_ref, dtt_ref, hist_ref, schist_ref):
    tm = x_ref.shape[0]

    @pl.when(pl.program_id(0) % tiles_per_seq == 0)
    def _():
        hist_ref[...] = jnp.zeros_like(hist_ref)
        schist_ref[...] = jnp.zeros_like(schist_ref)

    for r in range(0, tm, IN_SUB):
        rows = pl.ds(r, IN_SUB)
        x = x_ref[rows, :]
        xn = x * lax.rsqrt(jnp.mean(x * x, axis=-1, keepdims=True) + EPS) * nw_ref[...]
        xb = xn.astype(BF16)
        proj = lambda w_ref, o, n, xb=xb: _dot(xb, w_ref[:, o:o + n])
        z_ref[rows, :] = proj(wzx_ref, 0, D_SSD).astype(z_ref.dtype)

        conv = _causal_conv(hist_ref, proj(wzx_ref, D_SSD, D_XBC), cw_ref, cb_ref[...])
        xbc_ref[rows, :] = _silu(conv).astype(xbc_ref.dtype)

        cu = proj(wsc_ref, D_SC, D_SC) * proj(wsc_ref, 2 * D_SC, D_SC)
        v = _causal_conv(schist_ref, cu, scw_ref, jnp.zeros_like(cu))
        gv = proj(wsc_ref, 0, D_SC) * v
        y_sc = gv * lax.rsqrt(jnp.mean(gv * gv, axis=-1, keepdims=True) + EPS) * scnw_ref[...]
        ysc_ref[rows, :] = y_sc.astype(ysc_ref.dtype)

        dt_ref[rows, :] = _dot(xb, wdt_ref[...])
        dtt_ref[:, rows] = lax.dot_general(wdtt_ref[...], xb, (((1,), (1,)), ((), ())),
                                           preferred_element_type=F32)


def _inproj(x2, seq, nw, w_zx, w_sc, w_dt, w_dtt, cw, cb, scw, scnw):
    t = x2.shape[0]
    tm = IN_TILE
    assert seq % tm == 0, "in-proj tiles must not straddle sequences (causal conv carry)"
    row = lambda n: pl.BlockSpec((tm, n), lambda i: (i, 0))
    full = lambda a: pl.BlockSpec(a.shape, lambda i: (0,) * a.ndim)
    params = (nw, w_zx, w_sc, w_dt, w_dtt, cw, cb, scw, scnw)
    return pl.pallas_call(
        functools.partial(_inproj_kernel, seq // tm),
        grid=(t // tm,),
        in_specs=[row(D_MODEL)] + [full(a) for a in params],
        out_specs=[row(D_SSD), row(D_XBC), row(D_SC),
                   row(N_SSD_HEADS), pl.BlockSpec((N_SSD_HEADS, tm), lambda i: (0, i))],
        out_shape=[jax.ShapeDtypeStruct((t, D_SSD), BF16),
                   jax.ShapeDtypeStruct((t, D_XBC), BF16),
                   jax.ShapeDtypeStruct((t, D_SC), BF16),
                   jax.ShapeDtypeStruct((t, N_SSD_HEADS), F32),
                   jax.ShapeDtypeStruct((N_SSD_HEADS, t), F32)],
        scratch_shapes=[pltpu.VMEM((SUBLANES, D_XBC), F32), pltpu.VMEM((SUBLANES, D_SC), F32)],
        compiler_params=pltpu.CompilerParams(
            dimension_semantics=("arbitrary",), vmem_limit_bytes=VMEM_LIMIT_BYTES),
    )(x2, *params)


def _mixer_kernel(z_ref, xbc_ref, dt_ref, dtt_ref,
                  dtb_ref, dtbt_ref, alog_ref, alogt_ref, dskip_ref, nw_ref, out_ref, state_ref):
    q = CHUNK

    @pl.when(pl.program_id(1) == 0)
    def _():
        state_ref[...] = jnp.zeros_like(state_ref)

    ri = lax.broadcasted_iota(jnp.int32, (q, q), 0)
    ci = lax.broadcasted_iota(jnp.int32, (q, q), 1)
    causal = ri >= ci
    tri = jnp.where(causal, 1.0, 0.0).astype(BF16)
    trit = jnp.where(ri <= ci, 1.0, 0.0).astype(BF16)
    hh = lax.broadcasted_iota(jnp.int32, (N_SSD_HEADS, D_SSD), 0)
    hl = lax.broadcasted_iota(jnp.int32, (N_SSD_HEADS, D_SSD), 1) // SSD_HEAD_DIM
    expand = jnp.where(hh == hl, 1.0, 0.0).astype(BF16)
    lane_lo = lax.broadcasted_iota(jnp.int32, (q, 2 * SSD_HEAD_DIM), 1) < SSD_HEAD_DIM
    gw = HEADS_PER_GROUP * SSD_HEAD_DIM
    n_bc = N_GROUPS * D_STATE
    a_neg = -jnp.exp(alog_ref[...])
    a_negt = -jnp.exp(alogt_ref[...])
    d_x = _dot_exact_rhs01(dskip_ref[...], expand)

    for s in range(MIX_CHUNKS):
        rows = pl.ds(s * q, q)
        xs = xbc_ref[rows, :D_SSD].astype(F32)
        b_all = xbc_ref[rows, D_SSD:D_SSD + n_bc]
        c_all = xbc_ref[rows, D_SSD + n_bc:]

        dt = _softplus(dt_ref[rows, :] + dtb_ref[...])
        dtt = _softplus(dtt_ref[:, rows] + dtbt_ref[...])
        acum = _spread_rows(tri, dt * a_neg)
        acumt = _spread(dtt * a_negt, trit)
        a_last = acum[q - 1:q, :]

        xdt = xs * _spread(dt, expand)
        xdt_b = xdt.astype(BF16)
        dec_in = _spread(jnp.exp(acum), expand)
        dec_out = _spread(jnp.exp(a_last - acum), expand)
        dec_all = _dot_exact_rhs01(jnp.exp(a_last), expand)
        xw_b = (xdt * dec_out).astype(BF16)

        y_parts = []
        for g in range(N_GROUPS):
            b_g = b_all[:, g * D_STATE:(g + 1) * D_STATE]
            c_g = c_all[:, g * D_STATE:(g + 1) * D_STATE]
            cb = lax.dot_general(c_g, b_g, (((1,), (1,)), ((), ())), preferred_element_type=F32)
            st = state_ref[:, g * gw:(g + 1) * gw]
            y_inter = _dot(c_g, st.astype(BF16)) * dec_in[:, g * gw:(g + 1) * gw]
            new_st = lax.dot_general(b_g, xw_b[:, g * gw:(g + 1) * gw], (((0,), (0,)), ((), ())),
                                     preferred_element_type=F32)
            state_ref[:, g * gw:(g + 1) * gw] = st * dec_all[:, g * gw:(g + 1) * gw] + new_st
            pair_parts = []
            for pr in range(HEADS_PER_GROUP // 2):
                h0 = g * HEADS_PER_GROUP + 2 * pr
                lo = h0 * SSD_HEAD_DIM
                rhs = xdt_b[:, lo:lo + 2 * SSD_HEAD_DIM]
                ys = []
                for h in (h0, h0 + 1):
                    seg = acum[:, h:h + 1] - acumt[h:h + 1, :]
                    m = (cb * jnp.exp(jnp.where(causal, seg, -jnp.inf))).astype(BF16)
                    ys.append(_dot(m, rhs))
                pair_parts.append(jnp.where(lane_lo, ys[0], ys[1]))
            y_parts.append(jnp.concatenate(pair_parts, axis=-1) + y_inter)
        y = jnp.concatenate(y_parts, axis=-1) + d_x * xs

        gated = y * _silu(z_ref[rows, :].astype(F32))
        outs = []
        for g in range(N_GROUPS):
            gg = gated[:, g * gw:(g + 1) * gw]
            outs.append(gg * lax.rsqrt(jnp.mean(gg * gg, axis=-1, keepdims=True) + EPS))
        out_ref[rows, :] = (jnp.concatenate(outs, axis=-1) * nw_ref[...]).astype(out_ref.dtype)


def _mixer(z, xbc, dt, dtt, batch, seq, dtb, alog, dskip, nw):
    rows = CHUNK * MIX_CHUNKS
    assert seq % rows == 0
    ns = seq // rows
    t = batch * seq
    row = lambda n: pl.BlockSpec((rows, n), lambda b, c: (b * ns + c, 0))
    full = lambda a: pl.BlockSpec(a.shape, lambda b, c: (0,) * a.ndim)
    dtbt, alogt = dtb.reshape(N_SSD_HEADS, 1), alog.reshape(N_SSD_HEADS, 1)
    params = (dtb, dtbt, alog, alogt, dskip, nw)
    return pl.pallas_call(
        _mixer_kernel,
        grid=(batch, ns),
        in_specs=[row(D_SSD), row(D_XBC), row(N_SSD_HEADS),
                  pl.BlockSpec((N_SSD_HEADS, rows), lambda b, c: (0, b * ns + c))]
                 + [full(a) for a in params],
        out_specs=row(D_SSD),
        out_shape=jax.ShapeDtypeStruct((t, D_SSD), BF16),
        scratch_shapes=[pltpu.VMEM((D_STATE, D_SSD), F32)],
        compiler_params=pltpu.CompilerParams(
            dimension_semantics=("arbitrary", "arbitrary"), vmem_limit_bytes=VMEM_LIMIT_BYTES),
    )(z, xbc, dt, dtt, *params)


def _outproj_kernel(yssd_ref, ysc_ref, x_ref, wout_ref, nw_ref, wr_ref, br_ref,
                    h1_ref, xn_ref, idx_ref, gate_ref, rank_ref, cnt_ref, run_ref, earlier_ref):
    tm = x_ref.shape[0]

    @pl.when(pl.program_id(0) == 0)
    def _():
        run_ref[...] = jnp.zeros_like(run_ref)
        ri = lax.broadcasted_iota(jnp.int32, (tm, tm), 0)
        ci = lax.broadcasted_iota(jnp.int32, (tm, tm), 1)
        earlier_ref[...] = jnp.where(ri < ci, 1.0, 0.0).astype(BF16)

    h1 = (x_ref[...] + _dot(yssd_ref[...], wout_ref[:D_SSD, :])
          + _dot(ysc_ref[...], wout_ref[D_SSD:, :]))
    h1_ref[...] = h1
    xn = h1 * lax.rsqrt(jnp.mean(h1 * h1, axis=-1, keepdims=True) + EPS) * nw_ref[...]
    for c, words in enumerate(_pack_rows(xn)):
        xn_ref[c] = words

    xh = xn.astype(BF16)
    xl = (xn - xh.astype(F32)).astype(BF16)
    nt = lambda w, a: lax.dot_general(w, a, (((1,), (1,)), ((), ())), preferred_element_type=F32)
    w_x = nt(wr_ref[...], xh)
    logits = (w_x[:N_EXPERTS] + w_x[N_EXPERTS:] + nt(wr_ref[:N_EXPERTS, :], xl)
              + br_ref[...])

    expert = lax.broadcasted_iota(jnp.int32, (N_EXPERTS, tm), 0).astype(F32)
    work = logits
    vals, firsts, sels = [], [], []
    for _ in range(TOP_K):
        m = jnp.max(work, axis=0, keepdims=True)
        first = jnp.min(jnp.where(work == m, expert, float(N_EXPERTS)), axis=0, keepdims=True)
        sel = expert == first
        vals.append(m)
        firsts.append(first)
        sels.append(sel)
        work = jnp.where(sel, -jnp.inf, work)
    exps = [jnp.exp(v - vals[0]) for v in vals]
    denom = exps[0] + exps[1] + exps[2] + exps[3]

    onehot = jnp.zeros((N_EXPERTS, tm), F32)
    for sel in sels:
        onehot = onehot + jnp.where(sel, 1.0, 0.0)
    before = run_ref[...] + _dot(onehot.astype(BF16), earlier_ref[...])
    run_ref[...] = run_ref[...] + jnp.sum(onehot, axis=1, keepdims=True)
    cnt_ref[...] = run_ref[...]

    slot = lax.broadcasted_iota(jnp.int32, (TOP_K, tm), 0)
    idx = jnp.zeros((TOP_K, tm), F32)
    gate = jnp.zeros((TOP_K, tm), F32)
    rank = jnp.zeros((TOP_K, tm), F32)
    for k in range(TOP_K):
        r_k = jnp.sum(jnp.where(sels[k], before, 0.0), axis=0, keepdims=True)
        idx = jnp.where(slot == k, firsts[k], idx)
        gate = jnp.where(slot == k, exps[k] / denom, gate)
        rank = jnp.where(slot == k, r_k, rank)
    idx_ref[...] = idx.astype(jnp.int32)
    gate_ref[...] = gate
    rank_ref[...] = rank.astype(jnp.int32)


def _outproj(yssd, ysc, x2, w_out, nw, wr_split, br, first_tile, t):
    tm = OUT_TILE
    src = lambda n: pl.BlockSpec((tm, n), lambda i: (first_tile + i, 0))
    row = lambda n: pl.BlockSpec((tm, n), lambda i: (i, 0))
    slots = pl.BlockSpec((TOP_K, tm), lambda i: (0, i))
    full = lambda a: pl.BlockSpec(a.shape, lambda i: (0,) * a.ndim)
    return pl.pallas_call(
        _outproj_kernel,
        grid=(t // tm,),
        in_specs=[src(D_SSD), src(D_SC), src(D_MODEL), full(w_out), full(nw), full(wr_split),
                  full(br)],
        out_specs=[row(D_MODEL), pl.BlockSpec((ROW_PARTS, tm, LANES), lambda i: (0, i, 0)),
                   slots, slots, slots,
                   pl.BlockSpec((N_EXPERTS, 1), lambda i: (0, 0))],
        out_shape=[jax.ShapeDtypeStruct((t, D_MODEL), F32),
                   jax.ShapeDtypeStruct((ROW_PARTS, t, LANES), jnp.int32),
                   jax.ShapeDtypeStruct((TOP_K, t), jnp.int32),
                   jax.ShapeDtypeStruct((TOP_K, t), F32),
                   jax.ShapeDtypeStruct((TOP_K, t), jnp.int32),
                   jax.ShapeDtypeStruct((N_EXPERTS, 1), F32)],
        scratch_shapes=[pltpu.VMEM((N_EXPERTS, 1), F32), pltpu.VMEM((tm, tm), BF16)],
        compiler_params=pltpu.CompilerParams(
            dimension_semantics=("arbitrary",), vmem_limit_bytes=VMEM_LIMIT_BYTES),
    )(yssd, ysc, x2, w_out, nw, wr_split, br)


def _dest_kernel(pstart_ref, idx_ref, rank_ref, dest_ref):
    idx = idx_ref[...]
    base = jnp.zeros_like(idx)
    for e in range(N_EXPERTS):
        base = jnp.where(idx == e, pstart_ref[e], base)
    dest_ref[...] = base + rank_ref[...]


def _dest(pstarts, idx, rank):
    full = lambda a: pl.BlockSpec(a.shape, lambda i, ps: (0,) * a.ndim)
    return pl.pallas_call(
        _dest_kernel,
        grid_spec=pltpu.PrefetchScalarGridSpec(
            num_scalar_prefetch=1, grid=(1,),
            in_specs=[full(idx), full(rank)], out_specs=full(idx)),
        out_shape=jax.ShapeDtypeStruct(idx.shape, jnp.int32),
    )(pstarts, idx, rank)


def _sc_mesh():
    return plsc.VectorSubcoreMesh(core_axis_name="core", subcore_axis_name="subcore")


def _part_rows(dest, n_rows):
    off = (jnp.arange(ROW_PARTS, dtype=jnp.int32) * n_rows)[:, None, None]
    return (dest[None] + off).reshape(ROW_PARTS * dest.shape[0], dest.shape[1])


def _sc_dispatch(x, dest, n_rows):
    parts, t, width = x.shape
    n_slots = dest.shape[0]
    win = SC_WINDOW
    nwin = t // win

    @functools.partial(pl.kernel,
                       out_type=jax.ShapeDtypeStruct((parts * n_rows, width), jnp.int32),
                       mesh=_sc_mesh(), scratch_types=[])
    def scatter(x_hbm, idx_hbm, out_hbm):
        def body(x_vmem, idx_vmem):
            for k in range(n_slots):
                pltpu.sync_copy(x_vmem, out_hbm.at[idx_vmem.at[k]])

        pltpu.emit_pipeline(
            body,
            grid=(parts * nwin,),
            in_specs=[pl.BlockSpec((win, width), lambda j: (j, 0)),
                      pl.BlockSpec((n_slots, win), lambda j: (j // nwin, j % nwin))],
            out_specs=[],
            core_axis_name=("core", "subcore"),
            dimension_semantics=(pltpu.PARALLEL,),
        )(x_hbm, idx_hbm)

    xs = scatter(x.reshape(parts * t, width), _part_rows(dest, n_rows))
    return xs.reshape(parts, n_rows, width)


def _sc_collect(ys, dest):
    parts, n_rows, width = ys.shape
    n_slots, t = dest.shape
    win = SC_WINDOW
    nwin = t // win
    idx = _part_rows(dest, n_rows).reshape(parts, n_slots, t).swapaxes(0, 1)
    idx = idx.reshape(n_slots * parts, t)

    @functools.partial(pl.kernel,
                       out_type=jax.ShapeDtypeStruct((n_slots * parts * t, width), jnp.int32),
                       mesh=_sc_mesh(), scratch_types=[])
    def gather(y_hbm, idx_hbm, out_hbm):
        def body(idx_vmem, out_vmem):
            pltpu.sync_copy(y_hbm.at[idx_vmem.at[0]], out_vmem)

        pltpu.emit_pipeline(
            body,
            grid=(n_slots * parts * nwin,),
            in_specs=[pl.BlockSpec((1, win), lambda j: (j // nwin, j % nwin))],
            out_specs=[pl.BlockSpec((win, width), lambda j: (j, 0))],
            core_axis_name=("core", "subcore"),
            dimension_semantics=(pltpu.PARALLEL,),
        )(idx_hbm, out_hbm)

    out = gather(ys.reshape(parts * n_rows, width), idx)
    return out.reshape(n_slots, parts, t, width)


def _expert_kernel(blk_e_ref, blk_n_ref, xs_ref, wgu_ref, bgu_ref, wd_ref, bd_ref, out_ref,
                   wgu_b, wd_b):
    i = pl.program_id(0)
    n_valid = blk_n_ref[i]

    @pl.when((i == 0) | (blk_e_ref[i] != blk_e_ref[jnp.maximum(i - 1, 0)]))
    def _():
        wgu_b[...] = wgu_ref[0].astype(BF16)
        wd_b[...] = wd_ref[0].astype(BF16)

    @pl.when(n_valid == 0)
    def _():
        out_ref[...] = jnp.zeros_like(out_ref)

    @pl.when(n_valid > 0)
    def _():
        for r in range(0, ROW_BLOCK, EXPERT_SUB):
            words = [xs_ref[c, r:r + EXPERT_SUB] for c in range(ROW_PARTS)]
            rows = r + lax.broadcasted_iota(jnp.int32, words[0].shape, 0)
            xb = _unpack_rows([jnp.where(rows < n_valid, w, 0) for w in words]).astype(BF16)
            gu = _dot(xb, wgu_b[...]) + bgu_ref[0]
            gate = jnp.minimum(gu[:, :EXPERT_FF], SWIGLU_LIMIT)
            up = jnp.clip(gu[:, EXPERT_FF:], -SWIGLU_LIMIT, SWIGLU_LIMIT)
            hid = (up + 1.0) * (gate * (1.0 / (1.0 + jnp.exp(-SWIGLU_ALPHA * gate))))
            out = _dot(hid.astype(BF16), wd_b[...]) + bd_ref[0]
            for c, w in enumerate(_pack_rows(out)):
                out_ref[c, r:r + EXPERT_SUB] = w


def _experts(blk_e, blk_n, xs, wgu, bgu, wd, bd):
    rb = ROW_BLOCK
    by_expert = lambda a: pl.BlockSpec((1,) + a.shape[1:], lambda i, be, bn: (be[i], 0, 0))
    return pl.pallas_call(
        _expert_kernel,
        grid_spec=pltpu.PrefetchScalarGridSpec(
            num_scalar_prefetch=2, grid=(xs.shape[1] // rb,),
            in_specs=[pl.BlockSpec((ROW_PARTS, rb, LANES), lambda i, be, bn: (0, i, 0)),
                      by_expert(wgu), by_expert(bgu), by_expert(wd), by_expert(bd)],
            out_specs=pl.BlockSpec((ROW_PARTS, rb, LANES), lambda i, be, bn: (0, i, 0)),
            scratch_shapes=[pltpu.VMEM(wgu.shape[1:], BF16), pltpu.VMEM(wd.shape[1:], BF16)]),
        out_shape=jax.ShapeDtypeStruct(xs.shape, xs.dtype),
        compiler_params=pltpu.CompilerParams(
            dimension_semantics=("arbitrary",), vmem_limit_bytes=VMEM_LIMIT_BYTES),
    )(blk_e, blk_n, xs, wgu, bgu, wd, bd)


def _combine_kernel(gate_ref, h1_ref, fw_ref, y0_ref, y1_ref, y2_ref, y3_ref, *rest):
    out_ref = rest[-1]
    parts = _split3(gate_ref[...])
    slot = lax.broadcasted_iota(jnp.int32, (TOP_K, LANES), 0)
    h2 = h1_ref[...]
    for k, y_ref in enumerate((y0_ref, y1_ref, y2_ref, y3_ref)):
        pick = jnp.where(slot == k, 1.0, 0.0).astype(BF16)
        g = sum(lax.dot_general(p, pick, (((0,), (0,)), ((), ())), preferred_element_type=F32)
                for p in parts)
        y = _unpack_rows([y_ref[c] for c in range(ROW_PARTS)])
        h2 = h2 + jnp.concatenate([g] * (D_MODEL // LANES), axis=-1) * y
    out_ref[...] = h2 * lax.rsqrt(jnp.mean(h2 * h2, axis=-1, keepdims=True) + EPS) * fw_ref[...]


def _combine(gates, h1, fw, planes, first_tile, out_tile, t, out_prev):
    d = h1.shape[1]
    tm = COMBINE_TILE
    n = planes.shape[2]
    plane = lambda k: pl.BlockSpec((None, ROW_PARTS, tm, LANES), lambda i: (k, 0, i, 0))
    in_specs = [pl.BlockSpec((TOP_K, tm), lambda i: (0, first_tile + i)),
                pl.BlockSpec((tm, d), lambda i: (first_tile + i, 0)),
                pl.BlockSpec((1, d), lambda i: (0, 0))]
    in_specs += [plane(k) for k in range(TOP_K)]
    args = [gates, h1, fw] + [planes] * TOP_K
    aliases = {}
    if out_prev is not None:
        in_specs.append(pl.BlockSpec(memory_space=pl.ANY))
        aliases = {len(args): 0}
        args.append(out_prev)
    return pl.pallas_call(
        _combine_kernel,
        grid=(n // tm,),
        in_specs=in_specs,
        out_specs=pl.BlockSpec((tm, d), lambda i: (out_tile + i, 0)),
        out_shape=jax.ShapeDtypeStruct((t, d), F32),
        input_output_aliases=aliases,
        compiler_params=pltpu.CompilerParams(
            dimension_semantics=("arbitrary",), vmem_limit_bytes=VMEM_LIMIT_BYTES),
    )(*args)


def _block_plan(counts, n_blocks):
    rb = ROW_BLOCK
    pcounts = (counts + rb - 1) // rb * rb
    pends = jnp.cumsum(pcounts)
    pstarts = pends - pcounts
    block_start = jnp.arange(n_blocks, dtype=jnp.int32) * rb
    blk_e = jnp.minimum(jnp.sum(block_start[:, None] >= pends[None, :], axis=1),
                        N_EXPERTS - 1).astype(jnp.int32)
    mine = blk_e[:, None] == jnp.arange(N_EXPERTS, dtype=jnp.int32)[None, :]
    pick = lambda table: jnp.sum(jnp.where(mine, table[None, :], 0), axis=1)
    blk_n = jnp.clip(pick(counts) - (block_start - pick(pstarts)), 0, rb).astype(jnp.int32)
    return pstarts.astype(jnp.int32), blk_e, blk_n


def kernel(x, norm1_w, w_in, ssd_conv_w, ssd_conv_b, dt_bias, a_log, d_skip, ssd_norm_w,
           sc_conv_w, sc_norm_w, w_out, norm2_w, w_router, b_router, w_gate_up, b_gate_up,
           w_down, b_down, final_norm_w):
    assert norm1_w.shape[0] == 1, "single-layer problem"
    batch, seq, d = x.shape
    t = batch * seq
    x2 = x.reshape(t, d)
    row = lambda a: a.reshape(1, -1)

    s1, s2 = D_SSD + D_XBC, D_SSD + D_XBC + N_SSD_HEADS
    w = w_in[0]
    w_zx, w_dt, w_sc = w[:, :s1].astype(BF16), w[:, s1:s2].astype(BF16), w[:, s2:].astype(BF16)
    z, xbc, ysc, dt, dtt = _inproj(x2, seq, row(norm1_w[0]), w_zx, w_sc, w_dt, w_dt.T,
                                   ssd_conv_w[0], row(ssd_conv_b[0]), sc_conv_w[0],
                                   row(sc_norm_w[0]))
    yssd = _mixer(z, xbc, dt, dtt, batch, seq, row(dt_bias[0]), row(a_log[0]), row(d_skip[0]),
                  row(ssd_norm_w[0]))

    wr = w_router[0].T
    wr_hi = wr.astype(BF16)
    wr_split = jnp.concatenate([wr_hi, (wr - wr_hi.astype(F32)).astype(BF16)], axis=0)
    unit = t // sum(MOE_SPLIT)
    sizes = [unit * s for s in MOE_SPLIT]
    starts = [sum(sizes[:g]) for g in range(len(sizes))]
    assert sum(sizes) == t and all(n % (COMBINE_CHUNKS * COMBINE_TILE) == 0 and n % OUT_TILE == 0
                                   for n in sizes)
    w_out_b, nw2, br = w_out[0].astype(BF16), row(norm2_w[0]), b_router[0].reshape(-1, 1)
    routed = []
    for t0, tg in zip(starts, sizes):
        n_rows = tg * TOP_K + N_EXPERTS * ROW_BLOCK
        h1, xn2, idx, gates, rank, counts = _outproj(
            yssd, ysc, x2, w_out_b, nw2, wr_split, br, t0 // OUT_TILE, tg)
        pstarts, blk_e, blk_n = _block_plan(counts[:, 0].astype(jnp.int32),
                                            n_rows // ROW_BLOCK)
        dest = _dest(pstarts, idx, rank)
        xs = _sc_dispatch(xn2, dest, n_rows)
        routed.append((h1, gates, dest, blk_e, blk_n, xs))
    expert_out = [_experts(blk_e, blk_n, xs, w_gate_up[0], b_gate_up[0][:, None, :],
                           w_down[0], b_down[0][:, None, :])
                  for (_, _, _, blk_e, blk_n, xs) in routed]
    fw, out = row(final_norm_w), None
    for t0, tg, (h1, gates, dest, _, _, _), ys in zip(starts, sizes, routed, expert_out):
        tc = tg // COMBINE_CHUNKS
        for c in range(COMBINE_CHUNKS):
            planes = _sc_collect(ys, dest[:, c * tc:(c + 1) * tc])
            out = _combine(gates, h1, fw, planes, c * (tc // COMBINE_TILE),
                           (t0 + c * tc) // COMBINE_TILE, t, out)
    return out.reshape(batch, seq, d)
```

```python
import functools

import jax
import jax.numpy as jnp
from jax import lax
from jax.experimental import pallas as pl
from jax.experimental.pallas import tpu as pltpu
from jax.experimental.pallas import tpu_sc as plsc

D_MODEL = 1024
D_SSD = 1024
SSD_HEAD_DIM = 64
N_SSD_HEADS = 16
N_GROUPS = 2
HEADS_PER_GROUP = 8
D_STATE = 128
SSD_CONV = 4
CHUNK = 128
D_XBC = D_SSD + 2 * N_GROUPS * D_STATE
D_SC = 1024
SC_CONV = 3
N_EXPERTS = 32
TOP_K = 4
EXPERT_FF = 1024
SWIGLU_LIMIT = 7.0
SWIGLU_ALPHA = 1.702
EPS = 1e-5

VMEM_LIMIT_BYTES = 56 * 1024 * 1024
SUBLANES = 8
ROW_BLOCK = 512
EXPERT_SUB = 256
IN_TILE = 512
IN_SUB = 256
MIX_CHUNKS = 8
OUT_TILE = 1024
COMBINE_TILE = 512
MOE_SPLIT = (5, 3)
COMBINE_CHUNKS = 4
LANES = 128
ROW_PARTS = D_MODEL // (2 * LANES)
SC_WINDOW = 128
HIGH_HALF = -65536

F32 = jnp.float32
BF16 = jnp.bfloat16


def _dot(a, b):
    return jnp.dot(a, b, preferred_element_type=F32)


def _split3(v):
    hi = v.astype(BF16)
    r1 = v - hi.astype(F32)
    mid = r1.astype(BF16)
    lo = (r1 - mid.astype(F32)).astype(BF16)
    return hi, mid, lo


def _dot_exact_rhs01(v, m01):
    hi, mid, lo = _split3(v)
    return _dot(hi, m01) + _dot(mid, m01) + _dot(lo, m01)


def _split2(v):
    hi = v.astype(BF16)
    return hi, (v - hi.astype(F32)).astype(BF16)


def _spread(v, m01):
    hi, mid = _split2(v)
    return _dot(hi, m01) + _dot(mid, m01)


def _spread_rows(m01, v):
    hi, mid = _split2(v)
    return _dot(m01, hi) + _dot(m01, mid)


def _silu(v):
    return v * (1.0 / (1.0 + jnp.exp(-v)))


def _softplus(v):
    return jnp.maximum(v, 0.0) + jnp.log1p(jnp.exp(-jnp.abs(v)))


def _pack_rows(v):
    bits = lambda a: lax.bitcast_convert_type(a.astype(BF16).astype(F32), jnp.int32)
    parts = []
    for c in range(ROW_PARTS):
        lo = v[:, (2 * c) * LANES:(2 * c + 1) * LANES]
        hi = v[:, (2 * c + 1) * LANES:(2 * c + 2) * LANES]
        parts.append(bits(hi) | lax.shift_right_logical(bits(lo), 16))
    return parts


def _unpack_rows(parts):
    cols = []
    for w in parts:
        cols.append(lax.bitcast_convert_type(w << 16, F32))
        cols.append(lax.bitcast_convert_type(w & HIGH_HALF, F32))
    return jnp.concatenate(cols, axis=-1)


def _shifted(cur, tail, s):
    if s == 0:
        return cur
    rc = pltpu.roll(cur, s, axis=0)
    rt = pltpu.roll(tail, s, axis=0)
    row = lax.broadcasted_iota(jnp.int32, tail.shape, 0)
    first = jnp.where(row < s, rt, rc[0:SUBLANES])
    return jnp.concatenate([first, rc[SUBLANES:]], axis=0)


def _causal_conv(tail_ref, cur, w_ref, acc):
    rows = cur.shape[0]
    n_taps = w_ref.shape[0]
    tail = tail_ref[...]
    for k in range(n_taps):
        acc = acc + w_ref[k:k + 1, :] * _shifted(cur, tail, n_taps - 1 - k)
    tail_ref[...] = cur[rows - SUBLANES:rows]
    return acc


def _inproj_kernel(tiles_per_seq, x_ref, nw_ref, wzx_ref, wsc_ref, wdt_ref, wdtt_ref,
                   cw_ref, cb_ref, scw_ref, scnw_ref,
                   z_ref, xbc_ref, ysc_ref, dt_ref, dtt_ref, hist_ref, schist_ref):
    tm = x_ref.shape[0]

    @pl.when(pl.program_id(0) % tiles_per_seq == 0)
    def _():
        hist_ref[...] = jnp.zeros_like(hist_ref)
        schist_ref[...] = jnp.zeros_like(schist_ref)

    for r in range(0, tm, IN_SUB):
        rows = pl.ds(r, IN_SUB)
        x = x_ref[rows, :]
        xn = x * lax.rsqrt(jnp.mean(x * x, axis=-1, keepdims=True) + EPS) * nw_ref[...]
        xb = xn.astype(BF16)
        proj = lambda w_ref, o, n, xb=xb: _dot(xb, w_ref[:, o:o + n])
        z_ref[rows, :] = proj(wzx_ref, 0, D_SSD).astype(z_ref.dtype)

        conv = _causal_conv(hist_ref, proj(wzx_ref, D_SSD, D_XBC), cw_ref, cb_ref[...])
        xbc_ref[rows, :] = _silu(conv).astype(xbc_ref.dtype)

        cu = proj(wsc_ref, D_SC, D_SC) * proj(wsc_ref, 2 * D_SC, D_SC)
        v = _causal_conv(schist_ref, cu, scw_ref, jnp.zeros_like(cu))
        gv = proj(wsc_ref, 0, D_SC) * v
        y_sc = gv * lax.rsqrt(jnp.mean(gv * gv, axis=-1, keepdims=True) + EPS) * scnw_ref[...]
        ysc_ref[rows, :] = y_sc.astype(ysc_ref.dtype)

        dt_ref[rows, :] = _dot(xb, wdt_ref[...])
        dtt_ref[:, rows] = lax.dot_general(wdtt_ref[...], xb, (((1,), (1,)), ((), ())),
                                           preferred_element_type=F32)


def _inproj(x2, seq, nw, w_zx, w_sc, w_dt, w_dtt, cw, cb, scw, scnw):
    t = x2.shape[0]
    tm = IN_TILE
    assert seq % tm == 0, "in-proj tiles must not straddle sequences (causal conv carry)"
    row = lambda n: pl.BlockSpec((tm, n), lambda i: (i, 0))
    full = lambda a: pl.BlockSpec(a.shape, lambda i: (0,) * a.ndim)
    params = (nw, w_zx, w_sc, w_dt, w_dtt, cw, cb, scw, scnw)
    return pl.pallas_call(
        functools.partial(_inproj_kernel, seq // tm),
        grid=(t // tm,),
        in_specs=[row(D_MODEL)] + [full(a) for a in params],
        out_specs=[row(D_SSD), row(D_XBC), row(D_SC),
                   row(N_SSD_HEADS), pl.BlockSpec((N_SSD_HEADS, tm), lambda i: (0, i))],
        out_shape=[jax.ShapeDtypeStruct((t, D_SSD), BF16),
                   jax.ShapeDtypeStruct((t, D_XBC), BF16),
                   jax.ShapeDtypeStruct((t, D_SC), BF16),
                   jax.ShapeDtypeStruct((t, N_SSD_HEADS), F32),
                   jax.ShapeDtypeStruct((N_SSD_HEADS, t), F32)],
        scratch_shapes=[pltpu.VMEM((SUBLANES, D_XBC), F32), pltpu.VMEM((SUBLANES, D_SC), F32)],
        compiler_params=pltpu.CompilerParams(
            dimension_semantics=("arbitrary",), vmem_limit_bytes=VMEM_LIMIT_BYTES),
    )(x2, *params)


def _mixer_kernel(z_ref, xbc_ref, dt_ref, dtt_ref,
                  dtb_ref, dtbt_ref, alog_ref, alogt_ref, dskip_ref, nw_ref, out_ref, state_ref):
    q = CHUNK

    @pl.when(pl.program_id(1) == 0)
    def _():
        state_ref[...] = jnp.zeros_like(state_ref)

    ri = lax.broadcasted_iota(jnp.int32, (q, q), 0)
    ci = lax.broadcasted_iota(jnp.int32, (q, q), 1)
    causal = ri >= ci
    tri = jnp.where(causal, 1.0, 0.0).astype(BF16)
    trit = jnp.where(ri <= ci, 1.0, 0.0).astype(BF16)
    hh = lax.broadcasted_iota(jnp.int32, (N_SSD_HEADS, D_SSD), 0)
    hl = lax.broadcasted_iota(jnp.int32, (N_SSD_HEADS, D_SSD), 1) // SSD_HEAD_DIM
    expand = jnp.where(hh == hl, 1.0, 0.0).astype(BF16)
    lane_lo = lax.broadcasted_iota(jnp.int32, (q, 2 * SSD_HEAD_DIM), 1) < SSD_HEAD_DIM
    gw = HEADS_PER_GROUP * SSD_HEAD_DIM
    n_bc = N_GROUPS * D_STATE
    a_neg = -jnp.exp(alog_ref[...])
    a_negt = -jnp.exp(alogt_ref[...])
    d_x = _dot_exact_rhs01(dskip_ref[...], expand)

    for s in range(MIX_CHUNKS):
        rows = pl.ds(s * q, q)
        xs = xbc_ref[rows, :D_SSD].astype(F32)
        b_all = xbc_ref[rows, D_SSD:D_SSD + n_bc]
        c_all = xbc_ref[rows, D_SSD + n_bc:]

        dt = _softplus(dt_ref[rows, :] + dtb_ref[...])
        dtt = _softplus(dtt_ref[:, rows] + dtbt_ref[...])
        acum = _spread_rows(tri, dt * a_neg)
        acumt = _spread(dtt * a_negt, trit)
        a_last = acum[q - 1:q, :]

        xdt = xs * _spread(dt, expand)
        xdt_b = xdt.astype(BF16)
        dec_in = _spread(jnp.exp(acum), expand)
        dec_out = _spread(jnp.exp(a_last - acum), expand)
        dec_all = _dot_exact_rhs01(jnp.exp(a_last), expand)
        xw_b = (xdt * dec_out).astype(BF16)

        y_parts = []
        for g in range(N_GROUPS):
            b_g = b_all[:, g * D_STATE:(g + 1) * D_STATE]
            c_g = c_all[:, g * D_STATE:(g + 1) * D_STATE]
            cb = lax.dot_general(c_g, b_g, (((1,), (1,)), ((), ())), preferred_element_type=F32)
            st = state_ref[:, g * gw:(g + 1) * gw]
            y_inter = _dot(c_g, st.astype(BF16)) * dec_in[:, g * gw:(g + 1) * gw]
            new_st = lax.dot_general(b_g, xw_b[:, g * gw:(g + 1) * gw], (((0,), (0,)), ((), ())),
                                     preferred_element_type=F32)
            state_ref[:, g * gw:(g + 1) * gw] = st * dec_all[:, g * gw:(g + 1) * gw] + new_st
            pair_parts = []
            for pr in range(HEADS_PER_GROUP // 2):
                h0 = g * HEADS_PER_GROUP + 2 * pr
                lo = h0 * SSD_HEAD_DIM
                rhs = xdt_b[:, lo:lo + 2 * SSD_HEAD_DIM]
                ys = []
                for h in (h0, h0 + 1):
                    seg = acum[:, h:h + 1] - acumt[h:h + 1, :]
                    m = (cb * jnp.exp(jnp.where(causal, seg, -jnp.inf))).astype(BF16)
                    ys.append(_dot(m, rhs))
                pair_parts.append(jnp.where(lane_lo, ys[0], ys[1]))
            y_parts.append(jnp.concatenate(pair_parts, axis=-1) + y_inter)
        y = jnp.concatenate(y_parts, axis=-1) + d_x * xs

        gated = y * _silu(z_ref[rows, :].astype(F32))
        outs = []
        for g in range(N_GROUPS):
            gg = gated[:, g * gw:(g + 1) * gw]
            outs.append(gg * lax.rsqrt(jnp.mean(gg * gg, axis=-1, keepdims=True) + EPS))
        out_ref[rows, :] = (jnp.concatenate(outs, axis=-1) * nw_ref[...]).astype(out_ref.dtype)


def _mixer(z, xbc, dt, dtt, batch, seq, dtb, alog, dskip, nw):
    rows = CHUNK * MIX_CHUNKS
    assert seq % rows == 0
    ns = seq // rows
    t = batch * seq
    row = lambda n: pl.BlockSpec((rows, n), lambda b, c: (b * ns + c, 0))
    full = lambda a: pl.BlockSpec(a.shape, lambda b, c: (0,) * a.ndim)
    dtbt, alogt = dtb.reshape(N_SSD_HEADS, 1), alog.reshape(N_SSD_HEADS, 1)
    params = (dtb, dtbt, alog, alogt, dskip, nw)
    return pl.pallas_call(
        _mixer_kernel,
        grid=(batch, ns),
        in_specs=[row(D_SSD), row(D_XBC), row(N_SSD_HEADS),
                  pl.BlockSpec((N_SSD_HEADS, rows), lambda b, c: (0, b * ns + c))]
                 + [full(a) for a in params],
        out_specs=row(D_SSD),
        out_shape=jax.ShapeDtypeStruct((t, D_SSD), BF16),
        scratch_shapes=[pltpu.VMEM((D_STATE, D_SSD), F32)],
        compiler_params=pltpu.CompilerParams(
            dimension_semantics=("arbitrary", "arbitrary"), vmem_limit_bytes=VMEM_LIMIT_BYTES),
    )(z, xbc, dt, dtt, *params)


def _outproj_kernel(yssd_ref, ysc_ref, x_ref, wout_ref, nw_ref, wr_ref, br_ref,
                    h1_ref, xn_ref, idx_ref, gate_ref, rank_ref, cnt_ref, run_ref, earlier_ref):
    tm = x_ref.shape[0]

    @pl.when(pl.program_id(0) == 0)
    def _():
        run_ref[...] = jnp.zeros_like(run_ref)
        ri = lax.broadcasted_iota(jnp.int32, (tm, tm), 0)
        ci = lax.broadcasted_iota(jnp.int32, (tm, tm), 1)
        earlier_ref[...] = jnp.where(ri < ci, 1.0, 0.0).astype(BF16)

    h1 = (x_ref[...] + _dot(yssd_ref[...], wout_ref[:D_SSD, :])
          + _dot(ysc_ref[...], wout_ref[D_SSD:, :]))
    h1_ref[...] = h1
    xn = h1 * lax.rsqrt(jnp.mean(h1 * h1, axis=-1, keepdims=True) + EPS) * nw_ref[...]
    for c, words in enumerate(_pack_rows(xn)):
        xn_ref[c] = words

    xh = xn.astype(BF16)
    xl = (xn - xh.astype(F32)).astype(BF16)
    nt = lambda w, a: lax.dot_general(w, a, (((1,), (1,)), ((), ())), preferred_element_type=F32)
    w_x = nt(wr_ref[...], xh)
    logits = (w_x[:N_EXPERTS] + w_x[N_EXPERTS:] + nt(wr_ref[:N_EXPERTS, :], xl)
              + br_ref[...])

    expert = lax.broadcasted_iota(jnp.int32, (N_EXPERTS, tm), 0).astype(F32)
    work = logits
    vals, firsts, sels = [], [], []
    for _ in range(TOP_K):
        m = jnp.max(work, axis=0, keepdims=True)
        first = jnp.min(jnp.where(work == m, expert, float(N_EXPERTS)), axis=0, keepdims=True)
        sel = expert == first
        vals.append(m)
        firsts.append(first)
        sels.append(sel)
        work = jnp.where(sel, -jnp.inf, work)
    exps = [jnp.exp(v - vals[0]) for v in vals]
    denom = exps[0] + exps[1] + exps[2] + exps[3]

    onehot = jnp.zeros((N_EXPERTS, tm), F32)
    for sel in sels:
        onehot = onehot + jnp.where(sel, 1.0, 0.0)
    before = run_ref[...] + _dot(onehot.astype(BF16), earlier_ref[...])
    run_ref[...] = run_ref[...] + jnp.sum(onehot, axis=1, keepdims=True)
    cnt_ref[...] = run_ref[...]

    slot = lax.broadcasted_iota(jnp.int32, (TOP_K, tm), 0)
    idx = jnp.zeros((TOP_K, tm), F32)
    gate = jnp.zeros((TOP_K, tm), F32)
    rank = jnp.zeros((TOP_K, tm), F32)
    for k in range(TOP_K):
        r_k = jnp.sum(jnp.where(sels[k], before, 0.0), axis=0, keepdims=True)
        idx = jnp.where(slot == k, firsts[k], idx)
        gate = jnp.where(slot == k, exps[k] / denom, gate)
        rank = jnp.where(slot == k, r_k, rank)
    idx_ref[...] = idx.astype(jnp.int32)
    gate_ref[...] = gate
    rank_ref[...] = rank.astype(jnp.int32)


def _outproj(yssd, ysc, x2, w_out, nw, wr_split, br, first_tile, t):
    tm = OUT_TILE
    src = lambda n: pl.BlockSpec((tm, n), lambda i: (first_tile + i, 0))
    row = lambda n: pl.BlockSpec((tm, n), lambda i: (i, 0))
    slots = pl.BlockSpec((TOP_K, tm), lambda i: (0, i))
    full = lambda a: pl.BlockSpec(a.shape, lambda i: (0,) * a.ndim)
    return pl.pallas_call(
        _outproj_kernel,
        grid=(t // tm,),
        in_specs=[src(D_SSD), src(D_SC), src(D_MODEL), full(w_out), full(nw), full(wr_split),
                  full(br)],
        out_specs=[row(D_MODEL), pl.BlockSpec((ROW_PARTS, tm, LANES), lambda i: (0, i, 0)),
                   slots, slots, slots,
                   pl.BlockSpec((N_EXPERTS, 1), lambda i: (0, 0))],
        out_shape=[jax.ShapeDtypeStruct((t, D_MODEL), F32),
                   jax.ShapeDtypeStruct((ROW_PARTS, t, LANES), jnp.int32),
                   jax.ShapeDtypeStruct((TOP_K, t), jnp.int32),
                   jax.ShapeDtypeStruct((TOP_K, t), F32),
                   jax.ShapeDtypeStruct((TOP_K, t), jnp.int32),
                   jax.ShapeDtypeStruct((N_EXPERTS, 1), F32)],
        scratch_shapes=[pltpu.VMEM((N_EXPERTS, 1), F32), pltpu.VMEM((tm, tm), BF16)],
        compiler_params=pltpu.CompilerParams(
            dimension_semantics=("arbitrary",), vmem_limit_bytes=VMEM_LIMIT_BYTES),
    )(yssd, ysc, x2, w_out, nw, wr_split, br)


def _dest_kernel(pstart_ref, idx_ref, rank_ref, dest_ref):
    idx = idx_ref[...]
    base = jnp.zeros_like(idx)
    for e in range(N_EXPERTS):
        base = jnp.where(idx == e, pstart_ref[e], base)
    dest_ref[...] = base + rank_ref[...]


def _dest(pstarts, idx, rank):
    full = lambda a: pl.BlockSpec(a.shape, lambda i, ps: (0,) * a.ndim)
    return pl.pallas_call(
        _dest_kernel,
        grid_spec=pltpu.PrefetchScalarGridSpec(
            num_scalar_prefetch=1, grid=(1,),
            in_specs=[full(idx), full(rank)], out_specs=full(idx)),
        out_shape=jax.ShapeDtypeStruct(idx.shape, jnp.int32),
    )(pstarts, idx, rank)


def _sc_mesh():
    return plsc.VectorSubcoreMesh(core_axis_name="core", subcore_axis_name="subcore")


def _part_rows(dest, n_rows):
    off = (jnp.arange(ROW_PARTS, dtype=jnp.int32) * n_rows)[:, None, None]
    return (dest[None] + off).reshape(ROW_PARTS * dest.shape[0], dest.shape[1])


def _sc_dispatch(x, dest, n_rows):
    parts, t, width = x.shape
    n_slots = dest.shape[0]
    win = SC_WINDOW
    nwin = t // win

    @functools.partial(pl.kernel,
                       out_type=jax.ShapeDtypeStruct((parts * n_rows, width), jnp.int32),
                       mesh=_sc_mesh(), scratch_types=[])
    def scatter(x_hbm, idx_hbm, out_hbm):
        def body(x_vmem, idx_vmem):
            for k in range(n_slots):
                pltpu.sync_copy(x_vmem, out_hbm.at[idx_vmem.at[k]])

        pltpu.emit_pipeline(
            body,
            grid=(parts * nwin,),
            in_specs=[pl.BlockSpec((win, width), lambda j: (j, 0)),
                      pl.BlockSpec((n_slots, win), lambda j: (j // nwin, j % nwin))],
            out_specs=[],
            core_axis_name=("core", "subcore"),
            dimension_semantics=(pltpu.PARALLEL,),
        )(x_hbm, idx_hbm)

    xs = scatter(x.reshape(parts * t, width), _part_rows(dest, n_rows))
    return xs.reshape(parts, n_rows, width)


def _sc_collect(ys, dest):
    parts, n_rows, width = ys.shape
    n_slots, t = dest.shape
    win = SC_WINDOW
    nwin = t // win
    idx = _part_rows(dest, n_rows).reshape(parts, n_slots, t).swapaxes(0, 1)
    idx = idx.reshape(n_slots * parts, t)

    @functools.partial(pl.kernel,
                       out_type=jax.ShapeDtypeStruct((n_slots * parts * t, width), jnp.int32),
                       mesh=_sc_mesh(), scratch_types=[])
    def gather(y_hbm, idx_hbm, out_hbm):
        def body(idx_vmem, out_vmem):
            pltpu.sync_copy(y_hbm.at[idx_vmem.at[0]], out_vmem)

        pltpu.emit_pipeline(
            body,
            grid=(n_slots * parts * nwin,),
            in_specs=[pl.BlockSpec((1, win), lambda j: (j // nwin, j % nwin))],
            out_specs=[pl.BlockSpec((win, width), lambda j: (j, 0))],
            core_axis_name=("core", "subcore"),
            dimension_semantics=(pltpu.PARALLEL,),
        )(idx_hbm, out_hbm)

    out = gather(ys.reshape(parts * n_rows, width), idx)
    return out.reshape(n_slots, parts, t, width)


def _expert_kernel(blk_e_ref, blk_n_ref, xs_ref, wgu_ref, bgu_ref, wd_ref, bd_ref, out_ref,
                   wgu_b, wd_b):
    i = pl.program_id(0)
    n_valid = blk_n_ref[i]

    @pl.when((i == 0) | (blk_e_ref[i] != blk_e_ref[jnp.maximum(i - 1, 0)]))
    def _():
        wgu_b[...] = wgu_ref[0].astype(BF16)
        wd_b[...] = wd_ref[0].astype(BF16)

    @pl.when(n_valid == 0)
    def _():
        out_ref[...] = jnp.zeros_like(out_ref)

    @pl.when(n_valid > 0)
    def _():
        for r in range(0, ROW_BLOCK, EXPERT_SUB):
            words = [xs_ref[c, r:r + EXPERT_SUB] for c in range(ROW_PARTS)]
            rows = r + lax.broadcasted_iota(jnp.int32, words[0].shape, 0)
            xb = _unpack_rows([jnp.where(rows < n_valid, w, 0) for w in words]).astype(BF16)
            gu = _dot(xb, wgu_b[...]) + bgu_ref[0]
            gate = jnp.minimum(gu[:, :EXPERT_FF], SWIGLU_LIMIT)
            up = jnp.clip(gu[:, EXPERT_FF:], -SWIGLU_LIMIT, SWIGLU_LIMIT)
            hid = (up + 1.0) * (gate * (1.0 / (1.0 + jnp.exp(-SWIGLU_ALPHA * gate))))
            out = _dot(hid.astype(BF16), wd_b[...]) + bd_ref[0]
            for c, w in enumerate(_pack_rows(out)):
                out_ref[c, r:r + EXPERT_SUB] = w


def _experts(blk_e, blk_n, xs, wgu, bgu, wd, bd):
    rb = ROW_BLOCK
    by_expert = lambda a: pl.BlockSpec((1,) + a.shape[1:], lambda i, be, bn: (be[i], 0, 0))
    return pl.pallas_call(
        _expert_kernel,
        grid_spec=pltpu.PrefetchScalarGridSpec(
            num_scalar_prefetch=2, grid=(xs.shape[1] // rb,),
            in_specs=[pl.BlockSpec((ROW_PARTS, rb, LANES), lambda i, be, bn: (0, i, 0)),
                      by_expert(wgu), by_expert(bgu), by_expert(wd), by_expert(bd)],
            out_specs=pl.BlockSpec((ROW_PARTS, rb, LANES), lambda i, be, bn: (0, i, 0)),
            scratch_shapes=[pltpu.VMEM(wgu.shape[1:], BF16), pltpu.VMEM(wd.shape[1:], BF16)]),
        out_shape=jax.ShapeDtypeStruct(xs.shape, xs.dtype),
        compiler_params=pltpu.CompilerParams(
            dimension_semantics=("arbitrary",), vmem_limit_bytes=VMEM_LIMIT_BYTES),
    )(blk_e, blk_n, xs, wgu, bgu, wd, bd)


def _combine_kernel(gate_ref, h1_ref, fw_ref, y0_ref, y1_ref, y2_ref, y3_ref, *rest):
    out_ref = rest[-1]
    parts = _split3(gate_ref[...])
    slot = lax.broadcasted_iota(jnp.int32, (TOP_K, LANES), 0)
    h2 = h1_ref[...]
    for k, y_ref in enumerate((y0_ref, y1_ref, y2_ref, y3_ref)):
        pick = jnp.where(slot == k, 1.0, 0.0).astype(BF16)
        g = sum(lax.dot_general(p, pick, (((0,), (0,)), ((), ())), preferred_element_type=F32)
                for p in parts)
        y = _unpack_rows([y_ref[c] for c in range(ROW_PARTS)])
        h2 = h2 + jnp.concatenate([g] * (D_MODEL // LANES), axis=-1) * y
    out_ref[...] = h2 * lax.rsqrt(jnp.mean(h2 * h2, axis=-1, keepdims=True) + EPS) * fw_ref[...]


def _combine(gates, h1, fw, planes, first_tile, out_tile, t, out_prev):
    d = h1.shape[1]
    tm = COMBINE_TILE
    n = planes.shape[2]
    plane = lambda k: pl.BlockSpec((None, ROW_PARTS, tm, LANES), lambda i: (k, 0, i, 0))
    in_specs = [pl.BlockSpec((TOP_K, tm), lambda i: (0, first_tile + i)),
                pl.BlockSpec((tm, d), lambda i: (first_tile + i, 0)),
                pl.BlockSpec((1, d), lambda i: (0, 0))]
    in_specs += [plane(k) for k in range(TOP_K)]
    args = [gates, h1, fw] + [planes] * TOP_K
    aliases = {}
    if out_prev is not None:
        in_specs.append(pl.BlockSpec(memory_space=pl.ANY))
        aliases = {len(args): 0}
        args.append(out_prev)
    return pl.pallas_call(
        _combine_kernel,
        grid=(n // tm,),
        in_specs=in_specs,
        out_specs=pl.BlockSpec((tm, d), lambda i: (out_tile + i, 0)),
        out_shape=jax.ShapeDtypeStruct((t, d), F32),
        input_output_aliases=aliases,
        compiler_params=pltpu.CompilerParams(
            dimension_semantics=("arbitrary",), vmem_limit_bytes=VMEM_LIMIT_BYTES),
    )(*args)


def _block_plan(counts, n_blocks):
    rb = ROW_BLOCK
    pcounts = (counts + rb - 1) // rb * rb
    pends = jnp.cumsum(pcounts)
    pstarts = pends - pcounts
    block_start = jnp.arange(n_blocks, dtype=jnp.int32) * rb
    blk_e = jnp.minimum(jnp.sum(block_start[:, None] >= pends[None, :], axis=1),
                        N_EXPERTS - 1).astype(jnp.int32)
    mine = blk_e[:, None] == jnp.arange(N_EXPERTS, dtype=jnp.int32)[None, :]
    pick = lambda table: jnp.sum(jnp.where(mine, table[None, :], 0), axis=1)
    blk_n = jnp.clip(pick(counts) - (block_start - pick(pstarts)), 0, rb).astype(jnp.int32)
    return pstarts.astype(jnp.int32), blk_e, blk_n


def kernel(x, norm1_w, w_in, ssd_conv_w, ssd_conv_b, dt_bias, a_log, d_skip, ssd_norm_w,
           sc_conv_w, sc_norm_w, w_out, norm2_w, w_router, b_router, w_gate_up, b_gate_up,
           w_down, b_down, final_norm_w):
    assert norm1_w.shape[0] == 1, "single-layer problem"
    batch, seq, d = x.shape
    t = batch * seq
    x2 = x.reshape(t, d)
    row = lambda a: a.reshape(1, -1)

    s1, s2 = D_SSD + D_XBC, D_SSD + D_XBC + N_SSD_HEADS
    w = w_in[0]
    w_zx, w_dt, w_sc = w[:, :s1].astype(BF16), w[:, s1:s2].astype(BF16), w[:, s2:].astype(BF16)
    z, xbc, ysc, dt, dtt = _inproj(x2, seq, row(norm1_w[0]), w_zx, w_sc, w_dt, w_dt.T,
                                   ssd_conv_w[0], row(ssd_conv_b[0]), sc_conv_w[0],
                                   row(sc_norm_w[0]))
    yssd = _mixer(z, xbc, dt, dtt, batch, seq, row(dt_bias[0]), row(a_log[0]), row(d_skip[0]),
                  row(ssd_norm_w[0]))

    wr = w_router[0].T
    wr_hi = wr.astype(BF16)
    wr_split = jnp.concatenate([wr_hi, (wr - wr_hi.astype(F32)).astype(BF16)], axis=0)
    unit = t // sum(MOE_SPLIT)
    sizes = [unit * s for s in MOE_SPLIT]
    starts = [sum(sizes[:g]) for g in range(len(sizes))]
    assert sum(sizes) == t and all(n % (COMBINE_CHUNKS * COMBINE_TILE) == 0 and n % OUT_TILE == 0
                                   for n in sizes)
    w_out_b, nw2, br = w_out[0].astype(BF16), row(norm2_w[0]), b_router[0].reshape(-1, 1)
    routed = []
    for t0, tg in zip(starts, sizes):
        n_rows = tg * TOP_K + N_EXPERTS * ROW_BLOCK
        h1, xn2, idx, gates, rank, counts = _outproj(
            yssd, ysc, x2, w_out_b, nw2, wr_split, br, t0 // OUT_TILE, tg)
        pstarts, blk_e, blk_n = _block_plan(counts[:, 0].astype(jnp.int32),
                                            n_rows // ROW_BLOCK)
        dest = _dest(pstarts, idx, rank)
        xs = _sc_dispatch(xn2, dest, n_rows)
        routed.append((h1, gates, dest, blk_e, blk_n, xs))
        br = br + jnp.where(dest[0, 0] < 0, 1.0, 0.0)
    expert_out = [_experts(blk_e, blk_n, xs, w_gate_up[0], b_gate_up[0][:, None, :],
                           w_down[0], b_down[0][:, None, :])
                  for (_, _, _, blk_e, blk_n, xs) in routed]
    fw, out = row(final_norm_w), None
    for t0, tg, (h1, gates, dest, _, _, _), ys in zip(starts, sizes, routed, expert_out):
        tc = tg // COMBINE_CHUNKS
        for c in range(COMBINE_CHUNKS):
            planes = _sc_collect(ys, dest[:, c * tc:(c + 1) * tc])
            out = _combine(gates, h1, fw, planes, c * (tc // COMBINE_TILE),
                           (t0 + c * tc) // COMBINE_TILE, t, out)
    return out.reshape(batch, seq, d)
```

```python
import functools

import jax
import jax.numpy as jnp
from jax import lax
from jax.experimental import pallas as pl
from jax.experimental.pallas import tpu as pltpu
from jax.experimental.pallas import tpu_sc as plsc

D_MODEL = 1024
D_SSD = 1024
SSD_HEAD_DIM = 64
N_SSD_HEADS = 16
N_GROUPS = 2
HEADS_PER_GROUP = 8
D_STATE = 128
SSD_CONV = 4
CHUNK = 128
D_XBC = D_SSD + 2 * N_GROUPS * D_STATE
D_SC = 1024
SC_CONV = 3
N_EXPERTS = 32
TOP_K = 4
EXPERT_FF = 1024
SWIGLU_LIMIT = 7.0
SWIGLU_ALPHA = 1.702
EPS = 1e-5

VMEM_LIMIT_BYTES = 56 * 1024 * 1024
SUBLANES = 8
ROW_BLOCK = 512
EXPERT_SUB = 256
IN_TILE = 1024
IN_SUB = 256
MIX_CHUNKS = 8
OUT_TILE = 1024
COMBINE_TILE = 512
MOE_SPLIT = (5, 3)
COMBINE_CHUNKS = 4
LANES = 128
ROW_PARTS = D_MODEL // (2 * LANES)
SC_WINDOW = 128
HIGH_HALF = -65536

F32 = jnp.float32
BF16 = jnp.bfloat16


def _dot(a, b):
    return jnp.dot(a, b, preferred_element_type=F32)


def _split3(v):
    hi = v.astype(BF16)
    r1 = v - hi.astype(F32)
    mid = r1.astype(BF16)
    lo = (r1 - mid.astype(F32)).astype(BF16)
    return hi, mid, lo


def _dot_exact_rhs01(v, m01):
    hi, mid, lo = _split3(v)
    return _dot(hi, m01) + _dot(mid, m01) + _dot(lo, m01)


def _split2(v):
    hi = v.astype(BF16)
    return hi, (v - hi.astype(F32)).astype(BF16)


def _spread(v, m01):
    hi, mid = _split2(v)
    return _dot(hi, m01) + _dot(mid, m01)


def _spread_rows(m01, v):
    hi, mid = _split2(v)
    return _dot(m01, hi) + _dot(m01, mid)


def _silu(v):
    return v * (1.0 / (1.0 + jnp.exp(-v)))


def _softplus(v):
    return jnp.maximum(v, 0.0) + jnp.log1p(jnp.exp(-jnp.abs(v)))


def _pack_rows(v):
    bits = lambda a: lax.bitcast_convert_type(a.astype(BF16).astype(F32), jnp.int32)
    parts = []
    for c in range(ROW_PARTS):
        lo = v[:, (2 * c) * LANES:(2 * c + 1) * LANES]
        hi = v[:, (2 * c + 1) * LANES:(2 * c + 2) * LANES]
        parts.append(bits(hi) | lax.shift_right_logical(bits(lo), 16))
    return parts


def _unpack_rows(parts):
    cols = []
    for w in parts:
        cols.append(lax.bitcast_convert_type(w << 16, F32))
        cols.append(lax.bitcast_convert_type(w & HIGH_HALF, F32))
    return jnp.concatenate(cols, axis=-1)


def _shifted(cur, tail, s):
    if s == 0:
        return cur
    rc = pltpu.roll(cur, s, axis=0)
    rt = pltpu.roll(tail, s, axis=0)
    row = lax.broadcasted_iota(jnp.int32, tail.shape, 0)
    first = jnp.where(row < s, rt, rc[0:SUBLANES])
    return jnp.concatenate([first, rc[SUBLANES:]], axis=0)


def _causal_conv(tail_ref, cur, w_ref, acc):
    rows = cur.shape[0]
    n_taps = w_ref.shape[0]
    tail = tail_ref[...]
    for k in range(n_taps):
        acc = acc + w_ref[k:k + 1, :] * _shifted(cur, tail, n_taps - 1 - k)
    tail_ref[...] = cur[rows - SUBLANES:rows]
    return acc


def _inproj_kernel(tiles_per_seq, x_ref, nw_ref, wzx_ref, wsc_ref, wdt_ref, wdtt_ref,
                   cw_ref, cb_ref, scw_ref, scnw_ref,
                   z_ref, xbc_ref, ysc_ref, dt_ref, dtt_ref, hist_ref, schist_ref):
    tm = x_ref.shape[0]

    @pl.when(pl.program_id(0) % tiles_per_seq == 0)
    def _():
        hist_ref[...] = jnp.zeros_like(hist_ref)
        schist_ref[...] = jnp.zeros_like(schist_ref)

    for r in range(0, tm, IN_SUB):
        rows = pl.ds(r, IN_SUB)
        x = x_ref[rows, :]
        xn = x * lax.rsqrt(jnp.mean(x * x, axis=-1, keepdims=True) + EPS) * nw_ref[...]
        xb = xn.astype(BF16)
        proj = lambda w_ref, o, n, xb=xb: _dot(xb, w_ref[:, o:o + n])
        z_ref[rows, :] = proj(wzx_ref, 0, D_SSD).astype(z_ref.dtype)

        conv = _causal_conv(hist_ref, proj(wzx_ref, D_SSD, D_XBC), cw_ref, cb_ref[...])
        xbc_ref[rows, :] = _silu(conv).astype(xbc_ref.dtype)

        cu = proj(wsc_ref, D_SC, D_SC) * proj(wsc_ref, 2 * D_SC, D_SC)
        v = _causal_conv(schist_ref, cu, scw_ref, jnp.zeros_like(cu))
        gv = proj(wsc_ref, 0, D_SC) * v
        y_sc = gv * lax.rsqrt(jnp.mean(gv * gv, axis=-1, keepdims=True) + EPS) * scnw_ref[...]
        ysc_ref[rows, :] = y_sc.astype(ysc_ref.dtype)

        dt_ref[rows, :] = _dot(xb, wdt_ref[...])
        dtt_ref[:, rows] = lax.dot_general(wdtt_ref[...], xb, (((1,), (1,)), ((), ())),
                                           preferred_element_type=F32)


def _inproj(x2, seq, nw, w_zx, w_sc, w_dt, w_dtt, cw, cb, scw, scnw):
    t = x2.shape[0]
    tm = IN_TILE
    assert seq % tm == 0, "in-proj tiles must not straddle sequences (causal conv carry)"
    row = lambda n: pl.BlockSpec((tm, n), lambda i: (i, 0))
    full = lambda a: pl.BlockSpec(a.shape, lambda i: (0,) * a.ndim)
    params = (nw, w_zx, w_sc, w_dt, w_dtt, cw, cb, scw, scnw)
    return pl.pallas_call(
        functools.partial(_inproj_kernel, seq // tm),
        grid=(t // tm,),
        in_specs=[row(D_MODEL)] + [full(a) for a in params],
        out_specs=[row(D_SSD), row(D_XBC), row(D_SC),
                   row(N_SSD_HEADS), pl.BlockSpec((N_SSD_HEADS, tm), lambda i: (0, i))],
        out_shape=[jax.ShapeDtypeStruct((t, D_SSD), BF16),
                   jax.ShapeDtypeStruct((t, D_XBC), BF16),
                   jax.ShapeDtypeStruct((t, D_SC), BF16),
                   jax.ShapeDtypeStruct((t, N_SSD_HEADS), F32),
                   jax.ShapeDtypeStruct((N_SSD_HEADS, t), F32)],
        scratch_shapes=[pltpu.VMEM((SUBLANES, D_XBC), F32), pltpu.VMEM((SUBLANES, D_SC), F32)],
        compiler_params=pltpu.CompilerParams(
            dimension_semantics=("arbitrary",), vmem_limit_bytes=VMEM_LIMIT_BYTES),
    )(x2, *params)


def _mixer_kernel(z_ref, xbc_ref, dt_ref, dtt_ref,
                  dtb_ref, dtbt_ref, alog_ref, alogt_ref, dskip_ref, nw_ref, out_ref, state_ref):
    q = CHUNK

    @pl.when(pl.program_id(1) == 0)
    def _():
        state_ref[...] = jnp.zeros_like(state_ref)

    ri = lax.broadcasted_iota(jnp.int32, (q, q), 0)
    ci = lax.broadcasted_iota(jnp.int32, (q, q), 1)
    causal = ri >= ci
    tri = jnp.where(causal, 1.0, 0.0).astype(BF16)
    trit = jnp.where(ri <= ci, 1.0, 0.0).astype(BF16)
    hh = lax.broadcasted_iota(jnp.int32, (N_SSD_HEADS, D_SSD), 0)
    hl = lax.broadcasted_iota(jnp.int32, (N_SSD_HEADS, D_SSD), 1) // SSD_HEAD_DIM
    expand = jnp.where(hh == hl, 1.0, 0.0).astype(BF16)
    lane_lo = lax.broadcasted_iota(jnp.int32, (q, 2 * SSD_HEAD_DIM), 1) < SSD_HEAD_DIM
    gw = HEADS_PER_GROUP * SSD_HEAD_DIM
    n_bc = N_GROUPS * D_STATE
    a_neg = -jnp.exp(alog_ref[...])
    a_negt = -jnp.exp(alogt_ref[...])
    d_x = _dot_exact_rhs01(dskip_ref[...], expand)

    for s in range(MIX_CHUNKS):
        rows = pl.ds(s * q, q)
        xs = xbc_ref[rows, :D_SSD].astype(F32)
        b_all = xbc_ref[rows, D_SSD:D_SSD + n_bc]
        c_all = xbc_ref[rows, D_SSD + n_bc:]

        dt = _softplus(dt_ref[rows, :] + dtb_ref[...])
        dtt = _softplus(dtt_ref[:, rows] + dtbt_ref[...])
        acum = _spread_rows(tri, dt * a_neg)
        acumt = _spread(dtt * a_negt, trit)
        a_last = acum[q - 1:q, :]

        xdt = xs * _spread(dt, expand)
        xdt_b = xdt.astype(BF16)
        dec_in = _spread(jnp.exp(acum), expand)
        dec_out = _spread(jnp.exp(a_last - acum), expand)
        dec_all = _dot_exact_rhs01(jnp.exp(a_last), expand)
        xw_b = (xdt * dec_out).astype(BF16)

        y_parts = []
        for g in range(N_GROUPS):
            b_g = b_all[:, g * D_STATE:(g + 1) * D_STATE]
            c_g = c_all[:, g * D_STATE:(g + 1) * D_STATE]
            cb = lax.dot_general(c_g, b_g, (((1,), (1,)), ((), ())), preferred_element_type=F32)
            st = state_ref[:, g * gw:(g + 1) * gw]
            y_inter = _dot(c_g, st.astype(BF16)) * dec_in[:, g * gw:(g + 1) * gw]
            new_st = lax.dot_general(b_g, xw_b[:, g * gw:(g + 1) * gw], (((0,), (0,)), ((), ())),
                                     preferred_element_type=F32)
            state_ref[:, g * gw:(g + 1) * gw] = st * dec_all[:, g * gw:(g + 1) * gw] + new_st
            pair_parts = []
            for pr in range(HEADS_PER_GROUP // 2):
                h0 = g * HEADS_PER_GROUP + 2 * pr
                lo = h0 * SSD_HEAD_DIM
                rhs = xdt_b[:, lo:lo + 2 * SSD_HEAD_DIM]
                ys = []
                for h in (h0, h0 + 1):
                    seg = acum[:, h:h + 1] - acumt[h:h + 1, :]
                    m = (cb * jnp.exp(jnp.where(causal, seg, -jnp.inf))).astype(BF16)
                    ys.append(_dot(m, rhs))
                pair_parts.append(jnp.where(lane_lo, ys[0], ys[1]))
            y_parts.append(jnp.concatenate(pair_parts, axis=-1) + y_inter)
        y = jnp.concatenate(y_parts, axis=-1) + d_x * xs

        gated = y * _silu(z_ref[rows, :].astype(F32))
        outs = []
        for g in range(N_GROUPS):
            gg = gated[:, g * gw:(g + 1) * gw]
            outs.append(gg * lax.rsqrt(jnp.mean(gg * gg, axis=-1, keepdims=True) + EPS))
        out_ref[rows, :] = (jnp.concatenate(outs, axis=-1) * nw_ref[...]).astype(out_ref.dtype)


def _mixer(z, xbc, dt, dtt, batch, seq, dtb, alog, dskip, nw):
    rows = CHUNK * MIX_CHUNKS
    assert seq % rows == 0
    ns = seq // rows
    t = batch * seq
    row = lambda n: pl.BlockSpec((rows, n), lambda b, c: (b * ns + c, 0))
    full = lambda a: pl.BlockSpec(a.shape, lambda b, c: (0,) * a.ndim)
    dtbt, alogt = dtb.reshape(N_SSD_HEADS, 1), alog.reshape(N_SSD_HEADS, 1)
    params = (dtb, dtbt, alog, alogt, dskip, nw)
    return pl.pallas_call(
        _mixer_kernel,
        grid=(batch, ns),
        in_specs=[row(D_SSD), row(D_XBC), row(N_SSD_HEADS),
                  pl.BlockSpec((N_SSD_HEADS, rows), lambda b, c: (0, b * ns + c))]
                 + [full(a) for a in params],
        out_specs=row(D_SSD),
        out_shape=jax.ShapeDtypeStruct((t, D_SSD), BF16),
        scratch_shapes=[pltpu.VMEM((D_STATE, D_SSD), F32)],
        compiler_params=pltpu.CompilerParams(
            dimension_semantics=("arbitrary", "arbitrary"), vmem_limit_bytes=VMEM_LIMIT_BYTES),
    )(z, xbc, dt, dtt, *params)


def _outproj_kernel(yssd_ref, ysc_ref, x_ref, wout_ref, nw_ref, wr_ref, br_ref,
                    h1_ref, xn_ref, idx_ref, gate_ref, rank_ref, cnt_ref, run_ref, earlier_ref):
    tm = x_ref.shape[0]

    @pl.when(pl.program_id(0) == 0)
    def _():
        run_ref[...] = jnp.zeros_like(run_ref)
        ri = lax.broadcasted_iota(jnp.int32, (tm, tm), 0)
        ci = lax.broadcasted_iota(jnp.int32, (tm, tm), 1)
        earlier_ref[...] = jnp.where(ri < ci, 1.0, 0.0).astype(BF16)

    h1 = (x_ref[...] + _dot(yssd_ref[...], wout_ref[:D_SSD, :])
          + _dot(ysc_ref[...], wout_ref[D_SSD:, :]))
    h1_ref[...] = h1
    xn = h1 * lax.rsqrt(jnp.mean(h1 * h1, axis=-1, keepdims=True) + EPS) * nw_ref[...]
    for c, words in enumerate(_pack_rows(xn)):
        xn_ref[c] = words

    xh = xn.astype(BF16)
    xl = (xn - xh.astype(F32)).astype(BF16)
    nt = lambda w, a: lax.dot_general(w, a, (((1,), (1,)), ((), ())), preferred_element_type=F32)
    w_x = nt(wr_ref[...], xh)
    logits = (w_x[:N_EXPERTS] + w_x[N_EXPERTS:] + nt(wr_ref[:N_EXPERTS, :], xl)
              + br_ref[...])

    expert = lax.broadcasted_iota(jnp.int32, (N_EXPERTS, tm), 0).astype(F32)
    work = logits
    vals, firsts, sels = [], [], []
    for _ in range(TOP_K):
        m = jnp.max(work, axis=0, keepdims=True)
        first = jnp.min(jnp.where(work == m, expert, float(N_EXPERTS)), axis=0, keepdims=True)
        sel = expert == first
        vals.append(m)
        firsts.append(first)
        sels.append(sel)
        work = jnp.where(sel, -jnp.inf, work)
    exps = [jnp.exp(v - vals[0]) for v in vals]
    denom = exps[0] + exps[1] + exps[2] + exps[3]

    onehot = jnp.zeros((N_EXPERTS, tm), F32)
    for sel in sels:
        onehot = onehot + jnp.where(sel, 1.0, 0.0)
    before = run_ref[...] + _dot(onehot.astype(BF16), earlier_ref[...])
    run_ref[...] = run_ref[...] + jnp.sum(onehot, axis=1, keepdims=True)
    cnt_ref[...] = run_ref[...]

    slot = lax.broadcasted_iota(jnp.int32, (TOP_K, tm), 0)
    idx = jnp.zeros((TOP_K, tm), F32)
    gate = jnp.zeros((TOP_K, tm), F32)
    rank = jnp.zeros((TOP_K, tm), F32)
    for k in range(TOP_K):
        r_k = jnp.sum(jnp.where(sels[k], before, 0.0), axis=0, keepdims=True)
        idx = jnp.where(slot == k, firsts[k], idx)
        gate = jnp.where(slot == k, exps[k] / denom, gate)
        rank = jnp.where(slot == k, r_k, rank)
    idx_ref[...] = idx.astype(jnp.int32)
    gate_ref[...] = gate
    rank_ref[...] = rank.astype(jnp.int32)


def _outproj(yssd, ysc, x2, w_out, nw, wr_split, br, first_tile, t):
    tm = OUT_TILE
    src = lambda n: pl.BlockSpec((tm, n), lambda i: (first_tile + i, 0))
    row = lambda n: pl.BlockSpec((tm, n), lambda i: (i, 0))
    slots = pl.BlockSpec((TOP_K, tm), lambda i: (0, i))
    full = lambda a: pl.BlockSpec(a.shape, lambda i: (0,) * a.ndim)
    return pl.pallas_call(
        _outproj_kernel,
        grid=(t // tm,),
        in_specs=[src(D_SSD), src(D_SC), src(D_MODEL), full(w_out), full(nw), full(wr_split),
                  full(br)],
        out_specs=[row(D_MODEL), pl.BlockSpec((ROW_PARTS, tm, LANES), lambda i: (0, i, 0)),
                   slots, slots, slots,
                   pl.BlockSpec((N_EXPERTS, 1), lambda i: (0, 0))],
        out_shape=[jax.ShapeDtypeStruct((t, D_MODEL), F32),
                   jax.ShapeDtypeStruct((ROW_PARTS, t, LANES), jnp.int32),
                   jax.ShapeDtypeStruct((TOP_K, t), jnp.int32),
                   jax.ShapeDtypeStruct((TOP_K, t), F32),
                   jax.ShapeDtypeStruct((TOP_K, t), jnp.int32),
                   jax.ShapeDtypeStruct((N_EXPERTS, 1), F32)],
        scratch_shapes=[pltpu.VMEM((N_EXPERTS, 1), F32), pltpu.VMEM((tm, tm), BF16)],
        compiler_params=pltpu.CompilerParams(
            dimension_semantics=("arbitrary",), vmem_limit_bytes=VMEM_LIMIT_BYTES),
    )(yssd, ysc, x2, w_out, nw, wr_split, br)


def _dest_kernel(pstart_ref, idx_ref, rank_ref, dest_ref):
    idx = idx_ref[...]
    base = jnp.zeros_like(idx)
    for e in range(N_EXPERTS):
        base = jnp.where(idx == e, pstart_ref[e], base)
    dest_ref[...] = base + rank_ref[...]


def _dest(pstarts, idx, rank):
    full = lambda a: pl.BlockSpec(a.shape, lambda i, ps: (0,) * a.ndim)
    return pl.pallas_call(
        _dest_kernel,
        grid_spec=pltpu.PrefetchScalarGridSpec(
            num_scalar_prefetch=1, grid=(1,),
            in_specs=[full(idx), full(rank)], out_specs=full(idx)),
        out_shape=jax.ShapeDtypeStruct(idx.shape, jnp.int32),
    )(pstarts, idx, rank)


def _sc_mesh():
    return plsc.VectorSubcoreMesh(core_axis_name="core", subcore_axis_name="subcore")


def _part_rows(dest, n_rows):
    off = (jnp.arange(ROW_PARTS, dtype=jnp.int32) * n_rows)[:, None, None]
    return (dest[None] + off).reshape(ROW_PARTS * dest.shape[0], dest.shape[1])


def _sc_dispatch(x, dest, n_rows):
    parts, t, width = x.shape
    n_slots = dest.shape[0]
    win = SC_WINDOW
    nwin = t // win

    @functools.partial(pl.kernel,
                       out_type=jax.ShapeDtypeStruct((parts * n_rows, width), jnp.int32),
                       mesh=_sc_mesh(), scratch_types=[])
    def scatter(x_hbm, idx_hbm, out_hbm):
        def body(x_vmem, idx_vmem):
            for k in range(n_slots):
                pltpu.sync_copy(x_vmem, out_hbm.at[idx_vmem.at[k]])

        pltpu.emit_pipeline(
            body,
            grid=(parts * nwin,),
            in_specs=[pl.BlockSpec((win, width), lambda j: (j, 0)),
                      pl.BlockSpec((n_slots, win), lambda j: (j // nwin, j % nwin))],
            out_specs=[],
            core_axis_name=("core", "subcore"),
            dimension_semantics=(pltpu.PARALLEL,),
        )(x_hbm, idx_hbm)

    xs = scatter(x.reshape(parts * t, width), _part_rows(dest, n_rows))
    return xs.reshape(parts, n_rows, width)


def _sc_collect(ys, dest):
    parts, n_rows, width = ys.shape
    n_slots, t = dest.shape
    win = SC_WINDOW
    nwin = t // win
    idx = _part_rows(dest, n_rows).reshape(parts, n_slots, t).swapaxes(0, 1)
    idx = idx.reshape(n_slots * parts, t)

    @functools.partial(pl.kernel,
                       out_type=jax.ShapeDtypeStruct((n_slots * parts * t, width), jnp.int32),
                       mesh=_sc_mesh(), scratch_types=[])
    def gather(y_hbm, idx_hbm, out_hbm):
        def body(idx_vmem, out_vmem):
            pltpu.sync_copy(y_hbm.at[idx_vmem.at[0]], out_vmem)

        pltpu.emit_pipeline(
            body,
            grid=(n_slots * parts * nwin,),
            in_specs=[pl.BlockSpec((1, win), lambda j: (j // nwin, j % nwin))],
            out_specs=[pl.BlockSpec((win, width), lambda j: (j, 0))],
            core_axis_name=("core", "subcore"),
            dimension_semantics=(pltpu.PARALLEL,),
        )(idx_hbm, out_hbm)

    out = gather(ys.reshape(parts * n_rows, width), idx)
    return out.reshape(n_slots, parts, t, width)


def _expert_kernel(blk_e_ref, blk_n_ref, xs_ref, wgu_ref, bgu_ref, wd_ref, bd_ref, out_ref,
                   wgu_b, wd_b):
    i = pl.program_id(0)
    n_valid = blk_n_ref[i]

    @pl.when((i == 0) | (blk_e_ref[i] != blk_e_ref[jnp.maximum(i - 1, 0)]))
    def _():
        wgu_b[...] = wgu_ref[0].astype(BF16)
        wd_b[...] = wd_ref[0].astype(BF16)

    @pl.when(n_valid == 0)
    def _():
        out_ref[...] = jnp.zeros_like(out_ref)

    @pl.when(n_valid > 0)
    def _():
        for r in range(0, ROW_BLOCK, EXPERT_SUB):
            words = [xs_ref[c, r:r + EXPERT_SUB] for c in range(ROW_PARTS)]
            rows = r + lax.broadcasted_iota(jnp.int32, words[0].shape, 0)
            xb = _unpack_rows([jnp.where(rows < n_valid, w, 0) for w in words]).astype(BF16)
            gu = _dot(xb, wgu_b[...]) + bgu_ref[0]
            gate = jnp.minimum(gu[:, :EXPERT_FF], SWIGLU_LIMIT)
            up = jnp.clip(gu[:, EXPERT_FF:], -SWIGLU_LIMIT, SWIGLU_LIMIT)
            hid = (up + 1.0) * (gate * (1.0 / (1.0 + jnp.exp(-SWIGLU_ALPHA * gate))))
            out = _dot(hid.astype(BF16), wd_b[...]) + bd_ref[0]
            for c, w in enumerate(_pack_rows(out)):
                out_ref[c, r:r + EXPERT_SUB] = w


def _experts(blk_e, blk_n, xs, wgu, bgu, wd, bd):
    rb = ROW_BLOCK
    by_expert = lambda a: pl.BlockSpec((1,) + a.shape[1:], lambda i, be, bn: (be[i], 0, 0))
    return pl.pallas_call(
        _expert_kernel,
        grid_spec=pltpu.PrefetchScalarGridSpec(
            num_scalar_prefetch=2, grid=(xs.shape[1] // rb,),
            in_specs=[pl.BlockSpec((ROW_PARTS, rb, LANES), lambda i, be, bn: (0, i, 0)),
                      by_expert(wgu), by_expert(bgu), by_expert(wd), by_expert(bd)],
            out_specs=pl.BlockSpec((ROW_PARTS, rb, LANES), lambda i, be, bn: (0, i, 0)),
            scratch_shapes=[pltpu.VMEM(wgu.shape[1:], BF16), pltpu.VMEM(wd.shape[1:], BF16)]),
        out_shape=jax.ShapeDtypeStruct(xs.shape, xs.dtype),
        compiler_params=pltpu.CompilerParams(
            dimension_semantics=("arbitrary",), vmem_limit_bytes=VMEM_LIMIT_BYTES),
    )(blk_e, blk_n, xs, wgu, bgu, wd, bd)


def _combine_kernel(gate_ref, h1_ref, fw_ref, y0_ref, y1_ref, y2_ref, y3_ref, *rest):
    out_ref = rest[-1]
    parts = _split3(gate_ref[...])
    slot = lax.broadcasted_iota(jnp.int32, (TOP_K, LANES), 0)
    h2 = h1_ref[...]
    for k, y_ref in enumerate((y0_ref, y1_ref, y2_ref, y3_ref)):
        pick = jnp.where(slot == k, 1.0, 0.0).astype(BF16)
        g = sum(lax.dot_general(p, pick, (((0,), (0,)), ((), ())), preferred_element_type=F32)
                for p in parts)
        y = _unpack_rows([y_ref[c] for c in range(ROW_PARTS)])
        h2 = h2 + jnp.concatenate([g] * (D_MODEL // LANES), axis=-1) * y
    out_ref[...] = h2 * lax.rsqrt(jnp.mean(h2 * h2, axis=-1, keepdims=True) + EPS) * fw_ref[...]


def _combine(gates, h1, fw, planes, first_tile, out_tile, t, out_prev):
    d = h1.shape[1]
    tm = COMBINE_TILE
    n = planes.shape[2]
    plane = lambda k: pl.BlockSpec((None, ROW_PARTS, tm, LANES), lambda i: (k, 0, i, 0))
    in_specs = [pl.BlockSpec((TOP_K, tm), lambda i: (0, first_tile + i)),
                pl.BlockSpec((tm, d), lambda i: (first_tile + i, 0)),
                pl.BlockSpec((1, d), lambda i: (0, 0))]
    in_specs += [plane(k) for k in range(TOP_K)]
    args = [gates, h1, fw] + [planes] * TOP_K
    aliases = {}
    if out_prev is not None:
        in_specs.append(pl.BlockSpec(memory_space=pl.ANY))
        aliases = {len(args): 0}
        args.append(out_prev)
    return pl.pallas_call(
        _combine_kernel,
        grid=(n // tm,),
        in_specs=in_specs,
        out_specs=pl.BlockSpec((tm, d), lambda i: (out_tile + i, 0)),
        out_shape=jax.ShapeDtypeStruct((t, d), F32),
        input_output_aliases=aliases,
        compiler_params=pltpu.CompilerParams(
            dimension_semantics=("arbitrary",), vmem_limit_bytes=VMEM_LIMIT_BYTES),
    )(*args)


def _block_plan(counts, n_blocks):
    rb = ROW_BLOCK
    pcounts = (counts + rb - 1) // rb * rb
    pends = jnp.cumsum(pcounts)
    pstarts = pends - pcounts
    block_start = jnp.arange(n_blocks, dtype=jnp.int32) * rb
    blk_e = jnp.minimum(jnp.sum(block_start[:, None] >= pends[None, :], axis=1),
                        N_EXPERTS - 1).astype(jnp.int32)
    mine = blk_e[:, None] == jnp.arange(N_EXPERTS, dtype=jnp.int32)[None, :]
    pick = lambda table: jnp.sum(jnp.where(mine, table[None, :], 0), axis=1)
    blk_n = jnp.clip(pick(counts) - (block_start - pick(pstarts)), 0, rb).astype(jnp.int32)
    return pstarts.astype(jnp.int32), blk_e, blk_n


def kernel(x, norm1_w, w_in, ssd_conv_w, ssd_conv_b, dt_bias, a_log, d_skip, ssd_norm_w,
           sc_conv_w, sc_norm_w, w_out, norm2_w, w_router, b_router, w_gate_up, b_gate_up,
           w_down, b_down, final_norm_w):
    assert norm1_w.shape[0] == 1, "single-layer problem"
    batch, seq, d = x.shape
    t = batch * seq
    x2 = x.reshape(t, d)
    row = lambda a: a.reshape(1, -1)

    s1, s2 = D_SSD + D_XBC, D_SSD + D_XBC + N_SSD_HEADS
    w = w_in[0]
    w_zx, w_dt, w_sc = w[:, :s1].astype(BF16), w[:, s1:s2].astype(BF16), w[:, s2:].astype(BF16)
    z, xbc, ysc, dt, dtt = _inproj(x2, seq, row(norm1_w[0]), w_zx, w_sc, w_dt, w_dt.T,
                                   ssd_conv_w[0], row(ssd_conv_b[0]), sc_conv_w[0],
                                   row(sc_norm_w[0]))
    yssd = _mixer(z, xbc, dt, dtt, batch, seq, row(dt_bias[0]), row(a_log[0]), row(d_skip[0]),
                  row(ssd_norm_w[0]))

    wr = w_router[0].T
    wr_hi = wr.astype(BF16)
    wr_split = jnp.concatenate([wr_hi, (wr - wr_hi.astype(F32)).astype(BF16)], axis=0)
    unit = t // sum(MOE_SPLIT)
    sizes = [unit * s for s in MOE_SPLIT]
    starts = [sum(sizes[:g]) for g in range(len(sizes))]
    assert sum(sizes) == t and all(n % (COMBINE_CHUNKS * COMBINE_TILE) == 0 and n % OUT_TILE == 0
                                   for n in sizes)
    w_out_b, nw2, br = w_out[0].astype(BF16), row(norm2_w[0]), b_router[0].reshape(-1, 1)
    routed = []
    for t0, tg in zip(starts, sizes):
        n_rows = tg * TOP_K + N_EXPERTS * ROW_BLOCK
        h1, xn2, idx, gates, rank, counts = _outproj(
            yssd, ysc, x2, w_out_b, nw2, wr_split, br, t0 // OUT_TILE, tg)
        pstarts, blk_e, blk_n = _block_plan(counts[:, 0].astype(jnp.int32),
                                            n_rows // ROW_BLOCK)
        dest = _dest(pstarts, idx, rank)
        xs = _sc_dispatch(xn2, dest, n_rows)
        routed.append((h1, gates, dest, blk_e, blk_n, xs))
    expert_out = [_experts(blk_e, blk_n, xs, w_gate_up[0], b_gate_up[0][:, None, :],
                           w_down[0], b_down[0][:, None, :])
                  for (_, _, _, blk_e, blk_n, xs) in routed]
    fw, out = row(final_norm_w), None
    for t0, tg, (h1, gates, dest, _, _, _), ys in zip(starts, sizes, routed, expert_out):
        tc = tg // COMBINE_CHUNKS
        for c in range(COMBINE_CHUNKS):
            planes = _sc_collect(ys, dest[:, c * tc:(c + 1) * tc])
            out = _combine(gates, h1, fw, planes, c * (tc // COMBINE_TILE),
                           (t0 + c * tc) // COMBINE_TILE, t, out)
    return out.reshape(batch, seq, d)
```

```python
import functools

import jax
import jax.numpy as jnp
from jax import lax
from jax.experimental import pallas as pl
from jax.experimental.pallas import tpu as pltpu
from jax.experimental.pallas import tpu_sc as plsc

D_MODEL = 1024
D_SSD = 1024
SSD_HEAD_DIM = 64
N_SSD_HEADS = 16
N_GROUPS = 2
HEADS_PER_GROUP = 8
D_STATE = 128
SSD_CONV = 4
CHUNK = 128
D_XBC = D_SSD + 2 * N_GROUPS * D_STATE
D_SC = 1024
SC_CONV = 3
N_EXPERTS = 32
TOP_K = 4
EXPERT_FF = 1024
SWIGLU_LIMIT = 7.0
SWIGLU_ALPHA = 1.702
EPS = 1e-5

VMEM_LIMIT_BYTES = 56 * 1024 * 1024
SUBLANES = 8
ROW_BLOCK = 512
EXPERT_SUB = 256
IN_TILE = 512
IN_SUB = 256
MIX_CHUNKS = 8
OUT_TILE = 1024
COMBINE_TILE = 512
MOE_SPLIT = (5, 3)
COMBINE_CHUNKS = 2
LANES = 128
ROW_PARTS = D_MODEL // (2 * LANES)
SC_WINDOW = 128
HIGH_HALF = -65536

F32 = jnp.float32
BF16 = jnp.bfloat16


def _dot(a, b):
    return jnp.dot(a, b, preferred_element_type=F32)


def _split3(v):
    hi = v.astype(BF16)
    r1 = v - hi.astype(F32)
    mid = r1.astype(BF16)
    lo = (r1 - mid.astype(F32)).astype(BF16)
    return hi, mid, lo


def _dot_exact_rhs01(v, m01):
    hi, mid, lo = _split3(v)
    return _dot(hi, m01) + _dot(mid, m01) + _dot(lo, m01)


def _split2(v):
    hi = v.astype(BF16)
    return hi, (v - hi.astype(F32)).astype(BF16)


def _spread(v, m01):
    hi, mid = _split2(v)
    return _dot(hi, m01) + _dot(mid, m01)


def _spread_rows(m01, v):
    hi, mid = _split2(v)
    return _dot(m01, hi) + _dot(m01, mid)


def _silu(v):
    return v * (1.0 / (1.0 + jnp.exp(-v)))


def _softplus(v):
    return jnp.maximum(v, 0.0) + jnp.log1p(jnp.exp(-jnp.abs(v)))


def _pack_rows(v):
    bits = lambda a: lax.bitcast_convert_type(a.astype(BF16).astype(F32), jnp.int32)
    parts = []
    for c in range(ROW_PARTS):
        lo = v[:, (2 * c) * LANES:(2 * c + 1) * LANES]
        hi = v[:, (2 * c + 1) * LANES:(2 * c + 2) * LANES]
        parts.append(bits(hi) | lax.shift_right_logical(bits(lo), 16))
    return parts


def _unpack_rows(parts):
    cols = []
    for w in parts:
        cols.append(lax.bitcast_convert_type(w << 16, F32))
        cols.append(lax.bitcast_convert_type(w & HIGH_HALF, F32))
    return jnp.concatenate(cols, axis=-1)


def _shifted(cur, tail, s):
    if s == 0:
        return cur
    rc = pltpu.roll(cur, s, axis=0)
    rt = pltpu.roll(tail, s, axis=0)
    row = lax.broadcasted_iota(jnp.int32, tail.shape, 0)
    first = jnp.where(row < s, rt, rc[0:SUBLANES])
    return jnp.concatenate([first, rc[SUBLANES:]], axis=0)


def _causal_conv(tail_ref, cur, w_ref, acc):
    rows = cur.shape[0]
    n_taps = w_ref.shape[0]
    tail = tail_ref[...]
    for k in range(n_taps):
        acc = acc + w_ref[k:k + 1, :] * _shifted(cur, tail, n_taps - 1 - k)
    tail_ref[...] = cur[rows - SUBLANES:rows]
    return acc


def _inproj_kernel(tiles_per_seq, x_ref, nw_ref, wzx_ref, wsc_ref, wdt_ref, wdtt_ref,
                   cw_ref, cb_ref, scw_ref, scnw_ref,
                   z_ref, xbc_ref, ysc_ref, dt_ref, dtt_ref, hist_ref, schist_ref):
    tm = x_ref.shape[0]

    @pl.when(pl.program_id(0) % tiles_per_seq == 0)
    def _():
        hist_ref[...] = jnp.zeros_like(hist_ref)
        schist_ref[...] = jnp.zeros_like(schist_ref)

    for r in range(0, tm, IN_SUB):
        rows = pl.ds(r, IN_SUB)
        x = x_ref[rows, :]
        xn = x * lax.rsqrt(jnp.mean(x * x, axis=-1, keepdims=True) + EPS) * nw_ref[...]
        xb = xn.astype(BF16)
        proj = lambda w_ref, o, n, xb=xb: _dot(xb, w_ref[:, o:o + n])
        z_ref[rows, :] = proj(wzx_ref, 0, D_SSD).astype(z_ref.dtype)

        conv = _causal_conv(hist_ref, proj(wzx_ref, D_SSD, D_XBC), cw_ref, cb_ref[...])
        xbc_ref[rows, :] = _silu(conv).astype(xbc_ref.dtype)

        cu = proj(wsc_ref, D_SC, D_SC) * proj(wsc_ref, 2 * D_SC, D_SC)
        v = _causal_conv(schist_ref, cu, scw_ref, jnp.zeros_like(cu))
        gv = proj(wsc_ref, 0, D_SC) * v
        y_sc = gv * lax.rsqrt(jnp.mean(gv * gv, axis=-1, keepdims=True) + EPS) * scnw_ref[...]
        ysc_ref[rows, :] = y_sc.astype(ysc_ref.dtype)

        dt_ref[rows, :] = _dot(xb, wdt_ref[...])
        dtt_ref[:, rows] = lax.dot_general(wdtt_ref[...], xb, (((1,), (1,)), ((), ())),
                                           preferred_element_type=F32)


def _inproj(x2, seq, nw, w_zx, w_sc, w_dt, w_dtt, cw, cb, scw, scnw):
    t = x2.shape[0]
    tm = IN_TILE
    assert seq % tm == 0, "in-proj tiles must not straddle sequences (causal conv carry)"
    row = lambda n: pl.BlockSpec((tm, n), lambda i: (i, 0))
    full = lambda a: pl.BlockSpec(a.shape, lambda i: (0,) * a.ndim)
    params = (nw, w_zx, w_sc, w_dt, w_dtt, cw, cb, scw, scnw)
    return pl.pallas_call(
        functools.partial(_inproj_kernel, seq // tm),
        grid=(t // tm,),
        in_specs=[row(D_MODEL)] + [full(a) for a in params],
        out_specs=[row(D_SSD), row(D_XBC), row(D_SC),
                   row(N_SSD_HEADS), pl.BlockSpec((N_SSD_HEADS, tm), lambda i: (0, i))],
        out_shape=[jax.ShapeDtypeStruct((t, D_SSD), BF16),
                   jax.ShapeDtypeStruct((t, D_XBC), BF16),
                   jax.ShapeDtypeStruct((t, D_SC), BF16),
                   jax.ShapeDtypeStruct((t, N_SSD_HEADS), F32),
                   jax.ShapeDtypeStruct((N_SSD_HEADS, t), F32)],
        scratch_shapes=[pltpu.VMEM((SUBLANES, D_XBC), F32), pltpu.VMEM((SUBLANES, D_SC), F32)],
        compiler_params=pltpu.CompilerParams(
            dimension_semantics=("arbitrary",), vmem_limit_bytes=VMEM_LIMIT_BYTES),
    )(x2, *params)


def _mixer_kernel(z_ref, xbc_ref, dt_ref, dtt_ref,
                  dtb_ref, dtbt_ref, alog_ref, alogt_ref, dskip_ref, nw_ref, out_ref, state_ref):
    q = CHUNK

    @pl.when(pl.program_id(1) == 0)
    def _():
        state_ref[...] = jnp.zeros_like(state_ref)

    ri = lax.broadcasted_iota(jnp.int32, (q, q), 0)
    ci = lax.broadcasted_iota(jnp.int32, (q, q), 1)
    causal = ri >= ci
    tri = jnp.where(causal, 1.0, 0.0).astype(BF16)
    trit = jnp.where(ri <= ci, 1.0, 0.0).astype(BF16)
    hh = lax.broadcasted_iota(jnp.int32, (N_SSD_HEADS, D_SSD), 0)
    hl = lax.broadcasted_iota(jnp.int32, (N_SSD_HEADS, D_SSD), 1) // SSD_HEAD_DIM
    expand = jnp.where(hh == hl, 1.0, 0.0).astype(BF16)
    lane_lo = lax.broadcasted_iota(jnp.int32, (q, 2 * SSD_HEAD_DIM), 1) < SSD_HEAD_DIM
    gw = HEADS_PER_GROUP * SSD_HEAD_DIM
    n_bc = N_GROUPS * D_STATE
    a_neg = -jnp.exp(alog_ref[...])
    a_negt = -jnp.exp(alogt_ref[...])
    d_x = _dot_exact_rhs01(dskip_ref[...], expand)

    for s in range(MIX_CHUNKS):
        rows = pl.ds(s * q, q)
        xs = xbc_ref[rows, :D_SSD].astype(F32)
        b_all = xbc_ref[rows, D_SSD:D_SSD + n_bc]
        c_all = xbc_ref[rows, D_SSD + n_bc:]

        dt = _softplus(dt_ref[rows, :] + dtb_ref[...])
        dtt = _softplus(dtt_ref[:, rows] + dtbt_ref[...])
        acum = _spread_rows(tri, dt * a_neg)
        acumt = _spread(dtt * a_negt, trit)
        a_last = acum[q - 1:q, :]

        xdt = xs * _spread(dt, expand)
        xdt_b = xdt.astype(BF16)
        dec_in = _spread(jnp.exp(acum), expand)
        dec_out = _spread(jnp.exp(a_last - acum), expand)
        dec_all = _dot_exact_rhs01(jnp.exp(a_last), expand)
        xw_b = (xdt * dec_out).astype(BF16)

        y_parts = []
        for g in range(N_GROUPS):
            b_g = b_all[:, g * D_STATE:(g + 1) * D_STATE]
            c_g = c_all[:, g * D_STATE:(g + 1) * D_STATE]
            cb = lax.dot_general(c_g, b_g, (((1,), (1,)), ((), ())), preferred_element_type=F32)
            st = state_ref[:, g * gw:(g + 1) * gw]
            y_inter = _dot(c_g, st.astype(BF16)) * dec_in[:, g * gw:(g + 1) * gw]
            new_st = lax.dot_general(b_g, xw_b[:, g * gw:(g + 1) * gw], (((0,), (0,)), ((), ())),
                                     preferred_element_type=F32)
            state_ref[:, g * gw:(g + 1) * gw] = st * dec_all[:, g * gw:(g + 1) * gw] + new_st
            pair_parts = []
            for pr in range(HEADS_PER_GROUP // 2):
                h0 = g * HEADS_PER_GROUP + 2 * pr
                lo = h0 * SSD_HEAD_DIM
                rhs = xdt_b[:, lo:lo + 2 * SSD_HEAD_DIM]
                ys = []
                for h in (h0, h0 + 1):
                    seg = acum[:, h:h + 1] - acumt[h:h + 1, :]
                    m = (cb * jnp.exp(jnp.where(causal, seg, -jnp.inf))).astype(BF16)
                    ys.append(_dot(m, rhs))
                pair_parts.append(jnp.where(lane_lo, ys[0], ys[1]))
            y_parts.append(jnp.concatenate(pair_parts, axis=-1) + y_inter)
        y = jnp.concatenate(y_parts, axis=-1) + d_x * xs

        gated = y * _silu(z_ref[rows, :].astype(F32))
        outs = []
        for g in range(N_GROUPS):
            gg = gated[:, g * gw:(g + 1) * gw]
            outs.append(gg * lax.rsqrt(jnp.mean(gg * gg, axis=-1, keepdims=True) + EPS))
        out_ref[rows, :] = (jnp.concatenate(outs, axis=-1) * nw_ref[...]).astype(out_ref.dtype)


def _mixer(z, xbc, dt, dtt, batch, seq, dtb, alog, dskip, nw):
    rows = CHUNK * MIX_CHUNKS
    assert seq % rows == 0
    ns = seq // rows
    t = batch * seq
    row = lambda n: pl.BlockSpec((rows, n), lambda b, c: (b * ns + c, 0))
    full = lambda a: pl.BlockSpec(a.shape, lambda b, c: (0,) * a.ndim)
    dtbt, alogt = dtb.reshape(N_SSD_HEADS, 1), alog.reshape(N_SSD_HEADS, 1)
    params = (dtb, dtbt, alog, alogt, dskip, nw)
    return pl.pallas_call(
        _mixer_kernel,
        grid=(batch, ns),
        in_specs=[row(D_SSD), row(D_XBC), row(N_SSD_HEADS),
                  pl.BlockSpec((N_SSD_HEADS, rows), lambda b, c: (0, b * ns + c))]
                 + [full(a) for a in params],
        out_specs=row(D_SSD),
        out_shape=jax.ShapeDtypeStruct((t, D_SSD), BF16),
        scratch_shapes=[pltpu.VMEM((D_STATE, D_SSD), F32)],
        compiler_params=pltpu.CompilerParams(
            dimension_semantics=("arbitrary", "arbitrary"), vmem_limit_bytes=VMEM_LIMIT_BYTES),
    )(z, xbc, dt, dtt, *params)


def _outproj_kernel(yssd_ref, ysc_ref, x_ref, wout_ref, nw_ref, wr_ref, br_ref,
                    h1_ref, xn_ref, idx_ref, gate_ref, rank_ref, cnt_ref, run_ref, earlier_ref):
    tm = x_ref.shape[0]

    @pl.when(pl.program_id(0) == 0)
    def _():
        run_ref[...] = jnp.zeros_like(run_ref)
        ri = lax.broadcasted_iota(jnp.int32, (tm, tm), 0)
        ci = lax.broadcasted_iota(jnp.int32, (tm, tm), 1)
        earlier_ref[...] = jnp.where(ri < ci, 1.0, 0.0).astype(BF16)

    h1 = (x_ref[...] + _dot(yssd_ref[...], wout_ref[:D_SSD, :])
          + _dot(ysc_ref[...], wout_ref[D_SSD:, :]))
    h1_ref[...] = h1
    xn = h1 * lax.rsqrt(jnp.mean(h1 * h1, axis=-1, keepdims=True) + EPS) * nw_ref[...]
    for c, words in enumerate(_pack_rows(xn)):
        xn_ref[c] = words

    xh = xn.astype(BF16)
    xl = (xn - xh.astype(F32)).astype(BF16)
    nt = lambda w, a: lax.dot_general(w, a, (((1,), (1,)), ((), ())), preferred_element_type=F32)
    w_x = nt(wr_ref[...], xh)
    logits = (w_x[:N_EXPERTS] + w_x[N_EXPERTS:] + nt(wr_ref[:N_EXPERTS, :], xl)
              + br_ref[...])

    expert = lax.broadcasted_iota(jnp.int32, (N_EXPERTS, tm), 0).astype(F32)
    work = logits
    vals, firsts, sels = [], [], []
    for _ in range(TOP_K):
        m = jnp.max(work, axis=0, keepdims=True)
        first = jnp.min(jnp.where(work == m, expert, float(N_EXPERTS)), axis=0, keepdims=True)
        sel = expert == first
        vals.append(m)
        firsts.append(first)
        sels.append(sel)
        work = jnp.where(sel, -jnp.inf, work)
    exps = [jnp.exp(v - vals[0]) for v in vals]
    denom = exps[0] + exps[1] + exps[2] + exps[3]

    onehot = jnp.zeros((N_EXPERTS, tm), F32)
    for sel in sels:
        onehot = onehot + jnp.where(sel, 1.0, 0.0)
    before = run_ref[...] + _dot(onehot.astype(BF16), earlier_ref[...])
    run_ref[...] = run_ref[...] + jnp.sum(onehot, axis=1, keepdims=True)
    cnt_ref[...] = run_ref[...]

    slot = lax.broadcasted_iota(jnp.int32, (TOP_K, tm), 0)
    idx = jnp.zeros((TOP_K, tm), F32)
    gate = jnp.zeros((TOP_K, tm), F32)
    rank = jnp.zeros((TOP_K, tm), F32)
    for k in range(TOP_K):
        r_k = jnp.sum(jnp.where(sels[k], before, 0.0), axis=0, keepdims=True)
        idx = jnp.where(slot == k, firsts[k], idx)
        gate = jnp.where(slot == k, exps[k] / denom, gate)
        rank = jnp.where(slot == k, r_k, rank)
    idx_ref[...] = idx.astype(jnp.int32)
    gate_ref[...] = gate
    rank_ref[...] = rank.astype(jnp.int32)


def _outproj(yssd, ysc, x2, w_out, nw, wr_split, br, first_tile, t):
    tm = OUT_TILE
    src = lambda n: pl.BlockSpec((tm, n), lambda i: (first_tile + i, 0))
    row = lambda n: pl.BlockSpec((tm, n), lambda i: (i, 0))
    slots = pl.BlockSpec((TOP_K, tm), lambda i: (0, i))
    full = lambda a: pl.BlockSpec(a.shape, lambda i: (0,) * a.ndim)
    return pl.pallas_call(
        _outproj_kernel,
        grid=(t // tm,),
        in_specs=[src(D_SSD), src(D_SC), src(D_MODEL), full(w_out), full(nw), full(wr_split),
                  full(br)],
        out_specs=[row(D_MODEL), pl.BlockSpec((ROW_PARTS, tm, LANES), lambda i: (0, i, 0)),
                   slots, slots, slots,
                   pl.BlockSpec((N_EXPERTS, 1), lambda i: (0, 0))],
        out_shape=[jax.ShapeDtypeStruct((t, D_MODEL), F32),
                   jax.ShapeDtypeStruct((ROW_PARTS, t, LANES), jnp.int32),
                   jax.ShapeDtypeStruct((TOP_K, t), jnp.int32),
                   jax.ShapeDtypeStruct((TOP_K, t), F32),
                   jax.ShapeDtypeStruct((TOP_K, t), jnp.int32),
                   jax.ShapeDtypeStruct((N_EXPERTS, 1), F32)],
        scratch_shapes=[pltpu.VMEM((N_EXPERTS, 1), F32), pltpu.VMEM((tm, tm), BF16)],
        compiler_params=pltpu.CompilerParams(
            dimension_semantics=("arbitrary",), vmem_limit_bytes=VMEM_LIMIT_BYTES),
    )(yssd, ysc, x2, w_out, nw, wr_split, br)


def _dest_kernel(pstart_ref, idx_ref, rank_ref, dest_ref):
    idx = idx_ref[...]
    base = jnp.zeros_like(idx)
    for e in range(N_EXPERTS):
        base = jnp.where(idx == e, pstart_ref[e], base)
    dest_ref[...] = base + rank_ref[...]


def _dest(pstarts, idx, rank):
    full = lambda a: pl.BlockSpec(a.shape, lambda i, ps: (0,) * a.ndim)
    return pl.pallas_call(
        _dest_kernel,
        grid_spec=pltpu.PrefetchScalarGridSpec(
            num_scalar_prefetch=1, grid=(1,),
            in_specs=[full(idx), full(rank)], out_specs=full(idx)),
        out_shape=jax.ShapeDtypeStruct(idx.shape, jnp.int32),
    )(pstarts, idx, rank)


def _sc_mesh():
    return plsc.VectorSubcoreMesh(core_axis_name="core", subcore_axis_name="subcore")


def _part_rows(dest, n_rows):
    off = (jnp.arange(ROW_PARTS, dtype=jnp.int32) * n_rows)[:, None, None]
    return (dest[None] + off).reshape(ROW_PARTS * dest.shape[0], dest.shape[1])


def _sc_dispatch(x, dest, n_rows):
    parts, t, width = x.shape
    n_slots = dest.shape[0]
    win = SC_WINDOW
    nwin = t // win

    @functools.partial(pl.kernel,
                       out_type=jax.ShapeDtypeStruct((parts * n_rows, width), jnp.int32),
                       mesh=_sc_mesh(), scratch_types=[])
    def scatter(x_hbm, idx_hbm, out_hbm):
        def body(x_vmem, idx_vmem):
            for k in range(n_slots):
                pltpu.sync_copy(x_vmem, out_hbm.at[idx_vmem.at[k]])

        pltpu.emit_pipeline(
            body,
            grid=(parts * nwin,),
            in_specs=[pl.BlockSpec((win, width), lambda j: (j, 0)),
                      pl.BlockSpec((n_slots, win), lambda j: (j // nwin, j % nwin))],
            out_specs=[],
            core_axis_name=("core", "subcore"),
            dimension_semantics=(pltpu.PARALLEL,),
        )(x_hbm, idx_hbm)

    xs = scatter(x.reshape(parts * t, width), _part_rows(dest, n_rows))
    return xs.reshape(parts, n_rows, width)


def _sc_collect(ys, dest):
    parts, n_rows, width = ys.shape
    n_slots, t = dest.shape
    win = SC_WINDOW
    nwin = t // win
    idx = _part_rows(dest, n_rows).reshape(parts, n_slots, t).swapaxes(0, 1)
    idx = idx.reshape(n_slots * parts, t)

    @functools.partial(pl.kernel,
                       out_type=jax.ShapeDtypeStruct((n_slots * parts * t, width), jnp.int32),
                       mesh=_sc_mesh(), scratch_types=[])
    def gather(y_hbm, idx_hbm, out_hbm):
        def body(idx_vmem, out_vmem):
            pltpu.sync_copy(y_hbm.at[idx_vmem.at[0]], out_vmem)

        pltpu.emit_pipeline(
            body,
            grid=(n_slots * parts * nwin,),
            in_specs=[pl.BlockSpec((1, win), lambda j: (j // nwin, j % nwin))],
            out_specs=[pl.BlockSpec((win, width), lambda j: (j, 0))],
            core_axis_name=("core", "subcore"),
            dimension_semantics=(pltpu.PARALLEL,),
        )(idx_hbm, out_hbm)

    out = gather(ys.reshape(parts * n_rows, width), idx)
    return out.reshape(n_slots, parts, t, width)


def _expert_kernel(blk_e_ref, blk_n_ref, xs_ref, wgu_ref, bgu_ref, wd_ref, bd_ref, out_ref,
                   wgu_b, wd_b):
    i = pl.program_id(0)
    n_valid = blk_n_ref[i]

    @pl.when((i == 0) | (blk_e_ref[i] != blk_e_ref[jnp.maximum(i - 1, 0)]))
    def _():
        wgu_b[...] = wgu_ref[0].astype(BF16)
        wd_b[...] = wd_ref[0].astype(BF16)

    @pl.when(n_valid == 0)
    def _():
        out_ref[...] = jnp.zeros_like(out_ref)

    @pl.when(n_valid > 0)
    def _():
        for r in range(0, ROW_BLOCK, EXPERT_SUB):
            words = [xs_ref[c, r:r + EXPERT_SUB] for c in range(ROW_PARTS)]
            rows = r + lax.broadcasted_iota(jnp.int32, words[0].shape, 0)
            xb = _unpack_rows([jnp.where(rows < n_valid, w, 0) for w in words]).astype(BF16)
            gu = _dot(xb, wgu_b[...]) + bgu_ref[0]
            gate = jnp.minimum(gu[:, :EXPERT_FF], SWIGLU_LIMIT)
            up = jnp.clip(gu[:, EXPERT_FF:], -SWIGLU_LIMIT, SWIGLU_LIMIT)
            hid = (up + 1.0) * (gate * (1.0 / (1.0 + jnp.exp(-SWIGLU_ALPHA * gate))))
            out = _dot(hid.astype(BF16), wd_b[...]) + bd_ref[0]
            for c, w in enumerate(_pack_rows(out)):
                out_ref[c, r:r + EXPERT_SUB] = w


def _experts(blk_e, blk_n, xs, wgu, bgu, wd, bd):
    rb = ROW_BLOCK
    by_expert = lambda a: pl.BlockSpec((1,) + a.shape[1:], lambda i, be, bn: (be[i], 0, 0))
    return pl.pallas_call(
        _expert_kernel,
        grid_spec=pltpu.PrefetchScalarGridSpec(
            num_scalar_prefetch=2, grid=(xs.shape[1] // rb,),
            in_specs=[pl.BlockSpec((ROW_PARTS, rb, LANES), lambda i, be, bn: (0, i, 0)),
                      by_expert(wgu), by_expert(bgu), by_expert(wd), by_expert(bd)],
            out_specs=pl.BlockSpec((ROW_PARTS, rb, LANES), lambda i, be, bn: (0, i, 0)),
            scratch_shapes=[pltpu.VMEM(wgu.shape[1:], BF16), pltpu.VMEM(wd.shape[1:], BF16)]),
        out_shape=jax.ShapeDtypeStruct(xs.shape, xs.dtype),
        compiler_params=pltpu.CompilerParams(
            dimension_semantics=("arbitrary",), vmem_limit_bytes=VMEM_LIMIT_BYTES),
    )(blk_e, blk_n, xs, wgu, bgu, wd, bd)


def _combine_kernel(gate_ref, h1_ref, fw_ref, y0_ref, y1_ref, y2_ref, y3_ref, *rest):
    out_ref = rest[-1]
    parts = _split3(gate_ref[...])
    slot = lax.broadcasted_iota(jnp.int32, (TOP_K, LANES), 0)
    h2 = h1_ref[...]
    for k, y_ref in enumerate((y0_ref, y1_ref, y2_ref, y3_ref)):
        pick = jnp.where(slot == k, 1.0, 0.0).astype(BF16)
        g = sum(lax.dot_general(p, pick, (((0,), (0,)), ((), ())), preferred_element_type=F32)
                for p in parts)
        y = _unpack_rows([y_ref[c] for c in range(ROW_PARTS)])
        h2 = h2 + jnp.concatenate([g] * (D_MODEL // LANES), axis=-1) * y
    out_ref[...] = h2 * lax.rsqrt(jnp.mean(h2 * h2, axis=-1, keepdims=True) + EPS) * fw_ref[...]


def _combine(gates, h1, fw, planes, first_tile, out_tile, t, out_prev):
    d = h1.shape[1]
    tm = COMBINE_TILE
    n = planes.shape[2]
    plane = lambda k: pl.BlockSpec((None, ROW_PARTS, tm, LANES), lambda i: (k, 0, i, 0))
    in_specs = [pl.BlockSpec((TOP_K, tm), lambda i: (0, first_tile + i)),
                pl.BlockSpec((tm, d), lambda i: (first_tile + i, 0)),
                pl.BlockSpec((1, d), lambda i: (0, 0))]
    in_specs += [plane(k) for k in range(TOP_K)]
    args = [gates, h1, fw] + [planes] * TOP_K
    aliases = {}
    if out_prev is not None:
        in_specs.append(pl.BlockSpec(memory_space=pl.ANY))
        aliases = {len(args): 0}
        args.append(out_prev)
    return pl.pallas_call(
        _combine_kernel,
        grid=(n // tm,),
        in_specs=in_specs,
        out_specs=pl.BlockSpec((tm, d), lambda i: (out_tile + i, 0)),
        out_shape=jax.ShapeDtypeStruct((t, d), F32),
        input_output_aliases=aliases,
        compiler_params=pltpu.CompilerParams(
            dimension_semantics=("arbitrary",), vmem_limit_bytes=VMEM_LIMIT_BYTES),
    )(*args)


def _block_plan(counts, n_blocks):
    rb = ROW_BLOCK
    pcounts = (counts + rb - 1) // rb * rb
    pends = jnp.cumsum(pcounts)
    pstarts = pends - pcounts
    block_start = jnp.arange(n_blocks, dtype=jnp.int32) * rb
    blk_e = jnp.minimum(jnp.sum(block_start[:, None] >= pends[None, :], axis=1),
                        N_EXPERTS - 1).astype(jnp.int32)
    mine = blk_e[:, None] == jnp.arange(N_EXPERTS, dtype=jnp.int32)[None, :]
    pick = lambda table: jnp.sum(jnp.where(mine, table[None, :], 0), axis=1)
    blk_n = jnp.clip(pick(counts) - (block_start - pick(pstarts)), 0, rb).astype(jnp.int32)
    return pstarts.astype(jnp.int32), blk_e, blk_n


def kernel(x, norm1_w, w_in, ssd_conv_w, ssd_conv_b, dt_bias, a_log, d_skip, ssd_norm_w,
           sc_conv_w, sc_norm_w, w_out, norm2_w, w_router, b_router, w_gate_up, b_gate_up,
           w_down, b_down, final_norm_w):
    assert norm1_w.shape[0] == 1, "single-layer problem"
    batch, seq, d = x.shape
    t = batch * seq
    x2 = x.reshape(t, d)
    row = lambda a: a.reshape(1, -1)

    s1, s2 = D_SSD + D_XBC, D_SSD + D_XBC + N_SSD_HEADS
    w = w_in[0]
    w_zx, w_dt, w_sc = w[:, :s1].astype(BF16), w[:, s1:s2].astype(BF16), w[:, s2:].astype(BF16)
    z, xbc, ysc, dt, dtt = _inproj(x2, seq, row(norm1_w[0]), w_zx, w_sc, w_dt, w_dt.T,
                                   ssd_conv_w[0], row(ssd_conv_b[0]), sc_conv_w[0],
                                   row(sc_norm_w[0]))
    yssd = _mixer(z, xbc, dt, dtt, batch, seq, row(dt_bias[0]), row(a_log[0]), row(d_skip[0]),
                  row(ssd_norm_w[0]))

    wr = w_router[0].T
    wr_hi = wr.astype(BF16)
    wr_split = jnp.concatenate([wr_hi, (wr - wr_hi.astype(F32)).astype(BF16)], axis=0)
    unit = t // sum(MOE_SPLIT)
    sizes = [unit * s for s in MOE_SPLIT]
    starts = [sum(sizes[:g]) for g in range(len(sizes))]
    assert sum(sizes) == t and all(n % (COMBINE_CHUNKS * COMBINE_TILE) == 0 and n % OUT_TILE == 0
                                   for n in sizes)
    w_out_b, nw2, br = w_out[0].astype(BF16), row(norm2_w[0]), b_router[0].reshape(-1, 1)
    routed = []
    for t0, tg in zip(starts, sizes):
        n_rows = tg * TOP_K + N_EXPERTS * ROW_BLOCK
        h1, xn2, idx, gates, rank, counts = _outproj(
            yssd, ysc, x2, w_out_b, nw2, wr_split, br, t0 // OUT_TILE, tg)
        pstarts, blk_e, blk_n = _block_plan(counts[:, 0].astype(jnp.int32),
                                            n_rows // ROW_BLOCK)
        dest = _dest(pstarts, idx, rank)
        xs = _sc_dispatch(xn2, dest, n_rows)
        routed.append((h1, gates, dest, blk_e, blk_n, xs))
    expert_out = [_experts(blk_e, blk_n, xs, w_gate_up[0], b_gate_up[0][:, None, :],
                           w_down[0], b_down[0][:, None, :])
                  for (_, _, _, blk_e, blk_n, xs) in routed]
    fw, out = row(final_norm_w), None
    for t0, tg, (h1, gates, dest, _, _, _), ys in zip(starts, sizes, routed, expert_out):
        tc = tg // COMBINE_CHUNKS
        for c in range(COMBINE_CHUNKS):
            planes = _sc_collect(ys, dest[:, c * tc:(c + 1) * tc])
            out = _combine(gates, h1, fw, planes, c * (tc // COMBINE_TILE),
                           (t0 + c * tc) // COMBINE_TILE, t, out)
    return out.reshape(batch, seq, d)
```

```python
import functools

import jax
import jax.numpy as jnp
from jax import lax
from jax.experimental import pallas as pl
from jax.experimental.pallas import tpu as pltpu
from jax.experimental.pallas import tpu_sc as plsc

D_MODEL = 1024
D_SSD = 1024
SSD_HEAD_DIM = 64
N_SSD_HEADS = 16
N_GROUPS = 2
HEADS_PER_GROUP = 8
D_STATE = 128
SSD_CONV = 4
CHUNK = 128
D_XBC = D_SSD + 2 * N_GROUPS * D_STATE
D_SC = 1024
SC_CONV = 3
N_EXPERTS = 32
TOP_K = 4
EXPERT_FF = 1024
SWIGLU_LIMIT = 7.0
SWIGLU_ALPHA = 1.702
EPS = 1e-5

VMEM_LIMIT_BYTES = 56 * 1024 * 1024
SUBLANES = 8
ROW_BLOCK = 512
EXPERT_SUB = 256
IN_TILE = 512
IN_SUB = 256
MIX_CHUNKS = 8
OUT_TILE = 1024
COMBINE_TILE = 512
MOE_SPLIT = (5, 3)
COMBINE_CHUNKS = 1
LANES = 128
ROW_PARTS = D_MODEL // (2 * LANES)
SC_WINDOW = 128
HIGH_HALF = -65536

F32 = jnp.float32
BF16 = jnp.bfloat16


def _dot(a, b):
    return jnp.dot(a, b, preferred_element_type=F32)


def _split3(v):
    hi = v.astype(BF16)
    r1 = v - hi.astype(F32)
    mid = r1.astype(BF16)
    lo = (r1 - mid.astype(F32)).astype(BF16)
    return hi, mid, lo


def _dot_exact_rhs01(v, m01):
    hi, mid, lo = _split3(v)
    return _dot(hi, m01) + _dot(mid, m01) + _dot(lo, m01)


def _split2(v):
    hi = v.astype(BF16)
    return hi, (v - hi.astype(F32)).astype(BF16)


def _spread(v, m01):
    hi, mid = _split2(v)
    return _dot(hi, m01) + _dot(mid, m01)


def _spread_rows(m01, v):
    hi, mid = _split2(v)
    return _dot(m01, hi) + _dot(m01, mid)


def _silu(v):
    return v * (1.0 / (1.0 + jnp.exp(-v)))


def _softplus(v):
    return jnp.maximum(v, 0.0) + jnp.log1p(jnp.exp(-jnp.abs(v)))


def _pack_rows(v):
    bits = lambda a: lax.bitcast_convert_type(a.astype(BF16).astype(F32), jnp.int32)
    parts = []
    for c in range(ROW_PARTS):
        lo = v[:, (2 * c) * LANES:(2 * c + 1) * LANES]
        hi = v[:, (2 * c + 1) * LANES:(2 * c + 2) * LANES]
        parts.append(bits(hi) | lax.shift_right_logical(bits(lo), 16))
    return parts


def _unpack_rows(parts):
    cols = []
    for w in parts:
        cols.append(lax.bitcast_convert_type(w << 16, F32))
        cols.append(lax.bitcast_convert_type(w & HIGH_HALF, F32))
    return jnp.concatenate(cols, axis=-1)


def _shifted(cur, tail, s):
    if s == 0:
        return cur
    rc = pltpu.roll(cur, s, axis=0)
    rt = pltpu.roll(tail, s, axis=0)
    row = lax.broadcasted_iota(jnp.int32, tail.shape, 0)
    first = jnp.where(row < s, rt, rc[0:SUBLANES])
    return jnp.concatenate([first, rc[SUBLANES:]], axis=0)


def _causal_conv(tail_ref, cur, w_ref, acc):
    rows = cur.shape[0]
    n_taps = w_ref.shape[0]
    tail = tail_ref[...]
    for k in range(n_taps):
        acc = acc + w_ref[k:k + 1, :] * _shifted(cur, tail, n_taps - 1 - k)
    tail_ref[...] = cur[rows - SUBLANES:rows]
    return acc


def _inproj_kernel(tiles_per_seq, x_ref, nw_ref, wzx_ref, wsc_ref, wdt_ref, wdtt_ref,
                   cw_ref, cb_ref, scw_ref, scnw_ref,
                   z_ref, xbc_ref, ysc_ref, dt_ref, dtt_ref, hist_ref, schist_ref):
    tm = x_ref.shape[0]

    @pl.when(pl.program_id(0) % tiles_per_seq == 0)
    def _():
        hist_ref[...] = jnp.zeros_like(hist_ref)
        schist_ref[...] = jnp.zeros_like(schist_ref)

    for r in range(0, tm, IN_SUB):
        rows = pl.ds(r, IN_SUB)
        x = x_ref[rows, :]
        xn = x * lax.rsqrt(jnp.mean(x * x, axis=-1, keepdims=True) + EPS) * nw_ref[...]
        xb = xn.astype(BF16)
        proj = lambda w_ref, o, n, xb=xb: _dot(xb, w_ref[:, o:o + n])
        z_ref[rows, :] = proj(wzx_ref, 0, D_SSD).astype(z_ref.dtype)

        conv = _causal_conv(hist_ref, proj(wzx_ref, D_SSD, D_XBC), cw_ref, cb_ref[...])
        xbc_ref[rows, :] = _silu(conv).astype(xbc_ref.dtype)

        cu = proj(wsc_ref, D_SC, D_SC) * proj(wsc_ref, 2 * D_SC, D_SC)
        v = _causal_conv(schist_ref, cu, scw_ref, jnp.zeros_like(cu))
        gv = proj(wsc_ref, 0, D_SC) * v
        y_sc = gv * lax.rsqrt(jnp.mean(gv * gv, axis=-1, keepdims=True) + EPS) * scnw_ref[...]
        ysc_ref[rows, :] = y_sc.astype(ysc_ref.dtype)

        dt_ref[rows, :] = _dot(xb, wdt_ref[...])
        dtt_ref[:, rows] = lax.dot_general(wdtt_ref[...], xb, (((1,), (1,)), ((), ())),
                                           preferred_element_type=F32)


def _inproj(x2, seq, nw, w_zx, w_sc, w_dt, w_dtt, cw, cb, scw, scnw):
    t = x2.shape[0]
    tm = IN_TILE
    assert seq % tm == 0, "in-proj tiles must not straddle sequences (causal conv carry)"
    row = lambda n: pl.BlockSpec((tm, n), lambda i: (i, 0))
    full = lambda a: pl.BlockSpec(a.shape, lambda i: (0,) * a.ndim)
    params = (nw, w_zx, w_sc, w_dt, w_dtt, cw, cb, scw, scnw)
    return pl.pallas_call(
        functools.partial(_inproj_kernel, seq // tm),
        grid=(t // tm,),
        in_specs=[row(D_MODEL)] + [full(a) for a in params],
        out_specs=[row(D_SSD), row(D_XBC), row(D_SC),
                   row(N_SSD_HEADS), pl.BlockSpec((N_SSD_HEADS, tm), lambda i: (0, i))],
        out_shape=[jax.ShapeDtypeStruct((t, D_SSD), BF16),
                   jax.ShapeDtypeStruct((t, D_XBC), BF16),
                   jax.ShapeDtypeStruct((t, D_SC), BF16),
                   jax.ShapeDtypeStruct((t, N_SSD_HEADS), F32),
                   jax.ShapeDtypeStruct((N_SSD_HEADS, t), F32)],
        scratch_shapes=[pltpu.VMEM((SUBLANES, D_XBC), F32), pltpu.VMEM((SUBLANES, D_SC), F32)],
        compiler_params=pltpu.CompilerParams(
            dimension_semantics=("arbitrary",), vmem_limit_bytes=VMEM_LIMIT_BYTES),
    )(x2, *params)


def _mixer_kernel(z_ref, xbc_ref, dt_ref, dtt_ref,
                  dtb_ref, dtbt_ref, alog_ref, alogt_ref, dskip_ref, nw_ref, out_ref, state_ref):
    q = CHUNK

    @pl.when(pl.program_id(1) == 0)
    def _():
        state_ref[...] = jnp.zeros_like(state_ref)

    ri = lax.broadcasted_iota(jnp.int32, (q, q), 0)
    ci = lax.broadcasted_iota(jnp.int32, (q, q), 1)
    causal = ri >= ci
    tri = jnp.where(causal, 1.0, 0.0).astype(BF16)
    trit = jnp.where(ri <= ci, 1.0, 0.0).astype(BF16)
    hh = lax.broadcasted_iota(jnp.int32, (N_SSD_HEADS, D_SSD), 0)
    hl = lax.broadcasted_iota(jnp.int32, (N_SSD_HEADS, D_SSD), 1) // SSD_HEAD_DIM
    expand = jnp.where(hh == hl, 1.0, 0.0).astype(BF16)
    lane_lo = lax.broadcasted_iota(jnp.int32, (q, 2 * SSD_HEAD_DIM), 1) < SSD_HEAD_DIM
    gw = HEADS_PER_GROUP * SSD_HEAD_DIM
    n_bc = N_GROUPS * D_STATE
    a_neg = -jnp.exp(alog_ref[...])
    a_negt = -jnp.exp(alogt_ref[...])
    d_x = _dot_exact_rhs01(dskip_ref[...], expand)

    for s in range(MIX_CHUNKS):
        rows = pl.ds(s * q, q)
        xs = xbc_ref[rows, :D_SSD].astype(F32)
        b_all = xbc_ref[rows, D_SSD:D_SSD + n_bc]
        c_all = xbc_ref[rows, D_SSD + n_bc:]

        dt = _softplus(dt_ref[rows, :] + dtb_ref[...])
        dtt = _softplus(dtt_ref[:, rows] + dtbt_ref[...])
        acum = _spread_rows(tri, dt * a_neg)
        acumt = _spread(dtt * a_negt, trit)
        a_last = acum[q - 1:q, :]

        xdt = xs * _spread(dt, expand)
        xdt_b = xdt.astype(BF16)
        dec_in = _spread(jnp.exp(acum), expand)
        dec_out = _spread(jnp.exp(a_last - acum), expand)
        dec_all = _dot_exact_rhs01(jnp.exp(a_last), expand)
        xw_b = (xdt * dec_out).astype(BF16)

        y_parts = []
        for g in range(N_GROUPS):
            b_g = b_all[:, g * D_STATE:(g + 1) * D_STATE]
            c_g = c_all[:, g * D_STATE:(g + 1) * D_STATE]
            cb = lax.dot_general(c_g, b_g, (((1,), (1,)), ((), ())), preferred_element_type=F32)
            st = state_ref[:, g * gw:(g + 1) * gw]
            y_inter = _dot(c_g, st.astype(BF16)) * dec_in[:, g * gw:(g + 1) * gw]
            new_st = lax.dot_general(b_g, xw_b[:, g * gw:(g + 1) * gw], (((0,), (0,)), ((), ())),
                                     preferred_element_type=F32)
            state_ref[:, g * gw:(g + 1) * gw] = st * dec_all[:, g * gw:(g + 1) * gw] + new_st
            pair_parts = []
            for pr in range(HEADS_PER_GROUP // 2):
                h0 = g * HEADS_PER_GROUP + 2 * pr
                lo = h0 * SSD_HEAD_DIM
                rhs = xdt_b[:, lo:lo + 2 * SSD_HEAD_DIM]
                ys = []
                for h in (h0, h0 + 1):
                    seg = acum[:, h:h + 1] - acumt[h:h + 1, :]
                    m = (cb * jnp.exp(jnp.where(causal, seg, -jnp.inf))).astype(BF16)
                    ys.append(_dot(m, rhs))
                pair_parts.append(jnp.where(lane_lo, ys[0], ys[1]))
            y_parts.append(jnp.concatenate(pair_parts, axis=-1) + y_inter)
        y = jnp.concatenate(y_parts, axis=-1) + d_x * xs

        gated = y * _silu(z_ref[rows, :].astype(F32))
        outs = []
        for g in range(N_GROUPS):
            gg = gated[:, g * gw:(g + 1) * gw]
            outs.append(gg * lax.rsqrt(jnp.mean(gg * gg, axis=-1, keepdims=True) + EPS))
        out_ref[rows, :] = (jnp.concatenate(outs, axis=-1) * nw_ref[...]).astype(out_ref.dtype)


def _mixer(z, xbc, dt, dtt, batch, seq, dtb, alog, dskip, nw):
    rows = CHUNK * MIX_CHUNKS
    assert seq % rows == 0
    ns = seq // rows
    t = batch * seq
    row = lambda n: pl.BlockSpec((rows, n), lambda b, c: (b * ns + c, 0))
    full = lambda a: pl.BlockSpec(a.shape, lambda b, c: (0,) * a.ndim)
    dtbt, alogt = dtb.reshape(N_SSD_HEADS, 1), alog.reshape(N_SSD_HEADS, 1)
    params = (dtb, dtbt, alog, alogt, dskip, nw)
    return pl.pallas_call(
        _mixer_kernel,
        grid=(batch, ns),
        in_specs=[row(D_SSD), row(D_XBC), row(N_SSD_HEADS),
                  pl.BlockSpec((N_SSD_HEADS, rows), lambda b, c: (0, b * ns + c))]
                 + [full(a) for a in params],
        out_specs=row(D_SSD),
        out_shape=jax.ShapeDtypeStruct((t, D_SSD), BF16),
        scratch_shapes=[pltpu.VMEM((D_STATE, D_SSD), F32)],
        compiler_params=pltpu.CompilerParams(
            dimension_semantics=("arbitrary", "arbitrary"), vmem_limit_bytes=VMEM_LIMIT_BYTES),
    )(z, xbc, dt, dtt, *params)


def _outproj_kernel(yssd_ref, ysc_ref, x_ref, wout_ref, nw_ref, wr_ref, br_ref,
                    h1_ref, xn_ref, idx_ref, gate_ref, rank_ref, cnt_ref, run_ref, earlier_ref):
    tm = x_ref.shape[0]

    @pl.when(pl.program_id(0) == 0)
    def _():
        run_ref[...] = jnp.zeros_like(run_ref)
        ri = lax.broadcasted_iota(jnp.int32, (tm, tm), 0)
        ci = lax.broadcasted_iota(jnp.int32, (tm, tm), 1)
        earlier_ref[...] = jnp.where(ri < ci, 1.0, 0.0).astype(BF16)

    h1 = (x_ref[...] + _dot(yssd_ref[...], wout_ref[:D_SSD, :])
          + _dot(ysc_ref[...], wout_ref[D_SSD:, :]))
    h1_ref[...] = h1
    xn = h1 * lax.rsqrt(jnp.mean(h1 * h1, axis=-1, keepdims=True) + EPS) * nw_ref[...]
    for c, words in enumerate(_pack_rows(xn)):
        xn_ref[c] = words

    xh = xn.astype(BF16)
    xl = (xn - xh.astype(F32)).astype(BF16)
    nt = lambda w, a: lax.dot_general(w, a, (((1,), (1,)), ((), ())), preferred_element_type=F32)
    w_x = nt(wr_ref[...], xh)
    logits = (w_x[:N_EXPERTS] + w_x[N_EXPERTS:] + nt(wr_ref[:N_EXPERTS, :], xl)
              + br_ref[...])

    expert = lax.broadcasted_iota(jnp.int32, (N_EXPERTS, tm), 0).astype(F32)
    work = logits
    vals, firsts, sels = [], [], []
    for _ in range(TOP_K):
        m = jnp.max(work, axis=0, keepdims=True)
        first = jnp.min(jnp.where(work == m, expert, float(N_EXPERTS)), axis=0, keepdims=True)
        sel = expert == first
        vals.append(m)
        firsts.append(first)
        sels.append(sel)
        work = jnp.where(sel, -jnp.inf, work)
    exps = [jnp.exp(v - vals[0]) for v in vals]
    denom = exps[0] + exps[1] + exps[2] + exps[3]

    onehot = jnp.zeros((N_EXPERTS, tm), F32)
    for sel in sels:
        onehot = onehot + jnp.where(sel, 1.0, 0.0)
    before = run_ref[...] + _dot(onehot.astype(BF16), earlier_ref[...])
    run_ref[...] = run_ref[...] + jnp.sum(onehot, axis=1, keepdims=True)
    cnt_ref[...] = run_ref[...]

    slot = lax.broadcasted_iota(jnp.int32, (TOP_K, tm), 0)
    idx = jnp.zeros((TOP_K, tm), F32)
    gate = jnp.zeros((TOP_K, tm), F32)
    rank = jnp.zeros((TOP_K, tm), F32)
    for k in range(TOP_K):
        r_k = jnp.sum(jnp.where(sels[k], before, 0.0), axis=0, keepdims=True)
        idx = jnp.where(slot == k, firsts[k], idx)
        gate = jnp.where(slot == k, exps[k] / denom, gate)
        rank = jnp.where(slot == k, r_k, rank)
    idx_ref[...] = idx.astype(jnp.int32)
    gate_ref[...] = gate
    rank_ref[...] = rank.astype(jnp.int32)


def _outproj(yssd, ysc, x2, w_out, nw, wr_split, br, first_tile, t):
    tm = OUT_TILE
    src = lambda n: pl.BlockSpec((tm, n), lambda i: (first_tile + i, 0))
    row = lambda n: pl.BlockSpec((tm, n), lambda i: (i, 0))
    slots = pl.BlockSpec((TOP_K, tm), lambda i: (0, i))
    full = lambda a: pl.BlockSpec(a.shape, lambda i: (0,) * a.ndim)
    return pl.pallas_call(
        _outproj_kernel,
        grid=(t // tm,),
        in_specs=[src(D_SSD), src(D_SC), src(D_MODEL), full(w_out), full(nw), full(wr_split),
                  full(br)],
        out_specs=[row(D_MODEL), pl.BlockSpec((ROW_PARTS, tm, LANES), lambda i: (0, i, 0)),
                   slots, slots, slots,
                   pl.BlockSpec((N_EXPERTS, 1), lambda i: (0, 0))],
        out_shape=[jax.ShapeDtypeStruct((t, D_MODEL), F32),
                   jax.ShapeDtypeStruct((ROW_PARTS, t, LANES), jnp.int32),
                   jax.ShapeDtypeStruct((TOP_K, t), jnp.int32),
                   jax.ShapeDtypeStruct((TOP_K, t), F32),
                   jax.ShapeDtypeStruct((TOP_K, t), jnp.int32),
                   jax.ShapeDtypeStruct((N_EXPERTS, 1), F32)],
        scratch_shapes=[pltpu.VMEM((N_EXPERTS, 1), F32), pltpu.VMEM((tm, tm), BF16)],
        compiler_params=pltpu.CompilerParams(
            dimension_semantics=("arbitrary",), vmem_limit_bytes=VMEM_LIMIT_BYTES),
    )(yssd, ysc, x2, w_out, nw, wr_split, br)


def _dest_kernel(pstart_ref, idx_ref, rank_ref, dest_ref):
    idx = idx_ref[...]
    base = jnp.zeros_like(idx)
    for e in range(N_EXPERTS):
        base = jnp.where(idx == e, pstart_ref[e], base)
    dest_ref[...] = base + rank_ref[...]


def _dest(pstarts, idx, rank):
    full = lambda a: pl.BlockSpec(a.shape, lambda i, ps: (0,) * a.ndim)
    return pl.pallas_call(
        _dest_kernel,
        grid_spec=pltpu.PrefetchScalarGridSpec(
            num_scalar_prefetch=1, grid=(1,),
            in_specs=[full(idx), full(rank)], out_specs=full(idx)),
        out_shape=jax.ShapeDtypeStruct(idx.shape, jnp.int32),
    )(pstarts, idx, rank)


def _sc_mesh():
    return plsc.VectorSubcoreMesh(core_axis_name="core", subcore_axis_name="subcore")


def _part_rows(dest, n_rows):
    off = (jnp.arange(ROW_PARTS, dtype=jnp.int32) * n_rows)[:, None, None]
    return (dest[None] + off).reshape(ROW_PARTS * dest.shape[0], dest.shape[1])


def _sc_dispatch(x, dest, n_rows):
    parts, t, width = x.shape
    n_slots = dest.shape[0]
    win = SC_WINDOW
    nwin = t // win

    @functools.partial(pl.kernel,
                       out_type=jax.ShapeDtypeStruct((parts * n_rows, width), jnp.int32),
                       mesh=_sc_mesh(), scratch_types=[])
    def scatter(x_hbm, idx_hbm, out_hbm):
        def body(x_vmem, idx_vmem):
            for k in range(n_slots):
                pltpu.sync_copy(x_vmem, out_hbm.at[idx_vmem.at[k]])

        pltpu.emit_pipeline(
            body,
            grid=(parts * nwin,),
            in_specs=[pl.BlockSpec((win, width), lambda j: (j, 0)),
                      pl.BlockSpec((n_slots, win), lambda j: (j // nwin, j % nwin))],
            out_specs=[],
            core_axis_name=("core", "subcore"),
            dimension_semantics=(pltpu.PARALLEL,),
        )(x_hbm, idx_hbm)

    xs = scatter(x.reshape(parts * t, width), _part_rows(dest, n_rows))
    return xs.reshape(parts, n_rows, width)


def _sc_collect(ys, dest):
    parts, n_rows, width = ys.shape
    n_slots, t = dest.shape
    win = SC_WINDOW
    nwin = t // win
    idx = _part_rows(dest, n_rows).reshape(parts, n_slots, t).swapaxes(0, 1)
    idx = idx.reshape(n_slots * parts, t)

    @functools.partial(pl.kernel,
                       out_type=jax.ShapeDtypeStruct((n_slots * parts * t, width), jnp.int32),
                       mesh=_sc_mesh(), scratch_types=[])
    def gather(y_hbm, idx_hbm, out_hbm):
        def body(idx_vmem, out_vmem):
            pltpu.sync_copy(y_hbm.at[idx_vmem.at[0]], out_vmem)

        pltpu.emit_pipeline(
            body,
            grid=(n_slots * parts * nwin,),
            in_specs=[pl.BlockSpec((1, win), lambda j: (j // nwin, j % nwin))],
            out_specs=[pl.BlockSpec((win, width), lambda j: (j, 0))],
            core_axis_name=("core", "subcore"),
            dimension_semantics=(pltpu.PARALLEL,),
        )(idx_hbm, out_hbm)

    out = gather(ys.reshape(parts * n_rows, width), idx)
    return out.reshape(n_slots, parts, t, width)


def _expert_kernel(blk_e_ref, blk_n_ref, xs_ref, wgu_ref, bgu_ref, wd_ref, bd_ref, out_ref,
                   wgu_b, wd_b):
    i = pl.program_id(0)
    n_valid = blk_n_ref[i]

    @pl.when((i == 0) | (blk_e_ref[i] != blk_e_ref[jnp.maximum(i - 1, 0)]))
    def _():
        wgu_b[...] = wgu_ref[0].astype(BF16)
        wd_b[...] = wd_ref[0].astype(BF16)

    @pl.when(n_valid == 0)
    def _():
        out_ref[...] = jnp.zeros_like(out_ref)

    @pl.when(n_valid > 0)
    def _():
        for r in range(0, ROW_BLOCK, EXPERT_SUB):
            words = [xs_ref[c, r:r + EXPERT_SUB] for c in range(ROW_PARTS)]
            rows = r + lax.broadcasted_iota(jnp.int32, words[0].shape, 0)
            xb = _unpack_rows([jnp.where(rows < n_valid, w, 0) for w in words]).astype(BF16)
            gu = _dot(xb, wgu_b[...]) + bgu_ref[0]
            gate = jnp.minimum(gu[:, :EXPERT_FF], SWIGLU_LIMIT)
            up = jnp.clip(gu[:, EXPERT_FF:], -SWIGLU_LIMIT, SWIGLU_LIMIT)
            hid = (up + 1.0) * (gate * (1.0 / (1.0 + jnp.exp(-SWIGLU_ALPHA * gate))))
            out = _dot(hid.astype(BF16), wd_b[...]) + bd_ref[0]
            for c, w in enumerate(_pack_rows(out)):
                out_ref[c, r:r + EXPERT_SUB] = w


def _experts(blk_e, blk_n, xs, wgu, bgu, wd, bd):
    rb = ROW_BLOCK
    by_expert = lambda a: pl.BlockSpec((1,) + a.shape[1:], lambda i, be, bn: (be[i], 0, 0))
    return pl.pallas_call(
        _expert_kernel,
        grid_spec=pltpu.PrefetchScalarGridSpec(
            num_scalar_prefetch=2, grid=(xs.shape[1] // rb,),
            in_specs=[pl.BlockSpec((ROW_PARTS, rb, LANES), lambda i, be, bn: (0, i, 0)),
                      by_expert(wgu), by_expert(bgu), by_expert(wd), by_expert(bd)],
            out_specs=pl.BlockSpec((ROW_PARTS, rb, LANES), lambda i, be, bn: (0, i, 0)),
            scratch_shapes=[pltpu.VMEM(wgu.shape[1:], BF16), pltpu.VMEM(wd.shape[1:], BF16)]),
        out_shape=jax.ShapeDtypeStruct(xs.shape, xs.dtype),
        compiler_params=pltpu.CompilerParams(
            dimension_semantics=("arbitrary",), vmem_limit_bytes=VMEM_LIMIT_BYTES),
    )(blk_e, blk_n, xs, wgu, bgu, wd, bd)


def _combine_kernel(gate_ref, h1_ref, fw_ref, y0_ref, y1_ref, y2_ref, y3_ref, *rest):
    out_ref = rest[-1]
    parts = _split3(gate_ref[...])
    slot = lax.broadcasted_iota(jnp.int32, (TOP_K, LANES), 0)
    h2 = h1_ref[...]
    for k, y_ref in enumerate((y0_ref, y1_ref, y2_ref, y3_ref)):
        pick = jnp.where(slot == k, 1.0, 0.0).astype(BF16)
        g = sum(lax.dot_general(p, pick, (((0,), (0,)), ((), ())), preferred_element_type=F32)
                for p in parts)
        y = _unpack_rows([y_ref[c] for c in range(ROW_PARTS)])
        h2 = h2 + jnp.concatenate([g] * (D_MODEL // LANES), axis=-1) * y
    out_ref[...] = h2 * lax.rsqrt(jnp.mean(h2 * h2, axis=-1, keepdims=True) + EPS) * fw_ref[...]


def _combine(gates, h1, fw, planes, first_tile, out_tile, t, out_prev):
    d = h1.shape[1]
    tm = COMBINE_TILE
    n = planes.shape[2]
    plane = lambda k: pl.BlockSpec((None, ROW_PARTS, tm, LANES), lambda i: (k, 0, i, 0))
    in_specs = [pl.BlockSpec((TOP_K, tm), lambda i: (0, first_tile + i)),
                pl.BlockSpec((tm, d), lambda i: (first_tile + i, 0)),
                pl.BlockSpec((1, d), lambda i: (0, 0))]
    in_specs += [plane(k) for k in range(TOP_K)]
    args = [gates, h1, fw] + [planes] * TOP_K
    aliases = {}
    if out_prev is not None:
        in_specs.append(pl.BlockSpec(memory_space=pl.ANY))
        aliases = {len(args): 0}
        args.append(out_prev)
    return pl.pallas_call(
        _combine_kernel,
        grid=(n // tm,),
        in_specs=in_specs,
        out_specs=pl.BlockSpec((tm, d), lambda i: (out_tile + i, 0)),
        out_shape=jax.ShapeDtypeStruct((t, d), F32),
        input_output_aliases=aliases,
        compiler_params=pltpu.CompilerParams(
            dimension_semantics=("arbitrary",), vmem_limit_bytes=VMEM_LIMIT_BYTES),
    )(*args)


def _block_plan(counts, n_blocks):
    rb = ROW_BLOCK
    pcounts = (counts + rb - 1) // rb * rb
    pends = jnp.cumsum(pcounts)
    pstarts = pends - pcounts
    block_start = jnp.arange(n_blocks, dtype=jnp.int32) * rb
    blk_e = jnp.minimum(jnp.sum(block_start[:, None] >= pends[None, :], axis=1),
                        N_EXPERTS - 1).astype(jnp.int32)
    mine = blk_e[:, None] == jnp.arange(N_EXPERTS, dtype=jnp.int32)[None, :]
    pick = lambda table: jnp.sum(jnp.where(mine, table[None, :], 0), axis=1)
    blk_n = jnp.clip(pick(counts) - (block_start - pick(pstarts)), 0, rb).astype(jnp.int32)
    return pstarts.astype(jnp.int32), blk_e, blk_n


def kernel(x, norm1_w, w_in, ssd_conv_w, ssd_conv_b, dt_bias, a_log, d_skip, ssd_norm_w,
           sc_conv_w, sc_norm_w, w_out, norm2_w, w_router, b_router, w_gate_up, b_gate_up,
           w_down, b_down, final_norm_w):
    assert norm1_w.shape[0] == 1, "single-layer problem"
    batch, seq, d = x.shape
    t = batch * seq
    x2 = x.reshape(t, d)
    row = lambda a: a.reshape(1, -1)

    s1, s2 = D_SSD + D_XBC, D_SSD + D_XBC + N_SSD_HEADS
    w = w_in[0]
    w_zx, w_dt, w_sc = w[:, :s1].astype(BF16), w[:, s1:s2].astype(BF16), w[:, s2:].astype(BF16)
    z, xbc, ysc, dt, dtt = _inproj(x2, seq, row(norm1_w[0]), w_zx, w_sc, w_dt, w_dt.T,
                                   ssd_conv_w[0], row(ssd_conv_b[0]), sc_conv_w[0],
                                   row(sc_norm_w[0]))
    yssd = _mixer(z, xbc, dt, dtt, batch, seq, row(dt_bias[0]), row(a_log[0]), row(d_skip[0]),
                  row(ssd_norm_w[0]))

    wr = w_router[0].T
    wr_hi = wr.astype(BF16)
    wr_split = jnp.concatenate([wr_hi, (wr - wr_hi.astype(F32)).astype(BF16)], axis=0)
    unit = t // sum(MOE_SPLIT)
    sizes = [unit * s for s in MOE_SPLIT]
    starts = [sum(sizes[:g]) for g in range(len(sizes))]
    assert sum(sizes) == t and all(n % (COMBINE_CHUNKS * COMBINE_TILE) == 0 and n % OUT_TILE == 0
                                   for n in sizes)
    w_out_b, nw2, br = w_out[0].astype(BF16), row(norm2_w[0]), b_router[0].reshape(-1, 1)
    routed = []
    for t0, tg in zip(starts, sizes):
        n_rows = tg * TOP_K + N_EXPERTS * ROW_BLOCK
        h1, xn2, idx, gates, rank, counts = _outproj(
            yssd, ysc, x2, w_out_b, nw2, wr_split, br, t0 // OUT_TILE, tg)
        pstarts, blk_e, blk_n = _block_plan(counts[:, 0].astype(jnp.int32),
                                            n_rows // ROW_BLOCK)
        dest = _dest(pstarts, idx, rank)
        xs = _sc_dispatch(xn2, dest, n_rows)
        routed.append((h1, gates, dest, blk_e, blk_n, xs))
    expert_out = [_experts(blk_e, blk_n, xs, w_gate_up[0], b_gate_up[0][:, None, :],
                           w_down[0], b_down[0][:, None, :])
                  for (_, _, _, blk_e, blk_n, xs) in routed]
    fw, out = row(final_norm_w), None
    for t0, tg, (h1, gates, dest, _, _, _), ys in zip(starts, sizes, routed, expert_out):
        tc = tg // COMBINE_CHUNKS
        for c in range(COMBINE_CHUNKS):
            planes = _sc_collect(ys, dest[:, c * tc:(c + 1) * tc])
            out = _combine(gates, h1, fw, planes, c * (tc // COMBINE_TILE),
                           (t0 + c * tc) // COMBINE_TILE, t, out)
    return out.reshape(batch, seq, d)
```

```python
import functools

import jax
import jax.numpy as jnp
from jax import lax
from jax.experimental import pallas as pl
from jax.experimental.pallas import tpu as pltpu
from jax.experimental.pallas import tpu_sc as plsc

D_MODEL = 1024
D_SSD = 1024
SSD_HEAD_DIM = 64
N_SSD_HEADS = 16
N_GROUPS = 2
HEADS_PER_GROUP = 8
D_STATE = 128
SSD_CONV = 4
CHUNK = 128
D_XBC = D_SSD + 2 * N_GROUPS * D_STATE
D_SC = 1024
SC_CONV = 3
N_EXPERTS = 32
TOP_K = 4
EXPERT_FF = 1024
SWIGLU_LIMIT = 7.0
SWIGLU_ALPHA = 1.702
EPS = 1e-5

VMEM_LIMIT_BYTES = 56 * 1024 * 1024
SUBLANES = 8
ROW_BLOCK = 512
EXPERT_SUB = 256
IN_TILE = 512
IN_SUB = 256
MIX_CHUNKS = 8
OUT_TILE = 1024
COMBINE_TILE = 1024
MOE_SPLIT = (5, 3)
COMBINE_CHUNKS = 2
LANES = 128
ROW_PARTS = D_MODEL // (2 * LANES)
SC_WINDOW = 128
HIGH_HALF = -65536

F32 = jnp.float32
BF16 = jnp.bfloat16


def _dot(a, b):
    return jnp.dot(a, b, preferred_element_type=F32)


def _split3(v):
    hi = v.astype(BF16)
    r1 = v - hi.astype(F32)
    mid = r1.astype(BF16)
    lo = (r1 - mid.astype(F32)).astype(BF16)
    return hi, mid, lo


def _dot_exact_rhs01(v, m01):
    hi, mid, lo = _split3(v)
    return _dot(hi, m01) + _dot(mid, m01) + _dot(lo, m01)


def _split2(v):
    hi = v.astype(BF16)
    return hi, (v - hi.astype(F32)).astype(BF16)


def _spread(v, m01):
    hi, mid = _split2(v)
    return _dot(hi, m01) + _dot(mid, m01)


def _spread_rows(m01, v):
    hi, mid = _split2(v)
    return _dot(m01, hi) + _dot(m01, mid)


def _silu(v):
    return v * (1.0 / (1.0 + jnp.exp(-v)))


def _softplus(v):
    return jnp.maximum(v, 0.0) + jnp.log1p(jnp.exp(-jnp.abs(v)))


def _pack_rows(v):
    bits = lambda a: lax.bitcast_convert_type(a.astype(BF16).astype(F32), jnp.int32)
    parts = []
    for c in range(ROW_PARTS):
        lo = v[:, (2 * c) * LANES:(2 * c + 1) * LANES]
        hi = v[:, (2 * c + 1) * LANES:(2 * c + 2) * LANES]
        parts.append(bits(hi) | lax.shift_right_logical(bits(lo), 16))
    return parts


def _unpack_rows(parts):
    cols = []
    for w in parts:
        cols.append(lax.bitcast_convert_type(w << 16, F32))
        cols.append(lax.bitcast_convert_type(w & HIGH_HALF, F32))
    return jnp.concatenate(cols, axis=-1)


def _shifted(cur, tail, s):
    if s == 0:
        return cur
    rc = pltpu.roll(cur, s, axis=0)
    rt = pltpu.roll(tail, s, axis=0)
    row = lax.broadcasted_iota(jnp.int32, tail.shape, 0)
    first = jnp.where(row < s, rt, rc[0:SUBLANES])
    return jnp.concatenate([first, rc[SUBLANES:]], axis=0)


def _causal_conv(tail_ref, cur, w_ref, acc):
    rows = cur.shape[0]
    n_taps = w_ref.shape[0]
    tail = tail_ref[...]
    for k in range(n_taps):
        acc = acc + w_ref[k:k + 1, :] * _shifted(cur, tail, n_taps - 1 - k)
    tail_ref[...] = cur[rows - SUBLANES:rows]
    return acc


def _inproj_kernel(tiles_per_seq, x_ref, nw_ref, wzx_ref, wsc_ref, wdt_ref, wdtt_ref,
                   cw_ref, cb_ref, scw_ref, scnw_ref,
                   z_ref, xbc_ref, ysc_ref, dt_ref, dtt_ref, hist_ref, schist_ref):
    tm = x_ref.shape[0]

    @pl.when(pl.program_id(0) % tiles_per_seq == 0)
    def _():
        hist_ref[...] = jnp.zeros_like(hist_ref)
        schist_ref[...] = jnp.zeros_like(schist_ref)

    for r in range(0, tm, IN_SUB):
        rows = pl.ds(r, IN_SUB)
        x = x_ref[rows, :]
        xn = x * lax.rsqrt(jnp.mean(x * x, axis=-1, keepdims=True) + EPS) * nw_ref[...]
        xb = xn.astype(BF16)
        proj = lambda w_ref, o, n, xb=xb: _dot(xb, w_ref[:, o:o + n])
        z_ref[rows, :] = proj(wzx_ref, 0, D_SSD).astype(z_ref.dtype)

        conv = _causal_conv(hist_ref, proj(wzx_ref, D_SSD, D_XBC), cw_ref, cb_ref[...])
        xbc_ref[rows, :] = _silu(conv).astype(xbc_ref.dtype)

        cu = proj(wsc_ref, D_SC, D_SC) * proj(wsc_ref, 2 * D_SC, D_SC)
        v = _causal_conv(schist_ref, cu, scw_ref, jnp.zeros_like(cu))
        gv = proj(wsc_ref, 0, D_SC) * v
        y_sc = gv * lax.rsqrt(jnp.mean(gv * gv, axis=-1, keepdims=True) + EPS) * scnw_ref[...]
        ysc_ref[rows, :] = y_sc.astype(ysc_ref.dtype)

        dt_ref[rows, :] = _dot(xb, wdt_ref[...])
        dtt_ref[:, rows] = lax.dot_general(wdtt_ref[...], xb, (((1,), (1,)), ((), ())),
                                           preferred_element_type=F32)


def _inproj(x2, seq, nw, w_zx, w_sc, w_dt, w_dtt, cw, cb, scw, scnw):
    t = x2.shape[0]
    tm = IN_TILE
    assert seq % tm == 0, "in-proj tiles must not straddle sequences (causal conv carry)"
    row = lambda n: pl.BlockSpec((tm, n), lambda i: (i, 0))
    full = lambda a: pl.BlockSpec(a.shape, lambda i: (0,) * a.ndim)
    params = (nw, w_zx, w_sc, w_dt, w_dtt, cw, cb, scw, scnw)
    return pl.pallas_call(
        functools.partial(_inproj_kernel, seq // tm),
        grid=(t // tm,),
        in_specs=[row(D_MODEL)] + [full(a) for a in params],
        out_specs=[row(D_SSD), row(D_XBC), row(D_SC),
                   row(N_SSD_HEADS), pl.BlockSpec((N_SSD_HEADS, tm), lambda i: (0, i))],
        out_shape=[jax.ShapeDtypeStruct((t, D_SSD), BF16),
                   jax.ShapeDtypeStruct((t, D_XBC), BF16),
                   jax.ShapeDtypeStruct((t, D_SC), BF16),
                   jax.ShapeDtypeStruct((t, N_SSD_HEADS), F32),
                   jax.ShapeDtypeStruct((N_SSD_HEADS, t), F32)],
        scratch_shapes=[pltpu.VMEM((SUBLANES, D_XBC), F32), pltpu.VMEM((SUBLANES, D_SC), F32)],
        compiler_params=pltpu.CompilerParams(
            dimension_semantics=("arbitrary",), vmem_limit_bytes=VMEM_LIMIT_BYTES),
    )(x2, *params)


def _mixer_kernel(z_ref, xbc_ref, dt_ref, dtt_ref,
                  dtb_ref, dtbt_ref, alog_ref, alogt_ref, dskip_ref, nw_ref, out_ref, state_ref):
    q = CHUNK

    @pl.when(pl.program_id(1) == 0)
    def _():
        state_ref[...] = jnp.zeros_like(state_ref)

    ri = lax.broadcasted_iota(jnp.int32, (q, q), 0)
    ci = lax.broadcasted_iota(jnp.int32, (q, q), 1)
    causal = ri >= ci
    tri = jnp.where(causal, 1.0, 0.0).astype(BF16)
    trit = jnp.where(ri <= ci, 1.0, 0.0).astype(BF16)
    hh = lax.broadcasted_iota(jnp.int32, (N_SSD_HEADS, D_SSD), 0)
    hl = lax.broadcasted_iota(jnp.int32, (N_SSD_HEADS, D_SSD), 1) // SSD_HEAD_DIM
    expand = jnp.where(hh == hl, 1.0, 0.0).astype(BF16)
    lane_lo = lax.broadcasted_iota(jnp.int32, (q, 2 * SSD_HEAD_DIM), 1) < SSD_HEAD_DIM
    gw = HEADS_PER_GROUP * SSD_HEAD_DIM
    n_bc = N_GROUPS * D_STATE
    a_neg = -jnp.exp(alog_ref[...])
    a_negt = -jnp.exp(alogt_ref[...])
    d_x = _dot_exact_rhs01(dskip_ref[...], expand)

    for s in range(MIX_CHUNKS):
        rows = pl.ds(s * q, q)
        xs = xbc_ref[rows, :D_SSD].astype(F32)
        b_all = xbc_ref[rows, D_SSD:D_SSD + n_bc]
        c_all = xbc_ref[rows, D_SSD + n_bc:]

        dt = _softplus(dt_ref[rows, :] + dtb_ref[...])
        dtt = _softplus(dtt_ref[:, rows] + dtbt_ref[...])
        acum = _spread_rows(tri, dt * a_neg)
        acumt = _spread(dtt * a_negt, trit)
        a_last = acum[q - 1:q, :]

        xdt = xs * _spread(dt, expand)
        xdt_b = xdt.astype(BF16)
        dec_in = _spread(jnp.exp(acum), expand)
        dec_out = _spread(jnp.exp(a_last - acum), expand)
        dec_all = _dot_exact_rhs01(jnp.exp(a_last), expand)
        xw_b = (xdt * dec_out).astype(BF16)

        y_parts = []
        for g in range(N_GROUPS):
            b_g = b_all[:, g * D_STATE:(g + 1) * D_STATE]
            c_g = c_all[:, g * D_STATE:(g + 1) * D_STATE]
            cb = lax.dot_general(c_g, b_g, (((1,), (1,)), ((), ())), preferred_element_type=F32)
            st = state_ref[:, g * gw:(g + 1) * gw]
            y_inter = _dot(c_g, st.astype(BF16)) * dec_in[:, g * gw:(g + 1) * gw]
            new_st = lax.dot_general(b_g, xw_b[:, g * gw:(g + 1) * gw], (((0,), (0,)), ((), ())),
                                     preferred_element_type=F32)
            state_ref[:, g * gw:(g + 1) * gw] = st * dec_all[:, g * gw:(g + 1) * gw] + new_st
            pair_parts = []
            for pr in range(HEADS_PER_GROUP // 2):
                h0 = g * HEADS_PER_GROUP + 2 * pr
                lo = h0 * SSD_HEAD_DIM
                rhs = xdt_b[:, lo:lo + 2 * SSD_HEAD_DIM]
                ys = []
                for h in (h0, h0 + 1):
                    seg = acum[:, h:h + 1] - acumt[h:h + 1, :]
                    m = (cb * jnp.exp(jnp.where(causal, seg, -jnp.inf))).astype(BF16)
                    ys.append(_dot(m, rhs))
                pair_parts.append(jnp.where(lane_lo, ys[0], ys[1]))
            y_parts.append(jnp.concatenate(pair_parts, axis=-1) + y_inter)
        y = jnp.concatenate(y_parts, axis=-1) + d_x * xs

        gated = y * _silu(z_ref[rows, :].astype(F32))
        outs = []
        for g in range(N_GROUPS):
            gg = gated[:, g * gw:(g + 1) * gw]
            outs.append(gg * lax.rsqrt(jnp.mean(gg * gg, axis=-1, keepdims=True) + EPS))
        out_ref[rows, :] = (jnp.concatenate(outs, axis=-1) * nw_ref[...]).astype(out_ref.dtype)


def _mixer(z, xbc, dt, dtt, batch, seq, dtb, alog, dskip, nw):
    rows = CHUNK * MIX_CHUNKS
    assert seq % rows == 0
    ns = seq // rows
    t = batch * seq
    row = lambda n: pl.BlockSpec((rows, n), lambda b, c: (b * ns + c, 0))
    full = lambda a: pl.BlockSpec(a.shape, lambda b, c: (0,) * a.ndim)
    dtbt, alogt = dtb.reshape(N_SSD_HEADS, 1), alog.reshape(N_SSD_HEADS, 1)
    params = (dtb, dtbt, alog, alogt, dskip, nw)
    return pl.pallas_call(
        _mixer_kernel,
        grid=(batch, ns),
        in_specs=[row(D_SSD), row(D_XBC), row(N_SSD_HEADS),
                  pl.BlockSpec((N_SSD_HEADS, rows), lambda b, c: (0, b * ns + c))]
                 + [full(a) for a in params],
        out_specs=row(D_SSD),
        out_shape=jax.ShapeDtypeStruct((t, D_SSD), BF16),
        scratch_shapes=[pltpu.VMEM((D_STATE, D_SSD), F32)],
        compiler_params=pltpu.CompilerParams(
            dimension_semantics=("arbitrary", "arbitrary"), vmem_limit_bytes=VMEM_LIMIT_BYTES),
    )(z, xbc, dt, dtt, *params)


def _outproj_kernel(yssd_ref, ysc_ref, x_ref, wout_ref, nw_ref, wr_ref, br_ref,
                    h1_ref, xn_ref, idx_ref, gate_ref, rank_ref, cnt_ref, run_ref, earlier_ref):
    tm = x_ref.shape[0]

    @pl.when(pl.program_id(0) == 0)
    def _():
        run_ref[...] = jnp.zeros_like(run_ref)
        ri = lax.broadcasted_iota(jnp.int32, (tm, tm), 0)
        ci = lax.broadcasted_iota(jnp.int32, (tm, tm), 1)
        earlier_ref[...] = jnp.where(ri < ci, 1.0, 0.0).astype(BF16)

    h1 = (x_ref[...] + _dot(yssd_ref[...], wout_ref[:D_SSD, :])
          + _dot(ysc_ref[...], wout_ref[D_SSD:, :]))
    h1_ref[...] = h1
    xn = h1 * lax.rsqrt(jnp.mean(h1 * h1, axis=-1, keepdims=True) + EPS) * nw_ref[...]
    for c, words in enumerate(_pack_rows(xn)):
        xn_ref[c] = words

    xh = xn.astype(BF16)
    xl = (xn - xh.astype(F32)).astype(BF16)
    nt = lambda w, a: lax.dot_general(w, a, (((1,), (1,)), ((), ())), preferred_element_type=F32)
    w_x = nt(wr_ref[...], xh)
    logits = (w_x[:N_EXPERTS] + w_x[N_EXPERTS:] + nt(wr_ref[:N_EXPERTS, :], xl)
              + br_ref[...])

    expert = lax.broadcasted_iota(jnp.int32, (N_EXPERTS, tm), 0).astype(F32)
    work = logits
    vals, firsts, sels = [], [], []
    for _ in range(TOP_K):
        m = jnp.max(work, axis=0, keepdims=True)
        first = jnp.min(jnp.where(work == m, expert, float(N_EXPERTS)), axis=0, keepdims=True)
        sel = expert == first
        vals.append(m)
        firsts.append(first)
        sels.append(sel)
        work = jnp.where(sel, -jnp.inf, work)
    exps = [jnp.exp(v - vals[0]) for v in vals]
    denom = exps[0] + exps[1] + exps[2] + exps[3]

    onehot = jnp.zeros((N_EXPERTS, tm), F32)
    for sel in sels:
        onehot = onehot + jnp.where(sel, 1.0, 0.0)
    before = run_ref[...] + _dot(onehot.astype(BF16), earlier_ref[...])
    run_ref[...] = run_ref[...] + jnp.sum(onehot, axis=1, keepdims=True)
    cnt_ref[...] = run_ref[...]

    slot = lax.broadcasted_iota(jnp.int32, (TOP_K, tm), 0)
    idx = jnp.zeros((TOP_K, tm), F32)
    gate = jnp.zeros((TOP_K, tm), F32)
    rank = jnp.zeros((TOP_K, tm), F32)
    for k in range(TOP_K):
        r_k = jnp.sum(jnp.where(sels[k], before, 0.0), axis=0, keepdims=True)
        idx = jnp.where(slot == k, firsts[k], idx)
        gate = jnp.where(slot == k, exps[k] / denom, gate)
        rank = jnp.where(slot == k, r_k, rank)
    idx_ref[...] = idx.astype(jnp.int32)
    gate_ref[...] = gate
    rank_ref[...] = rank.astype(jnp.int32)


def _outproj(yssd, ysc, x2, w_out, nw, wr_split, br, first_tile, t):
    tm = OUT_TILE
    src = lambda n: pl.BlockSpec((tm, n), lambda i: (first_tile + i, 0))
    row = lambda n: pl.BlockSpec((tm, n), lambda i: (i, 0))
    slots = pl.BlockSpec((TOP_K, tm), lambda i: (0, i))
    full = lambda a: pl.BlockSpec(a.shape, lambda i: (0,) * a.ndim)
    return pl.pallas_call(
        _outproj_kernel,
        grid=(t // tm,),
        in_specs=[src(D_SSD), src(D_SC), src(D_MODEL), full(w_out), full(nw), full(wr_split),
                  full(br)],
        out_specs=[row(D_MODEL), pl.BlockSpec((ROW_PARTS, tm, LANES), lambda i: (0, i, 0)),
                   slots, slots, slots,
                   pl.BlockSpec((N_EXPERTS, 1), lambda i: (0, 0))],
        out_shape=[jax.ShapeDtypeStruct((t, D_MODEL), F32),
                   jax.ShapeDtypeStruct((ROW_PARTS, t, LANES), jnp.int32),
                   jax.ShapeDtypeStruct((TOP_K, t), jnp.int32),
                   jax.ShapeDtypeStruct((TOP_K, t), F32),
                   jax.ShapeDtypeStruct((TOP_K, t), jnp.int32),
                   jax.ShapeDtypeStruct((N_EXPERTS, 1), F32)],
        scratch_shapes=[pltpu.VMEM((N_EXPERTS, 1), F32), pltpu.VMEM((tm, tm), BF16)],
        compiler_params=pltpu.CompilerParams(
            dimension_semantics=("arbitrary",), vmem_limit_bytes=VMEM_LIMIT_BYTES),
    )(yssd, ysc, x2, w_out, nw, wr_split, br)


def _dest_kernel(pstart_ref, idx_ref, rank_ref, dest_ref):
    idx = idx_ref[...]
    base = jnp.zeros_like(idx)
    for e in range(N_EXPERTS):
        base = jnp.where(idx == e, pstart_ref[e], base)
    dest_ref[...] = base + rank_ref[...]


def _dest(pstarts, idx, rank):
    full = lambda a: pl.BlockSpec(a.shape, lambda i, ps: (0,) * a.ndim)
    return pl.pallas_call(
        _dest_kernel,
        grid_spec=pltpu.PrefetchScalarGridSpec(
            num_scalar_prefetch=1, grid=(1,),
            in_specs=[full(idx), full(rank)], out_specs=full(idx)),
        out_shape=jax.ShapeDtypeStruct(idx.shape, jnp.int32),
    )(pstarts, idx, rank)


def _sc_mesh():
    return plsc.VectorSubcoreMesh(core_axis_name="core", subcore_axis_name="subcore")


def _part_rows(dest, n_rows):
    off = (jnp.arange(ROW_PARTS, dtype=jnp.int32) * n_rows)[:, None, None]
    return (dest[None] + off).reshape(ROW_PARTS * dest.shape[0], dest.shape[1])


def _sc_dispatch(x, dest, n_rows):
    parts, t, width = x.shape
    n_slots = dest.shape[0]
    win = SC_WINDOW
    nwin = t // win

    @functools.partial(pl.kernel,
                       out_type=jax.ShapeDtypeStruct((parts * n_rows, width), jnp.int32),
                       mesh=_sc_mesh(), scratch_types=[])
    def scatter(x_hbm, idx_hbm, out_hbm):
        def body(x_vmem, idx_vmem):
            for k in range(n_slots):
                pltpu.sync_copy(x_vmem, out_hbm.at[idx_vmem.at[k]])

        pltpu.emit_pipeline(
            body,
            grid=(parts * nwin,),
            in_specs=[pl.BlockSpec((win, width), lambda j: (j, 0)),
                      pl.BlockSpec((n_slots, win), lambda j: (j // nwin, j % nwin))],
            out_specs=[],
            core_axis_name=("core", "subcore"),
            dimension_semantics=(pltpu.PARALLEL,),
        )(x_hbm, idx_hbm)

    xs = scatter(x.reshape(parts * t, width), _part_rows(dest, n_rows))
    return xs.reshape(parts, n_rows, width)


def _sc_collect(ys, dest):
    parts, n_rows, width = ys.shape
    n_slots, t = dest.shape
    win = SC_WINDOW
    nwin = t // win
    idx = _part_rows(dest, n_rows).reshape(parts, n_slots, t).swapaxes(0, 1)
    idx = idx.reshape(n_slots * parts, t)

    @functools.partial(pl.kernel,
                       out_type=jax.ShapeDtypeStruct((n_slots * parts * t, width), jnp.int32),
                       mesh=_sc_mesh(), scratch_types=[])
    def gather(y_hbm, idx_hbm, out_hbm):
        def body(idx_vmem, out_vmem):
            pltpu.sync_copy(y_hbm.at[idx_vmem.at[0]], out_vmem)

        pltpu.emit_pipeline(
            body,
            grid=(n_slots * parts * nwin,),
            in_specs=[pl.BlockSpec((1, win), lambda j: (j // nwin, j % nwin))],
            out_specs=[pl.BlockSpec((win, width), lambda j: (j, 0))],
            core_axis_name=("core", "subcore"),
            dimension_semantics=(pltpu.PARALLEL,),
        )(idx_hbm, out_hbm)

    out = gather(ys.reshape(parts * n_rows, width), idx)
    return out.reshape(n_slots, parts, t, width)


def _expert_kernel(blk_e_ref, blk_n_ref, xs_ref, wgu_ref, bgu_ref, wd_ref, bd_ref, out_ref,
                   wgu_b, wd_b):
    i = pl.program_id(0)
    n_valid = blk_n_ref[i]

    @pl.when((i == 0) | (blk_e_ref[i] != blk_e_ref[jnp.maximum(i - 1, 0)]))
    def _():
        wgu_b[...] = wgu_ref[0].astype(BF16)
        wd_b[...] = wd_ref[0].astype(BF16)

    @pl.when(n_valid == 0)
    def _():
        out_ref[...] = jnp.zeros_like(out_ref)

    @pl.when(n_valid > 0)
    def _():
        for r in range(0, ROW_BLOCK, EXPERT_SUB):
            words = [xs_ref[c, r:r + EXPERT_SUB] for c in range(ROW_PARTS)]
            rows = r + lax.broadcasted_iota(jnp.int32, words[0].shape, 0)
            xb = _unpack_rows([jnp.where(rows < n_valid, w, 0) for w in words]).astype(BF16)
            gu = _dot(xb, wgu_b[...]) + bgu_ref[0]
            gate = jnp.minimum(gu[:, :EXPERT_FF], SWIGLU_LIMIT)
            up = jnp.clip(gu[:, EXPERT_FF:], -SWIGLU_LIMIT, SWIGLU_LIMIT)
            hid = (up + 1.0) * (gate * (1.0 / (1.0 + jnp.exp(-SWIGLU_ALPHA * gate))))
            out = _dot(hid.astype(BF16), wd_b[...]) + bd_ref[0]
            for c, w in enumerate(_pack_rows(out)):
                out_ref[c, r:r + EXPERT_SUB] = w


def _experts(blk_e, blk_n, xs, wgu, bgu, wd, bd):
    rb = ROW_BLOCK
    by_expert = lambda a: pl.BlockSpec((1,) + a.shape[1:], lambda i, be, bn: (be[i], 0, 0))
    return pl.pallas_call(
        _expert_kernel,
        grid_spec=pltpu.PrefetchScalarGridSpec(
            num_scalar_prefetch=2, grid=(xs.shape[1] // rb,),
            in_specs=[pl.BlockSpec((ROW_PARTS, rb, LANES), lambda i, be, bn: (0, i, 0)),
                      by_expert(wgu), by_expert(bgu), by_expert(wd), by_expert(bd)],
            out_specs=pl.BlockSpec((ROW_PARTS, rb, LANES), lambda i, be, bn: (0, i, 0)),
            scratch_shapes=[pltpu.VMEM(wgu.shape[1:], BF16), pltpu.VMEM(wd.shape[1:], BF16)]),
        out_shape=jax.ShapeDtypeStruct(xs.shape, xs.dtype),
        compiler_params=pltpu.CompilerParams(
            dimension_semantics=("arbitrary",), vmem_limit_bytes=VMEM_LIMIT_BYTES),
    )(blk_e, blk_n, xs, wgu, bgu, wd, bd)


def _combine_kernel(gate_ref, h1_ref, fw_ref, y0_ref, y1_ref, y2_ref, y3_ref, *rest):
    out_ref = rest[-1]
    parts = _split3(gate_ref[...])
    slot = lax.broadcasted_iota(jnp.int32, (TOP_K, LANES), 0)
    h2 = h1_ref[...]
    for k, y_ref in enumerate((y0_ref, y1_ref, y2_ref, y3_ref)):
        pick = jnp.where(slot == k, 1.0, 0.0).astype(BF16)
        g = sum(lax.dot_general(p, pick, (((0,), (0,)), ((), ())), preferred_element_type=F32)
                for p in parts)
        y = _unpack_rows([y_ref[c] for c in range(ROW_PARTS)])
        h2 = h2 + jnp.concatenate([g] * (D_MODEL // LANES), axis=-1) * y
    out_ref[...] = h2 * lax.rsqrt(jnp.mean(h2 * h2, axis=-1, keepdims=True) + EPS) * fw_ref[...]


def _combine(gates, h1, fw, planes, first_tile, out_tile, t, out_prev):
    d = h1.shape[1]
    tm = COMBINE_TILE
    n = planes.shape[2]
    plane = lambda k: pl.BlockSpec((None, ROW_PARTS, tm, LANES), lambda i: (k, 0, i, 0))
    in_specs = [pl.BlockSpec((TOP_K, tm), lambda i: (0, first_tile + i)),
                pl.BlockSpec((tm, d), lambda i: (first_tile + i, 0)),
                pl.BlockSpec((1, d), lambda i: (0, 0))]
    in_specs += [plane(k) for k in range(TOP_K)]
    args = [gates, h1, fw] + [planes] * TOP_K
    aliases = {}
    if out_prev is not None:
        in_specs.append(pl.BlockSpec(memory_space=pl.ANY))
        aliases = {len(args): 0}
        args.append(out_prev)
    return pl.pallas_call(
        _combine_kernel,
        grid=(n // tm,),
        in_specs=in_specs,
        out_specs=pl.BlockSpec((tm, d), lambda i: (out_tile + i, 0)),
        out_shape=jax.ShapeDtypeStruct((t, d), F32),
        input_output_aliases=aliases,
        compiler_params=pltpu.CompilerParams(
            dimension_semantics=("arbitrary",), vmem_limit_bytes=VMEM_LIMIT_BYTES),
    )(*args)


def _block_plan(counts, n_blocks):
    rb = ROW_BLOCK
    pcounts = (counts + rb - 1) // rb * rb
    pends = jnp.cumsum(pcounts)
    pstarts = pends - pcounts
    block_start = jnp.arange(n_blocks, dtype=jnp.int32) * rb
    blk_e = jnp.minimum(jnp.sum(block_start[:, None] >= pends[None, :], axis=1),
                        N_EXPERTS - 1).astype(jnp.int32)
    mine = blk_e[:, None] == jnp.arange(N_EXPERTS, dtype=jnp.int32)[None, :]
    pick = lambda table: jnp.sum(jnp.where(mine, table[None, :], 0), axis=1)
    blk_n = jnp.clip(pick(counts) - (block_start - pick(pstarts)), 0, rb).astype(jnp.int32)
    return pstarts.astype(jnp.int32), blk_e, blk_n


def kernel(x, norm1_w, w_in, ssd_conv_w, ssd_conv_b, dt_bias, a_log, d_skip, ssd_norm_w,
           sc_conv_w, sc_norm_w, w_out, norm2_w, w_router, b_router, w_gate_up, b_gate_up,
           w_down, b_down, final_norm_w):
    assert norm1_w.shape[0] == 1, "single-layer problem"
    batch, seq, d = x.shape
    t = batch * seq
    x2 = x.reshape(t, d)
    row = lambda a: a.reshape(1, -1)

    s1, s2 = D_SSD + D_XBC, D_SSD + D_XBC + N_SSD_HEADS
    w = w_in[0]
    w_zx, w_dt, w_sc = w[:, :s1].astype(BF16), w[:, s1:s2].astype(BF16), w[:, s2:].astype(BF16)
    z, xbc, ysc, dt, dtt = _inproj(x2, seq, row(norm1_w[0]), w_zx, w_sc, w_dt, w_dt.T,
                                   ssd_conv_w[0], row(ssd_conv_b[0]), sc_conv_w[0],
                                   row(sc_norm_w[0]))
    yssd = _mixer(z, xbc, dt, dtt, batch, seq, row(dt_bias[0]), row(a_log[0]), row(d_skip[0]),
                  row(ssd_norm_w[0]))

    wr = w_router[0].T
    wr_hi = wr.astype(BF16)
    wr_split = jnp.concatenate([wr_hi, (wr - wr_hi.astype(F32)).astype(BF16)], axis=0)
    unit = t // sum(MOE_SPLIT)
    sizes = [unit * s for s in MOE_SPLIT]
    starts = [sum(sizes[:g]) for g in range(len(sizes))]
    assert sum(sizes) == t and all(n % (COMBINE_CHUNKS * COMBINE_TILE) == 0 and n % OUT_TILE == 0
                                   for n in sizes)
    w_out_b, nw2, br = w_out[0].astype(BF16), row(norm2_w[0]), b_router[0].reshape(-1, 1)
    routed = []
    for t0, tg in zip(starts, sizes):
        n_rows = tg * TOP_K + N_EXPERTS * ROW_BLOCK
        h1, xn2, idx, gates, rank, counts = _outproj(
            yssd, ysc, x2, w_out_b, nw2, wr_split, br, t0 // OUT_TILE, tg)
        pstarts, blk_e, blk_n = _block_plan(counts[:, 0].astype(jnp.int32),
                                            n_rows // ROW_BLOCK)
        dest = _dest(pstarts, idx, rank)
        xs = _sc_dispatch(xn2, dest, n_rows)
        routed.append((h1, gates, dest, blk_e, blk_n, xs))
    expert_out = [_experts(blk_e, blk_n, xs, w_gate_up[0], b_gate_up[0][:, None, :],
                           w_down[0], b_down[0][:, None, :])
                  for (_, _, _, blk_e, blk_n, xs) in routed]
    fw, out = row(final_norm_w), None
    for t0, tg, (h1, gates, dest, _, _, _), ys in zip(starts, sizes, routed, expert_out):
        tc = tg // COMBINE_CHUNKS
        for c in range(COMBINE_CHUNKS):
            planes = _sc_collect(ys, dest[:, c * tc:(c + 1) * tc])
            out = _combine(gates, h1, fw, planes, c * (tc // COMBINE_TILE),
                           (t0 + c * tc) // COMBINE_TILE, t, out)
    return out.reshape(batch, seq, d)
```
